```python
import jax, jax.numpy as jnp
from jax import lax
import numpy as np

D_MODEL = 1024
BATCH = 2
SEQ = 16384
DEPTH = 2

GRID_W = 64
CTX_LEN = 256
N_HEADS = 16
QK_NOPE = 64
QK_ROPE = 32
ROPE_FREQS = QK_ROPE // 4
V_HEAD = 64
Q_LORA = 256
KV_LORA = 128
ROPE_THETA = 10000.0
ATTN_DIM = N_HEADS * V_HEAD
SM_SCALE = (QK_NOPE + QK_ROPE) ** -0.5
Q_BLOCK = 128
F_GROUPS = 4
F_GROUP_DIM = 128
FOURIER_DIM = F_GROUPS * F_GROUP_DIM
OFF_KV = Q_LORA
OFF_KR = OFF_KV + KV_LORA
OFF_F = OFF_KR + QK_ROPE
OFF_G = OFF_F + FOURIER_DIM
IN_COLS = OFF_G + 2 * D_MODEL
D_FF = 2816
N_EXPERTS = 8
TOP_K = 2
D_FF_EXPERT = 3584
N_DENSE = (DEPTH + 1) // 2
N_MOE = DEPTH // 2
ALPHA = (2 * DEPTH) ** 0.25
BETA = (8 * DEPTH) ** -0.25
LN_EPS = 1e-6
RMS_EPS = 1e-6

kernel_name = "hybrid_mla_fnet_moe_diffusion_block"


def _layer_norm_f32(x):
    xf = x.astype(jnp.float32)
    mu = jnp.mean(xf, axis=-1, keepdims=True)
    var = jnp.mean(jnp.square(xf - mu), axis=-1, keepdims=True)
    return (xf - mu) * lax.rsqrt(var + LN_EPS)


def ln_affine(x, g, b):
    return (_layer_norm_f32(x) * g + b).astype(x.dtype)


def ln_modulate(x, shift, scale):
    return (_layer_norm_f32(x) * (1.0 + scale) + shift).astype(x.dtype)


def rms_norm(x, g):
    xf = x.astype(jnp.float32)
    y = xf * lax.rsqrt(jnp.mean(jnp.square(xf), axis=-1, keepdims=True) + RMS_EPS)
    return (y * g).astype(x.dtype)


def axial_rope_tables(n_tokens):
    n_rows = n_tokens // GRID_W
    rows = jnp.repeat(jnp.arange(n_rows), GRID_W)
    cols = jnp.tile(jnp.arange(GRID_W), n_rows)
    pos = jnp.stack([rows, cols], axis=-1).astype(jnp.float32)
    inv_freq = 1.0 / (ROPE_THETA ** (jnp.arange(ROPE_FREQS, dtype=jnp.float32) / ROPE_FREQS))
    ang = pos[..., None] * inv_freq
    return jnp.cos(ang), jnp.sin(ang)


def rope_2d(x, cos, sin):
    xs = x.reshape(x.shape[:-1] + (2, 2, ROPE_FREQS))
    x1, x2 = xs[..., 0, :], xs[..., 1, :]
    c, s = cos.astype(x.dtype), sin.astype(x.dtype)
    out = jnp.stack([x1 * c - x2 * s, x2 * c + x1 * s], axis=-2)
    return out.reshape(x.shape)


def mla_queries(proj, g_q, w_uq, rope):
    lead = proj.shape[:-1]
    q = (rms_norm(proj[..., :Q_LORA], g_q) @ w_uq).reshape(lead + (N_HEADS, QK_NOPE + QK_ROPE))
    qn, qr = q[..., :QK_NOPE], q[..., QK_NOPE:]
    if rope is not None:
        cos, sin = rope
        qr = rope_2d(qr, cos[:, None], sin[:, None])
    return qn, qr


def mla_keys(kv_proj, g_kv, w_uk, w_uv, rope):
    lead = kv_proj.shape[:-1]
    ckv = rms_norm(kv_proj[..., :KV_LORA], g_kv)
    kr = kv_proj[..., KV_LORA:]
    kn = (ckv @ w_uk).reshape(lead + (N_HEADS, QK_NOPE))
    v = (ckv @ w_uv).reshape(lead + (N_HEADS, V_HEAD))
    if rope is not None:
        cos, sin = rope
        kr = rope_2d(kr, cos, sin)
    return kn, kr, v


def attend(qn, qr, kn, kr, v):
    s = jnp.einsum('bqhd,bkhd->bhqk', qn, kn) + jnp.einsum('bqhr,bkr->bhqk', qr, kr)
    p = jax.nn.softmax(s.astype(jnp.float32) * SM_SCALE, axis=-1).astype(v.dtype)
    return jnp.einsum('bhqk,bkhd->bqhd', p, v)


def latent_attention(qn, qr, kn_all, kr_all, v_all):
    b, n = qn.shape[:2]
    nb = n // Q_BLOCK

    def to_blocks(t):
        return jnp.moveaxis(t.reshape((b, nb, Q_BLOCK) + t.shape[2:]), 1, 0)

    out = lax.map(lambda qs: attend(qs[0], qs[1], kn_all, kr_all, v_all), (to_blocks(qn), to_blocks(qr)))
    return jnp.moveaxis(out, 0, 1).reshape(b, n, ATTN_DIM)


def fourier_mix(u):
    b, n, _ = u.shape
    ug = u.reshape(b, n, F_GROUPS, F_GROUP_DIM).astype(jnp.float32)
    z = jnp.fft.fft2(ug, axes=(1, 3), norm="ortho")
    return jnp.real(z).astype(u.dtype).reshape(b, n, FOURIER_DIM)


def mixer_out(proj, attn, w_ao, w_fo, w_out):
    four = fourier_mix(proj[..., OFF_F:OFF_G])
    g_a = jax.nn.sigmoid(proj[..., OFF_G:OFF_G + D_MODEL])
    g_f = jax.nn.sigmoid(proj[..., OFF_G + D_MODEL:])
    merged = g_a * (attn @ w_ao) + g_f * (four @ w_fo)
    return merged @ w_out


def swiglu(h, wg, wu, wd):
    return (jax.nn.silu(h @ wg) * (h @ wu)) @ wd


def moe_ffn(h, w_router, b_router, wg, wu, wd):
    shape = h.shape
    t = h.reshape(-1, shape[-1])
    logits = (t @ w_router).astype(jnp.float32) + b_router.astype(jnp.float32)
    top_v, top_i = lax.top_k(logits, TOP_K)
    w = jax.nn.softmax(top_v, axis=-1)
    combine = jnp.sum(jax.nn.one_hot(top_i, N_EXPERTS, dtype=jnp.float32) * w[..., None], axis=1).astype(h.dtype)
    out = jnp.zeros_like(t)
    for e in range(N_EXPERTS):
        out = out + combine[:, e:e + 1] * swiglu(t, wg[e], wu[e], wd[e])
    return out.reshape(shape)


def setup_inputs(seed: int = 0) -> dict:
    key = jax.random.key(seed)
    ks = jax.random.split(key, 32)

    def nrm(k, shape, scale):
        return jax.random.normal(k, shape, jnp.float32) * scale

    D = D_MODEL
    return {
        "x": nrm(ks[0], (BATCH, SEQ, D), 1.0),
        "c": nrm(ks[1], (BATCH, D), 1.0),
        "ctx": nrm(ks[2], (BATCH, CTX_LEN, D), 1.0),
        "c_ctx": nrm(ks[3], (D,), 1.0),
        "w_mod": nrm(ks[4], (DEPTH, D, 6 * D), 0.5 * D ** -0.5),
        "b_mod": nrm(ks[5], (DEPTH, 6 * D), 0.02),
        "w_in": nrm(ks[6], (DEPTH, D, IN_COLS), D ** -0.5),
        "g_q": 1.0 + nrm(ks[7], (DEPTH, Q_LORA), 0.02),
        "w_uq": nrm(ks[8], (DEPTH, Q_LORA, N_HEADS * (QK_NOPE + QK_ROPE)), Q_LORA ** -0.5),
        "g_kv": 1.0 + nrm(ks[9], (DEPTH, KV_LORA), 0.02),
        "w_uk": nrm(ks[10], (DEPTH, KV_LORA, N_HEADS * QK_NOPE), KV_LORA ** -0.5),
        "w_uv": nrm(ks[11], (DEPTH, KV_LORA, N_HEADS * V_HEAD), BETA * KV_LORA ** -0.5),
        "w_ao": nrm(ks[12], (DEPTH, ATTN_DIM, D), ATTN_DIM ** -0.5),
        "w_fo": nrm(ks[13], (DEPTH, FOURIER_DIM, D), BETA * FOURIER_DIM ** -0.5),
        "w_out": nrm(ks[14], (DEPTH, D, D), BETA * D ** -0.5),
        "ln1_g": 1.0 + nrm(ks[15], (DEPTH, D), 0.02),
        "ln1_b": nrm(ks[16], (DEPTH, D), 0.02),
        "ln2_g": 1.0 + nrm(ks[17], (DEPTH, D), 0.02),
        "ln2_b": nrm(ks[18], (DEPTH, D), 0.02),
        "w_ff_gate": nrm(ks[19], (N_DENSE, D, D_FF), D ** -0.5),
        "w_ff_up": nrm(ks[20], (N_DENSE, D, D_FF), BETA * D ** -0.5),
        "w_ff_down": nrm(ks[21], (N_DENSE, D_FF, D), BETA * D_FF ** -0.5),
        "w_router": nrm(ks[22], (N_MOE, D, N_EXPERTS), D ** -0.5),
        "b_router": nrm(ks[23], (N_MOE, N_EXPERTS), 0.01),
        "w_e_gate": nrm(ks[24], (N_MOE, N_EXPERTS, D, D_FF_EXPERT), D ** -0.5),
        "w_e_up": nrm(ks[25], (N_MOE, N_EXPERTS, D, D_FF_EXPERT), BETA * D ** -0.5),
        "w_e_down": nrm(ks[26], (N_MOE, N_EXPERTS, D_FF_EXPERT, D), BETA * D_FF_EXPERT ** -0.5),
    }


def reference(x, c, ctx, c_ctx, w_mod, b_mod, w_in, g_q, w_uq, g_kv, w_uk, w_uv, w_ao, w_fo, w_out,
              ln1_g, ln1_b, ln2_g, ln2_b, w_ff_gate, w_ff_up, w_ff_down,
              w_router, b_router, w_e_gate, w_e_up, w_e_down):
    n_lat = x.shape[1]
    rope = axial_rope_tables(n_lat)

    def channel_mixer(h, l):
        i = l // 2
        if l % 2 == 0:
            return swiglu(h, w_ff_gate[i], w_ff_up[i], w_ff_down[i])
        return moe_ffn(h, w_router[i], b_router[i], w_e_gate[i], w_e_up[i], w_e_down[i])

    for l in range(DEPTH):
        last = l == DEPTH - 1
        mod_x = jnp.split(jax.nn.silu(c) @ w_mod[l] + b_mod[l], 6, axis=-1)
        mod_x = [m[:, None, :] for m in mod_x]
        mod_c = jnp.split(jax.nn.silu(c_ctx) @ w_mod[l] + b_mod[l], 6, axis=-1)
        sh1x, sc1x, gt1x, sh2x, sc2x, gt2x = mod_x
        sh1c, sc1c, gt1c, sh2c, sc2c, gt2c = mod_c

        h_x = ln_modulate(x, sh1x, sc1x)
        h_c = ln_modulate(ctx, sh1c, sc1c)
        proj_x = h_x @ w_in[l]
        qn_x, qr_x = mla_queries(proj_x, g_q[l], w_uq[l], rope)
        kn_x, kr_x, v_x = mla_keys(proj_x[..., OFF_KV:OFF_F], g_kv[l], w_uk[l], w_uv[l], rope)
        if last:
            kv_c = h_c @ w_in[l][:, OFF_KV:OFF_F]
        else:
            proj_c = h_c @ w_in[l]
            kv_c = proj_c[..., OFF_KV:OFF_F]
        kn_c, kr_c, v_c = mla_keys(kv_c, g_kv[l], w_uk[l], w_uv[l], None)

        attn_x = latent_attention(qn_x, qr_x,
                                  jnp.concatenate([kn_c, kn_x], axis=1),
                                  jnp.concatenate([kr_c, kr_x], axis=1),
                                  jnp.concatenate([v_c, v_x], axis=1))
        y_x = mixer_out(proj_x, attn_x, w_ao[l], w_fo[l], w_out[l])
        x = ln_affine(ALPHA * x + gt1x * y_x, ln1_g[l], ln1_b[l])

        if not last:
            qn_c, qr_c = mla_queries(proj_c, g_q[l], w_uq[l], None)
            attn_c = attend(qn_c, qr_c, kn_c, kr_c, v_c).reshape(ctx.shape[0], ctx.shape[1], ATTN_DIM)
            y_c = mixer_out(proj_c, attn_c, w_ao[l], w_fo[l], w_out[l])
            ctx = ln_affine(ALPHA * ctx + gt1c * y_c, ln1_g[l], ln1_b[l])

        f_x = channel_mixer(ln_modulate(x, sh2x, sc2x), l)
        x = ln_affine(ALPHA * x + gt2x * f_x, ln2_g[l], ln2_b[l])
        if not last:
            f_c = channel_mixer(ln_modulate(ctx, sh2c, sc2c), l)
            ctx = ln_affine(ALPHA * ctx + gt2c * f_c, ln2_g[l], ln2_b[l])

    return x
```

```python
import functools
import math

import numpy as np
import jax
import jax.numpy as jnp
from jax import lax
from jax.experimental import pallas as pl
from jax.experimental.pallas import tpu as pltpu

D_MODEL = 1024
BATCH = 2
SEQ = 16384
DEPTH = 2
GRID_W = 64
CTX_LEN = 256
N_HEADS = 16
QK_NOPE = 64
QK_ROPE = 32
ROPE_FREQS = QK_ROPE // 4
V_HEAD = 64
Q_LORA = 256
KV_LORA = 128
ROPE_THETA = 10000.0
SM_SCALE = (QK_NOPE + QK_ROPE) ** -0.5
F_GROUPS = 4
F_GROUP_DIM = 128
FOURIER_DIM = F_GROUPS * F_GROUP_DIM
OFF_KV = Q_LORA
OFF_KR = OFF_KV + KV_LORA
OFF_F = OFF_KR + QK_ROPE
OFF_G = OFF_F + FOURIER_DIM
D_FF = 2816
N_EXPERTS = 8
D_FF_EXPERT = 3584
ALPHA = (2 * DEPTH) ** 0.25
LN_EPS = 1e-6
RMS_EPS = 1e-6

LANES = 128
HEAD_PAD = 128
FFT_R = 128
VMEM_LIMIT = 56 * 1024 * 1024

BF16 = jnp.bfloat16
F32 = jnp.float32


def _cp(sem, vmem=VMEM_LIMIT):
    return pltpu.CompilerParams(dimension_semantics=sem, vmem_limit_bytes=vmem)


def _dot(a, b):
    return jnp.dot(a, b, preferred_element_type=F32)


def _ln(x):
    mu = jnp.mean(x, axis=-1, keepdims=True)
    xc = x - mu
    var = jnp.mean(xc * xc, axis=-1, keepdims=True)
    return xc * lax.rsqrt(var + LN_EPS)


def _rms(x, g):
    return x * lax.rsqrt(jnp.mean(x * x, axis=-1, keepdims=True) + RMS_EPS) * g


def _sigmoid(x):
    return 1.0 / (1.0 + jnp.exp(-x))


def _mod_kernel(cs_ref, w_ref, b_ref, o_ref):
    cs = cs_ref[...]
    a = (cs * _sigmoid(cs)).astype(BF16)
    o_ref[0] = _dot(a, w_ref[0]) + b_ref[0]


def _modulation(cs, w_mod, b_mod):
    n_chunk = 1024
    n_out = w_mod.shape[-1]
    return pl.pallas_call(
        _mod_kernel,
        grid=(DEPTH, n_out // n_chunk),
        in_specs=[
            pl.BlockSpec((8, D_MODEL), lambda l, j: (0, 0)),
            pl.BlockSpec((1, D_MODEL, n_chunk), lambda l, j: (l, 0, j)),
            pl.BlockSpec((1, 1, n_chunk), lambda l, j: (l, 0, j)),
        ],
        out_specs=pl.BlockSpec((1, 8, n_chunk), lambda l, j: (l, 0, j)),
        out_shape=jax.ShapeDtypeStruct((DEPTH, 8, n_out), F32),
        compiler_params=_cp(("parallel", "parallel")),
        name="mod",
    )(cs, w_mod, b_mod.reshape(DEPTH, 1, n_out))


IP_Q = 0
IP_KV = Q_LORA
IP_KR = IP_KV + KV_LORA
IP_F = IP_KR + LANES
IP_G = IP_F + FOURIER_DIM
IP_COLS = IP_G + 2 * D_MODEL


def _inproj_kernel(x_ref, mod_ref, w_ref, gq_ref, gkv_ref, tk_ref, bd_ref,
                   qlat_ref, ckr_ref, wf_ref, gates_ref):
    x = x_ref[...]
    mod = mod_ref[0]
    h = (_ln(x) * (1.0 + mod[1:2]) + mod[0:1]).astype(BF16)
    p0 = _dot(h, w_ref[:, IP_Q:IP_F])
    qlat_ref[...] = _rms(p0[:, IP_Q:IP_KV], gq_ref[...]).astype(BF16)
    ckr_ref[:, 0:KV_LORA] = _rms(p0[:, IP_KV:IP_KR], gkv_ref[...]).astype(BF16)
    ckr_ref[:, KV_LORA:] = (p0[:, IP_KR:IP_F] * tk_ref[...]).astype(BF16)
    uf = _dot(h, w_ref[:, IP_F:IP_G]).astype(BF16)
    for g in range(F_GROUPS):
        r = _dot(uf[:, g * LANES:(g + 1) * LANES], bd_ref[...])
        wf_ref[:, g * LANES:(g + 1) * LANES] = r[:, :LANES].astype(BF16)
        wf_ref[:, FOURIER_DIM + g * LANES:FOURIER_DIM + (g + 1) * LANES] = r[:, LANES:].astype(BF16)
    gc = 512
    for c in range(2 * D_MODEL // gc):
        gl = _dot(h, w_ref[:, IP_G + c * gc:IP_G + (c + 1) * gc])
        gates_ref[:, c * gc:(c + 1) * gc] = _sigmoid(gl).astype(BF16)


def _inproj(x2d, mod, w, gq, gkv, tk, bd, tm):
    t = x2d.shape[0]
    nb = mod.shape[0]
    tpb = t // nb // tm
    return pl.pallas_call(
        _inproj_kernel,
        grid=(t // tm,),
        in_specs=[
            pl.BlockSpec((tm, D_MODEL), lambda i: (i, 0)),
            pl.BlockSpec((1, 8, D_MODEL), lambda i: (i // tpb, 0, 0)),
            pl.BlockSpec((D_MODEL, IP_COLS), lambda i: (0, 0)),
            pl.BlockSpec((1, Q_LORA), lambda i: (0, 0)),
            pl.BlockSpec((1, KV_LORA), lambda i: (0, 0)),
            pl.BlockSpec((tm, LANES), lambda i: (i % tpb, 0)),
            pl.BlockSpec((LANES, 2 * LANES), lambda i: (0, 0)),
        ],
        out_specs=[
            pl.BlockSpec((tm, Q_LORA), lambda i: (i, 0)),
            pl.BlockSpec((tm, 2 * LANES), lambda i: (i, 0)),
            pl.BlockSpec((tm, 2 * FOURIER_DIM), lambda i: (i, 0)),
            pl.BlockSpec((tm, 2 * D_MODEL), lambda i: (i, 0)),
        ],
        out_shape=[
            jax.ShapeDtypeStruct((t, Q_LORA), BF16),
            jax.ShapeDtypeStruct((t, 2 * LANES), BF16),
            jax.ShapeDtypeStruct((t, 2 * FOURIER_DIM), BF16),
            jax.ShapeDtypeStruct((t, 2 * D_MODEL), BF16),
        ],
        compiler_params=_cp(("parallel",)),
        name="inproj",
    )(x2d, mod, w, gq, gkv, tk, bd)


def _qkv_kernel(qlat_ref, ckr_ref, tq_ref, wq_ref, wkt_ref, wv_ref, q_ref, kt_ref, v_ref):
    ql = qlat_ref[...]
    ckr = ckr_ref[...]
    tq = tq_ref[...]
    ckv = ckr[:, :KV_LORA]
    for h in range(N_HEADS):
        cols = slice(h * HEAD_PAD, (h + 1) * HEAD_PAD)
        q_ref[0, h] = (_dot(ql, wq_ref[:, cols]) * tq).astype(BF16)
        kt_ref[0, h] = lax.dot_general(
            wkt_ref[h], ckr, (((1,), (1,)), ((), ())), preferred_element_type=F32).astype(BF16)
        v_ref[0, h] = _dot(ckv, wv_ref[:, cols]).astype(BF16)


def _qkv(qlat, ckr, tq, wq, wkt, wv, nb, tm):
    t = qlat.shape[0]
    n = t // nb
    tpb = n // tm
    return pl.pallas_call(
        _qkv_kernel,
        grid=(t // tm,),
        in_specs=[
            pl.BlockSpec((tm, Q_LORA), lambda i: (i, 0)),
            pl.BlockSpec((tm, 2 * LANES), lambda i: (i, 0)),
            pl.BlockSpec((tm, HEAD_PAD), lambda i: (i % tpb, 0)),
            pl.BlockSpec((Q_LORA, N_HEADS * HEAD_PAD), lambda i: (0, 0)),
            pl.BlockSpec((N_HEADS, HEAD_PAD, 2 * LANES), lambda i: (0, 0, 0)),
            pl.BlockSpec((KV_LORA, N_HEADS * HEAD_PAD), lambda i: (0, 0)),
        ],
        out_specs=[
            pl.BlockSpec((1, N_HEADS, tm, HEAD_PAD), lambda i: (i // tpb, 0, i % tpb, 0)),
            pl.BlockSpec((1, N_HEADS, HEAD_PAD, tm), lambda i: (i // tpb, 0, 0, i % tpb)),
            pl.BlockSpec((1, N_HEADS, tm, HEAD_PAD), lambda i: (i // tpb, 0, i % tpb, 0)),
        ],
        out_shape=[
            jax.ShapeDtypeStruct((nb, N_HEADS, n, HEAD_PAD), BF16),
            jax.ShapeDtypeStruct((nb, N_HEADS, HEAD_PAD, n), BF16),
            jax.ShapeDtypeStruct((nb, N_HEADS, n, HEAD_PAD), BF16),
        ],
        compiler_params=_cp(("parallel",)),
        name="qkv",
    )(qlat, ckr, tq, wq, wkt, wv)


def _attn_kernel(q_ref, ktc_ref, vc_ref, *rest, n_chunks, tk):
    if n_chunks:
        kt_ref, v_ref, o_ref = rest
    else:
        (o_ref,) = rest
    out = None
    for hh in range(2):
        q = q_ref[0, hh]
        s = _dot(q, ktc_ref[0, hh])
        m = jnp.max(s, axis=1, keepdims=True)
        p = jnp.exp2(s - m)
        l = jnp.sum(p, axis=1, keepdims=True)
        acc = _dot(p.astype(BF16), vc_ref[0, hh])

        if n_chunks:
            def body(j, carry, hh=hh, q=q):
                m, l, acc = carry
                off = pl.multiple_of(j * tk, tk)
                s = _dot(q, kt_ref[0, hh, :, pl.ds(off, tk)])
                m_new = jnp.maximum(m, jnp.max(s, axis=1, keepdims=True))
                a = jnp.exp2(m - m_new)
                p = jnp.exp2(s - m_new)
                l = a * l + jnp.sum(p, axis=1, keepdims=True)
                acc = a * acc + _dot(p.astype(BF16), v_ref[0, hh, pl.ds(off, tk), :])
                return m_new, l, acc

            m, l, acc = lax.fori_loop(0, n_chunks, body, (m, l, acc))
        o = acc * (1.0 / l)
        out = o if out is None else out + o
    o_ref[0] = out.astype(BF16)


def _attention(q, kt_c, v_c, kt, v, tq, tk):
    nb, _, n, _ = q.shape
    nc = kt_c.shape[-1]
    n_chunks = 0 if kt is None else kt.shape[-1] // tk
    in_specs = [
        pl.BlockSpec((1, 2, tq, HEAD_PAD), lambda b, p, i: (b, p, i, 0)),
        pl.BlockSpec((1, 2, HEAD_PAD, nc), lambda b, p, i: (b, p, 0, 0)),
        pl.BlockSpec((1, 2, nc, HEAD_PAD), lambda b, p, i: (b, p, 0, 0)),
    ]
    args = [q, kt_c, v_c]
    if n_chunks:
        nk = kt.shape[-1]
        in_specs += [
            pl.BlockSpec((1, 2, HEAD_PAD, nk), lambda b, p, i: (b, p, 0, 0)),
            pl.BlockSpec((1, 2, nk, HEAD_PAD), lambda b, p, i: (b, p, 0, 0)),
        ]
        args += [kt, v]
    return pl.pallas_call(
        functools.partial(_attn_kernel, n_chunks=n_chunks, tk=tk),
        grid=(nb, N_HEADS // 2, n // tq),
        in_specs=in_specs,
        out_specs=pl.BlockSpec((1, tq, 2 * V_HEAD), lambda b, p, i: (b, i, p)),
        out_shape=jax.ShapeDtypeStruct((nb, n, N_HEADS * V_HEAD), BF16),
        compiler_params=_cp(("parallel", "parallel", "arbitrary")),
        name="attn",
    )(*args)


def _fft1_kernel(x_ref, c_ref, s_ref, tc_ref, ts_ref, o_ref, *, n2t):
    x = x_ref[0]
    cx = _dot(c_ref[...], x)
    sx = _dot(s_ref[...], x)
    w = 2 * FOURIER_DIM
    for t in range(n2t):
        re = slice(t * w, t * w + FOURIER_DIM)
        im = slice(t * w + FOURIER_DIM, (t + 1) * w)
        yr = cx[:, re] + sx[:, im]
        yi = cx[:, im] - sx[:, re]
        tc = jnp.concatenate([tc_ref[:, t * LANES:(t + 1) * LANES]] * F_GROUPS, axis=1)
        ts = jnp.concatenate([ts_ref[:, t * LANES:(t + 1) * LANES]] * F_GROUPS, axis=1)
        o_ref[0, :, re] = (yr * tc + yi * ts).astype(BF16)
        o_ref[0, :, im] = (yi * tc - yr * ts).astype(BF16)


def _fft2_kernel(y_ref, c_ref, s_ref, o_ref, *, k1t):
    for t in range(k1t):
        y = y_ref[0, t]
        zr = _dot(c_ref[...], y[:, :FOURIER_DIM]) + _dot(s_ref[...], y[:, FOURIER_DIM:])
        o_ref[0, :, t * FOURIER_DIM:(t + 1) * FOURIER_DIM] = zr.astype(BF16)


def _fourier_latent(wf, cmat, smat, tc, ts):
    nb = wf.shape[0]
    w = 2 * FOURIER_DIM
    n2t = 8
    y = pl.pallas_call(
        functools.partial(_fft1_kernel, n2t=n2t),
        grid=(nb, FFT_R // n2t),
        in_specs=[
            pl.BlockSpec((1, FFT_R, n2t * w), lambda b, j: (b, 0, j)),
            pl.BlockSpec((FFT_R, FFT_R), lambda b, j: (0, 0)),
            pl.BlockSpec((FFT_R, FFT_R), lambda b, j: (0, 0)),
            pl.BlockSpec((FFT_R, n2t * LANES), lambda b, j: (0, j)),
            pl.BlockSpec((FFT_R, n2t * LANES), lambda b, j: (0, j)),
        ],
        out_specs=pl.BlockSpec((1, FFT_R, n2t * w), lambda b, j: (b, 0, j)),
        out_shape=jax.ShapeDtypeStruct((nb, FFT_R, FFT_R * w), BF16),
        compiler_params=_cp(("parallel", "parallel")),
        name="fft1",
    )(wf.reshape(nb, FFT_R, FFT_R * w), cmat, smat, tc, ts)
    k1t = 8
    four = pl.pallas_call(
        functools.partial(_fft2_kernel, k1t=k1t),
        grid=(nb, FFT_R // k1t),
        in_specs=[
            pl.BlockSpec((1, k1t, FFT_R, w), lambda b, j: (b, j, 0, 0)),
            pl.BlockSpec((FFT_R, FFT_R), lambda b, j: (0, 0)),
            pl.BlockSpec((FFT_R, FFT_R), lambda b, j: (0, 0)),
        ],
        out_specs=pl.BlockSpec((1, FFT_R, k1t * FOURIER_DIM), lambda b, j: (b, 0, j)),
        out_shape=jax.ShapeDtypeStruct((nb, FFT_R, FFT_R * FOURIER_DIM), BF16),
        compiler_params=_cp(("parallel", "parallel")),
        name="fft2",
    )(y.reshape(nb, FFT_R, FFT_R, w), cmat, smat)
    return four.reshape(nb, SEQ, FOURIER_DIM)


def _dft_ctx_kernel(x_ref, c_ref, s_ref, o_ref):
    x = x_ref[0]
    zr = _dot(c_ref[...], x[:, :FOURIER_DIM]) + _dot(s_ref[...], x[:, FOURIER_DIM:])
    o_ref[0] = zr.astype(BF16)


def _fourier_ctx(wf, cmat, smat):
    nb, n, w = wf.shape
    return pl.pallas_call(
        _dft_ctx_kernel,
        grid=(nb,),
        in_specs=[
            pl.BlockSpec((1, n, w), lambda b: (b, 0, 0)),
            pl.BlockSpec((n, n), lambda b: (0, 0)),
            pl.BlockSpec((n, n), lambda b: (0, 0)),
        ],
        out_specs=pl.BlockSpec((1, n, FOURIER_DIM), lambda b: (b, 0, 0)),
        out_shape=jax.ShapeDtypeStruct((nb, n, FOURIER_DIM), BF16),
        compiler_params=_cp(("parallel",)),
        name="dft_ctx",
    )(wf, cmat, smat)


def _mixer_kernel(attn_ref, four_ref, gates_ref, x_ref, mod_ref, g_ref, b_ref,
                  wao_ref, wfo_ref, wout_ref, *rest, route):
    if route:
        wr_ref, br_ref, x1_ref, h2_ref, comb_ref = rest
    else:
        x1_ref, h2_ref = rest
    mod = mod_ref[0]
    a = _dot(attn_ref[...], wao_ref[...])
    f = _dot(four_ref[...], wfo_ref[...])
    merged = gates_ref[:, :D_MODEL].astype(F32) * a + gates_ref[:, D_MODEL:].astype(F32) * f
    y = _dot(merged.astype(BF16), wout_ref[...])
    x1 = _ln(ALPHA * x_ref[...] + mod[2:3] * y) * g_ref[...] + b_ref[...]
    x1_ref[...] = x1
    h2 = _ln(x1) * (1.0 + mod[4:5]) + mod[3:4]
    h2_ref[...] = h2.astype(BF16)
    if route:
        logits = jnp.dot(h2, wr_ref[...], preferred_element_type=F32,
                         precision=lax.Precision.HIGHEST) + br_ref[...]
        lane = lax.broadcasted_iota(jnp.int32, logits.shape, 1)
        neg = jnp.float32(-jnp.inf)
        lg = jnp.where(lane < N_EXPERTS, logits, neg)
        m1 = jnp.max(lg, axis=1, keepdims=True)
        i1 = jnp.min(jnp.where(lg == m1, lane, LANES), axis=1, keepdims=True)
        lg2 = jnp.where(lane == i1, neg, lg)
        m2 = jnp.max(lg2, axis=1, keepdims=True)
        i2 = jnp.min(jnp.where(lg2 == m2, lane, LANES), axis=1, keepdims=True)
        e2 = jnp.exp(m2 - m1)
        w1 = 1.0 / (1.0 + e2)
        w2 = e2 * w1
        comb_ref[...] = jnp.where(lane == i1, w1, 0.0) + jnp.where(lane == i2, w2, 0.0)


def _mixer(attn, four, gates, x2d, mod, g, b, wao, wfo, wout, tm, router=None):
    t = x2d.shape[0]
    nb = mod.shape[0]
    tpb = t // nb // tm
    row = lambda i: (i, 0)
    const = lambda i: (0, 0)
    in_specs = [
        pl.BlockSpec((tm, D_MODEL), row),
        pl.BlockSpec((tm, FOURIER_DIM), row),
        pl.BlockSpec((tm, 2 * D_MODEL), row),
        pl.BlockSpec((tm, D_MODEL), row),
        pl.BlockSpec((1, 8, D_MODEL), lambda i: (i // tpb, 0, 0)),
        pl.BlockSpec((1, D_MODEL), const),
        pl.BlockSpec((1, D_MODEL), const),
        pl.BlockSpec((D_MODEL, D_MODEL), const),
        pl.BlockSpec((FOURIER_DIM, D_MODEL), const),
        pl.BlockSpec((D_MODEL, D_MODEL), const),
    ]
    args = [attn, four, gates, x2d, mod, g, b, wao, wfo, wout]
    out_specs = [pl.BlockSpec((tm, D_MODEL), row), pl.BlockSpec((tm, D_MODEL), row)]
    out_shape = [jax.ShapeDtypeStruct((t, D_MODEL), F32), jax.ShapeDtypeStruct((t, D_MODEL), BF16)]
    if router is not None:
        in_specs += [pl.BlockSpec((D_MODEL, LANES), const), pl.BlockSpec((1, LANES), const)]
        args += list(router)
        out_specs.append(pl.BlockSpec((tm, LANES), row))
        out_shape.append(jax.ShapeDtypeStruct((t, LANES), F32))
    return pl.pallas_call(
        functools.partial(_mixer_kernel, route=router is not None),
        grid=(t // tm,),
        in_specs=in_specs,
        out_specs=out_specs,
        out_shape=out_shape,
        compiler_params=_cp(("parallel",)),
        name="mixer",
    )(*args)


def _ffn_kernel(h_ref, x1_ref, comb_ref, mod_ref, g_ref, b_ref, wg_ref, wu_ref, wd_ref,
                o_ref, acc_ref):
    e = pl.program_id(1)
    j = pl.program_id(2)

    @pl.when((e == 0) & (j == 0))
    def _():
        acc_ref[...] = jnp.zeros_like(acc_ref)

    h = h_ref[...]
    gate = _dot(h, wg_ref[0])
    up = _dot(h, wu_ref[0])
    act = (gate * _sigmoid(gate) * up).astype(BF16)
    comb = comb_ref[...]
    lane = lax.broadcasted_iota(jnp.int32, comb.shape, 1)
    cw = jnp.sum(jnp.where(lane == e, comb, 0.0), axis=1, keepdims=True)
    acc_ref[...] += cw * _dot(act, wd_ref[0])

    @pl.when((e == pl.num_programs(1) - 1) & (j == pl.num_programs(2) - 1))
    def _():
        mod = mod_ref[0]
        o_ref[...] = _ln(ALPHA * x1_ref[...] + mod[5:6] * acc_ref[...]) * g_ref[...] + b_ref[...]


def _ffn(h2, x1, comb, mod, g, b, wg, wu, wd, tm, fc):
    t = h2.shape[0]
    nb = mod.shape[0]
    tpb = t // nb // tm
    ne, _, dff = wg.shape
    row = lambda i, e, j: (i, 0)
    const = lambda i, e, j: (0, 0)
    return pl.pallas_call(
        _ffn_kernel,
        grid=(t // tm, ne, dff // fc),
        in_specs=[
            pl.BlockSpec((tm, D_MODEL), row),
            pl.BlockSpec((tm, D_MODEL), row),
            pl.BlockSpec((tm, LANES), row),
            pl.BlockSpec((1, 8, D_MODEL), lambda i, e, j: (i // tpb, 0, 0)),
            pl.BlockSpec((1, D_MODEL), const),
            pl.BlockSpec((1, D_MODEL), const),
            pl.BlockSpec((1, D_MODEL, fc), lambda i, e, j: (e, 0, j)),
            pl.BlockSpec((1, D_MODEL, fc), lambda i, e, j: (e, 0, j)),
            pl.BlockSpec((1, fc, D_MODEL), lambda i, e, j: (e, j, 0)),
        ],
        out_specs=pl.BlockSpec((tm, D_MODEL), row),
        out_shape=jax.ShapeDtypeStruct((t, D_MODEL), F32),
        scratch_shapes=[pltpu.VMEM((tm, D_MODEL), F32)],
        compiler_params=_cp(("parallel", "arbitrary", "arbitrary")),
        name="ffn",
    )(h2, x1, comb, mod, g, b, wg, wu, wd)


def _rope_np(n_tokens):
    rows = np.repeat(np.arange(n_tokens // GRID_W), GRID_W)
    cols = np.tile(np.arange(GRID_W), n_tokens // GRID_W)
    pos = np.stack([rows, cols], axis=-1).astype(np.float64)
    inv_freq = 1.0 / (ROPE_THETA ** (np.arange(ROPE_FREQS, dtype=np.float64) / ROPE_FREQS))
    ang = pos[..., None] * inv_freq
    cos, sin = np.cos(ang), np.sin(ang)
    cos32 = np.broadcast_to(cos[:, :, None, :], (n_tokens, 2, 2, ROPE_FREQS)).reshape(n_tokens, QK_ROPE)
    sgn = np.array([-1.0, 1.0])[None, None, :, None]
    sin32 = (np.broadcast_to(sin[:, :, None, :], (n_tokens, 2, 2, ROPE_FREQS)) * sgn).reshape(n_tokens, QK_ROPE)
    return cos32, sin32


def _half_swap_perm():
    p = np.arange(QK_ROPE).reshape(2, 2, ROPE_FREQS)
    return p[:, ::-1, :].reshape(QK_ROPE)


def _tables():
    c0 = SM_SCALE * math.log2(math.e)
    cos32, sin32 = _rope_np(SEQ)
    one64 = np.ones((SEQ, QK_NOPE))
    tq_lat = c0 * np.concatenate([one64, cos32, sin32], axis=1)
    tk_lat = np.concatenate([cos32, sin32, np.zeros((SEQ, LANES - 2 * QK_ROPE))], axis=1)
    tq_ctx = c0 * np.concatenate([np.ones((CTX_LEN, QK_NOPE + QK_ROPE)), np.zeros((CTX_LEN, QK_ROPE))], axis=1)
    tk_ctx = np.concatenate([np.ones((CTX_LEN, QK_ROPE)), np.zeros((CTX_LEN, LANES - QK_ROPE))], axis=1)
    f32 = lambda a: jnp.asarray(a, F32)
    bf = lambda a: jnp.asarray(a, BF16)

    def dft(n):
        k = np.arange(n)
        ang = 2.0 * np.pi * ((k[:, None] * k[None, :]) % n) / n
        return np.cos(ang), np.sin(ang)

    c128, s128 = dft(FFT_R)
    c256, s256 = dft(CTX_LEN)
    bd_lat = np.concatenate([c128, -s128], axis=1) / math.sqrt(SEQ * F_GROUP_DIM)
    bd_ctx = np.concatenate([c128, -s128], axis=1) / math.sqrt(CTX_LEN * F_GROUP_DIM)
    k1 = np.arange(FFT_R)
    tw = 2.0 * np.pi * ((k1[:, None] * k1[None, :]) % SEQ) / SEQ
    tc = jnp.broadcast_to(f32(np.cos(tw))[:, :, None], (FFT_R, FFT_R, LANES)).reshape(FFT_R, FFT_R * LANES)
    ts = jnp.broadcast_to(f32(np.sin(tw))[:, :, None], (FFT_R, FFT_R, LANES)).reshape(FFT_R, FFT_R * LANES)
    return dict(tq_lat=f32(tq_lat), tk_lat=f32(tk_lat), tq_ctx=f32(tq_ctx), tk_ctx=f32(tk_ctx),
                c128=bf(c128), s128=bf(s128), c256=bf(c256), s256=bf(s256),
                bd_lat=bf(bd_lat), bd_ctx=bf(bd_ctx), tc=tc, ts=ts)


def _arrange_weights(l, w_in, w_uq, w_uk, w_uv):
    sw = _half_swap_perm()
    wi = w_in[l]
    kr = wi[:, OFF_KR:OFF_F]
    w_ip = jnp.concatenate([
        wi[:, :OFF_KR], kr, kr[:, sw], jnp.zeros((D_MODEL, LANES - 2 * QK_ROPE), F32),
        wi[:, OFF_F:]], axis=1).astype(BF16)
    uq = w_uq[l].reshape(Q_LORA, N_HEADS, QK_NOPE + QK_ROPE)
    qr = uq[:, :, QK_NOPE:]
    wq = jnp.concatenate([uq, qr[:, :, sw]], axis=-1).reshape(Q_LORA, N_HEADS * HEAD_PAD).astype(BF16)
    uk = w_uk[l].reshape(KV_LORA, N_HEADS, QK_NOPE)
    top = jnp.concatenate([uk, jnp.zeros((KV_LORA, N_HEADS, HEAD_PAD - QK_NOPE), F32)], axis=-1)
    rmat = np.zeros((LANES, HEAD_PAD), np.float32)
    for i in range(QK_ROPE):
        for r in (i, QK_ROPE + i):
            rmat[r, QK_NOPE + i] = 1.0
            rmat[r, QK_NOPE + QK_ROPE + i] = 1.0
    bot = jnp.broadcast_to(jnp.asarray(rmat)[:, None, :], (LANES, N_HEADS, HEAD_PAD))
    wkt = jnp.transpose(jnp.concatenate([top, bot], axis=0), (1, 2, 0)).astype(BF16)
    uv = w_uv[l].reshape(KV_LORA, N_HEADS // 2, 2, V_HEAD)
    z = jnp.zeros_like(uv[:, :, 0])
    even = jnp.concatenate([uv[:, :, 0], z], axis=-1)
    odd = jnp.concatenate([z, uv[:, :, 1]], axis=-1)
    wv = jnp.stack([even, odd], axis=2).reshape(KV_LORA, N_HEADS * HEAD_PAD).astype(BF16)
    return w_ip, wq, wkt, wv


def kernel(x, c, ctx, c_ctx, w_mod, b_mod, w_in, g_q, w_uq, g_kv, w_uk, w_uv, w_ao, w_fo, w_out,
           ln1_g, ln1_b, ln2_g, ln2_b, w_ff_gate, w_ff_up, w_ff_down,
           w_router, b_router, w_e_gate, w_e_up, w_e_down):
    tb = _tables()
    t_lat = BATCH * SEQ
    t_ctx = BATCH * CTX_LEN

    cs = jnp.concatenate([c, c_ctx[None, :], jnp.zeros((8 - BATCH - 1, D_MODEL), F32)], axis=0)
    mods = _modulation(cs, w_mod.astype(BF16), b_mod).reshape(DEPTH, 8, 6, D_MODEL)
    pad2 = jnp.zeros((DEPTH, 8, 2, D_MODEL), F32)
    mods = jnp.concatenate([mods, pad2], axis=2)

    xl = x.reshape(t_lat, D_MODEL)
    xc = ctx.reshape(t_ctx, D_MODEL)
    ones_comb = jnp.ones((t_lat, LANES), F32)
    ones_comb_c = jnp.ones((t_ctx, LANES), F32)

    for l in range(DEPTH):
        last = l == DEPTH - 1
        mod_x = mods[l, :BATCH]
        mod_c = jnp.broadcast_to(mods[l, BATCH:BATCH + 1], (BATCH, 8, D_MODEL))
        w_ip, wq, wkt, wv = _arrange_weights(l, w_in, w_uq, w_uk, w_uv)
        gq = g_q[l][None, :]
        gkv = g_kv[l][None, :]
        wao = w_ao[l].astype(BF16)
        wfo = w_fo[l].astype(BF16)
        wout = w_out[l].astype(BF16)
        g1, b1 = ln1_g[l][None, :], ln1_b[l][None, :]
        g2, b2 = ln2_g[l][None, :], ln2_b[l][None, :]

        qlat_c, ckr_c, wf_c, gates_c = _inproj(xc, mod_c, w_ip, gq, gkv, tb["tk_ctx"], tb["bd_ctx"], tm=CTX_LEN)
        q_c, kt_c, v_c = _qkv(qlat_c, ckr_c, tb["tq_ctx"], wq, wkt, wv, nb=BATCH, tm=CTX_LEN)

        qlat, ckr, wf, gates = _inproj(xl, mod_x, w_ip, gq, gkv, tb["tk_lat"], tb["bd_lat"], tm=512)
        q, kt, v = _qkv(qlat, ckr, tb["tq_lat"], wq, wkt, wv, nb=BATCH, tm=512)
        attn = _attention(q, kt_c, v_c, kt, v, tq=512, tk=512)
        four = _fourier_latent(wf.reshape(BATCH, SEQ, 2 * FOURIER_DIM), tb["c128"], tb["s128"], tb["tc"], tb["ts"])

        if l % 2 == 0:
            i = l // 2
            x1, h2 = _mixer(attn.reshape(t_lat, -1), four.reshape(t_lat, -1), gates, xl, mod_x, g1, b1,
                            wao, wfo, wout, tm=512)
            wg = w_ff_gate[i][None].astype(BF16)
            wu = w_ff_up[i][None].astype(BF16)
            wd = w_ff_down[i][None].astype(BF16)
            xl_new = _ffn(h2, x1, ones_comb, mod_x, g2, b2, wg, wu, wd, tm=512, fc=D_FF // 2)
        else:
            i = l // 2
            wr = jnp.concatenate([w_router[i], jnp.zeros((D_MODEL, LANES - N_EXPERTS), F32)], axis=1)
            br = jnp.concatenate([b_router[i], jnp.zeros((LANES - N_EXPERTS,), F32)])[None, :]
            x1, h2, comb = _mixer(attn.reshape(t_lat, -1), four.reshape(t_lat, -1), gates, xl, mod_x, g1, b1,
                                  wao, wfo, wout, tm=512, router=(wr, br))
            wg = w_e_gate[i].astype(BF16)
            wu = w_e_up[i].astype(BF16)
            wd = w_e_down[i].astype(BF16)
            xl_new = _ffn(h2, x1, comb, mod_x, g2, b2, wg, wu, wd, tm=512, fc=D_FF_EXPERT // 2)

        if not last:
            attn_c = _attention(q_c, kt_c, v_c, None, None, tq=CTX_LEN, tk=CTX_LEN)
            four_c = _fourier_ctx(wf_c.reshape(BATCH, CTX_LEN, 2 * FOURIER_DIM), tb["c256"], tb["s256"])
            if l % 2 == 0:
                x1c, h2c = _mixer(attn_c.reshape(t_ctx, -1), four_c.reshape(t_ctx, -1), gates_c, xc, mod_c,
                                  g1, b1, wao, wfo, wout, tm=CTX_LEN)
                xc = _ffn(h2c, x1c, ones_comb_c, mod_c, g2, b2, wg, wu, wd, tm=CTX_LEN, fc=D_FF // 2)
            else:
                x1c, h2c, comb_c = _mixer(attn_c.reshape(t_ctx, -1), four_c.reshape(t_ctx, -1), gates_c, xc,
                                          mod_c, g1, b1, wao, wfo, wout, tm=CTX_LEN, router=(wr, br))
                xc = _ffn(h2c, x1c, comb_c, mod_c, g2, b2, wg, wu, wd, tm=CTX_LEN, fc=D_FF_EXPERT // 2)
        xl = xl_new

    return xl.reshape(BATCH, SEQ, D_MODEL)
```

```python
import functools
import math

import numpy as np
import jax
import jax.numpy as jnp
from jax import lax
from jax.experimental import pallas as pl
from jax.experimental.pallas import tpu as pltpu

D_MODEL = 1024
BATCH = 2
SEQ = 16384
DEPTH = 2
GRID_W = 64
CTX_LEN = 256
N_HEADS = 16
QK_NOPE = 64
QK_ROPE = 32
ROPE_FREQS = QK_ROPE // 4
V_HEAD = 64
Q_LORA = 256
KV_LORA = 128
ROPE_THETA = 10000.0
SM_SCALE = (QK_NOPE + QK_ROPE) ** -0.5
F_GROUPS = 4
F_GROUP_DIM = 128
FOURIER_DIM = F_GROUPS * F_GROUP_DIM
OFF_KV = Q_LORA
OFF_KR = OFF_KV + KV_LORA
OFF_F = OFF_KR + QK_ROPE
OFF_G = OFF_F + FOURIER_DIM
D_FF = 2816
N_EXPERTS = 8
D_FF_EXPERT = 3584
ALPHA = (2 * DEPTH) ** 0.25
LN_EPS = 1e-6
RMS_EPS = 1e-6

LANES = 128
HEAD_PAD = 128
FFT_R = 128
V_ONES_LANE = (V_HEAD, 0)
VMEM_LIMIT = 56 * 1024 * 1024

BF16 = jnp.bfloat16
F32 = jnp.float32


def _cp(sem, vmem=VMEM_LIMIT):
    return pltpu.CompilerParams(dimension_semantics=sem, vmem_limit_bytes=vmem)


def _dot(a, b):
    return jnp.dot(a, b, preferred_element_type=F32)


def _ln(x):
    mu = jnp.mean(x, axis=-1, keepdims=True)
    xc = x - mu
    var = jnp.mean(xc * xc, axis=-1, keepdims=True)
    return xc * lax.rsqrt(var + LN_EPS)


def _rms(x, g):
    return x * lax.rsqrt(jnp.mean(x * x, axis=-1, keepdims=True) + RMS_EPS) * g


def _sigmoid(x):
    return 1.0 / (1.0 + jnp.exp(-x))


def _mod_kernel(cs_ref, w_ref, b_ref, o_ref):
    cs = cs_ref[...]
    a = (cs * _sigmoid(cs)).astype(BF16)
    o_ref[0] = _dot(a, w_ref[0]) + b_ref[0]


def _modulation(cs, w_mod, b_mod):
    n_chunk = 1024
    n_out = w_mod.shape[-1]
    return pl.pallas_call(
        _mod_kernel,
        grid=(DEPTH, n_out // n_chunk),
        in_specs=[
            pl.BlockSpec((8, D_MODEL), lambda l, j: (0, 0)),
            pl.BlockSpec((1, D_MODEL, n_chunk), lambda l, j: (l, 0, j)),
            pl.BlockSpec((1, 1, n_chunk), lambda l, j: (l, 0, j)),
        ],
        out_specs=pl.BlockSpec((1, 8, n_chunk), lambda l, j: (l, 0, j)),
        out_shape=jax.ShapeDtypeStruct((DEPTH, 8, n_out), F32),
        compiler_params=_cp(("parallel", "parallel")),
        name="mod",
    )(cs, w_mod, b_mod.reshape(DEPTH, 1, n_out))


IP_Q = 0
IP_KV = Q_LORA
IP_KR = IP_KV + KV_LORA
IP_F = IP_KR + LANES
IP_G = IP_F + FOURIER_DIM
IP_COLS = IP_G + 2 * D_MODEL


def _inproj_kernel(x_ref, mod_ref, w_ref, gq_ref, gkv_ref, tk_ref, bd_ref,
                   qlat_ref, ckr_ref, wf_ref, gates_ref):
    x = x_ref[...]
    mod = mod_ref[0]
    h = (_ln(x) * (1.0 + mod[1:2]) + mod[0:1]).astype(BF16)
    p0 = _dot(h, w_ref[:, IP_Q:IP_F])
    qlat_ref[...] = _rms(p0[:, IP_Q:IP_KV], gq_ref[...]).astype(BF16)
    ckr_ref[:, 0:KV_LORA] = _rms(p0[:, IP_KV:IP_KR], gkv_ref[...]).astype(BF16)
    ckr_ref[:, KV_LORA:] = (p0[:, IP_KR:IP_F] * tk_ref[...]).astype(BF16)
    uf = _dot(h, w_ref[:, IP_F:IP_G]).astype(BF16)
    for g in range(F_GROUPS):
        r = _dot(uf[:, g * LANES:(g + 1) * LANES], bd_ref[...])
        wf_ref[:, g * LANES:(g + 1) * LANES] = r[:, :LANES].astype(BF16)
        wf_ref[:, FOURIER_DIM + g * LANES:FOURIER_DIM + (g + 1) * LANES] = r[:, LANES:].astype(BF16)
    gc = 512
    for c in range(2 * D_MODEL // gc):
        gl = _dot(h, w_ref[:, IP_G + c * gc:IP_G + (c + 1) * gc])
        gates_ref[:, c * gc:(c + 1) * gc] = _sigmoid(gl).astype(BF16)


def _inproj(x2d, mod, w, gq, gkv, tk, bd, tm):
    t = x2d.shape[0]
    nb = mod.shape[0]
    tpb = t // nb // tm
    return pl.pallas_call(
        _inproj_kernel,
        grid=(t // tm,),
        in_specs=[
            pl.BlockSpec((tm, D_MODEL), lambda i: (i, 0)),
            pl.BlockSpec((1, 8, D_MODEL), lambda i: (i // tpb, 0, 0)),
            pl.BlockSpec((D_MODEL, IP_COLS), lambda i: (0, 0)),
            pl.BlockSpec((1, Q_LORA), lambda i: (0, 0)),
            pl.BlockSpec((1, KV_LORA), lambda i: (0, 0)),
            pl.BlockSpec((tm, LANES), lambda i: (i % tpb, 0)),
            pl.BlockSpec((LANES, 2 * LANES), lambda i: (0, 0)),
        ],
        out_specs=[
            pl.BlockSpec((tm, Q_LORA), lambda i: (i, 0)),
            pl.BlockSpec((tm, 2 * LANES), lambda i: (i, 0)),
            pl.BlockSpec((tm, 2 * FOURIER_DIM), lambda i: (i, 0)),
            pl.BlockSpec((tm, 2 * D_MODEL), lambda i: (i, 0)),
        ],
        out_shape=[
            jax.ShapeDtypeStruct((t, Q_LORA), BF16),
            jax.ShapeDtypeStruct((t, 2 * LANES), BF16),
            jax.ShapeDtypeStruct((t, 2 * FOURIER_DIM), BF16),
            jax.ShapeDtypeStruct((t, 2 * D_MODEL), BF16),
        ],
        compiler_params=_cp(("parallel",)),
        name="inproj",
    )(x2d, mod, w, gq, gkv, tk, bd)


def _qkv_kernel(qlat_ref, ckr_ref, tq_ref, wq_ref, wkt_ref, wv_ref, q_ref, kt_ref, v_ref):
    ql = qlat_ref[...]
    ckr = ckr_ref[...]
    tq = tq_ref[...]
    ckv = ckr[:, :KV_LORA]
    lane = lax.broadcasted_iota(jnp.int32, (ckr.shape[0], HEAD_PAD), 1)
    for h in range(N_HEADS):
        cols = slice(h * HEAD_PAD, (h + 1) * HEAD_PAD)
        q_ref[0, h] = (_dot(ql, wq_ref[:, cols]) * tq).astype(BF16)
        kt_ref[0, h] = lax.dot_general(
            wkt_ref[h], ckr, (((1,), (1,)), ((), ())), preferred_element_type=F32).astype(BF16)
        vh = _dot(ckv, wv_ref[:, cols])
        v_ref[0, h] = jnp.where(lane == V_ONES_LANE[h % 2], 1.0, vh).astype(BF16)


def _qkv(qlat, ckr, tq, wq, wkt, wv, nb, tm):
    t = qlat.shape[0]
    n = t // nb
    tpb = n // tm
    return pl.pallas_call(
        _qkv_kernel,
        grid=(t // tm,),
        in_specs=[
            pl.BlockSpec((tm, Q_LORA), lambda i: (i, 0)),
            pl.BlockSpec((tm, 2 * LANES), lambda i: (i, 0)),
            pl.BlockSpec((tm, HEAD_PAD), lambda i: (i % tpb, 0)),
            pl.BlockSpec((Q_LORA, N_HEADS * HEAD_PAD), lambda i: (0, 0)),
            pl.BlockSpec((N_HEADS, HEAD_PAD, 2 * LANES), lambda i: (0, 0, 0)),
            pl.BlockSpec((KV_LORA, N_HEADS * HEAD_PAD), lambda i: (0, 0)),
        ],
        out_specs=[
            pl.BlockSpec((1, N_HEADS, tm, HEAD_PAD), lambda i: (i // tpb, 0, i % tpb, 0)),
            pl.BlockSpec((1, N_HEADS, HEAD_PAD, tm), lambda i: (i // tpb, 0, 0, i % tpb)),
            pl.BlockSpec((1, N_HEADS, tm, HEAD_PAD), lambda i: (i // tpb, 0, i % tpb, 0)),
        ],
        out_shape=[
            jax.ShapeDtypeStruct((nb, N_HEADS, n, HEAD_PAD), BF16),
            jax.ShapeDtypeStruct((nb, N_HEADS, HEAD_PAD, n), BF16),
            jax.ShapeDtypeStruct((nb, N_HEADS, n, HEAD_PAD), BF16),
        ],
        compiler_params=_cp(("parallel",)),
        name="qkv",
    )(qlat, ckr, tq, wq, wkt, wv)


def _attn_kernel(q_ref, ktc_ref, vc_ref, *rest, n_chunks, tk):
    if n_chunks:
        kt_ref, v_ref, o_ref = rest
    else:
        (o_ref,) = rest

    def step(q, kt, v, m, acc):
        s = _dot(q, kt)
        m_new = jnp.max(s, axis=1, keepdims=True)
        if m is not None:
            m_new = jnp.maximum(m, m_new)
        p = jnp.exp2((s - m_new).astype(BF16))
        pv = _dot(p, v)
        if m is not None:
            pv = jnp.exp2(m - m_new) * acc + pv
        return m_new, pv

    qs = [q_ref[0, hh] for hh in range(2)]
    carry = []
    for hh in range(2):
        carry += step(qs[hh], ktc_ref[0, hh], vc_ref[0, hh], None, None)

    if n_chunks:
        def body(j, carry):
            off = pl.multiple_of(j * tk, tk)
            out = []
            for hh in range(2):
                out += step(qs[hh], kt_ref[0, hh, :, pl.ds(off, tk)], v_ref[0, hh, pl.ds(off, tk), :],
                            carry[2 * hh], carry[2 * hh + 1])
            return tuple(out)

        carry = lax.fori_loop(0, n_chunks, body, tuple(carry), unroll=2)

    lane = lax.broadcasted_iota(jnp.int32, carry[1].shape, 1)
    outs = []
    for hh in range(2):
        acc = carry[2 * hh + 1]
        l = jnp.sum(jnp.where(lane == V_ONES_LANE[hh], acc, 0.0), axis=1, keepdims=True)
        outs.append(acc * (1.0 / l))
    o_ref[0] = jnp.where(lane < V_HEAD, outs[0], outs[1]).astype(BF16)


def _attention(q, kt_c, v_c, kt, v, tq, tk):
    nb, _, n, _ = q.shape
    nc = kt_c.shape[-1]
    n_chunks = 0 if kt is None else kt.shape[-1] // tk
    in_specs = [
        pl.BlockSpec((1, 2, tq, HEAD_PAD), lambda b, p, i: (b, p, i, 0)),
        pl.BlockSpec((1, 2, HEAD_PAD, nc), lambda b, p, i: (b, p, 0, 0)),
        pl.BlockSpec((1, 2, nc, HEAD_PAD), lambda b, p, i: (b, p, 0, 0)),
    ]
    args = [q, kt_c, v_c]
    if n_chunks:
        nk = kt.shape[-1]
        in_specs += [
            pl.BlockSpec((1, 2, HEAD_PAD, nk), lambda b, p, i: (b, p, 0, 0)),
            pl.BlockSpec((1, 2, nk, HEAD_PAD), lambda b, p, i: (b, p, 0, 0)),
        ]
        args += [kt, v]
    return pl.pallas_call(
        functools.partial(_attn_kernel, n_chunks=n_chunks, tk=tk),
        grid=(nb, N_HEADS // 2, n // tq),
        in_specs=in_specs,
        out_specs=pl.BlockSpec((1, tq, 2 * V_HEAD), lambda b, p, i: (b, i, p)),
        out_shape=jax.ShapeDtypeStruct((nb, n, N_HEADS * V_HEAD), BF16),
        compiler_params=_cp(("parallel", "parallel", "arbitrary")),
        name="attn",
    )(*args)


def _fft1_kernel(x_ref, c_ref, s_ref, tc_ref, ts_ref, o_ref, *, n2t):
    x = x_ref[0]
    cx = _dot(c_ref[...], x)
    sx = _dot(s_ref[...], x)
    w = 2 * FOURIER_DIM
    for t in range(n2t):
        re = slice(t * w, t * w + FOURIER_DIM)
        im = slice(t * w + FOURIER_DIM, (t + 1) * w)
        yr = cx[:, re] + sx[:, im]
        yi = cx[:, im] - sx[:, re]
        tc = jnp.concatenate([tc_ref[:, t * LANES:(t + 1) * LANES]] * F_GROUPS, axis=1)
        ts = jnp.concatenate([ts_ref[:, t * LANES:(t + 1) * LANES]] * F_GROUPS, axis=1)
        o_ref[0, :, re] = (yr * tc + yi * ts).astype(BF16)
        o_ref[0, :, im] = (yi * tc - yr * ts).astype(BF16)


def _fft2_kernel(y_ref, c_ref, s_ref, o_ref, *, k1t):
    for t in range(k1t):
        y = y_ref[0, t]
        zr = _dot(c_ref[...], y[:, :FOURIER_DIM]) + _dot(s_ref[...], y[:, FOURIER_DIM:])
        o_ref[0, :, t * FOURIER_DIM:(t + 1) * FOURIER_DIM] = zr.astype(BF16)


def _fourier_latent(wf, cmat, smat, tc, ts):
    nb = wf.shape[0]
    w = 2 * FOURIER_DIM
    n2t = 8
    y = pl.pallas_call(
        functools.partial(_fft1_kernel, n2t=n2t),
        grid=(nb, FFT_R // n2t),
        in_specs=[
            pl.BlockSpec((1, FFT_R, n2t * w), lambda b, j: (b, 0, j)),
            pl.BlockSpec((FFT_R, FFT_R), lambda b, j: (0, 0)),
            pl.BlockSpec((FFT_R, FFT_R), lambda b, j: (0, 0)),
            pl.BlockSpec((FFT_R, n2t * LANES), lambda b, j: (0, j)),
            pl.BlockSpec((FFT_R, n2t * LANES), lambda b, j: (0, j)),
        ],
        out_specs=pl.BlockSpec((1, FFT_R, n2t * w), lambda b, j: (b, 0, j)),
        out_shape=jax.ShapeDtypeStruct((nb, FFT_R, FFT_R * w), BF16),
        compiler_params=_cp(("parallel", "parallel")),
        name="fft1",
    )(wf.reshape(nb, FFT_R, FFT_R * w), cmat, smat, tc, ts)
    k1t = 8
    four = pl.pallas_call(
        functools.partial(_fft2_kernel, k1t=k1t),
        grid=(nb, FFT_R // k1t),
        in_specs=[
            pl.BlockSpec((1, k1t, FFT_R, w), lambda b, j: (b, j, 0, 0)),
            pl.BlockSpec((FFT_R, FFT_R), lambda b, j: (0, 0)),
            pl.BlockSpec((FFT_R, FFT_R), lambda b, j: (0, 0)),
        ],
        out_specs=pl.BlockSpec((1, FFT_R, k1t * FOURIER_DIM), lambda b, j: (b, 0, j)),
        out_shape=jax.ShapeDtypeStruct((nb, FFT_R, FFT_R * FOURIER_DIM), BF16),
        compiler_params=_cp(("parallel", "parallel")),
        name="fft2",
    )(y.reshape(nb, FFT_R, FFT_R, w), cmat, smat)
    return four.reshape(nb, SEQ, FOURIER_DIM)


def _dft_ctx_kernel(x_ref, c_ref, s_ref, o_ref):
    x = x_ref[0]
    zr = _dot(c_ref[...], x[:, :FOURIER_DIM]) + _dot(s_ref[...], x[:, FOURIER_DIM:])
    o_ref[0] = zr.astype(BF16)


def _fourier_ctx(wf, cmat, smat):
    nb, n, w = wf.shape
    return pl.pallas_call(
        _dft_ctx_kernel,
        grid=(nb,),
        in_specs=[
            pl.BlockSpec((1, n, w), lambda b: (b, 0, 0)),
            pl.BlockSpec((n, n), lambda b: (0, 0)),
            pl.BlockSpec((n, n), lambda b: (0, 0)),
        ],
        out_specs=pl.BlockSpec((1, n, FOURIER_DIM), lambda b: (b, 0, 0)),
        out_shape=jax.ShapeDtypeStruct((nb, n, FOURIER_DIM), BF16),
        compiler_params=_cp(("parallel",)),
        name="dft_ctx",
    )(wf, cmat, smat)


def _mixer_kernel(attn_ref, four_ref, gates_ref, x_ref, mod_ref, g_ref, b_ref,
                  wao_ref, wfo_ref, wout_ref, *rest, route):
    if route:
        wr_ref, br_ref, x1_ref, h2_ref, comb_ref = rest
    else:
        x1_ref, h2_ref = rest
    mod = mod_ref[0]
    a = _dot(attn_ref[...], wao_ref[...])
    f = _dot(four_ref[...], wfo_ref[...])
    merged = gates_ref[:, :D_MODEL].astype(F32) * a + gates_ref[:, D_MODEL:].astype(F32) * f
    y = _dot(merged.astype(BF16), wout_ref[...])
    x1 = _ln(ALPHA * x_ref[...] + mod[2:3] * y) * g_ref[...] + b_ref[...]
    x1_ref[...] = x1
    h2 = _ln(x1) * (1.0 + mod[4:5]) + mod[3:4]
    h2_ref[...] = h2.astype(BF16)
    if route:
        logits = jnp.dot(h2, wr_ref[...], preferred_element_type=F32,
                         precision=lax.Precision.HIGHEST) + br_ref[...]
        lane = lax.broadcasted_iota(jnp.int32, logits.shape, 1)
        neg = jnp.float32(-jnp.inf)
        lg = jnp.where(lane < N_EXPERTS, logits, neg)
        m1 = jnp.max(lg, axis=1, keepdims=True)
        i1 = jnp.min(jnp.where(lg == m1, lane, LANES), axis=1, keepdims=True)
        lg2 = jnp.where(lane == i1, neg, lg)
        m2 = jnp.max(lg2, axis=1, keepdims=True)
        i2 = jnp.min(jnp.where(lg2 == m2, lane, LANES), axis=1, keepdims=True)
        e2 = jnp.exp(m2 - m1)
        w1 = 1.0 / (1.0 + e2)
        w2 = e2 * w1
        comb_ref[...] = jnp.where(lane == i1, w1, 0.0) + jnp.where(lane == i2, w2, 0.0)


def _mixer(attn, four, gates, x2d, mod, g, b, wao, wfo, wout, tm, router=None):
    t = x2d.shape[0]
    nb = mod.shape[0]
    tpb = t // nb // tm
    row = lambda i: (i, 0)
    const = lambda i: (0, 0)
    in_specs = [
        pl.BlockSpec((tm, D_MODEL), row),
        pl.BlockSpec((tm, FOURIER_DIM), row),
        pl.BlockSpec((tm, 2 * D_MODEL), row),
        pl.BlockSpec((tm, D_MODEL), row),
        pl.BlockSpec((1, 8, D_MODEL), lambda i: (i // tpb, 0, 0)),
        pl.BlockSpec((1, D_MODEL), const),
        pl.BlockSpec((1, D_MODEL), const),
        pl.BlockSpec((D_MODEL, D_MODEL), const),
        pl.BlockSpec((FOURIER_DIM, D_MODEL), const),
        pl.BlockSpec((D_MODEL, D_MODEL), const),
    ]
    args = [attn, four, gates, x2d, mod, g, b, wao, wfo, wout]
    out_specs = [pl.BlockSpec((tm, D_MODEL), row), pl.BlockSpec((tm, D_MODEL), row)]
    out_shape = [jax.ShapeDtypeStruct((t, D_MODEL), F32), jax.ShapeDtypeStruct((t, D_MODEL), BF16)]
    if router is not None:
        in_specs += [pl.BlockSpec((D_MODEL, LANES), const), pl.BlockSpec((1, LANES), const)]
        args += list(router)
        out_specs.append(pl.BlockSpec((tm, LANES), row))
        out_shape.append(jax.ShapeDtypeStruct((t, LANES), F32))
    return pl.pallas_call(
        functools.partial(_mixer_kernel, route=router is not None),
        grid=(t // tm,),
        in_specs=in_specs,
        out_specs=out_specs,
        out_shape=out_shape,
        compiler_params=_cp(("parallel",)),
        name="mixer",
    )(*args)


def _ffn_kernel(h_ref, x1_ref, comb_ref, mod_ref, g_ref, b_ref, wg_ref, wu_ref, wd_ref,
                o_ref, acc_ref):
    e = pl.program_id(1)
    j = pl.program_id(2)

    @pl.when((e == 0) & (j == 0))
    def _():
        acc_ref[...] = jnp.zeros_like(acc_ref)

    h = h_ref[...]
    gate = _dot(h, wg_ref[0])
    up = _dot(h, wu_ref[0])
    act = (gate * _sigmoid(gate) * up).astype(BF16)
    comb = comb_ref[...]
    lane = lax.broadcasted_iota(jnp.int32, comb.shape, 1)
    cw = jnp.sum(jnp.where(lane == e, comb, 0.0), axis=1, keepdims=True)
    acc_ref[...] += cw * _dot(act, wd_ref[0])

    @pl.when((e == pl.num_programs(1) - 1) & (j == pl.num_programs(2) - 1))
    def _():
        mod = mod_ref[0]
        o_ref[...] = _ln(ALPHA * x1_ref[...] + mod[5:6] * acc_ref[...]) * g_ref[...] + b_ref[...]


def _ffn(h2, x1, comb, mod, g, b, wg, wu, wd, tm, fc):
    t = h2.shape[0]
    nb = mod.shape[0]
    tpb = t // nb // tm
    ne, _, dff = wg.shape
    row = lambda i, e, j: (i, 0)
    const = lambda i, e, j: (0, 0)
    return pl.pallas_call(
        _ffn_kernel,
        grid=(t // tm, ne, dff // fc),
        in_specs=[
            pl.BlockSpec((tm, D_MODEL), row),
            pl.BlockSpec((tm, D_MODEL), row),
            pl.BlockSpec((tm, LANES), row),
            pl.BlockSpec((1, 8, D_MODEL), lambda i, e, j: (i // tpb, 0, 0)),
            pl.BlockSpec((1, D_MODEL), const),
            pl.BlockSpec((1, D_MODEL), const),
            pl.BlockSpec((1, D_MODEL, fc), lambda i, e, j: (e, 0, j)),
            pl.BlockSpec((1, D_MODEL, fc), lambda i, e, j: (e, 0, j)),
            pl.BlockSpec((1, fc, D_MODEL), lambda i, e, j: (e, j, 0)),
        ],
        out_specs=pl.BlockSpec((tm, D_MODEL), row),
        out_shape=jax.ShapeDtypeStruct((t, D_MODEL), F32),
        scratch_shapes=[pltpu.VMEM((tm, D_MODEL), F32)],
        compiler_params=_cp(("parallel", "arbitrary", "arbitrary")),
        name="ffn",
    )(h2, x1, comb, mod, g, b, wg, wu, wd)


def _rope_np(n_tokens):
    rows = np.repeat(np.arange(n_tokens // GRID_W), GRID_W)
    cols = np.tile(np.arange(GRID_W), n_tokens // GRID_W)
    pos = np.stack([rows, cols], axis=-1).astype(np.float64)
    inv_freq = 1.0 / (ROPE_THETA ** (np.arange(ROPE_FREQS, dtype=np.float64) / ROPE_FREQS))
    ang = pos[..., None] * inv_freq
    cos, sin = np.cos(ang), np.sin(ang)
    cos32 = np.broadcast_to(cos[:, :, None, :], (n_tokens, 2, 2, ROPE_FREQS)).reshape(n_tokens, QK_ROPE)
    sgn = np.array([-1.0, 1.0])[None, None, :, None]
    sin32 = (np.broadcast_to(sin[:, :, None, :], (n_tokens, 2, 2, ROPE_FREQS)) * sgn).reshape(n_tokens, QK_ROPE)
    return cos32, sin32


def _half_swap_perm():
    p = np.arange(QK_ROPE).reshape(2, 2, ROPE_FREQS)
    return p[:, ::-1, :].reshape(QK_ROPE)


def _tables():
    c0 = SM_SCALE * math.log2(math.e)
    cos32, sin32 = _rope_np(SEQ)
    one64 = np.ones((SEQ, QK_NOPE))
    tq_lat = c0 * np.concatenate([one64, cos32, sin32], axis=1)
    tk_lat = np.concatenate([cos32, sin32, np.zeros((SEQ, LANES - 2 * QK_ROPE))], axis=1)
    tq_ctx = c0 * np.concatenate([np.ones((CTX_LEN, QK_NOPE + QK_ROPE)), np.zeros((CTX_LEN, QK_ROPE))], axis=1)
    tk_ctx = np.concatenate([np.ones((CTX_LEN, QK_ROPE)), np.zeros((CTX_LEN, LANES - QK_ROPE))], axis=1)
    f32 = lambda a: jnp.asarray(a, F32)
    bf = lambda a: jnp.asarray(a, BF16)

    def dft(n):
        k = np.arange(n)
        ang = 2.0 * np.pi * ((k[:, None] * k[None, :]) % n) / n
        return np.cos(ang), np.sin(ang)

    c128, s128 = dft(FFT_R)
    c256, s256 = dft(CTX_LEN)
    bd_lat = np.concatenate([c128, -s128], axis=1) / math.sqrt(SEQ * F_GROUP_DIM)
    bd_ctx = np.concatenate([c128, -s128], axis=1) / math.sqrt(CTX_LEN * F_GROUP_DIM)
    k1 = np.arange(FFT_R)
    tw = 2.0 * np.pi * ((k1[:, None] * k1[None, :]) % SEQ) / SEQ
    tc = jnp.broadcast_to(f32(np.cos(tw))[:, :, None], (FFT_R, FFT_R, LANES)).reshape(FFT_R, FFT_R * LANES)
    ts = jnp.broadcast_to(f32(np.sin(tw))[:, :, None], (FFT_R, FFT_R, LANES)).reshape(FFT_R, FFT_R * LANES)
    return dict(tq_lat=f32(tq_lat), tk_lat=f32(tk_lat), tq_ctx=f32(tq_ctx), tk_ctx=f32(tk_ctx),
                c128=bf(c128), s128=bf(s128), c256=bf(c256), s256=bf(s256),
                bd_lat=bf(bd_lat), bd_ctx=bf(bd_ctx), tc=tc, ts=ts)


def _arrange_weights(l, w_in, w_uq, w_uk, w_uv):
    sw = _half_swap_perm()
    wi = w_in[l]
    kr = wi[:, OFF_KR:OFF_F]
    w_ip = jnp.concatenate([
        wi[:, :OFF_KR], kr, kr[:, sw], jnp.zeros((D_MODEL, LANES - 2 * QK_ROPE), F32),
        wi[:, OFF_F:]], axis=1).astype(BF16)
    uq = w_uq[l].reshape(Q_LORA, N_HEADS, QK_NOPE + QK_ROPE)
    qr = uq[:, :, QK_NOPE:]
    wq = jnp.concatenate([uq, qr[:, :, sw]], axis=-1).reshape(Q_LORA, N_HEADS * HEAD_PAD).astype(BF16)
    uk = w_uk[l].reshape(KV_LORA, N_HEADS, QK_NOPE)
    top = jnp.concatenate([uk, jnp.zeros((KV_LORA, N_HEADS, HEAD_PAD - QK_NOPE), F32)], axis=-1)
    rmat = np.zeros((LANES, HEAD_PAD), np.float32)
    for i in range(QK_ROPE):
        for r in (i, QK_ROPE + i):
            rmat[r, QK_NOPE + i] = 1.0
            rmat[r, QK_NOPE + QK_ROPE + i] = 1.0
    bot = jnp.broadcast_to(jnp.asarray(rmat)[:, None, :], (LANES, N_HEADS, HEAD_PAD))
    wkt = jnp.transpose(jnp.concatenate([top, bot], axis=0), (1, 2, 0)).astype(BF16)
    uv = w_uv[l].reshape(KV_LORA, N_HEADS // 2, 2, V_HEAD)
    z = jnp.zeros_like(uv[:, :, 0])
    even = jnp.concatenate([uv[:, :, 0], z], axis=-1)
    odd = jnp.concatenate([z, uv[:, :, 1]], axis=-1)
    wv = jnp.stack([even, odd], axis=2).reshape(KV_LORA, N_HEADS * HEAD_PAD).astype(BF16)
    return w_ip, wq, wkt, wv


def kernel(x, c, ctx, c_ctx, w_mod, b_mod, w_in, g_q, w_uq, g_kv, w_uk, w_uv, w_ao, w_fo, w_out,
           ln1_g, ln1_b, ln2_g, ln2_b, w_ff_gate, w_ff_up, w_ff_down,
           w_router, b_router, w_e_gate, w_e_up, w_e_down):
    tb = _tables()
    t_lat = BATCH * SEQ
    t_ctx = BATCH * CTX_LEN

    cs = jnp.concatenate([c, c_ctx[None, :], jnp.zeros((8 - BATCH - 1, D_MODEL), F32)], axis=0)
    mods = _modulation(cs, w_mod.astype(BF16), b_mod).reshape(DEPTH, 8, 6, D_MODEL)
    pad2 = jnp.zeros((DEPTH, 8, 2, D_MODEL), F32)
    mods = jnp.concatenate([mods, pad2], axis=2)

    xl = x.reshape(t_lat, D_MODEL)
    xc = ctx.reshape(t_ctx, D_MODEL)
    ones_comb = jnp.ones((t_lat, LANES), F32)
    ones_comb_c = jnp.ones((t_ctx, LANES), F32)

    for l in range(DEPTH):
        last = l == DEPTH - 1
        mod_x = mods[l, :BATCH]
        mod_c = jnp.broadcast_to(mods[l, BATCH:BATCH + 1], (BATCH, 8, D_MODEL))
        w_ip, wq, wkt, wv = _arrange_weights(l, w_in, w_uq, w_uk, w_uv)
        gq = g_q[l][None, :]
        gkv = g_kv[l][None, :]
        wao = w_ao[l].astype(BF16)
        wfo = w_fo[l].astype(BF16)
        wout = w_out[l].astype(BF16)
        g1, b1 = ln1_g[l][None, :], ln1_b[l][None, :]
        g2, b2 = ln2_g[l][None, :], ln2_b[l][None, :]

        qlat_c, ckr_c, wf_c, gates_c = _inproj(xc, mod_c, w_ip, gq, gkv, tb["tk_ctx"], tb["bd_ctx"], tm=CTX_LEN)
        q_c, kt_c, v_c = _qkv(qlat_c, ckr_c, tb["tq_ctx"], wq, wkt, wv, nb=BATCH, tm=CTX_LEN)

        qlat, ckr, wf, gates = _inproj(xl, mod_x, w_ip, gq, gkv, tb["tk_lat"], tb["bd_lat"], tm=512)
        q, kt, v = _qkv(qlat, ckr, tb["tq_lat"], wq, wkt, wv, nb=BATCH, tm=512)
        attn = _attention(q, kt_c, v_c, kt, v, tq=512, tk=512)
        four = _fourier_latent(wf.reshape(BATCH, SEQ, 2 * FOURIER_DIM), tb["c128"], tb["s128"], tb["tc"], tb["ts"])

        if l % 2 == 0:
            i = l // 2
            x1, h2 = _mixer(attn.reshape(t_lat, -1), four.reshape(t_lat, -1), gates, xl, mod_x, g1, b1,
                            wao, wfo, wout, tm=512)
            wg = w_ff_gate[i][None].astype(BF16)
            wu = w_ff_up[i][None].astype(BF16)
            wd = w_ff_down[i][None].astype(BF16)
            xl_new = _ffn(h2, x1, ones_comb, mod_x, g2, b2, wg, wu, wd, tm=512, fc=D_FF // 2)
        else:
            i = l // 2
            wr = jnp.concatenate([w_router[i], jnp.zeros((D_MODEL, LANES - N_EXPERTS), F32)], axis=1)
            br = jnp.concatenate([b_router[i], jnp.zeros((LANES - N_EXPERTS,), F32)])[None, :]
            x1, h2, comb = _mixer(attn.reshape(t_lat, -1), four.reshape(t_lat, -1), gates, xl, mod_x, g1, b1,
                                  wao, wfo, wout, tm=512, router=(wr, br))
            wg = w_e_gate[i].astype(BF16)
            wu = w_e_up[i].astype(BF16)
            wd = w_e_down[i].astype(BF16)
            xl_new = _ffn(h2, x1, comb, mod_x, g2, b2, wg, wu, wd, tm=512, fc=D_FF_EXPERT // 2)

        if not last:
            attn_c = _attention(q_c, kt_c, v_c, None, None, tq=CTX_LEN, tk=CTX_LEN)
            four_c = _fourier_ctx(wf_c.reshape(BATCH, CTX_LEN, 2 * FOURIER_DIM), tb["c256"], tb["s256"])
            if l % 2 == 0:
                x1c, h2c = _mixer(attn_c.reshape(t_ctx, -1), four_c.reshape(t_ctx, -1), gates_c, xc, mod_c,
                                  g1, b1, wao, wfo, wout, tm=CTX_LEN)
                xc = _ffn(h2c, x1c, ones_comb_c, mod_c, g2, b2, wg, wu, wd, tm=CTX_LEN, fc=D_FF // 2)
            else:
                x1c, h2c, comb_c = _mixer(attn_c.reshape(t_ctx, -1), four_c.reshape(t_ctx, -1), gates_c, xc,
                                          mod_c, g1, b1, wao, wfo, wout, tm=CTX_LEN, router=(wr, br))
                xc = _ffn(h2c, x1c, comb_c, mod_c, g2, b2, wg, wu, wd, tm=CTX_LEN, fc=D_FF_EXPERT // 2)
        xl = xl_new

    return xl.reshape(BATCH, SEQ, D_MODEL)
```

```python
import functools
import math

import numpy as np
import jax
import jax.numpy as jnp
from jax import lax
from jax.experimental import pallas as pl
from jax.experimental.pallas import tpu as pltpu

D_MODEL = 1024
BATCH = 2
SEQ = 16384
DEPTH = 2
GRID_W = 64
CTX_LEN = 256
N_HEADS = 16
QK_NOPE = 64
QK_ROPE = 32
ROPE_FREQS = QK_ROPE // 4
V_HEAD = 64
Q_LORA = 256
KV_LORA = 128
ROPE_THETA = 10000.0
SM_SCALE = (QK_NOPE + QK_ROPE) ** -0.5
F_GROUPS = 4
F_GROUP_DIM = 128
FOURIER_DIM = F_GROUPS * F_GROUP_DIM
OFF_KV = Q_LORA
OFF_KR = OFF_KV + KV_LORA
OFF_F = OFF_KR + QK_ROPE
OFF_G = OFF_F + FOURIER_DIM
D_FF = 2816
N_EXPERTS = 8
D_FF_EXPERT = 3584
ALPHA = (2 * DEPTH) ** 0.25
LN_EPS = 1e-6
RMS_EPS = 1e-6

LANES = 128
HEAD_PAD = 128
FFT_R = 128
V_ONES_LANE = (V_HEAD, 0)
VMEM_LIMIT = 56 * 1024 * 1024
ATTN_TQ = 1024
ATTN_TK = 512
ATTN_UNROLL = 8

BF16 = jnp.bfloat16
F32 = jnp.float32


def _cp(sem, vmem=VMEM_LIMIT):
    return pltpu.CompilerParams(dimension_semantics=sem, vmem_limit_bytes=vmem)


def _dot(a, b):
    return jnp.dot(a, b, preferred_element_type=F32)


def _ln(x):
    mu = jnp.mean(x, axis=-1, keepdims=True)
    xc = x - mu
    var = jnp.mean(xc * xc, axis=-1, keepdims=True)
    return xc * lax.rsqrt(var + LN_EPS)


def _rms(x, g):
    return x * lax.rsqrt(jnp.mean(x * x, axis=-1, keepdims=True) + RMS_EPS) * g


def _sigmoid(x):
    return 1.0 / (1.0 + jnp.exp(-x))


def _mod_kernel(cs_ref, w_ref, b_ref, o_ref):
    cs = cs_ref[...]
    a = (cs * _sigmoid(cs)).astype(BF16)
    o_ref[0] = _dot(a, w_ref[0]) + b_ref[0]


def _modulation(cs, w_mod, b_mod):
    n_chunk = 1024
    n_out = w_mod.shape[-1]
    return pl.pallas_call(
        _mod_kernel,
        grid=(DEPTH, n_out // n_chunk),
        in_specs=[
            pl.BlockSpec((8, D_MODEL), lambda l, j: (0, 0)),
            pl.BlockSpec((1, D_MODEL, n_chunk), lambda l, j: (l, 0, j)),
            pl.BlockSpec((1, 1, n_chunk), lambda l, j: (l, 0, j)),
        ],
        out_specs=pl.BlockSpec((1, 8, n_chunk), lambda l, j: (l, 0, j)),
        out_shape=jax.ShapeDtypeStruct((DEPTH, 8, n_out), F32),
        compiler_params=_cp(("parallel", "parallel")),
        name="mod",
    )(cs, w_mod, b_mod.reshape(DEPTH, 1, n_out))


IP_Q = 0
IP_KV = Q_LORA
IP_KR = IP_KV + KV_LORA
IP_F = IP_KR + LANES
IP_G = IP_F + FOURIER_DIM
IP_COLS = IP_G + 2 * D_MODEL


def _inproj_kernel(x_ref, mod_ref, w_ref, gq_ref, gkv_ref, tk_ref, bd_ref,
                   qlat_ref, ckr_ref, wf_ref, gates_ref):
    x = x_ref[...]
    mod = mod_ref[0]
    h = (_ln(x) * (1.0 + mod[1:2]) + mod[0:1]).astype(BF16)
    p0 = _dot(h, w_ref[:, IP_Q:IP_F])
    qlat_ref[...] = _rms(p0[:, IP_Q:IP_KV], gq_ref[...]).astype(BF16)
    ckr_ref[:, 0:KV_LORA] = _rms(p0[:, IP_KV:IP_KR], gkv_ref[...]).astype(BF16)
    ckr_ref[:, KV_LORA:] = (p0[:, IP_KR:IP_F] * tk_ref[...]).astype(BF16)
    uf = _dot(h, w_ref[:, IP_F:IP_G]).astype(BF16)
    for g in range(F_GROUPS):
        r = _dot(uf[:, g * LANES:(g + 1) * LANES], bd_ref[...])
        wf_ref[:, g * LANES:(g + 1) * LANES] = r[:, :LANES].astype(BF16)
        wf_ref[:, FOURIER_DIM + g * LANES:FOURIER_DIM + (g + 1) * LANES] = r[:, LANES:].astype(BF16)
    gc = 512
    for c in range(2 * D_MODEL // gc):
        gl = _dot(h, w_ref[:, IP_G + c * gc:IP_G + (c + 1) * gc])
        gates_ref[:, c * gc:(c + 1) * gc] = _sigmoid(gl).astype(BF16)


def _inproj(x2d, mod, w, gq, gkv, tk, bd, tm):
    t = x2d.shape[0]
    nb = mod.shape[0]
    tpb = t // nb // tm
    return pl.pallas_call(
        _inproj_kernel,
        grid=(t // tm,),
        in_specs=[
            pl.BlockSpec((tm, D_MODEL), lambda i: (i, 0)),
            pl.BlockSpec((1, 8, D_MODEL), lambda i: (i // tpb, 0, 0)),
            pl.BlockSpec((D_MODEL, IP_COLS), lambda i: (0, 0)),
            pl.BlockSpec((1, Q_LORA), lambda i: (0, 0)),
            pl.BlockSpec((1, KV_LORA), lambda i: (0, 0)),
            pl.BlockSpec((tm, LANES), lambda i: (i % tpb, 0)),
            pl.BlockSpec((LANES, 2 * LANES), lambda i: (0, 0)),
        ],
        out_specs=[
            pl.BlockSpec((tm, Q_LORA), lambda i: (i, 0)),
            pl.BlockSpec((tm, 2 * LANES), lambda i: (i, 0)),
            pl.BlockSpec((tm, 2 * FOURIER_DIM), lambda i: (i, 0)),
            pl.BlockSpec((tm, 2 * D_MODEL), lambda i: (i, 0)),
        ],
        out_shape=[
            jax.ShapeDtypeStruct((t, Q_LORA), BF16),
            jax.ShapeDtypeStruct((t, 2 * LANES), BF16),
            jax.ShapeDtypeStruct((t, 2 * FOURIER_DIM), BF16),
            jax.ShapeDtypeStruct((t, 2 * D_MODEL), BF16),
        ],
        compiler_params=_cp(("parallel",)),
        name="inproj",
    )(x2d, mod, w, gq, gkv, tk, bd)


def _qkv_kernel(qlat_ref, ckr_ref, tq_ref, wq_ref, wkt_ref, wv_ref, q_ref, kt_ref, v_ref):
    ql = qlat_ref[...]
    ckr = ckr_ref[...]
    tq = tq_ref[...]
    ckv = ckr[:, :KV_LORA]
    lane = lax.broadcasted_iota(jnp.int32, (ckr.shape[0], HEAD_PAD), 1)
    for h in range(N_HEADS):
        cols = slice(h * HEAD_PAD, (h + 1) * HEAD_PAD)
        q_ref[0, h] = (_dot(ql, wq_ref[:, cols]) * tq).astype(BF16)
        kt_ref[0, h] = lax.dot_general(
            wkt_ref[h], ckr, (((1,), (1,)), ((), ())), preferred_element_type=F32).astype(BF16)
        vh = _dot(ckv, wv_ref[:, cols])
        v_ref[0, h] = jnp.where(lane == V_ONES_LANE[h % 2], 1.0, vh).astype(BF16)


def _qkv(qlat, ckr, tq, wq, wkt, wv, nb, tm):
    t = qlat.shape[0]
    n = t // nb
    tpb = n // tm
    return pl.pallas_call(
        _qkv_kernel,
        grid=(t // tm,),
        in_specs=[
            pl.BlockSpec((tm, Q_LORA), lambda i: (i, 0)),
            pl.BlockSpec((tm, 2 * LANES), lambda i: (i, 0)),
            pl.BlockSpec((tm, HEAD_PAD), lambda i: (i % tpb, 0)),
            pl.BlockSpec((Q_LORA, N_HEADS * HEAD_PAD), lambda i: (0, 0)),
            pl.BlockSpec((N_HEADS, HEAD_PAD, 2 * LANES), lambda i: (0, 0, 0)),
            pl.BlockSpec((KV_LORA, N_HEADS * HEAD_PAD), lambda i: (0, 0)),
        ],
        out_specs=[
            pl.BlockSpec((1, N_HEADS, tm, HEAD_PAD), lambda i: (i // tpb, 0, i % tpb, 0)),
            pl.BlockSpec((1, N_HEADS, HEAD_PAD, tm), lambda i: (i // tpb, 0, 0, i % tpb)),
            pl.BlockSpec((1, N_HEADS, tm, HEAD_PAD), lambda i: (i // tpb, 0, i % tpb, 0)),
        ],
        out_shape=[
            jax.ShapeDtypeStruct((nb, N_HEADS, n, HEAD_PAD), BF16),
            jax.ShapeDtypeStruct((nb, N_HEADS, HEAD_PAD, n), BF16),
            jax.ShapeDtypeStruct((nb, N_HEADS, n, HEAD_PAD), BF16),
        ],
        compiler_params=_cp(("parallel",)),
        name="qkv",
    )(qlat, ckr, tq, wq, wkt, wv)


def _attn_kernel(q_ref, ktc_ref, vc_ref, *rest, n_chunks, tk, unroll):
    if n_chunks:
        kt_ref, v_ref, o_ref = rest
    else:
        (o_ref,) = rest

    def step(q, kt, v, m, acc):
        s = _dot(q, kt)
        m_new = jnp.max(s, axis=1, keepdims=True)
        if m is not None:
            m_new = jnp.maximum(m, m_new)
        p = jnp.exp2((s - m_new).astype(BF16))
        pv = _dot(p, v)
        if m is not None:
            pv = jnp.exp2(m - m_new) * acc + pv
        return m_new, pv

    qs = [q_ref[0, hh] for hh in range(2)]
    carry = []
    for hh in range(2):
        carry += step(qs[hh], ktc_ref[0, hh], vc_ref[0, hh], None, None)

    if n_chunks:
        def body(j, carry):
            off = pl.multiple_of(j * tk, tk)
            out = []
            for hh in range(2):
                out += step(qs[hh], kt_ref[0, hh, :, pl.ds(off, tk)], v_ref[0, hh, pl.ds(off, tk), :],
                            carry[2 * hh], carry[2 * hh + 1])
            return tuple(out)

        carry = lax.fori_loop(0, n_chunks, body, tuple(carry), unroll=unroll)

    lane = lax.broadcasted_iota(jnp.int32, carry[1].shape, 1)
    outs = []
    for hh in range(2):
        acc = carry[2 * hh + 1]
        l = jnp.sum(jnp.where(lane == V_ONES_LANE[hh], acc, 0.0), axis=1, keepdims=True)
        outs.append(acc * (1.0 / l))
    o_ref[0] = jnp.where(lane < V_HEAD, outs[0], outs[1]).astype(BF16)


def _attention(q, kt_c, v_c, kt, v, tq, tk, unroll=1):
    nb, _, n, _ = q.shape
    nc = kt_c.shape[-1]
    n_chunks = 0 if kt is None else kt.shape[-1] // tk
    in_specs = [
        pl.BlockSpec((1, 2, tq, HEAD_PAD), lambda b, p, i: (b, p, i, 0)),
        pl.BlockSpec((1, 2, HEAD_PAD, nc), lambda b, p, i: (b, p, 0, 0)),
        pl.BlockSpec((1, 2, nc, HEAD_PAD), lambda b, p, i: (b, p, 0, 0)),
    ]
    args = [q, kt_c, v_c]
    if n_chunks:
        nk = kt.shape[-1]
        in_specs += [
            pl.BlockSpec((1, 2, HEAD_PAD, nk), lambda b, p, i: (b, p, 0, 0)),
            pl.BlockSpec((1, 2, nk, HEAD_PAD), lambda b, p, i: (b, p, 0, 0)),
        ]
        args += [kt, v]
    return pl.pallas_call(
        functools.partial(_attn_kernel, n_chunks=n_chunks, tk=tk, unroll=unroll),
        grid=(nb, N_HEADS // 2, n // tq),
        in_specs=in_specs,
        out_specs=pl.BlockSpec((1, tq, 2 * V_HEAD), lambda b, p, i: (b, i, p)),
        out_shape=jax.ShapeDtypeStruct((nb, n, N_HEADS * V_HEAD), BF16),
        compiler_params=_cp(("parallel", "parallel", "arbitrary")),
        name="attn",
    )(*args)


def _fft1_kernel(x_ref, c_ref, s_ref, tc_ref, ts_ref, o_ref, *, n2t):
    x = x_ref[0]
    cx = _dot(c_ref[...], x)
    sx = _dot(s_ref[...], x)
    w = 2 * FOURIER_DIM
    for t in range(n2t):
        re = slice(t * w, t * w + FOURIER_DIM)
        im = slice(t * w + FOURIER_DIM, (t + 1) * w)
        yr = cx[:, re] + sx[:, im]
        yi = cx[:, im] - sx[:, re]
        tc = jnp.concatenate([tc_ref[:, t * LANES:(t + 1) * LANES]] * F_GROUPS, axis=1)
        ts = jnp.concatenate([ts_ref[:, t * LANES:(t + 1) * LANES]] * F_GROUPS, axis=1)
        o_ref[0, :, re] = (yr * tc + yi * ts).astype(BF16)
        o_ref[0, :, im] = (yi * tc - yr * ts).astype(BF16)


def _fft2_kernel(y_ref, c_ref, s_ref, o_ref, *, k1t):
    for t in range(k1t):
        y = y_ref[0, t]
        zr = _dot(c_ref[...], y[:, :FOURIER_DIM]) + _dot(s_ref[...], y[:, FOURIER_DIM:])
        o_ref[0, :, t * FOURIER_DIM:(t + 1) * FOURIER_DIM] = zr.astype(BF16)


def _fourier_latent(wf, cmat, smat, tc, ts):
    nb = wf.shape[0]
    w = 2 * FOURIER_DIM
    n2t = 8
    y = pl.pallas_call(
        functools.partial(_fft1_kernel, n2t=n2t),
        grid=(nb, FFT_R // n2t),
        in_specs=[
            pl.BlockSpec((1, FFT_R, n2t * w), lambda b, j: (b, 0, j)),
            pl.BlockSpec((FFT_R, FFT_R), lambda b, j: (0, 0)),
            pl.BlockSpec((FFT_R, FFT_R), lambda b, j: (0, 0)),
            pl.BlockSpec((FFT_R, n2t * LANES), lambda b, j: (0, j)),
            pl.BlockSpec((FFT_R, n2t * LANES), lambda b, j: (0, j)),
        ],
        out_specs=pl.BlockSpec((1, FFT_R, n2t * w), lambda b, j: (b, 0, j)),
        out_shape=jax.ShapeDtypeStruct((nb, FFT_R, FFT_R * w), BF16),
        compiler_params=_cp(("parallel", "parallel")),
        name="fft1",
    )(wf.reshape(nb, FFT_R, FFT_R * w), cmat, smat, tc, ts)
    k1t = 8
    four = pl.pallas_call(
        functools.partial(_fft2_kernel, k1t=k1t),
        grid=(nb, FFT_R // k1t),
        in_specs=[
            pl.BlockSpec((1, k1t, FFT_R, w), lambda b, j: (b, j, 0, 0)),
            pl.BlockSpec((FFT_R, FFT_R), lambda b, j: (0, 0)),
            pl.BlockSpec((FFT_R, FFT_R), lambda b, j: (0, 0)),
        ],
        out_specs=pl.BlockSpec((1, FFT_R, k1t * FOURIER_DIM), lambda b, j: (b, 0, j)),
        out_shape=jax.ShapeDtypeStruct((nb, FFT_R, FFT_R * FOURIER_DIM), BF16),
        compiler_params=_cp(("parallel", "parallel")),
        name="fft2",
    )(y.reshape(nb, FFT_R, FFT_R, w), cmat, smat)
    return four.reshape(nb, SEQ, FOURIER_DIM)


def _dft_ctx_kernel(x_ref, c_ref, s_ref, o_ref):
    x = x_ref[0]
    zr = _dot(c_ref[...], x[:, :FOURIER_DIM]) + _dot(s_ref[...], x[:, FOURIER_DIM:])
    o_ref[0] = zr.astype(BF16)


def _fourier_ctx(wf, cmat, smat):
    nb, n, w = wf.shape
    return pl.pallas_call(
        _dft_ctx_kernel,
        grid=(nb,),
        in_specs=[
            pl.BlockSpec((1, n, w), lambda b: (b, 0, 0)),
            pl.BlockSpec((n, n), lambda b: (0, 0)),
            pl.BlockSpec((n, n), lambda b: (0, 0)),
        ],
        out_specs=pl.BlockSpec((1, n, FOURIER_DIM), lambda b: (b, 0, 0)),
        out_shape=jax.ShapeDtypeStruct((nb, n, FOURIER_DIM), BF16),
        compiler_params=_cp(("parallel",)),
        name="dft_ctx",
    )(wf, cmat, smat)


def _mixer_kernel(attn_ref, four_ref, gates_ref, x_ref, mod_ref, g_ref, b_ref,
                  wao_ref, wfo_ref, wout_ref, *rest, route):
    if route:
        wr_ref, br_ref, x1_ref, h2_ref, comb_ref = rest
    else:
        x1_ref, h2_ref = rest
    mod = mod_ref[0]
    a = _dot(attn_ref[...], wao_ref[...])
    f = _dot(four_ref[...], wfo_ref[...])
    merged = gates_ref[:, :D_MODEL].astype(F32) * a + gates_ref[:, D_MODEL:].astype(F32) * f
    y = _dot(merged.astype(BF16), wout_ref[...])
    x1 = _ln(ALPHA * x_ref[...] + mod[2:3] * y) * g_ref[...] + b_ref[...]
    x1_ref[...] = x1
    h2 = _ln(x1) * (1.0 + mod[4:5]) + mod[3:4]
    h2_ref[...] = h2.astype(BF16)
    if route:
        logits = jnp.dot(h2, wr_ref[...], preferred_element_type=F32,
                         precision=lax.Precision.HIGHEST) + br_ref[...]
        lane = lax.broadcasted_iota(jnp.int32, logits.shape, 1)
        neg = jnp.float32(-jnp.inf)
        lg = jnp.where(lane < N_EXPERTS, logits, neg)
        m1 = jnp.max(lg, axis=1, keepdims=True)
        i1 = jnp.min(jnp.where(lg == m1, lane, LANES), axis=1, keepdims=True)
        lg2 = jnp.where(lane == i1, neg, lg)
        m2 = jnp.max(lg2, axis=1, keepdims=True)
        i2 = jnp.min(jnp.where(lg2 == m2, lane, LANES), axis=1, keepdims=True)
        e2 = jnp.exp(m2 - m1)
        w1 = 1.0 / (1.0 + e2)
        w2 = e2 * w1
        comb_ref[...] = jnp.where(lane == i1, w1, 0.0) + jnp.where(lane == i2, w2, 0.0)


def _mixer(attn, four, gates, x2d, mod, g, b, wao, wfo, wout, tm, router=None):
    t = x2d.shape[0]
    nb = mod.shape[0]
    tpb = t // nb // tm
    row = lambda i: (i, 0)
    const = lambda i: (0, 0)
    in_specs = [
        pl.BlockSpec((tm, D_MODEL), row),
        pl.BlockSpec((tm, FOURIER_DIM), row),
        pl.BlockSpec((tm, 2 * D_MODEL), row),
        pl.BlockSpec((tm, D_MODEL), row),
        pl.BlockSpec((1, 8, D_MODEL), lambda i: (i // tpb, 0, 0)),
        pl.BlockSpec((1, D_MODEL), const),
        pl.BlockSpec((1, D_MODEL), const),
        pl.BlockSpec((D_MODEL, D_MODEL), const),
        pl.BlockSpec((FOURIER_DIM, D_MODEL), const),
        pl.BlockSpec((D_MODEL, D_MODEL), const),
    ]
    args = [attn, four, gates, x2d, mod, g, b, wao, wfo, wout]
    out_specs = [pl.BlockSpec((tm, D_MODEL), row), pl.BlockSpec((tm, D_MODEL), row)]
    out_shape = [jax.ShapeDtypeStruct((t, D_MODEL), F32), jax.ShapeDtypeStruct((t, D_MODEL), BF16)]
    if router is not None:
        in_specs += [pl.BlockSpec((D_MODEL, LANES), const), pl.BlockSpec((1, LANES), const)]
        args += list(router)
        out_specs.append(pl.BlockSpec((tm, LANES), row))
        out_shape.append(jax.ShapeDtypeStruct((t, LANES), F32))
    return pl.pallas_call(
        functools.partial(_mixer_kernel, route=router is not None),
        grid=(t // tm,),
        in_specs=in_specs,
        out_specs=out_specs,
        out_shape=out_shape,
        compiler_params=_cp(("parallel",)),
        name="mixer",
    )(*args)


def _ffn_kernel(h_ref, x1_ref, comb_ref, mod_ref, g_ref, b_ref, wg_ref, wu_ref, wd_ref,
                o_ref, acc_ref):
    e = pl.program_id(1)
    j = pl.program_id(2)

    @pl.when((e == 0) & (j == 0))
    def _():
        acc_ref[...] = jnp.zeros_like(acc_ref)

    h = h_ref[...]
    gate = _dot(h, wg_ref[0])
    up = _dot(h, wu_ref[0])
    act = (gate * _sigmoid(gate) * up).astype(BF16)
    comb = comb_ref[...]
    lane = lax.broadcasted_iota(jnp.int32, comb.shape, 1)
    cw = jnp.sum(jnp.where(lane == e, comb, 0.0), axis=1, keepdims=True)
    acc_ref[...] += cw * _dot(act, wd_ref[0])

    @pl.when((e == pl.num_programs(1) - 1) & (j == pl.num_programs(2) - 1))
    def _():
        mod = mod_ref[0]
        o_ref[...] = _ln(ALPHA * x1_ref[...] + mod[5:6] * acc_ref[...]) * g_ref[...] + b_ref[...]


def _ffn(h2, x1, comb, mod, g, b, wg, wu, wd, tm, fc):
    t = h2.shape[0]
    nb = mod.shape[0]
    tpb = t // nb // tm
    ne, _, dff = wg.shape
    row = lambda i, e, j: (i, 0)
    const = lambda i, e, j: (0, 0)
    return pl.pallas_call(
        _ffn_kernel,
        grid=(t // tm, ne, dff // fc),
        in_specs=[
            pl.BlockSpec((tm, D_MODEL), row),
            pl.BlockSpec((tm, D_MODEL), row),
            pl.BlockSpec((tm, LANES), row),
            pl.BlockSpec((1, 8, D_MODEL), lambda i, e, j: (i // tpb, 0, 0)),
            pl.BlockSpec((1, D_MODEL), const),
            pl.BlockSpec((1, D_MODEL), const),
            pl.BlockSpec((1, D_MODEL, fc), lambda i, e, j: (e, 0, j)),
            pl.BlockSpec((1, D_MODEL, fc), lambda i, e, j: (e, 0, j)),
            pl.BlockSpec((1, fc, D_MODEL), lambda i, e, j: (e, j, 0)),
        ],
        out_specs=pl.BlockSpec((tm, D_MODEL), row),
        out_shape=jax.ShapeDtypeStruct((t, D_MODEL), F32),
        scratch_shapes=[pltpu.VMEM((tm, D_MODEL), F32)],
        compiler_params=_cp(("parallel", "arbitrary", "arbitrary")),
        name="ffn",
    )(h2, x1, comb, mod, g, b, wg, wu, wd)


def _rope_np(n_tokens):
    rows = np.repeat(np.arange(n_tokens // GRID_W), GRID_W)
    cols = np.tile(np.arange(GRID_W), n_tokens // GRID_W)
    pos = np.stack([rows, cols], axis=-1).astype(np.float64)
    inv_freq = 1.0 / (ROPE_THETA ** (np.arange(ROPE_FREQS, dtype=np.float64) / ROPE_FREQS))
    ang = pos[..., None] * inv_freq
    cos, sin = np.cos(ang), np.sin(ang)
    cos32 = np.broadcast_to(cos[:, :, None, :], (n_tokens, 2, 2, ROPE_FREQS)).reshape(n_tokens, QK_ROPE)
    sgn = np.array([-1.0, 1.0])[None, None, :, None]
    sin32 = (np.broadcast_to(sin[:, :, None, :], (n_tokens, 2, 2, ROPE_FREQS)) * sgn).reshape(n_tokens, QK_ROPE)
    return cos32, sin32


def _half_swap_perm():
    p = np.arange(QK_ROPE).reshape(2, 2, ROPE_FREQS)
    return p[:, ::-1, :].reshape(QK_ROPE)


def _tables():
    c0 = SM_SCALE * math.log2(math.e)
    cos32, sin32 = _rope_np(SEQ)
    one64 = np.ones((SEQ, QK_NOPE))
    tq_lat = c0 * np.concatenate([one64, cos32, sin32], axis=1)
    tk_lat = np.concatenate([cos32, sin32, np.zeros((SEQ, LANES - 2 * QK_ROPE))], axis=1)
    tq_ctx = c0 * np.concatenate([np.ones((CTX_LEN, QK_NOPE + QK_ROPE)), np.zeros((CTX_LEN, QK_ROPE))], axis=1)
    tk_ctx = np.concatenate([np.ones((CTX_LEN, QK_ROPE)), np.zeros((CTX_LEN, LANES - QK_ROPE))], axis=1)
    f32 = lambda a: jnp.asarray(a, F32)
    bf = lambda a: jnp.asarray(a, BF16)

    def dft(n):
        k = np.arange(n)
        ang = 2.0 * np.pi * ((k[:, None] * k[None, :]) % n) / n
        return np.cos(ang), np.sin(ang)

    c128, s128 = dft(FFT_R)
    c256, s256 = dft(CTX_LEN)
    bd_lat = np.concatenate([c128, -s128], axis=1) / math.sqrt(SEQ * F_GROUP_DIM)
    bd_ctx = np.concatenate([c128, -s128], axis=1) / math.sqrt(CTX_LEN * F_GROUP_DIM)
    k1 = np.arange(FFT_R)
    tw = 2.0 * np.pi * ((k1[:, None] * k1[None, :]) % SEQ) / SEQ
    tc = jnp.broadcast_to(f32(np.cos(tw))[:, :, None], (FFT_R, FFT_R, LANES)).reshape(FFT_R, FFT_R * LANES)
    ts = jnp.broadcast_to(f32(np.sin(tw))[:, :, None], (FFT_R, FFT_R, LANES)).reshape(FFT_R, FFT_R * LANES)
    return dict(tq_lat=f32(tq_lat), tk_lat=f32(tk_lat), tq_ctx=f32(tq_ctx), tk_ctx=f32(tk_ctx),
                c128=bf(c128), s128=bf(s128), c256=bf(c256), s256=bf(s256),
                bd_lat=bf(bd_lat), bd_ctx=bf(bd_ctx), tc=tc, ts=ts)


def _arrange_weights(l, w_in, w_uq, w_uk, w_uv):
    sw = _half_swap_perm()
    wi = w_in[l]
    kr = wi[:, OFF_KR:OFF_F]
    w_ip = jnp.concatenate([
        wi[:, :OFF_KR], kr, kr[:, sw], jnp.zeros((D_MODEL, LANES - 2 * QK_ROPE), F32),
        wi[:, OFF_F:]], axis=1).astype(BF16)
    uq = w_uq[l].reshape(Q_LORA, N_HEADS, QK_NOPE + QK_ROPE)
    qr = uq[:, :, QK_NOPE:]
    wq = jnp.concatenate([uq, qr[:, :, sw]], axis=-1).reshape(Q_LORA, N_HEADS * HEAD_PAD).astype(BF16)
    uk = w_uk[l].reshape(KV_LORA, N_HEADS, QK_NOPE)
    top = jnp.concatenate([uk, jnp.zeros((KV_LORA, N_HEADS, HEAD_PAD - QK_NOPE), F32)], axis=-1)
    rmat = np.zeros((LANES, HEAD_PAD), np.float32)
    for i in range(QK_ROPE):
        for r in (i, QK_ROPE + i):
            rmat[r, QK_NOPE + i] = 1.0
            rmat[r, QK_NOPE + QK_ROPE + i] = 1.0
    bot = jnp.broadcast_to(jnp.asarray(rmat)[:, None, :], (LANES, N_HEADS, HEAD_PAD))
    wkt = jnp.transpose(jnp.concatenate([top, bot], axis=0), (1, 2, 0)).astype(BF16)
    uv = w_uv[l].reshape(KV_LORA, N_HEADS // 2, 2, V_HEAD)
    z = jnp.zeros_like(uv[:, :, 0])
    even = jnp.concatenate([uv[:, :, 0], z], axis=-1)
    odd = jnp.concatenate([z, uv[:, :, 1]], axis=-1)
    wv = jnp.stack([even, odd], axis=2).reshape(KV_LORA, N_HEADS * HEAD_PAD).astype(BF16)
    return w_ip, wq, wkt, wv


def kernel(x, c, ctx, c_ctx, w_mod, b_mod, w_in, g_q, w_uq, g_kv, w_uk, w_uv, w_ao, w_fo, w_out,
           ln1_g, ln1_b, ln2_g, ln2_b, w_ff_gate, w_ff_up, w_ff_down,
           w_router, b_router, w_e_gate, w_e_up, w_e_down):
    tb = _tables()
    t_lat = BATCH * SEQ
    t_ctx = BATCH * CTX_LEN

    cs = jnp.concatenate([c, c_ctx[None, :], jnp.zeros((8 - BATCH - 1, D_MODEL), F32)], axis=0)
    mods = _modulation(cs, w_mod.astype(BF16), b_mod).reshape(DEPTH, 8, 6, D_MODEL)
    pad2 = jnp.zeros((DEPTH, 8, 2, D_MODEL), F32)
    mods = jnp.concatenate([mods, pad2], axis=2)

    xl = x.reshape(t_lat, D_MODEL)
    xc = ctx.reshape(t_ctx, D_MODEL)
    ones_comb = jnp.ones((t_lat, LANES), F32)
    ones_comb_c = jnp.ones((t_ctx, LANES), F32)

    for l in range(DEPTH):
        last = l == DEPTH - 1
        mod_x = mods[l, :BATCH]
        mod_c = jnp.broadcast_to(mods[l, BATCH:BATCH + 1], (BATCH, 8, D_MODEL))
        w_ip, wq, wkt, wv = _arrange_weights(l, w_in, w_uq, w_uk, w_uv)
        gq = g_q[l][None, :]
        gkv = g_kv[l][None, :]
        wao = w_ao[l].astype(BF16)
        wfo = w_fo[l].astype(BF16)
        wout = w_out[l].astype(BF16)
        g1, b1 = ln1_g[l][None, :], ln1_b[l][None, :]
        g2, b2 = ln2_g[l][None, :], ln2_b[l][None, :]

        qlat_c, ckr_c, wf_c, gates_c = _inproj(xc, mod_c, w_ip, gq, gkv, tb["tk_ctx"], tb["bd_ctx"], tm=CTX_LEN)
        q_c, kt_c, v_c = _qkv(qlat_c, ckr_c, tb["tq_ctx"], wq, wkt, wv, nb=BATCH, tm=CTX_LEN)

        qlat, ckr, wf, gates = _inproj(xl, mod_x, w_ip, gq, gkv, tb["tk_lat"], tb["bd_lat"], tm=512)
        q, kt, v = _qkv(qlat, ckr, tb["tq_lat"], wq, wkt, wv, nb=BATCH, tm=512)
        attn = _attention(q, kt_c, v_c, kt, v, tq=ATTN_TQ, tk=ATTN_TK, unroll=ATTN_UNROLL)
        four = _fourier_latent(wf.reshape(BATCH, SEQ, 2 * FOURIER_DIM), tb["c128"], tb["s128"], tb["tc"], tb["ts"])

        if l % 2 == 0:
            i = l // 2
            x1, h2 = _mixer(attn.reshape(t_lat, -1), four.reshape(t_lat, -1), gates, xl, mod_x, g1, b1,
                            wao, wfo, wout, tm=512)
            wg = w_ff_gate[i][None].astype(BF16)
            wu = w_ff_up[i][None].astype(BF16)
            wd = w_ff_down[i][None].astype(BF16)
            xl_new = _ffn(h2, x1, ones_comb, mod_x, g2, b2, wg, wu, wd, tm=512, fc=D_FF // 2)
        else:
            i = l // 2
            wr = jnp.concatenate([w_router[i], jnp.zeros((D_MODEL, LANES - N_EXPERTS), F32)], axis=1)
            br = jnp.concatenate([b_router[i], jnp.zeros((LANES - N_EXPERTS,), F32)])[None, :]
            x1, h2, comb = _mixer(attn.reshape(t_lat, -1), four.reshape(t_lat, -1), gates, xl, mod_x, g1, b1,
                                  wao, wfo, wout, tm=512, router=(wr, br))
            wg = w_e_gate[i].astype(BF16)
            wu = w_e_up[i].astype(BF16)
            wd = w_e_down[i].astype(BF16)
            xl_new = _ffn(h2, x1, comb, mod_x, g2, b2, wg, wu, wd, tm=512, fc=D_FF_EXPERT // 2)

        if not last:
            attn_c = _attention(q_c, kt_c, v_c, None, None, tq=CTX_LEN, tk=CTX_LEN)
            four_c = _fourier_ctx(wf_c.reshape(BATCH, CTX_LEN, 2 * FOURIER_DIM), tb["c256"], tb["s256"])
            if l % 2 == 0:
                x1c, h2c = _mixer(attn_c.reshape(t_ctx, -1), four_c.reshape(t_ctx, -1), gates_c, xc, mod_c,
                                  g1, b1, wao, wfo, wout, tm=CTX_LEN)
                xc = _ffn(h2c, x1c, ones_comb_c, mod_c, g2, b2, wg, wu, wd, tm=CTX_LEN, fc=D_FF // 2)
            else:
                x1c, h2c, comb_c = _mixer(attn_c.reshape(t_ctx, -1), four_c.reshape(t_ctx, -1), gates_c, xc,
                                          mod_c, g1, b1, wao, wfo, wout, tm=CTX_LEN, router=(wr, br))
                xc = _ffn(h2c, x1c, comb_c, mod_c, g2, b2, wg, wu, wd, tm=CTX_LEN, fc=D_FF_EXPERT // 2)
        xl = xl_new

    return xl.reshape(BATCH, SEQ, D_MODEL)
```

```python
import functools
import math

import numpy as np
import jax
import jax.numpy as jnp
from jax import lax
from jax.experimental import pallas as pl
from jax.experimental.pallas import tpu as pltpu

D_MODEL = 1024
BATCH = 2
SEQ = 16384
DEPTH = 2
GRID_W = 64
CTX_LEN = 256
N_HEADS = 16
QK_NOPE = 64
QK_ROPE = 32
ROPE_FREQS = QK_ROPE // 4
V_HEAD = 64
Q_LORA = 256
KV_LORA = 128
ROPE_THETA = 10000.0
SM_SCALE = (QK_NOPE + QK_ROPE) ** -0.5
F_GROUPS = 4
F_GROUP_DIM = 128
FOURIER_DIM = F_GROUPS * F_GROUP_DIM
OFF_KV = Q_LORA
OFF_KR = OFF_KV + KV_LORA
OFF_F = OFF_KR + QK_ROPE
OFF_G = OFF_F + FOURIER_DIM
D_FF = 2816
N_EXPERTS = 8
D_FF_EXPERT = 3584
ALPHA = (2 * DEPTH) ** 0.25
LN_EPS = 1e-6
RMS_EPS = 1e-6

LANES = 128
HEAD_PAD = 128
FFT_R = 128
V_ONES_LANE = (V_HEAD, 0)
VMEM_LIMIT = 56 * 1024 * 1024
ATTN_TQ = 1024
ATTN_TK = 512
ATTN_UNROLL = 8

BF16 = jnp.bfloat16
F32 = jnp.float32


def _cp(sem, vmem=VMEM_LIMIT):
    return pltpu.CompilerParams(dimension_semantics=sem, vmem_limit_bytes=vmem)


def _dot(a, b):
    return jnp.dot(a, b, preferred_element_type=F32)


def _ln(x):
    mu = jnp.mean(x, axis=-1, keepdims=True)
    xc = x - mu
    var = jnp.mean(xc * xc, axis=-1, keepdims=True)
    return xc * lax.rsqrt(var + LN_EPS)


def _rms(x, g):
    return x * lax.rsqrt(jnp.mean(x * x, axis=-1, keepdims=True) + RMS_EPS) * g


def _sigmoid(x):
    return 1.0 / (1.0 + jnp.exp(-x))


def _mod_kernel(cs_ref, w_ref, b_ref, o_ref):
    cs = cs_ref[...]
    a = (cs * _sigmoid(cs)).astype(BF16)
    o_ref[0] = _dot(a, w_ref[0]) + b_ref[0]


def _modulation(cs, w_mod, b_mod):
    n_chunk = 1024
    n_out = w_mod.shape[-1]
    return pl.pallas_call(
        _mod_kernel,
        grid=(DEPTH, n_out // n_chunk),
        in_specs=[
            pl.BlockSpec((8, D_MODEL), lambda l, j: (0, 0)),
            pl.BlockSpec((1, D_MODEL, n_chunk), lambda l, j: (l, 0, j)),
            pl.BlockSpec((1, 1, n_chunk), lambda l, j: (l, 0, j)),
        ],
        out_specs=pl.BlockSpec((1, 8, n_chunk), lambda l, j: (l, 0, j)),
        out_shape=jax.ShapeDtypeStruct((DEPTH, 8, n_out), F32),
        compiler_params=_cp(("parallel", "parallel")),
        name="mod",
    )(cs, w_mod, b_mod.reshape(DEPTH, 1, n_out))


IP_Q = 0
IP_KV = Q_LORA
IP_KR = IP_KV + KV_LORA
IP_F = IP_KR + LANES
IP_G = IP_F + FOURIER_DIM
IP_COLS = IP_G + 2 * D_MODEL


def _inproj_kernel(x_ref, mod_ref, w_ref, gq_ref, gkv_ref, tk_ref, bd_ref,
                   qlat_ref, ckr_ref, wf_ref, gates_ref):
    x = x_ref[...]
    mod = mod_ref[0]
    h = (_ln(x) * (1.0 + mod[1:2]) + mod[0:1]).astype(BF16)
    p0 = _dot(h, w_ref[:, IP_Q:IP_F])
    qlat_ref[...] = _rms(p0[:, IP_Q:IP_KV], gq_ref[...]).astype(BF16)
    ckr_ref[:, 0:KV_LORA] = _rms(p0[:, IP_KV:IP_KR], gkv_ref[...]).astype(BF16)
    ckr_ref[:, KV_LORA:] = (p0[:, IP_KR:IP_F] * tk_ref[...]).astype(BF16)
    uf = _dot(h, w_ref[:, IP_F:IP_G]).astype(BF16)
    for g in range(F_GROUPS):
        r = _dot(uf[:, g * LANES:(g + 1) * LANES], bd_ref[...])
        wf_ref[:, g * LANES:(g + 1) * LANES] = r[:, :LANES].astype(BF16)
        wf_ref[:, FOURIER_DIM + g * LANES:FOURIER_DIM + (g + 1) * LANES] = r[:, LANES:].astype(BF16)
    gc = 512
    for c in range(2 * D_MODEL // gc):
        gl = _dot(h, w_ref[:, IP_G + c * gc:IP_G + (c + 1) * gc])
        gates_ref[:, c * gc:(c + 1) * gc] = _sigmoid(gl).astype(BF16)


def _inproj(x2d, mod, w, gq, gkv, tk, bd, tm):
    t = x2d.shape[0]
    nb = mod.shape[0]
    tpb = t // nb // tm
    return pl.pallas_call(
        _inproj_kernel,
        grid=(t // tm,),
        in_specs=[
            pl.BlockSpec((tm, D_MODEL), lambda i: (i, 0)),
            pl.BlockSpec((1, 8, D_MODEL), lambda i: (i // tpb, 0, 0)),
            pl.BlockSpec((D_MODEL, IP_COLS), lambda i: (0, 0)),
            pl.BlockSpec((1, Q_LORA), lambda i: (0, 0)),
            pl.BlockSpec((1, KV_LORA), lambda i: (0, 0)),
            pl.BlockSpec((tm, LANES), lambda i: (i % tpb, 0)),
            pl.BlockSpec((LANES, 2 * LANES), lambda i: (0, 0)),
        ],
        out_specs=[
            pl.BlockSpec((tm, Q_LORA), lambda i: (i, 0)),
            pl.BlockSpec((tm, 2 * LANES), lambda i: (i, 0)),
            pl.BlockSpec((tm, 2 * FOURIER_DIM), lambda i: (i, 0)),
            pl.BlockSpec((tm, 2 * D_MODEL), lambda i: (i, 0)),
        ],
        out_shape=[
            jax.ShapeDtypeStruct((t, Q_LORA), BF16),
            jax.ShapeDtypeStruct((t, 2 * LANES), BF16),
            jax.ShapeDtypeStruct((t, 2 * FOURIER_DIM), BF16),
            jax.ShapeDtypeStruct((t, 2 * D_MODEL), BF16),
        ],
        compiler_params=_cp(("parallel",)),
        name="inproj",
    )(x2d, mod, w, gq, gkv, tk, bd)


def _qkv_kernel(qlat_ref, ckr_ref, tq_ref, wq_ref, wkt_ref, wv_ref, q_ref, kt_ref, v_ref):
    ql = qlat_ref[...]
    ckr = ckr_ref[...]
    tq = tq_ref[...]
    ckv = ckr[:, :KV_LORA]
    lane = lax.broadcasted_iota(jnp.int32, (ckr.shape[0], HEAD_PAD), 1)
    for h in range(N_HEADS):
        cols = slice(h * HEAD_PAD, (h + 1) * HEAD_PAD)
        q_ref[0, h] = (_dot(ql, wq_ref[:, cols]) * tq).astype(BF16)
        kt_ref[0, h] = lax.dot_general(
            wkt_ref[h], ckr, (((1,), (1,)), ((), ())), preferred_element_type=F32).astype(BF16)
        vh = _dot(ckv, wv_ref[:, cols])
        v_ref[0, h] = jnp.where(lane == V_ONES_LANE[h % 2], 1.0, vh).astype(BF16)


def _qkv(qlat, ckr, tq, wq, wkt, wv, nb, tm):
    t = qlat.shape[0]
    n = t // nb
    tpb = n // tm
    return pl.pallas_call(
        _qkv_kernel,
        grid=(t // tm,),
        in_specs=[
            pl.BlockSpec((tm, Q_LORA), lambda i: (i, 0)),
            pl.BlockSpec((tm, 2 * LANES), lambda i: (i, 0)),
            pl.BlockSpec((tm, HEAD_PAD), lambda i: (i % tpb, 0)),
            pl.BlockSpec((Q_LORA, N_HEADS * HEAD_PAD), lambda i: (0, 0)),
            pl.BlockSpec((N_HEADS, HEAD_PAD, 2 * LANES), lambda i: (0, 0, 0)),
            pl.BlockSpec((KV_LORA, N_HEADS * HEAD_PAD), lambda i: (0, 0)),
        ],
        out_specs=[
            pl.BlockSpec((1, N_HEADS, tm, HEAD_PAD), lambda i: (i // tpb, 0, i % tpb, 0)),
            pl.BlockSpec((1, N_HEADS, HEAD_PAD, tm), lambda i: (i // tpb, 0, 0, i % tpb)),
            pl.BlockSpec((1, N_HEADS, tm, HEAD_PAD), lambda i: (i // tpb, 0, i % tpb, 0)),
        ],
        out_shape=[
            jax.ShapeDtypeStruct((nb, N_HEADS, n, HEAD_PAD), BF16),
            jax.ShapeDtypeStruct((nb, N_HEADS, HEAD_PAD, n), BF16),
            jax.ShapeDtypeStruct((nb, N_HEADS, n, HEAD_PAD), BF16),
        ],
        compiler_params=_cp(("parallel",)),
        name="qkv",
    )(qlat, ckr, tq, wq, wkt, wv)


def _attn_kernel(q_ref, ktc_ref, vc_ref, *rest, n_chunks, tk, unroll):
    if n_chunks:
        kt_ref, v_ref, o_ref = rest
    else:
        (o_ref,) = rest

    def step(q, kt, v, m, acc):
        s = _dot(q, kt)
        m_new = jnp.max(s, axis=1, keepdims=True)
        if m is not None:
            m_new = jnp.maximum(m, m_new)
        p = jnp.exp2((s - m_new).astype(BF16))
        pv = _dot(p, v)
        if m is not None:
            pv = jnp.exp2(m - m_new) * acc + pv
        return m_new, pv

    qs = [q_ref[0, hh] for hh in range(2)]
    carry = []
    for hh in range(2):
        carry += step(qs[hh], ktc_ref[0, hh], vc_ref[0, hh], None, None)

    if n_chunks:
        def body(j, carry):
            off = pl.multiple_of(j * tk, tk)
            out = []
            for hh in range(2):
                out += step(qs[hh], kt_ref[0, hh, :, pl.ds(off, tk)], v_ref[0, hh, pl.ds(off, tk), :],
                            carry[2 * hh], carry[2 * hh + 1])
            return tuple(out)

        carry = lax.fori_loop(0, n_chunks, body, tuple(carry), unroll=unroll)

    lane = lax.broadcasted_iota(jnp.int32, carry[1].shape, 1)
    outs = []
    for hh in range(2):
        acc = carry[2 * hh + 1]
        l = jnp.sum(jnp.where(lane == V_ONES_LANE[hh], acc, 0.0), axis=1, keepdims=True)
        outs.append(acc * (1.0 / l))
    o_ref[0] = jnp.where(lane < V_HEAD, outs[0], outs[1]).astype(BF16)


def _attention(q, kt_c, v_c, kt, v, tq, tk, unroll=1):
    nb, _, n, _ = q.shape
    nc = kt_c.shape[-1]
    n_chunks = 0 if kt is None else kt.shape[-1] // tk
    in_specs = [
        pl.BlockSpec((1, 2, tq, HEAD_PAD), lambda b, p, i: (b, p, i, 0)),
        pl.BlockSpec((1, 2, HEAD_PAD, nc), lambda b, p, i: (b, p, 0, 0)),
        pl.BlockSpec((1, 2, nc, HEAD_PAD), lambda b, p, i: (b, p, 0, 0)),
    ]
    args = [q, kt_c, v_c]
    if n_chunks:
        nk = kt.shape[-1]
        in_specs += [
            pl.BlockSpec((1, 2, HEAD_PAD, nk), lambda b, p, i: (b, p, 0, 0)),
            pl.BlockSpec((1, 2, nk, HEAD_PAD), lambda b, p, i: (b, p, 0, 0)),
        ]
        args += [kt, v]
    return pl.pallas_call(
        functools.partial(_attn_kernel, n_chunks=n_chunks, tk=tk, unroll=unroll),
        grid=(nb, N_HEADS // 2, n // tq),
        in_specs=in_specs,
        out_specs=pl.BlockSpec((1, tq, 2 * V_HEAD), lambda b, p, i: (b, i, p)),
        out_shape=jax.ShapeDtypeStruct((nb, n, N_HEADS * V_HEAD), BF16),
        compiler_params=_cp(("parallel", "parallel", "arbitrary")),
        name="attn",
    )(*args)


def _fft1_kernel(x_ref, c_ref, s_ref, tc_ref, ts_ref, o_ref, *, n2t):
    x = x_ref[0]
    cx = _dot(c_ref[...], x)
    sx = _dot(s_ref[...], x)
    w = 2 * FOURIER_DIM
    for t in range(n2t):
        re = slice(t * w, t * w + FOURIER_DIM)
        im = slice(t * w + FOURIER_DIM, (t + 1) * w)
        yr = cx[:, re] + sx[:, im]
        yi = cx[:, im] - sx[:, re]
        tc = jnp.concatenate([tc_ref[:, t * LANES:(t + 1) * LANES]] * F_GROUPS, axis=1)
        ts = jnp.concatenate([ts_ref[:, t * LANES:(t + 1) * LANES]] * F_GROUPS, axis=1)
        o_ref[0, :, re] = (yr * tc + yi * ts).astype(BF16)
        o_ref[0, :, im] = (yi * tc - yr * ts).astype(BF16)


def _fft2_kernel(y_ref, c_ref, s_ref, o_ref, *, k1t):
    for t in range(k1t):
        y = y_ref[0, t]
        zr = _dot(c_ref[...], y[:, :FOURIER_DIM]) + _dot(s_ref[...], y[:, FOURIER_DIM:])
        o_ref[0, :, t * FOURIER_DIM:(t + 1) * FOURIER_DIM] = zr.astype(BF16)


def _fourier_latent(wf, cmat, smat, tc, ts):
    nb = wf.shape[0]
    w = 2 * FOURIER_DIM
    n2t = 8
    y = pl.pallas_call(
        functools.partial(_fft1_kernel, n2t=n2t),
        grid=(nb, FFT_R // n2t),
        in_specs=[
            pl.BlockSpec((1, FFT_R, n2t * w), lambda b, j: (b, 0, j)),
            pl.BlockSpec((FFT_R, FFT_R), lambda b, j: (0, 0)),
            pl.BlockSpec((FFT_R, FFT_R), lambda b, j: (0, 0)),
            pl.BlockSpec((FFT_R, n2t * LANES), lambda b, j: (0, j)),
            pl.BlockSpec((FFT_R, n2t * LANES), lambda b, j: (0, j)),
        ],
        out_specs=pl.BlockSpec((1, FFT_R, n2t * w), lambda b, j: (b, 0, j)),
        out_shape=jax.ShapeDtypeStruct((nb, FFT_R, FFT_R * w), BF16),
        compiler_params=_cp(("parallel", "parallel")),
        name="fft1",
    )(wf.reshape(nb, FFT_R, FFT_R * w), cmat, smat, tc, ts)
    k1t = 8
    four = pl.pallas_call(
        functools.partial(_fft2_kernel, k1t=k1t),
        grid=(nb, FFT_R // k1t),
        in_specs=[
            pl.BlockSpec((1, k1t, FFT_R, w), lambda b, j: (b, j, 0, 0)),
            pl.BlockSpec((FFT_R, FFT_R), lambda b, j: (0, 0)),
            pl.BlockSpec((FFT_R, FFT_R), lambda b, j: (0, 0)),
        ],
        out_specs=pl.BlockSpec((1, FFT_R, k1t * FOURIER_DIM), lambda b, j: (b, 0, j)),
        out_shape=jax.ShapeDtypeStruct((nb, FFT_R, FFT_R * FOURIER_DIM), BF16),
        compiler_params=_cp(("parallel", "parallel")),
        name="fft2",
    )(y.reshape(nb, FFT_R, FFT_R, w), cmat, smat)
    return four.reshape(nb, SEQ, FOURIER_DIM)


def _dft_ctx_kernel(x_ref, c_ref, s_ref, o_ref):
    x = x_ref[0]
    zr = _dot(c_ref[...], x[:, :FOURIER_DIM]) + _dot(s_ref[...], x[:, FOURIER_DIM:])
    o_ref[0] = zr.astype(BF16)


def _fourier_ctx(wf, cmat, smat):
    nb, n, w = wf.shape
    return pl.pallas_call(
        _dft_ctx_kernel,
        grid=(nb,),
        in_specs=[
            pl.BlockSpec((1, n, w), lambda b: (b, 0, 0)),
            pl.BlockSpec((n, n), lambda b: (0, 0)),
            pl.BlockSpec((n, n), lambda b: (0, 0)),
        ],
        out_specs=pl.BlockSpec((1, n, FOURIER_DIM), lambda b: (b, 0, 0)),
        out_shape=jax.ShapeDtypeStruct((nb, n, FOURIER_DIM), BF16),
        compiler_params=_cp(("parallel",)),
        name="dft_ctx",
    )(wf, cmat, smat)


def _mixer_kernel(attn_ref, four_ref, gates_ref, x_ref, mod_ref, g_ref, b_ref,
                  wao_ref, wfo_ref, wout_ref, *rest, route):
    if route:
        wr_ref, br_ref, x1_ref, h2_ref, rinfo_ref = rest
    else:
        x1_ref, h2_ref = rest
    mod = mod_ref[0]
    a = _dot(attn_ref[...], wao_ref[...])
    f = _dot(four_ref[...], wfo_ref[...])
    merged = gates_ref[:, :D_MODEL].astype(F32) * a + gates_ref[:, D_MODEL:].astype(F32) * f
    y = _dot(merged.astype(BF16), wout_ref[...])
    x1 = _ln(ALPHA * x_ref[...] + mod[2:3] * y) * g_ref[...] + b_ref[...]
    x1_ref[...] = x1
    h2 = _ln(x1) * (1.0 + mod[4:5]) + mod[3:4]
    h2_ref[...] = h2.astype(h2_ref.dtype)
    if route:
        logits = jnp.dot(h2, wr_ref[...], preferred_element_type=F32,
                         precision=lax.Precision.HIGHEST) + br_ref[...]
        lane = lax.broadcasted_iota(jnp.int32, logits.shape, 1)
        neg = jnp.float32(-jnp.inf)
        lg = jnp.where(lane < N_EXPERTS, logits, neg)
        m1 = jnp.max(lg, axis=1, keepdims=True)
        i1 = jnp.min(jnp.where(lg == m1, lane, LANES), axis=1, keepdims=True)
        lg2 = jnp.where(lane == i1, neg, lg)
        m2 = jnp.max(lg2, axis=1, keepdims=True)
        i2 = jnp.min(jnp.where(lg2 == m2, lane, LANES), axis=1, keepdims=True)
        e2 = jnp.exp(m2 - m1)
        w1 = 1.0 / (1.0 + e2)
        w2 = e2 * w1
        rinfo = jnp.where((lane == i1) | (lane == i2), 1.0, 0.0)
        rinfo = jnp.where(lane == RI_E1, i1.astype(F32), rinfo)
        rinfo = jnp.where(lane == RI_E2, i2.astype(F32), rinfo)
        rinfo = jnp.where(lane == RI_W1, w1, rinfo)
        rinfo_ref[...] = jnp.where(lane == RI_W2, w2, rinfo)


def _mixer(attn, four, gates, x2d, mod, g, b, wao, wfo, wout, tm, router=None):
    t = x2d.shape[0]
    nb = mod.shape[0]
    tpb = t // nb // tm
    row = lambda i: (i, 0)
    const = lambda i: (0, 0)
    in_specs = [
        pl.BlockSpec((tm, D_MODEL), row),
        pl.BlockSpec((tm, FOURIER_DIM), row),
        pl.BlockSpec((tm, 2 * D_MODEL), row),
        pl.BlockSpec((tm, D_MODEL), row),
        pl.BlockSpec((1, 8, D_MODEL), lambda i: (i // tpb, 0, 0)),
        pl.BlockSpec((1, D_MODEL), const),
        pl.BlockSpec((1, D_MODEL), const),
        pl.BlockSpec((D_MODEL, D_MODEL), const),
        pl.BlockSpec((FOURIER_DIM, D_MODEL), const),
        pl.BlockSpec((D_MODEL, D_MODEL), const),
    ]
    args = [attn, four, gates, x2d, mod, g, b, wao, wfo, wout]
    out_specs = [pl.BlockSpec((tm, D_MODEL), row), pl.BlockSpec((tm, D_MODEL), row)]
    h2_dtype = BF16 if router is None else F32
    out_shape = [jax.ShapeDtypeStruct((t, D_MODEL), F32), jax.ShapeDtypeStruct((t, D_MODEL), h2_dtype)]
    if router is not None:
        in_specs += [pl.BlockSpec((D_MODEL, LANES), const), pl.BlockSpec((1, LANES), const)]
        args += list(router)
        out_specs.append(pl.BlockSpec((tm, LANES), row))
        out_shape.append(jax.ShapeDtypeStruct((t, LANES), F32))
    return pl.pallas_call(
        functools.partial(_mixer_kernel, route=router is not None),
        grid=(t // tm,),
        in_specs=in_specs,
        out_specs=out_specs,
        out_shape=out_shape,
        compiler_params=_cp(("parallel",)),
        name="mixer",
    )(*args)


def _ffn_kernel(h_ref, x1_ref, comb_ref, mod_ref, g_ref, b_ref, wg_ref, wu_ref, wd_ref,
                o_ref, acc_ref):
    e = pl.program_id(1)
    j = pl.program_id(2)

    @pl.when((e == 0) & (j == 0))
    def _():
        acc_ref[...] = jnp.zeros_like(acc_ref)

    h = h_ref[...]
    gate = _dot(h, wg_ref[0])
    up = _dot(h, wu_ref[0])
    act = (gate * _sigmoid(gate) * up).astype(BF16)
    comb = comb_ref[...]
    lane = lax.broadcasted_iota(jnp.int32, comb.shape, 1)
    cw = jnp.sum(jnp.where(lane == e, comb, 0.0), axis=1, keepdims=True)
    acc_ref[...] += cw * _dot(act, wd_ref[0])

    @pl.when((e == pl.num_programs(1) - 1) & (j == pl.num_programs(2) - 1))
    def _():
        mod = mod_ref[0]
        o_ref[...] = _ln(ALPHA * x1_ref[...] + mod[5:6] * acc_ref[...]) * g_ref[...] + b_ref[...]


def _ffn(h2, x1, comb, mod, g, b, wg, wu, wd, tm, fc):
    t = h2.shape[0]
    nb = mod.shape[0]
    tpb = t // nb // tm
    ne, _, dff = wg.shape
    row = lambda i, e, j: (i, 0)
    const = lambda i, e, j: (0, 0)
    return pl.pallas_call(
        _ffn_kernel,
        grid=(t // tm, ne, dff // fc),
        in_specs=[
            pl.BlockSpec((tm, D_MODEL), row),
            pl.BlockSpec((tm, D_MODEL), row),
            pl.BlockSpec((tm, LANES), row),
            pl.BlockSpec((1, 8, D_MODEL), lambda i, e, j: (i // tpb, 0, 0)),
            pl.BlockSpec((1, D_MODEL), const),
            pl.BlockSpec((1, D_MODEL), const),
            pl.BlockSpec((1, D_MODEL, fc), lambda i, e, j: (e, 0, j)),
            pl.BlockSpec((1, D_MODEL, fc), lambda i, e, j: (e, 0, j)),
            pl.BlockSpec((1, fc, D_MODEL), lambda i, e, j: (e, j, 0)),
        ],
        out_specs=pl.BlockSpec((tm, D_MODEL), row),
        out_shape=jax.ShapeDtypeStruct((t, D_MODEL), F32),
        scratch_shapes=[pltpu.VMEM((tm, D_MODEL), F32)],
        compiler_params=_cp(("parallel", "arbitrary", "arbitrary")),
        name="ffn",
    )(h2, x1, comb, mod, g, b, wg, wu, wd)


RI_E1, RI_E2, RI_W1, RI_W2 = N_EXPERTS, N_EXPERTS + 1, N_EXPERTS + 2, N_EXPERTS + 3
MOE_TM = 512
MOE_TT = 512


def _moe_rows(n_tokens):
    return 2 * n_tokens + N_EXPERTS * MOE_TM


def _scan_kernel(r_ref, o_ref, cnt_ref, run_ref):
    @pl.when(pl.program_id(0) == 0)
    def _():
        run_ref[...] = jnp.zeros_like(run_ref)

    r = r_ref[...]
    ts = r.shape[0]
    lane = lax.broadcasted_iota(jnp.int32, r.shape, 1)
    sel = jnp.where(lane < N_EXPERTS, r, 0.0)
    rows = lax.broadcasted_iota(jnp.int32, (ts, ts), 0)
    cols = lax.broadcasted_iota(jnp.int32, (ts, ts), 1)
    ltri = jnp.where(rows > cols, 1.0, 0.0).astype(BF16)
    prefix = _dot(ltri, sel.astype(BF16)) + run_ref[0:1, :]
    lane_f = lane.astype(F32)
    pick = lambda k: jnp.sum(jnp.where(lane == k, r, 0.0), axis=1, keepdims=True)
    e1, e2 = pick(RI_E1), pick(RI_E2)
    p1 = jnp.sum(jnp.where(lane_f == e1, prefix, 0.0), axis=1, keepdims=True)
    p2 = jnp.sum(jnp.where(lane_f == e2, prefix, 0.0), axis=1, keepdims=True)
    out = jnp.where(lane == 0, e1, jnp.where(lane == 1, e2, jnp.where(lane == 2, p1, jnp.where(lane == 3, p2, 0.0))))
    o_ref[...] = out.astype(jnp.int32)
    run_ref[...] = run_ref[...] + jnp.sum(sel, axis=0, keepdims=True)
    cnt_ref[...] = run_ref[...]


def _route_scan(rinfo, ts=512):
    t = rinfo.shape[0]
    return pl.pallas_call(
        _scan_kernel,
        grid=(t // ts,),
        in_specs=[pl.BlockSpec((ts, LANES), lambda i: (i, 0))],
        out_specs=[pl.BlockSpec((ts, LANES), lambda i: (i, 0)), pl.BlockSpec((8, LANES), lambda i: (0, 0))],
        out_shape=[jax.ShapeDtypeStruct((t, LANES), jnp.int32), jax.ShapeDtypeStruct((8, LANES), F32)],
        scratch_shapes=[pltpu.VMEM((8, LANES), F32)],
        compiler_params=_cp(("arbitrary",)),
        name="route_scan",
    )(rinfo)


def _row_copy(src_ref, src_row, dst_ref, dst_row, sem):
    return pltpu.make_async_copy(src_ref.at[pl.ds(src_row, 1)], dst_ref.at[pl.ds(dst_row, 1)], sem)


def _dispatch_kernel(dest_ref, h_ref, xs_in_ref, xs_ref, sem):
    del xs_in_ref
    base = pl.program_id(0) * MOE_TT

    def issue(t, c):
        for k in range(2):
            _row_copy(h_ref, base + t, xs_ref, dest_ref[0, k, t], sem).start()
        return c

    lax.fori_loop(0, MOE_TT, issue, 0)

    def drain(t, c):
        for k in range(2):
            _row_copy(h_ref, 0, xs_ref, 0, sem).wait()
        return c

    lax.fori_loop(0, MOE_TT, drain, 0)


def _dispatch(dest, h2):
    t = h2.shape[0]
    xs0 = jnp.zeros((_moe_rows(t), D_MODEL), F32)
    return pl.pallas_call(
        _dispatch_kernel,
        grid=(t // MOE_TT,),
        in_specs=[
            pl.BlockSpec((1, 2, MOE_TT), lambda i: (i, 0, 0), memory_space=pltpu.SMEM),
            pl.BlockSpec(memory_space=pl.ANY),
            pl.BlockSpec(memory_space=pl.ANY),
        ],
        out_specs=pl.BlockSpec(memory_space=pl.ANY),
        out_shape=jax.ShapeDtypeStruct(xs0.shape, F32),
        scratch_shapes=[pltpu.SemaphoreType.DMA(())],
        input_output_aliases={2: 0},
        compiler_params=_cp(("arbitrary",)),
        name="moe_dispatch",
    )(dest, h2, xs0)


def _experts_kernel(te_ref, nt_ref, xs_ref, wg_ref, wu_ref, wd_ref, y_ref, acc_ref):
    j = pl.program_id(1)

    @pl.when(pl.program_id(0) < nt_ref[0])
    def _():
        h = xs_ref[...].astype(BF16)
        gate = _dot(h, wg_ref[0])
        up = _dot(h, wu_ref[0])
        act = (gate * _sigmoid(gate) * up).astype(BF16)
        part = _dot(act, wd_ref[0])

        @pl.when(j == 0)
        def _():
            acc_ref[...] = part

        @pl.when(j > 0)
        def _():
            acc_ref[...] += part

        @pl.when(j == pl.num_programs(1) - 1)
        def _():
            y_ref[...] = acc_ref[...]

    @pl.when((pl.program_id(0) >= nt_ref[0]) & (j == pl.num_programs(1) - 1))
    def _():
        y_ref[...] = jnp.zeros_like(y_ref)


def _experts(tile_expert, n_tiles_used, xs, wg, wu, wd, fc):
    ne, _, dff = wg.shape
    grid_spec = pltpu.PrefetchScalarGridSpec(
        num_scalar_prefetch=2,
        grid=(xs.shape[0] // MOE_TM, dff // fc),
        in_specs=[
            pl.BlockSpec((MOE_TM, D_MODEL), lambda i, j, te, nt: (i, 0)),
            pl.BlockSpec((1, D_MODEL, fc), lambda i, j, te, nt: (te[i], 0, j)),
            pl.BlockSpec((1, D_MODEL, fc), lambda i, j, te, nt: (te[i], 0, j)),
            pl.BlockSpec((1, fc, D_MODEL), lambda i, j, te, nt: (te[i], j, 0)),
        ],
        out_specs=pl.BlockSpec((MOE_TM, D_MODEL), lambda i, j, te, nt: (i, 0)),
        scratch_shapes=[pltpu.VMEM((MOE_TM, D_MODEL), F32)],
    )
    return pl.pallas_call(
        _experts_kernel,
        grid_spec=grid_spec,
        out_shape=jax.ShapeDtypeStruct(xs.shape, F32),
        compiler_params=_cp(("arbitrary", "arbitrary")),
        name="moe_experts",
    )(tile_expert, n_tiles_used, xs, wg, wu, wd)


def _combine_kernel(dest_ref, y_ref, x1_ref, rinfo_ref, mod_ref, g_ref, b_ref, o_ref, ybuf_ref, sem):
    def issue(t, c):
        for k in range(2):
            _row_copy(y_ref, dest_ref[0, k, t], ybuf_ref.at[k], t, sem).start()
        return c

    lax.fori_loop(0, MOE_TT, issue, 0)

    def drain(t, c):
        for k in range(2):
            _row_copy(y_ref, 0, ybuf_ref.at[k], 0, sem).wait()
        return c

    lax.fori_loop(0, MOE_TT, drain, 0)

    r = rinfo_ref[...]
    lane = lax.broadcasted_iota(jnp.int32, r.shape, 1)
    w1 = jnp.sum(jnp.where(lane == RI_W1, r, 0.0), axis=1, keepdims=True)
    w2 = jnp.sum(jnp.where(lane == RI_W2, r, 0.0), axis=1, keepdims=True)
    f = w1 * ybuf_ref[0] + w2 * ybuf_ref[1]
    mod = mod_ref[0]
    o_ref[...] = _ln(ALPHA * x1_ref[...] + mod[5:6] * f) * g_ref[...] + b_ref[...]


def _combine(dest, y, x1, rinfo, mod, g, b):
    t = x1.shape[0]
    nb = mod.shape[0]
    tpb = t // nb // MOE_TT
    row = lambda i: (i, 0)
    const = lambda i: (0, 0)
    return pl.pallas_call(
        _combine_kernel,
        grid=(t // MOE_TT,),
        in_specs=[
            pl.BlockSpec((1, 2, MOE_TT), lambda i: (i, 0, 0), memory_space=pltpu.SMEM),
            pl.BlockSpec(memory_space=pl.ANY),
            pl.BlockSpec((MOE_TT, D_MODEL), row),
            pl.BlockSpec((MOE_TT, LANES), row),
            pl.BlockSpec((1, 8, D_MODEL), lambda i: (i // tpb, 0, 0)),
            pl.BlockSpec((1, D_MODEL), const),
            pl.BlockSpec((1, D_MODEL), const),
        ],
        out_specs=pl.BlockSpec((MOE_TT, D_MODEL), row),
        out_shape=jax.ShapeDtypeStruct((t, D_MODEL), F32),
        scratch_shapes=[pltpu.VMEM((2, MOE_TT, D_MODEL), F32), pltpu.SemaphoreType.DMA(())],
        compiler_params=_cp(("arbitrary",)),
        name="moe_combine",
    )(dest, y, x1, rinfo, mod, g, b)


def _moe(h2, x1, rinfo, mod, g, b, wg, wu, wd):
    t = h2.shape[0]
    route, counts = _route_scan(rinfo)
    cnt = counts[0, :N_EXPERTS].astype(jnp.int32)
    padded = (cnt + MOE_TM - 1) // MOE_TM * MOE_TM
    ends = jnp.cumsum(padded)
    starts = ends - padded
    dest = jnp.stack([starts[route[:, 0]] + route[:, 2], starts[route[:, 1]] + route[:, 3]])
    dest = dest.reshape(2, t // MOE_TT, MOE_TT).transpose(1, 0, 2)
    tile_row0 = jnp.arange(_moe_rows(t) // MOE_TM, dtype=jnp.int32) * MOE_TM
    tile_expert = jnp.minimum(jnp.sum(tile_row0[:, None] >= ends[None, :], axis=1), N_EXPERTS - 1).astype(jnp.int32)
    n_tiles_used = (ends[-1:] // MOE_TM).astype(jnp.int32)
    xs = _dispatch(dest, h2)
    y = _experts(tile_expert, n_tiles_used, xs, wg, wu, wd, fc=D_FF_EXPERT // 2)
    return _combine(dest, y, x1, rinfo, mod, g, b)


def _rope_np(n_tokens):
    rows = np.repeat(np.arange(n_tokens // GRID_W), GRID_W)
    cols = np.tile(np.arange(GRID_W), n_tokens // GRID_W)
    pos = np.stack([rows, cols], axis=-1).astype(np.float64)
    inv_freq = 1.0 / (ROPE_THETA ** (np.arange(ROPE_FREQS, dtype=np.float64) / ROPE_FREQS))
    ang = pos[..., None] * inv_freq
    cos, sin = np.cos(ang), np.sin(ang)
    cos32 = np.broadcast_to(cos[:, :, None, :], (n_tokens, 2, 2, ROPE_FREQS)).reshape(n_tokens, QK_ROPE)
    sgn = np.array([-1.0, 1.0])[None, None, :, None]
    sin32 = (np.broadcast_to(sin[:, :, None, :], (n_tokens, 2, 2, ROPE_FREQS)) * sgn).reshape(n_tokens, QK_ROPE)
    return cos32, sin32


def _half_swap_perm():
    p = np.arange(QK_ROPE).reshape(2, 2, ROPE_FREQS)
    return p[:, ::-1, :].reshape(QK_ROPE)


def _tables():
    c0 = SM_SCALE * math.log2(math.e)
    cos32, sin32 = _rope_np(SEQ)
    one64 = np.ones((SEQ, QK_NOPE))
    tq_lat = c0 * np.concatenate([one64, cos32, sin32], axis=1)
    tk_lat = np.concatenate([cos32, sin32, np.zeros((SEQ, LANES - 2 * QK_ROPE))], axis=1)
    tq_ctx = c0 * np.concatenate([np.ones((CTX_LEN, QK_NOPE + QK_ROPE)), np.zeros((CTX_LEN, QK_ROPE))], axis=1)
    tk_ctx = np.concatenate([np.ones((CTX_LEN, QK_ROPE)), np.zeros((CTX_LEN, LANES - QK_ROPE))], axis=1)
    f32 = lambda a: jnp.asarray(a, F32)
    bf = lambda a: jnp.asarray(a, BF16)

    def dft(n):
        k = np.arange(n)
        ang = 2.0 * np.pi * ((k[:, None] * k[None, :]) % n) / n
        return np.cos(ang), np.sin(ang)

    c128, s128 = dft(FFT_R)
    c256, s256 = dft(CTX_LEN)
    bd_lat = np.concatenate([c128, -s128], axis=1) / math.sqrt(SEQ * F_GROUP_DIM)
    bd_ctx = np.concatenate([c128, -s128], axis=1) / math.sqrt(CTX_LEN * F_GROUP_DIM)
    k1 = np.arange(FFT_R)
    tw = 2.0 * np.pi * ((k1[:, None] * k1[None, :]) % SEQ) / SEQ
    tc = jnp.broadcast_to(f32(np.cos(tw))[:, :, None], (FFT_R, FFT_R, LANES)).reshape(FFT_R, FFT_R * LANES)
    ts = jnp.broadcast_to(f32(np.sin(tw))[:, :, None], (FFT_R, FFT_R, LANES)).reshape(FFT_R, FFT_R * LANES)
    return dict(tq_lat=f32(tq_lat), tk_lat=f32(tk_lat), tq_ctx=f32(tq_ctx), tk_ctx=f32(tk_ctx),
                c128=bf(c128), s128=bf(s128), c256=bf(c256), s256=bf(s256),
                bd_lat=bf(bd_lat), bd_ctx=bf(bd_ctx), tc=tc, ts=ts)


def _arrange_weights(l, w_in, w_uq, w_uk, w_uv):
    sw = _half_swap_perm()
    wi = w_in[l]
    kr = wi[:, OFF_KR:OFF_F]
    w_ip = jnp.concatenate([
        wi[:, :OFF_KR], kr, kr[:, sw], jnp.zeros((D_MODEL, LANES - 2 * QK_ROPE), F32),
        wi[:, OFF_F:]], axis=1).astype(BF16)
    uq = w_uq[l].reshape(Q_LORA, N_HEADS, QK_NOPE + QK_ROPE)
    qr = uq[:, :, QK_NOPE:]
    wq = jnp.concatenate([uq, qr[:, :, sw]], axis=-1).reshape(Q_LORA, N_HEADS * HEAD_PAD).astype(BF16)
    uk = w_uk[l].reshape(KV_LORA, N_HEADS, QK_NOPE)
    top = jnp.concatenate([uk, jnp.zeros((KV_LORA, N_HEADS, HEAD_PAD - QK_NOPE), F32)], axis=-1)
    rmat = np.zeros((LANES, HEAD_PAD), np.float32)
    for i in range(QK_ROPE):
        for r in (i, QK_ROPE + i):
            rmat[r, QK_NOPE + i] = 1.0
            rmat[r, QK_NOPE + QK_ROPE + i] = 1.0
    bot = jnp.broadcast_to(jnp.asarray(rmat)[:, None, :], (LANES, N_HEADS, HEAD_PAD))
    wkt = jnp.transpose(jnp.concatenate([top, bot], axis=0), (1, 2, 0)).astype(BF16)
    uv = w_uv[l].reshape(KV_LORA, N_HEADS // 2, 2, V_HEAD)
    z = jnp.zeros_like(uv[:, :, 0])
    even = jnp.concatenate([uv[:, :, 0], z], axis=-1)
    odd = jnp.concatenate([z, uv[:, :, 1]], axis=-1)
    wv = jnp.stack([even, odd], axis=2).reshape(KV_LORA, N_HEADS * HEAD_PAD).astype(BF16)
    return w_ip, wq, wkt, wv


def kernel(x, c, ctx, c_ctx, w_mod, b_mod, w_in, g_q, w_uq, g_kv, w_uk, w_uv, w_ao, w_fo, w_out,
           ln1_g, ln1_b, ln2_g, ln2_b, w_ff_gate, w_ff_up, w_ff_down,
           w_router, b_router, w_e_gate, w_e_up, w_e_down):
    tb = _tables()
    t_lat = BATCH * SEQ
    t_ctx = BATCH * CTX_LEN

    cs = jnp.concatenate([c, c_ctx[None, :], jnp.zeros((8 - BATCH - 1, D_MODEL), F32)], axis=0)
    mods = _modulation(cs, w_mod.astype(BF16), b_mod).reshape(DEPTH, 8, 6, D_MODEL)
    pad2 = jnp.zeros((DEPTH, 8, 2, D_MODEL), F32)
    mods = jnp.concatenate([mods, pad2], axis=2)

    xl = x.reshape(t_lat, D_MODEL)
    xc = ctx.reshape(t_ctx, D_MODEL)
    ones_comb = jnp.ones((t_lat, LANES), F32)
    ones_comb_c = jnp.ones((t_ctx, LANES), F32)

    for l in range(DEPTH):
        last = l == DEPTH - 1
        mod_x = mods[l, :BATCH]
        mod_c = jnp.broadcast_to(mods[l, BATCH:BATCH + 1], (BATCH, 8, D_MODEL))
        w_ip, wq, wkt, wv = _arrange_weights(l, w_in, w_uq, w_uk, w_uv)
        gq = g_q[l][None, :]
        gkv = g_kv[l][None, :]
        wao = w_ao[l].astype(BF16)
        wfo = w_fo[l].astype(BF16)
        wout = w_out[l].astype(BF16)
        g1, b1 = ln1_g[l][None, :], ln1_b[l][None, :]
        g2, b2 = ln2_g[l][None, :], ln2_b[l][None, :]

        qlat_c, ckr_c, wf_c, gates_c = _inproj(xc, mod_c, w_ip, gq, gkv, tb["tk_ctx"], tb["bd_ctx"], tm=CTX_LEN)
        q_c, kt_c, v_c = _qkv(qlat_c, ckr_c, tb["tq_ctx"], wq, wkt, wv, nb=BATCH, tm=CTX_LEN)

        qlat, ckr, wf, gates = _inproj(xl, mod_x, w_ip, gq, gkv, tb["tk_lat"], tb["bd_lat"], tm=512)
        q, kt, v = _qkv(qlat, ckr, tb["tq_lat"], wq, wkt, wv, nb=BATCH, tm=512)
        attn = _attention(q, kt_c, v_c, kt, v, tq=ATTN_TQ, tk=ATTN_TK, unroll=ATTN_UNROLL)
        four = _fourier_latent(wf.reshape(BATCH, SEQ, 2 * FOURIER_DIM), tb["c128"], tb["s128"], tb["tc"], tb["ts"])

        if l % 2 == 0:
            i = l // 2
            x1, h2 = _mixer(attn.reshape(t_lat, -1), four.reshape(t_lat, -1), gates, xl, mod_x, g1, b1,
                            wao, wfo, wout, tm=512)
            wg = w_ff_gate[i][None].astype(BF16)
            wu = w_ff_up[i][None].astype(BF16)
            wd = w_ff_down[i][None].astype(BF16)
            xl_new = _ffn(h2, x1, ones_comb, mod_x, g2, b2, wg, wu, wd, tm=512, fc=D_FF // 2)
        else:
            i = l // 2
            wr = jnp.concatenate([w_router[i], jnp.zeros((D_MODEL, LANES - N_EXPERTS), F32)], axis=1)
            br = jnp.concatenate([b_router[i], jnp.zeros((LANES - N_EXPERTS,), F32)])[None, :]
            x1, h2, rinfo = _mixer(attn.reshape(t_lat, -1), four.reshape(t_lat, -1), gates, xl, mod_x, g1, b1,
                                   wao, wfo, wout, tm=512, router=(wr, br))
            wg = w_e_gate[i].astype(BF16)
            wu = w_e_up[i].astype(BF16)
            wd = w_e_down[i].astype(BF16)
            xl_new = _moe(h2, x1, rinfo, mod_x, g2, b2, wg, wu, wd)

        if not last:
            assert l % 2 == 0
            attn_c = _attention(q_c, kt_c, v_c, None, None, tq=CTX_LEN, tk=CTX_LEN)
            four_c = _fourier_ctx(wf_c.reshape(BATCH, CTX_LEN, 2 * FOURIER_DIM), tb["c256"], tb["s256"])
            x1c, h2c = _mixer(attn_c.reshape(t_ctx, -1), four_c.reshape(t_ctx, -1), gates_c, xc, mod_c,
                              g1, b1, wao, wfo, wout, tm=CTX_LEN)
            xc = _ffn(h2c, x1c, ones_comb_c, mod_c, g2, b2, wg, wu, wd, tm=CTX_LEN, fc=D_FF // 2)
        xl = xl_new

    return xl.reshape(BATCH, SEQ, D_MODEL)
```

```python
import functools
import math

import numpy as np
import jax
import jax.numpy as jnp
from jax import lax
from jax.experimental import pallas as pl
from jax.experimental.pallas import tpu as pltpu

D_MODEL = 1024
BATCH = 2
SEQ = 16384
DEPTH = 2
GRID_W = 64
CTX_LEN = 256
N_HEADS = 16
QK_NOPE = 64
QK_ROPE = 32
ROPE_FREQS = QK_ROPE // 4
V_HEAD = 64
Q_LORA = 256
KV_LORA = 128
ROPE_THETA = 10000.0
SM_SCALE = (QK_NOPE + QK_ROPE) ** -0.5
F_GROUPS = 4
F_GROUP_DIM = 128
FOURIER_DIM = F_GROUPS * F_GROUP_DIM
OFF_KV = Q_LORA
OFF_KR = OFF_KV + KV_LORA
OFF_F = OFF_KR + QK_ROPE
OFF_G = OFF_F + FOURIER_DIM
D_FF = 2816
N_EXPERTS = 8
D_FF_EXPERT = 3584
ALPHA = (2 * DEPTH) ** 0.25
LN_EPS = 1e-6
RMS_EPS = 1e-6

LANES = 128
HEAD_PAD = 128
FFT_R = 128
V_ONES_LANE = (V_HEAD, 0)
VMEM_LIMIT = 56 * 1024 * 1024
ATTN_TQ = 1024
ATTN_TK = 512
ATTN_UNROLL = 8

BF16 = jnp.bfloat16
F32 = jnp.float32


def _cp(sem, vmem=VMEM_LIMIT):
    return pltpu.CompilerParams(dimension_semantics=sem, vmem_limit_bytes=vmem)


def _dot(a, b):
    return jnp.dot(a, b, preferred_element_type=F32)


def _ln(x):
    mu = jnp.mean(x, axis=-1, keepdims=True)
    xc = x - mu
    var = jnp.mean(xc * xc, axis=-1, keepdims=True)
    return xc * lax.rsqrt(var + LN_EPS)


def _rms(x, g):
    return x * lax.rsqrt(jnp.mean(x * x, axis=-1, keepdims=True) + RMS_EPS) * g


def _sigmoid(x):
    return 1.0 / (1.0 + jnp.exp(-x))


def _mod_kernel(cs_ref, w_ref, b_ref, o_ref):
    cs = cs_ref[...]
    a = (cs * _sigmoid(cs)).astype(BF16)
    o_ref[0] = _dot(a, w_ref[0]) + b_ref[0]


def _modulation(cs, w_mod, b_mod):
    n_chunk = 1024
    n_out = w_mod.shape[-1]
    return pl.pallas_call(
        _mod_kernel,
        grid=(DEPTH, n_out // n_chunk),
        in_specs=[
            pl.BlockSpec((8, D_MODEL), lambda l, j: (0, 0)),
            pl.BlockSpec((1, D_MODEL, n_chunk), lambda l, j: (l, 0, j)),
            pl.BlockSpec((1, 1, n_chunk), lambda l, j: (l, 0, j)),
        ],
        out_specs=pl.BlockSpec((1, 8, n_chunk), lambda l, j: (l, 0, j)),
        out_shape=jax.ShapeDtypeStruct((DEPTH, 8, n_out), F32),
        compiler_params=_cp(("parallel", "parallel")),
        name="mod",
    )(cs, w_mod, b_mod.reshape(DEPTH, 1, n_out))


IP_Q = 0
IP_KV = Q_LORA
IP_KR = IP_KV + KV_LORA
IP_F = IP_KR + LANES
IP_G = IP_F + FOURIER_DIM
IP_COLS = IP_G + 2 * D_MODEL


def _inproj_kernel(x_ref, mod_ref, w_ref, gq_ref, gkv_ref, tk_ref, bd_ref,
                   qlat_ref, ckr_ref, wf_ref, gates_ref):
    x = x_ref[...]
    mod = mod_ref[0]
    h = (_ln(x) * (1.0 + mod[1:2]) + mod[0:1]).astype(BF16)
    p0 = _dot(h, w_ref[:, IP_Q:IP_F])
    qlat_ref[...] = _rms(p0[:, IP_Q:IP_KV], gq_ref[...]).astype(BF16)
    ckr_ref[:, 0:KV_LORA] = _rms(p0[:, IP_KV:IP_KR], gkv_ref[...]).astype(BF16)
    ckr_ref[:, KV_LORA:] = (p0[:, IP_KR:IP_F] * tk_ref[...]).astype(BF16)
    uf = _dot(h, w_ref[:, IP_F:IP_G]).astype(BF16)
    for g in range(F_GROUPS):
        r = _dot(uf[:, g * LANES:(g + 1) * LANES], bd_ref[...])
        wf_ref[:, g * LANES:(g + 1) * LANES] = r[:, :LANES].astype(BF16)
        wf_ref[:, FOURIER_DIM + g * LANES:FOURIER_DIM + (g + 1) * LANES] = r[:, LANES:].astype(BF16)
    gc = 512
    for c in range(2 * D_MODEL // gc):
        gl = _dot(h, w_ref[:, IP_G + c * gc:IP_G + (c + 1) * gc])
        gates_ref[:, c * gc:(c + 1) * gc] = _sigmoid(gl).astype(BF16)


def _inproj(x2d, mod, w, gq, gkv, tk, bd, tm):
    t = x2d.shape[0]
    nb = mod.shape[0]
    tpb = t // nb // tm
    return pl.pallas_call(
        _inproj_kernel,
        grid=(t // tm,),
        in_specs=[
            pl.BlockSpec((tm, D_MODEL), lambda i: (i, 0)),
            pl.BlockSpec((1, 8, D_MODEL), lambda i: (i // tpb, 0, 0)),
            pl.BlockSpec((D_MODEL, IP_COLS), lambda i: (0, 0)),
            pl.BlockSpec((1, Q_LORA), lambda i: (0, 0)),
            pl.BlockSpec((1, KV_LORA), lambda i: (0, 0)),
            pl.BlockSpec((tm, LANES), lambda i: (i % tpb, 0)),
            pl.BlockSpec((LANES, 2 * LANES), lambda i: (0, 0)),
        ],
        out_specs=[
            pl.BlockSpec((tm, Q_LORA), lambda i: (i, 0)),
            pl.BlockSpec((tm, 2 * LANES), lambda i: (i, 0)),
            pl.BlockSpec((tm, 2 * FOURIER_DIM), lambda i: (i, 0)),
            pl.BlockSpec((tm, 2 * D_MODEL), lambda i: (i, 0)),
        ],
        out_shape=[
            jax.ShapeDtypeStruct((t, Q_LORA), BF16),
            jax.ShapeDtypeStruct((t, 2 * LANES), BF16),
            jax.ShapeDtypeStruct((t, 2 * FOURIER_DIM), BF16),
            jax.ShapeDtypeStruct((t, 2 * D_MODEL), BF16),
        ],
        compiler_params=_cp(("parallel",)),
        name="inproj",
    )(x2d, mod, w, gq, gkv, tk, bd)


def _qkv_kernel(qlat_ref, ckr_ref, tq_ref, wq_ref, wkt_ref, wv_ref, q_ref, kt_ref, v_ref):
    ql = qlat_ref[...]
    ckr = ckr_ref[...]
    tq = tq_ref[...]
    ckv = ckr[:, :KV_LORA]
    lane = lax.broadcasted_iota(jnp.int32, (ckr.shape[0], HEAD_PAD), 1)
    for h in range(N_HEADS):
        cols = slice(h * HEAD_PAD, (h + 1) * HEAD_PAD)
        q_ref[0, h] = (_dot(ql, wq_ref[:, cols]) * tq).astype(BF16)
        kt_ref[0, h] = lax.dot_general(
            wkt_ref[h], ckr, (((1,), (1,)), ((), ())), preferred_element_type=F32).astype(BF16)
        vh = _dot(ckv, wv_ref[:, cols])
        v_ref[0, h] = jnp.where(lane == V_ONES_LANE[h % 2], 1.0, vh).astype(BF16)


def _qkv(qlat, ckr, tq, wq, wkt, wv, nb, tm):
    t = qlat.shape[0]
    n = t // nb
    tpb = n // tm
    return pl.pallas_call(
        _qkv_kernel,
        grid=(t // tm,),
        in_specs=[
            pl.BlockSpec((tm, Q_LORA), lambda i: (i, 0)),
            pl.BlockSpec((tm, 2 * LANES), lambda i: (i, 0)),
            pl.BlockSpec((tm, HEAD_PAD), lambda i: (i % tpb, 0)),
            pl.BlockSpec((Q_LORA, N_HEADS * HEAD_PAD), lambda i: (0, 0)),
            pl.BlockSpec((N_HEADS, HEAD_PAD, 2 * LANES), lambda i: (0, 0, 0)),
            pl.BlockSpec((KV_LORA, N_HEADS * HEAD_PAD), lambda i: (0, 0)),
        ],
        out_specs=[
            pl.BlockSpec((1, N_HEADS, tm, HEAD_PAD), lambda i: (i // tpb, 0, i % tpb, 0)),
            pl.BlockSpec((1, N_HEADS, HEAD_PAD, tm), lambda i: (i // tpb, 0, 0, i % tpb)),
            pl.BlockSpec((1, N_HEADS, tm, HEAD_PAD), lambda i: (i // tpb, 0, i % tpb, 0)),
        ],
        out_shape=[
            jax.ShapeDtypeStruct((nb, N_HEADS, n, HEAD_PAD), BF16),
            jax.ShapeDtypeStruct((nb, N_HEADS, HEAD_PAD, n), BF16),
            jax.ShapeDtypeStruct((nb, N_HEADS, n, HEAD_PAD), BF16),
        ],
        compiler_params=_cp(("parallel",)),
        name="qkv",
    )(qlat, ckr, tq, wq, wkt, wv)


def _attn_kernel(q_ref, ktc_ref, vc_ref, *rest, n_chunks, tk, unroll):
    if n_chunks:
        kt_ref, v_ref, o_ref = rest
    else:
        (o_ref,) = rest

    def step(q, kt, v, m, acc):
        s = _dot(q, kt)
        m_new = jnp.max(s, axis=1, keepdims=True)
        if m is not None:
            m_new = jnp.maximum(m, m_new)
        p = jnp.exp2((s - m_new).astype(BF16))
        pv = _dot(p, v)
        if m is not None:
            pv = jnp.exp2(m - m_new) * acc + pv
        return m_new, pv

    qs = [q_ref[0, hh] for hh in range(2)]
    carry = []
    for hh in range(2):
        carry += step(qs[hh], ktc_ref[0, hh], vc_ref[0, hh], None, None)

    if n_chunks:
        def body(j, carry):
            off = pl.multiple_of(j * tk, tk)
            out = []
            for hh in range(2):
                out += step(qs[hh], kt_ref[0, hh, :, pl.ds(off, tk)], v_ref[0, hh, pl.ds(off, tk), :],
                            carry[2 * hh], carry[2 * hh + 1])
            return tuple(out)

        carry = lax.fori_loop(0, n_chunks, body, tuple(carry), unroll=unroll)

    lane = lax.broadcasted_iota(jnp.int32, carry[1].shape, 1)
    outs = []
    for hh in range(2):
        acc = carry[2 * hh + 1]
        l = jnp.sum(jnp.where(lane == V_ONES_LANE[hh], acc, 0.0), axis=1, keepdims=True)
        outs.append(acc * (1.0 / l))
    o_ref[0] = jnp.where(lane < V_HEAD, outs[0], outs[1]).astype(BF16)


def _attention(q, kt_c, v_c, kt, v, tq, tk, unroll=1):
    nb, _, n, _ = q.shape
    nc = kt_c.shape[-1]
    n_chunks = 0 if kt is None else kt.shape[-1] // tk
    in_specs = [
        pl.BlockSpec((1, 2, tq, HEAD_PAD), lambda b, p, i: (b, p, i, 0)),
        pl.BlockSpec((1, 2, HEAD_PAD, nc), lambda b, p, i: (b, p, 0, 0)),
        pl.BlockSpec((1, 2, nc, HEAD_PAD), lambda b, p, i: (b, p, 0, 0)),
    ]
    args = [q, kt_c, v_c]
    if n_chunks:
        nk = kt.shape[-1]
        in_specs += [
            pl.BlockSpec((1, 2, HEAD_PAD, nk), lambda b, p, i: (b, p, 0, 0)),
            pl.BlockSpec((1, 2, nk, HEAD_PAD), lambda b, p, i: (b, p, 0, 0)),
        ]
        args += [kt, v]
    return pl.pallas_call(
        functools.partial(_attn_kernel, n_chunks=n_chunks, tk=tk, unroll=unroll),
        grid=(nb, N_HEADS // 2, n // tq),
        in_specs=in_specs,
        out_specs=pl.BlockSpec((1, tq, 2 * V_HEAD), lambda b, p, i: (b, i, p)),
        out_shape=jax.ShapeDtypeStruct((nb, n, N_HEADS * V_HEAD), BF16),
        compiler_params=_cp(("parallel", "parallel", "arbitrary")),
        name="attn",
    )(*args)


def _fft1_kernel(x_ref, c_ref, s_ref, tc_ref, ts_ref, o_ref, *, n2t):
    x = x_ref[0]
    cx = _dot(c_ref[...], x)
    sx = _dot(s_ref[...], x)
    w = 2 * FOURIER_DIM
    for t in range(n2t):
        re = slice(t * w, t * w + FOURIER_DIM)
        im = slice(t * w + FOURIER_DIM, (t + 1) * w)
        yr = cx[:, re] + sx[:, im]
        yi = cx[:, im] - sx[:, re]
        tc = jnp.concatenate([tc_ref[:, t * LANES:(t + 1) * LANES]] * F_GROUPS, axis=1)
        ts = jnp.concatenate([ts_ref[:, t * LANES:(t + 1) * LANES]] * F_GROUPS, axis=1)
        o_ref[0, :, re] = (yr * tc + yi * ts).astype(BF16)
        o_ref[0, :, im] = (yi * tc - yr * ts).astype(BF16)


def _fft2_kernel(y_ref, c_ref, s_ref, o_ref, *, k1t):
    for t in range(k1t):
        y = y_ref[0, t]
        zr = _dot(c_ref[...], y[:, :FOURIER_DIM]) + _dot(s_ref[...], y[:, FOURIER_DIM:])
        o_ref[0, :, t * FOURIER_DIM:(t + 1) * FOURIER_DIM] = zr.astype(BF16)


def _fourier_latent(wf, cmat, smat, tc, ts):
    nb = wf.shape[0]
    w = 2 * FOURIER_DIM
    n2t = 8
    y = pl.pallas_call(
        functools.partial(_fft1_kernel, n2t=n2t),
        grid=(nb, FFT_R // n2t),
        in_specs=[
            pl.BlockSpec((1, FFT_R, n2t * w), lambda b, j: (b, 0, j)),
            pl.BlockSpec((FFT_R, FFT_R), lambda b, j: (0, 0)),
            pl.BlockSpec((FFT_R, FFT_R), lambda b, j: (0, 0)),
            pl.BlockSpec((FFT_R, n2t * LANES), lambda b, j: (0, j)),
            pl.BlockSpec((FFT_R, n2t * LANES), lambda b, j: (0, j)),
        ],
        out_specs=pl.BlockSpec((1, FFT_R, n2t * w), lambda b, j: (b, 0, j)),
        out_shape=jax.ShapeDtypeStruct((nb, FFT_R, FFT_R * w), BF16),
        compiler_params=_cp(("parallel", "parallel")),
        name="fft1",
    )(wf.reshape(nb, FFT_R, FFT_R * w), cmat, smat, tc, ts)
    k1t = 8
    four = pl.pallas_call(
        functools.partial(_fft2_kernel, k1t=k1t),
        grid=(nb, FFT_R // k1t),
        in_specs=[
            pl.BlockSpec((1, k1t, FFT_R, w), lambda b, j: (b, j, 0, 0)),
            pl.BlockSpec((FFT_R, FFT_R), lambda b, j: (0, 0)),
            pl.BlockSpec((FFT_R, FFT_R), lambda b, j: (0, 0)),
        ],
        out_specs=pl.BlockSpec((1, FFT_R, k1t * FOURIER_DIM), lambda b, j: (b, 0, j)),
        out_shape=jax.ShapeDtypeStruct((nb, FFT_R, FFT_R * FOURIER_DIM), BF16),
        compiler_params=_cp(("parallel", "parallel")),
        name="fft2",
    )(y.reshape(nb, FFT_R, FFT_R, w), cmat, smat)
    return four.reshape(nb, SEQ, FOURIER_DIM)


def _dft_ctx_kernel(x_ref, c_ref, s_ref, o_ref):
    x = x_ref[0]
    zr = _dot(c_ref[...], x[:, :FOURIER_DIM]) + _dot(s_ref[...], x[:, FOURIER_DIM:])
    o_ref[0] = zr.astype(BF16)


def _fourier_ctx(wf, cmat, smat):
    nb, n, w = wf.shape
    return pl.pallas_call(
        _dft_ctx_kernel,
        grid=(nb,),
        in_specs=[
            pl.BlockSpec((1, n, w), lambda b: (b, 0, 0)),
            pl.BlockSpec((n, n), lambda b: (0, 0)),
            pl.BlockSpec((n, n), lambda b: (0, 0)),
        ],
        out_specs=pl.BlockSpec((1, n, FOURIER_DIM), lambda b: (b, 0, 0)),
        out_shape=jax.ShapeDtypeStruct((nb, n, FOURIER_DIM), BF16),
        compiler_params=_cp(("parallel",)),
        name="dft_ctx",
    )(wf, cmat, smat)


def _mixer_kernel(attn_ref, four_ref, gates_ref, x_ref, mod_ref, g_ref, b_ref,
                  wao_ref, wfo_ref, wout_ref, *rest, route):
    if route:
        wr_ref, br_ref, x1_ref, h2_ref, rinfo_ref = rest
    else:
        x1_ref, h2_ref = rest
    mod = mod_ref[0]
    a = _dot(attn_ref[...], wao_ref[...])
    f = _dot(four_ref[...], wfo_ref[...])
    merged = gates_ref[:, :D_MODEL].astype(F32) * a + gates_ref[:, D_MODEL:].astype(F32) * f
    y = _dot(merged.astype(BF16), wout_ref[...])
    x1 = _ln(ALPHA * x_ref[...] + mod[2:3] * y) * g_ref[...] + b_ref[...]
    x1_ref[...] = x1
    h2 = _ln(x1) * (1.0 + mod[4:5]) + mod[3:4]
    h2_ref[...] = h2.astype(h2_ref.dtype)
    if route:
        logits = jnp.dot(h2, wr_ref[...], preferred_element_type=F32,
                         precision=lax.Precision.HIGHEST) + br_ref[...]
        lane = lax.broadcasted_iota(jnp.int32, logits.shape, 1)
        neg = jnp.float32(-jnp.inf)
        lg = jnp.where(lane < N_EXPERTS, logits, neg)
        m1 = jnp.max(lg, axis=1, keepdims=True)
        i1 = jnp.min(jnp.where(lg == m1, lane, LANES), axis=1, keepdims=True)
        lg2 = jnp.where(lane == i1, neg, lg)
        m2 = jnp.max(lg2, axis=1, keepdims=True)
        i2 = jnp.min(jnp.where(lg2 == m2, lane, LANES), axis=1, keepdims=True)
        e2 = jnp.exp(m2 - m1)
        w1 = 1.0 / (1.0 + e2)
        w2 = e2 * w1
        rinfo = jnp.where((lane == i1) | (lane == i2), 1.0, 0.0)
        rinfo = jnp.where(lane == RI_E1, i1.astype(F32), rinfo)
        rinfo = jnp.where(lane == RI_E2, i2.astype(F32), rinfo)
        rinfo = jnp.where(lane == RI_W1, w1, rinfo)
        rinfo_ref[...] = jnp.where(lane == RI_W2, w2, rinfo)


def _mixer(attn, four, gates, x2d, mod, g, b, wao, wfo, wout, tm, router=None):
    t = x2d.shape[0]
    nb = mod.shape[0]
    tpb = t // nb // tm
    row = lambda i: (i, 0)
    const = lambda i: (0, 0)
    in_specs = [
        pl.BlockSpec((tm, D_MODEL), row),
        pl.BlockSpec((tm, FOURIER_DIM), row),
        pl.BlockSpec((tm, 2 * D_MODEL), row),
        pl.BlockSpec((tm, D_MODEL), row),
        pl.BlockSpec((1, 8, D_MODEL), lambda i: (i // tpb, 0, 0)),
        pl.BlockSpec((1, D_MODEL), const),
        pl.BlockSpec((1, D_MODEL), const),
        pl.BlockSpec((D_MODEL, D_MODEL), const),
        pl.BlockSpec((FOURIER_DIM, D_MODEL), const),
        pl.BlockSpec((D_MODEL, D_MODEL), const),
    ]
    args = [attn, four, gates, x2d, mod, g, b, wao, wfo, wout]
    out_specs = [pl.BlockSpec((tm, D_MODEL), row), pl.BlockSpec((tm, D_MODEL), row)]
    h2_dtype = BF16 if router is None else F32
    out_shape = [jax.ShapeDtypeStruct((t, D_MODEL), F32), jax.ShapeDtypeStruct((t, D_MODEL), h2_dtype)]
    if router is not None:
        in_specs += [pl.BlockSpec((D_MODEL, LANES), const), pl.BlockSpec((1, LANES), const)]
        args += list(router)
        out_specs.append(pl.BlockSpec((tm, LANES), row))
        out_shape.append(jax.ShapeDtypeStruct((t, LANES), F32))
    return pl.pallas_call(
        functools.partial(_mixer_kernel, route=router is not None),
        grid=(t // tm,),
        in_specs=in_specs,
        out_specs=out_specs,
        out_shape=out_shape,
        compiler_params=_cp(("parallel",)),
        name="mixer",
    )(*args)


def _ffn_kernel(h_ref, x1_ref, comb_ref, mod_ref, g_ref, b_ref, wg_ref, wu_ref, wd_ref,
                o_ref, acc_ref):
    e = pl.program_id(1)
    j = pl.program_id(2)

    @pl.when((e == 0) & (j == 0))
    def _():
        acc_ref[...] = jnp.zeros_like(acc_ref)

    h = h_ref[...]
    gate = _dot(h, wg_ref[0])
    up = _dot(h, wu_ref[0])
    act = (gate * _sigmoid(gate) * up).astype(BF16)
    comb = comb_ref[...]
    lane = lax.broadcasted_iota(jnp.int32, comb.shape, 1)
    cw = jnp.sum(jnp.where(lane == e, comb, 0.0), axis=1, keepdims=True)
    acc_ref[...] += cw * _dot(act, wd_ref[0])

    @pl.when((e == pl.num_programs(1) - 1) & (j == pl.num_programs(2) - 1))
    def _():
        mod = mod_ref[0]
        o_ref[...] = _ln(ALPHA * x1_ref[...] + mod[5:6] * acc_ref[...]) * g_ref[...] + b_ref[...]


def _ffn(h2, x1, comb, mod, g, b, wg, wu, wd, tm, fc):
    t = h2.shape[0]
    nb = mod.shape[0]
    tpb = t // nb // tm
    ne, _, dff = wg.shape
    row = lambda i, e, j: (i, 0)
    const = lambda i, e, j: (0, 0)
    return pl.pallas_call(
        _ffn_kernel,
        grid=(t // tm, ne, dff // fc),
        in_specs=[
            pl.BlockSpec((tm, D_MODEL), row),
            pl.BlockSpec((tm, D_MODEL), row),
            pl.BlockSpec((tm, LANES), row),
            pl.BlockSpec((1, 8, D_MODEL), lambda i, e, j: (i // tpb, 0, 0)),
            pl.BlockSpec((1, D_MODEL), const),
            pl.BlockSpec((1, D_MODEL), const),
            pl.BlockSpec((1, D_MODEL, fc), lambda i, e, j: (e, 0, j)),
            pl.BlockSpec((1, D_MODEL, fc), lambda i, e, j: (e, 0, j)),
            pl.BlockSpec((1, fc, D_MODEL), lambda i, e, j: (e, j, 0)),
        ],
        out_specs=pl.BlockSpec((tm, D_MODEL), row),
        out_shape=jax.ShapeDtypeStruct((t, D_MODEL), F32),
        scratch_shapes=[pltpu.VMEM((tm, D_MODEL), F32)],
        compiler_params=_cp(("parallel", "arbitrary", "arbitrary")),
        name="ffn",
    )(h2, x1, comb, mod, g, b, wg, wu, wd)


RI_E1, RI_E2, RI_W1, RI_W2 = N_EXPERTS, N_EXPERTS + 1, N_EXPERTS + 2, N_EXPERTS + 3
MOE_TM = 512
MOE_TT = 512


def _moe_rows(n_tokens):
    return 2 * n_tokens + N_EXPERTS * MOE_TM


def _scan_kernel(r_ref, o_ref, cnt_ref, run_ref):
    @pl.when(pl.program_id(0) == 0)
    def _():
        run_ref[...] = jnp.zeros_like(run_ref)

    r = r_ref[...]
    ts = r.shape[0]
    lane = lax.broadcasted_iota(jnp.int32, r.shape, 1)
    sel = jnp.where(lane < N_EXPERTS, r, 0.0)
    rows = lax.broadcasted_iota(jnp.int32, (ts, ts), 0)
    cols = lax.broadcasted_iota(jnp.int32, (ts, ts), 1)
    ltri = jnp.where(rows > cols, 1.0, 0.0).astype(BF16)
    prefix = _dot(ltri, sel.astype(BF16)) + run_ref[0:1, :]
    lane_f = lane.astype(F32)
    pick = lambda k: jnp.sum(jnp.where(lane == k, r, 0.0), axis=1, keepdims=True)
    e1, e2 = pick(RI_E1), pick(RI_E2)
    p1 = jnp.sum(jnp.where(lane_f == e1, prefix, 0.0), axis=1, keepdims=True)
    p2 = jnp.sum(jnp.where(lane_f == e2, prefix, 0.0), axis=1, keepdims=True)
    out = jnp.where(lane == 0, e1, jnp.where(lane == 1, e2, jnp.where(lane == 2, p1, jnp.where(lane == 3, p2, 0.0))))
    o_ref[...] = out.astype(jnp.int32)
    run_ref[...] = run_ref[...] + jnp.sum(sel, axis=0, keepdims=True)
    cnt_ref[...] = run_ref[...]


def _route_scan(rinfo, ts=512):
    t = rinfo.shape[0]
    return pl.pallas_call(
        _scan_kernel,
        grid=(t // ts,),
        in_specs=[pl.BlockSpec((ts, LANES), lambda i: (i, 0))],
        out_specs=[pl.BlockSpec((ts, LANES), lambda i: (i, 0)), pl.BlockSpec((8, LANES), lambda i: (0, 0))],
        out_shape=[jax.ShapeDtypeStruct((t, LANES), jnp.int32), jax.ShapeDtypeStruct((8, LANES), F32)],
        scratch_shapes=[pltpu.VMEM((8, LANES), F32)],
        compiler_params=_cp(("arbitrary",)),
        name="route_scan",
    )(rinfo)


def _row_copy(src_ref, src_row, dst_ref, dst_row, sem):
    return pltpu.make_async_copy(src_ref.at[pl.ds(src_row, 1)], dst_ref.at[pl.ds(dst_row, 1)], sem)


def _dispatch_kernel(dest_ref, h_ref, xs_in_ref, xs_ref, sem):
    del xs_in_ref

    def issue(t, c):
        for k in range(2):
            _row_copy(h_ref, t, xs_ref, dest_ref[0, k, t], sem).start(priority=k)
        return c

    lax.fori_loop(0, MOE_TT, issue, 0)

    def drain(t, c):
        for k in range(2):
            _row_copy(h_ref, 0, xs_ref, 0, sem).wait()
        return c

    lax.fori_loop(0, MOE_TT, drain, 0)


def _dispatch(dest, h2):
    t = h2.shape[0]
    xs0 = jnp.zeros((_moe_rows(t), D_MODEL), F32)
    return pl.pallas_call(
        _dispatch_kernel,
        grid=(t // MOE_TT,),
        in_specs=[
            pl.BlockSpec((1, 2, MOE_TT), lambda i: (i, 0, 0), memory_space=pltpu.SMEM),
            pl.BlockSpec((MOE_TT, D_MODEL), lambda i: (i, 0)),
            pl.BlockSpec(memory_space=pl.ANY),
        ],
        out_specs=pl.BlockSpec(memory_space=pl.ANY),
        out_shape=jax.ShapeDtypeStruct(xs0.shape, F32),
        scratch_shapes=[pltpu.SemaphoreType.DMA(())],
        input_output_aliases={2: 0},
        compiler_params=_cp(("arbitrary",)),
        name="moe_dispatch",
    )(dest, h2, xs0)


def _experts_kernel(te_ref, nt_ref, xs_ref, wg_ref, wu_ref, wd_ref, y_ref, acc_ref):
    j = pl.program_id(1)

    @pl.when(pl.program_id(0) < nt_ref[0])
    def _():
        h = xs_ref[...].astype(BF16)
        gate = _dot(h, wg_ref[0])
        up = _dot(h, wu_ref[0])
        act = (gate * _sigmoid(gate) * up).astype(BF16)
        part = _dot(act, wd_ref[0])

        @pl.when(j == 0)
        def _():
            acc_ref[...] = part

        @pl.when(j > 0)
        def _():
            acc_ref[...] += part

        @pl.when(j == pl.num_programs(1) - 1)
        def _():
            y_ref[...] = acc_ref[...]

    @pl.when((pl.program_id(0) >= nt_ref[0]) & (j == pl.num_programs(1) - 1))
    def _():
        y_ref[...] = jnp.zeros_like(y_ref)


def _experts(tile_expert, n_tiles_used, xs, wg, wu, wd, fc):
    ne, _, dff = wg.shape
    grid_spec = pltpu.PrefetchScalarGridSpec(
        num_scalar_prefetch=2,
        grid=(xs.shape[0] // MOE_TM, dff // fc),
        in_specs=[
            pl.BlockSpec((MOE_TM, D_MODEL), lambda i, j, te, nt: (i, 0)),
            pl.BlockSpec((1, D_MODEL, fc), lambda i, j, te, nt: (te[i], 0, j)),
            pl.BlockSpec((1, D_MODEL, fc), lambda i, j, te, nt: (te[i], 0, j)),
            pl.BlockSpec((1, fc, D_MODEL), lambda i, j, te, nt: (te[i], j, 0)),
        ],
        out_specs=pl.BlockSpec((MOE_TM, D_MODEL), lambda i, j, te, nt: (i, 0)),
        scratch_shapes=[pltpu.VMEM((MOE_TM, D_MODEL), F32)],
    )
    return pl.pallas_call(
        _experts_kernel,
        grid_spec=grid_spec,
        out_shape=jax.ShapeDtypeStruct(xs.shape, F32),
        compiler_params=_cp(("arbitrary", "arbitrary")),
        name="moe_experts",
    )(tile_expert, n_tiles_used, xs, wg, wu, wd)


def _combine_kernel(dest_ref, dest_next_ref, y_ref, x1_ref, rinfo_ref, mod_ref, g_ref, b_ref, o_ref,
                    ybuf_ref, sems):
    i = pl.program_id(0)
    slot = i % 2

    def gather(d_ref, s):
        def issue(t, c):
            for k in range(2):
                _row_copy(y_ref, d_ref[0, k, t], ybuf_ref.at[s, k], t, sems.at[s]).start(priority=k)
            return c
        lax.fori_loop(0, MOE_TT, issue, 0)

    @pl.when(i == 0)
    def _():
        gather(dest_ref, slot)

    @pl.when(i + 1 < pl.num_programs(0))
    def _():
        gather(dest_next_ref, 1 - slot)

    def drain(t, c):
        for k in range(2):
            _row_copy(y_ref, 0, ybuf_ref.at[slot, k], 0, sems.at[slot]).wait()
        return c

    lax.fori_loop(0, MOE_TT, drain, 0)

    r = rinfo_ref[...]
    lane = lax.broadcasted_iota(jnp.int32, r.shape, 1)
    w1 = jnp.sum(jnp.where(lane == RI_W1, r, 0.0), axis=1, keepdims=True)
    w2 = jnp.sum(jnp.where(lane == RI_W2, r, 0.0), axis=1, keepdims=True)
    f = w1 * ybuf_ref[slot, 0] + w2 * ybuf_ref[slot, 1]
    mod = mod_ref[0]
    o_ref[...] = _ln(ALPHA * x1_ref[...] + mod[5:6] * f) * g_ref[...] + b_ref[...]


def _combine(dest, y, x1, rinfo, mod, g, b):
    t = x1.shape[0]
    nb = mod.shape[0]
    tpb = t // nb // MOE_TT
    row = lambda i: (i, 0)
    const = lambda i: (0, 0)
    n_steps = t // MOE_TT
    return pl.pallas_call(
        _combine_kernel,
        grid=(n_steps,),
        in_specs=[
            pl.BlockSpec((1, 2, MOE_TT), lambda i: (i, 0, 0), memory_space=pltpu.SMEM),
            pl.BlockSpec((1, 2, MOE_TT), lambda i: (jnp.minimum(i + 1, n_steps - 1), 0, 0),
                         memory_space=pltpu.SMEM),
            pl.BlockSpec(memory_space=pl.ANY),
            pl.BlockSpec((MOE_TT, D_MODEL), row),
            pl.BlockSpec((MOE_TT, LANES), row),
            pl.BlockSpec((1, 8, D_MODEL), lambda i: (i // tpb, 0, 0)),
            pl.BlockSpec((1, D_MODEL), const),
            pl.BlockSpec((1, D_MODEL), const),
        ],
        out_specs=pl.BlockSpec((MOE_TT, D_MODEL), row),
        out_shape=jax.ShapeDtypeStruct((t, D_MODEL), F32),
        scratch_shapes=[pltpu.VMEM((2, 2, MOE_TT, D_MODEL), F32), pltpu.SemaphoreType.DMA((2,))],
        compiler_params=_cp(("arbitrary",)),
        name="moe_combine",
    )(dest, dest, y, x1, rinfo, mod, g, b)


def _moe(h2, x1, rinfo, mod, g, b, wg, wu, wd):
    t = h2.shape[0]
    route, counts = _route_scan(rinfo)
    cnt = counts[0, :N_EXPERTS].astype(jnp.int32)
    padded = (cnt + MOE_TM - 1) // MOE_TM * MOE_TM
    ends = jnp.cumsum(padded)
    starts = ends - padded
    dest = jnp.stack([starts[route[:, 0]] + route[:, 2], starts[route[:, 1]] + route[:, 3]])
    dest = dest.reshape(2, t // MOE_TT, MOE_TT).transpose(1, 0, 2)
    tile_row0 = jnp.arange(_moe_rows(t) // MOE_TM, dtype=jnp.int32) * MOE_TM
    tile_expert = jnp.minimum(jnp.sum(tile_row0[:, None] >= ends[None, :], axis=1), N_EXPERTS - 1).astype(jnp.int32)
    n_tiles_used = (ends[-1:] // MOE_TM).astype(jnp.int32)
    xs = _dispatch(dest, h2)
    y = _experts(tile_expert, n_tiles_used, xs, wg, wu, wd, fc=D_FF_EXPERT // 2)
    return _combine(dest, y, x1, rinfo, mod, g, b)


def _rope_np(n_tokens):
    rows = np.repeat(np.arange(n_tokens // GRID_W), GRID_W)
    cols = np.tile(np.arange(GRID_W), n_tokens // GRID_W)
    pos = np.stack([rows, cols], axis=-1).astype(np.float64)
    inv_freq = 1.0 / (ROPE_THETA ** (np.arange(ROPE_FREQS, dtype=np.float64) / ROPE_FREQS))
    ang = pos[..., None] * inv_freq
    cos, sin = np.cos(ang), np.sin(ang)
    cos32 = np.broadcast_to(cos[:, :, None, :], (n_tokens, 2, 2, ROPE_FREQS)).reshape(n_tokens, QK_ROPE)
    sgn = np.array([-1.0, 1.0])[None, None, :, None]
    sin32 = (np.broadcast_to(sin[:, :, None, :], (n_tokens, 2, 2, ROPE_FREQS)) * sgn).reshape(n_tokens, QK_ROPE)
    return cos32, sin32


def _half_swap_perm():
    p = np.arange(QK_ROPE).reshape(2, 2, ROPE_FREQS)
    return p[:, ::-1, :].reshape(QK_ROPE)


def _tables():
    c0 = SM_SCALE * math.log2(math.e)
    cos32, sin32 = _rope_np(SEQ)
    one64 = np.ones((SEQ, QK_NOPE))
    tq_lat = c0 * np.concatenate([one64, cos32, sin32], axis=1)
    tk_lat = np.concatenate([cos32, sin32, np.zeros((SEQ, LANES - 2 * QK_ROPE))], axis=1)
    tq_ctx = c0 * np.concatenate([np.ones((CTX_LEN, QK_NOPE + QK_ROPE)), np.zeros((CTX_LEN, QK_ROPE))], axis=1)
    tk_ctx = np.concatenate([np.ones((CTX_LEN, QK_ROPE)), np.zeros((CTX_LEN, LANES - QK_ROPE))], axis=1)
    f32 = lambda a: jnp.asarray(a, F32)
    bf = lambda a: jnp.asarray(a, BF16)

    def dft(n):
        k = np.arange(n)
        ang = 2.0 * np.pi * ((k[:, None] * k[None, :]) % n) / n
        return np.cos(ang), np.sin(ang)

    c128, s128 = dft(FFT_R)
    c256, s256 = dft(CTX_LEN)
    bd_lat = np.concatenate([c128, -s128], axis=1) / math.sqrt(SEQ * F_GROUP_DIM)
    bd_ctx = np.concatenate([c128, -s128], axis=1) / math.sqrt(CTX_LEN * F_GROUP_DIM)
    k1 = np.arange(FFT_R)
    tw = 2.0 * np.pi * ((k1[:, None] * k1[None, :]) % SEQ) / SEQ
    tc = jnp.broadcast_to(f32(np.cos(tw))[:, :, None], (FFT_R, FFT_R, LANES)).reshape(FFT_R, FFT_R * LANES)
    ts = jnp.broadcast_to(f32(np.sin(tw))[:, :, None], (FFT_R, FFT_R, LANES)).reshape(FFT_R, FFT_R * LANES)
    return dict(tq_lat=f32(tq_lat), tk_lat=f32(tk_lat), tq_ctx=f32(tq_ctx), tk_ctx=f32(tk_ctx),
                c128=bf(c128), s128=bf(s128), c256=bf(c256), s256=bf(s256),
                bd_lat=bf(bd_lat), bd_ctx=bf(bd_ctx), tc=tc, ts=ts)


def _arrange_weights(l, w_in, w_uq, w_uk, w_uv):
    sw = _half_swap_perm()
    wi = w_in[l]
    kr = wi[:, OFF_KR:OFF_F]
    w_ip = jnp.concatenate([
        wi[:, :OFF_KR], kr, kr[:, sw], jnp.zeros((D_MODEL, LANES - 2 * QK_ROPE), F32),
        wi[:, OFF_F:]], axis=1).astype(BF16)
    uq = w_uq[l].reshape(Q_LORA, N_HEADS, QK_NOPE + QK_ROPE)
    qr = uq[:, :, QK_NOPE:]
    wq = jnp.concatenate([uq, qr[:, :, sw]], axis=-1).reshape(Q_LORA, N_HEADS * HEAD_PAD).astype(BF16)
    uk = w_uk[l].reshape(KV_LORA, N_HEADS, QK_NOPE)
    top = jnp.concatenate([uk, jnp.zeros((KV_LORA, N_HEADS, HEAD_PAD - QK_NOPE), F32)], axis=-1)
    rmat = np.zeros((LANES, HEAD_PAD), np.float32)
    for i in range(QK_ROPE):
        for r in (i, QK_ROPE + i):
            rmat[r, QK_NOPE + i] = 1.0
            rmat[r, QK_NOPE + QK_ROPE + i] = 1.0
    bot = jnp.broadcast_to(jnp.asarray(rmat)[:, None, :], (LANES, N_HEADS, HEAD_PAD))
    wkt = jnp.transpose(jnp.concatenate([top, bot], axis=0), (1, 2, 0)).astype(BF16)
    uv = w_uv[l].reshape(KV_LORA, N_HEADS // 2, 2, V_HEAD)
    z = jnp.zeros_like(uv[:, :, 0])
    even = jnp.concatenate([uv[:, :, 0], z], axis=-1)
    odd = jnp.concatenate([z, uv[:, :, 1]], axis=-1)
    wv = jnp.stack([even, odd], axis=2).reshape(KV_LORA, N_HEADS * HEAD_PAD).astype(BF16)
    return w_ip, wq, wkt, wv


def kernel(x, c, ctx, c_ctx, w_mod, b_mod, w_in, g_q, w_uq, g_kv, w_uk, w_uv, w_ao, w_fo, w_out,
           ln1_g, ln1_b, ln2_g, ln2_b, w_ff_gate, w_ff_up, w_ff_down,
           w_router, b_router, w_e_gate, w_e_up, w_e_down):
    tb = _tables()
    t_lat = BATCH * SEQ
    t_ctx = BATCH * CTX_LEN

    cs = jnp.concatenate([c, c_ctx[None, :], jnp.zeros((8 - BATCH - 1, D_MODEL), F32)], axis=0)
    mods = _modulation(cs, w_mod.astype(BF16), b_mod).reshape(DEPTH, 8, 6, D_MODEL)
    pad2 = jnp.zeros((DEPTH, 8, 2, D_MODEL), F32)
    mods = jnp.concatenate([mods, pad2], axis=2)

    xl = x.reshape(t_lat, D_MODEL)
    xc = ctx.reshape(t_ctx, D_MODEL)
    ones_comb = jnp.ones((t_lat, LANES), F32)
    ones_comb_c = jnp.ones((t_ctx, LANES), F32)

    for l in range(DEPTH):
        last = l == DEPTH - 1
        mod_x = mods[l, :BATCH]
        mod_c = jnp.broadcast_to(mods[l, BATCH:BATCH + 1], (BATCH, 8, D_MODEL))
        w_ip, wq, wkt, wv = _arrange_weights(l, w_in, w_uq, w_uk, w_uv)
        gq = g_q[l][None, :]
        gkv = g_kv[l][None, :]
        wao = w_ao[l].astype(BF16)
        wfo = w_fo[l].astype(BF16)
        wout = w_out[l].astype(BF16)
        g1, b1 = ln1_g[l][None, :], ln1_b[l][None, :]
        g2, b2 = ln2_g[l][None, :], ln2_b[l][None, :]

        qlat_c, ckr_c, wf_c, gates_c = _inproj(xc, mod_c, w_ip, gq, gkv, tb["tk_ctx"], tb["bd_ctx"], tm=CTX_LEN)
        q_c, kt_c, v_c = _qkv(qlat_c, ckr_c, tb["tq_ctx"], wq, wkt, wv, nb=BATCH, tm=CTX_LEN)

        qlat, ckr, wf, gates = _inproj(xl, mod_x, w_ip, gq, gkv, tb["tk_lat"], tb["bd_lat"], tm=512)
        q, kt, v = _qkv(qlat, ckr, tb["tq_lat"], wq, wkt, wv, nb=BATCH, tm=512)
        attn = _attention(q, kt_c, v_c, kt, v, tq=ATTN_TQ, tk=ATTN_TK, unroll=ATTN_UNROLL)
        four = _fourier_latent(wf.reshape(BATCH, SEQ, 2 * FOURIER_DIM), tb["c128"], tb["s128"], tb["tc"], tb["ts"])

        if l % 2 == 0:
            i = l // 2
            x1, h2 = _mixer(attn.reshape(t_lat, -1), four.reshape(t_lat, -1), gates, xl, mod_x, g1, b1,
                            wao, wfo, wout, tm=512)
            wg = w_ff_gate[i][None].astype(BF16)
            wu = w_ff_up[i][None].astype(BF16)
            wd = w_ff_down[i][None].astype(BF16)
            xl_new = _ffn(h2, x1, ones_comb, mod_x, g2, b2, wg, wu, wd, tm=512, fc=D_FF // 2)
        else:
            i = l // 2
            wr = jnp.concatenate([w_router[i], jnp.zeros((D_MODEL, LANES - N_EXPERTS), F32)], axis=1)
            br = jnp.concatenate([b_router[i], jnp.zeros((LANES - N_EXPERTS,), F32)])[None, :]
            x1, h2, rinfo = _mixer(attn.reshape(t_lat, -1), four.reshape(t_lat, -1), gates, xl, mod_x, g1, b1,
                                   wao, wfo, wout, tm=512, router=(wr, br))
            wg = w_e_gate[i].astype(BF16)
            wu = w_e_up[i].astype(BF16)
            wd = w_e_down[i].astype(BF16)
            xl_new = _moe(h2, x1, rinfo, mod_x, g2, b2, wg, wu, wd)

        if not last:
            assert l % 2 == 0
            attn_c = _attention(q_c, kt_c, v_c, None, None, tq=CTX_LEN, tk=CTX_LEN)
            four_c = _fourier_ctx(wf_c.reshape(BATCH, CTX_LEN, 2 * FOURIER_DIM), tb["c256"], tb["s256"])
            x1c, h2c = _mixer(attn_c.reshape(t_ctx, -1), four_c.reshape(t_ctx, -1), gates_c, xc, mod_c,
                              g1, b1, wao, wfo, wout, tm=CTX_LEN)
            xc = _ffn(h2c, x1c, ones_comb_c, mod_c, g2, b2, wg, wu, wd, tm=CTX_LEN, fc=D_FF // 2)
        xl = xl_new

    return xl.reshape(BATCH, SEQ, D_MODEL)
```

```python
import functools
import math

import numpy as np
import jax
import jax.numpy as jnp
from jax import lax
from jax.experimental import pallas as pl
from jax.experimental.pallas import tpu as pltpu

D_MODEL = 1024
BATCH = 2
SEQ = 16384
DEPTH = 2
GRID_W = 64
CTX_LEN = 256
N_HEADS = 16
QK_NOPE = 64
QK_ROPE = 32
ROPE_FREQS = QK_ROPE // 4
V_HEAD = 64
Q_LORA = 256
KV_LORA = 128
ROPE_THETA = 10000.0
SM_SCALE = (QK_NOPE + QK_ROPE) ** -0.5
F_GROUPS = 4
F_GROUP_DIM = 128
FOURIER_DIM = F_GROUPS * F_GROUP_DIM
OFF_KV = Q_LORA
OFF_KR = OFF_KV + KV_LORA
OFF_F = OFF_KR + QK_ROPE
OFF_G = OFF_F + FOURIER_DIM
D_FF = 2816
N_EXPERTS = 8
D_FF_EXPERT = 3584
ALPHA = (2 * DEPTH) ** 0.25
LN_EPS = 1e-6
RMS_EPS = 1e-6

LANES = 128
HEAD_PAD = 128
FFT_R = 128
V_ONES_LANE = (V_HEAD, 0)
VMEM_LIMIT = 56 * 1024 * 1024
ATTN_TQ = 1024
ATTN_TK = 512
ATTN_UNROLL = 8

BF16 = jnp.bfloat16
F32 = jnp.float32


def _cp(sem, vmem=VMEM_LIMIT):
    return pltpu.CompilerParams(dimension_semantics=sem, vmem_limit_bytes=vmem)


def _dot(a, b):
    return jnp.dot(a, b, preferred_element_type=F32)


def _ln(x):
    mu = jnp.mean(x, axis=-1, keepdims=True)
    xc = x - mu
    var = jnp.mean(xc * xc, axis=-1, keepdims=True)
    return xc * lax.rsqrt(var + LN_EPS)


def _rms(x, g):
    return x * lax.rsqrt(jnp.mean(x * x, axis=-1, keepdims=True) + RMS_EPS) * g


def _sigmoid(x):
    return 1.0 / (1.0 + jnp.exp(-x))


def _mod_kernel(cs_ref, w_ref, b_ref, o_ref):
    cs = cs_ref[...]
    a = (cs * _sigmoid(cs)).astype(BF16)
    o_ref[0] = _dot(a, w_ref[0]) + b_ref[0]


def _modulation(cs, w_mod, b_mod):
    n_chunk = 1024
    n_out = w_mod.shape[-1]
    return pl.pallas_call(
        _mod_kernel,
        grid=(DEPTH, n_out // n_chunk),
        in_specs=[
            pl.BlockSpec((8, D_MODEL), lambda l, j: (0, 0)),
            pl.BlockSpec((1, D_MODEL, n_chunk), lambda l, j: (l, 0, j)),
            pl.BlockSpec((1, 1, n_chunk), lambda l, j: (l, 0, j)),
        ],
        out_specs=pl.BlockSpec((1, 8, n_chunk), lambda l, j: (l, 0, j)),
        out_shape=jax.ShapeDtypeStruct((DEPTH, 8, n_out), F32),
        compiler_params=_cp(("parallel", "parallel")),
        name="mod",
    )(cs, w_mod, b_mod.reshape(DEPTH, 1, n_out))


IP_Q = 0
IP_KV = Q_LORA
IP_KR = IP_KV + KV_LORA
IP_F = IP_KR + LANES
IP_G = IP_F + FOURIER_DIM
IP_COLS = IP_G + 2 * D_MODEL


def _inproj_kernel(x_ref, mod_ref, w_ref, gq_ref, gkv_ref, tk_ref, bd_ref,
                   qlat_ref, ckr_ref, wf_ref, gates_ref):
    x = x_ref[...]
    mod = mod_ref[0]
    h = (_ln(x) * (1.0 + mod[1:2]) + mod[0:1]).astype(BF16)
    p0 = _dot(h, w_ref[:, IP_Q:IP_F])
    qlat_ref[...] = _rms(p0[:, IP_Q:IP_KV], gq_ref[...]).astype(BF16)
    ckr_ref[:, 0:KV_LORA] = _rms(p0[:, IP_KV:IP_KR], gkv_ref[...]).astype(BF16)
    ckr_ref[:, KV_LORA:] = (p0[:, IP_KR:IP_F] * tk_ref[...]).astype(BF16)
    uf = _dot(h, w_ref[:, IP_F:IP_G]).astype(BF16)
    for g in range(F_GROUPS):
        r = _dot(uf[:, g * LANES:(g + 1) * LANES], bd_ref[...])
        wf_ref[:, g * LANES:(g + 1) * LANES] = r[:, :LANES].astype(BF16)
        wf_ref[:, FOURIER_DIM + g * LANES:FOURIER_DIM + (g + 1) * LANES] = r[:, LANES:].astype(BF16)
    gc = 512
    for c in range(2 * D_MODEL // gc):
        gl = _dot(h, w_ref[:, IP_G + c * gc:IP_G + (c + 1) * gc])
        gates_ref[:, c * gc:(c + 1) * gc] = _sigmoid(gl).astype(BF16)


def _inproj(x2d, mod, w, gq, gkv, tk, bd, tm):
    t = x2d.shape[0]
    nb = mod.shape[0]
    tpb = t // nb // tm
    return pl.pallas_call(
        _inproj_kernel,
        grid=(t // tm,),
        in_specs=[
            pl.BlockSpec((tm, D_MODEL), lambda i: (i, 0)),
            pl.BlockSpec((1, 8, D_MODEL), lambda i: (i // tpb, 0, 0)),
            pl.BlockSpec((D_MODEL, IP_COLS), lambda i: (0, 0)),
            pl.BlockSpec((1, Q_LORA), lambda i: (0, 0)),
            pl.BlockSpec((1, KV_LORA), lambda i: (0, 0)),
            pl.BlockSpec((tm, LANES), lambda i: (i % tpb, 0)),
            pl.BlockSpec((LANES, 2 * LANES), lambda i: (0, 0)),
        ],
        out_specs=[
            pl.BlockSpec((tm, Q_LORA), lambda i: (i, 0)),
            pl.BlockSpec((tm, 2 * LANES), lambda i: (i, 0)),
            pl.BlockSpec((tm, 2 * FOURIER_DIM), lambda i: (i, 0)),
            pl.BlockSpec((tm, 2 * D_MODEL), lambda i: (i, 0)),
        ],
        out_shape=[
            jax.ShapeDtypeStruct((t, Q_LORA), BF16),
            jax.ShapeDtypeStruct((t, 2 * LANES), BF16),
            jax.ShapeDtypeStruct((t, 2 * FOURIER_DIM), BF16),
            jax.ShapeDtypeStruct((t, 2 * D_MODEL), BF16),
        ],
        compiler_params=_cp(("parallel",)),
        name="inproj",
    )(x2d, mod, w, gq, gkv, tk, bd)


def _qkv_kernel(qlat_ref, ckr_ref, tq_ref, wq_ref, wkt_ref, wv_ref, q_ref, kt_ref, v_ref):
    ql = qlat_ref[...]
    ckr = ckr_ref[...]
    tq = tq_ref[...]
    ckv = ckr[:, :KV_LORA]
    lane = lax.broadcasted_iota(jnp.int32, (ckr.shape[0], HEAD_PAD), 1)
    for h in range(N_HEADS):
        cols = slice(h * HEAD_PAD, (h + 1) * HEAD_PAD)
        q_ref[0, h] = (_dot(ql, wq_ref[:, cols]) * tq).astype(BF16)
        kt_ref[0, h] = lax.dot_general(
            wkt_ref[h], ckr, (((1,), (1,)), ((), ())), preferred_element_type=F32).astype(BF16)
        vh = _dot(ckv, wv_ref[:, cols])
        v_ref[0, h] = jnp.where(lane == V_ONES_LANE[h % 2], 1.0, vh).astype(BF16)


def _qkv(qlat, ckr, tq, wq, wkt, wv, nb, tm):
    t = qlat.shape[0]
    n = t // nb
    tpb = n // tm
    return pl.pallas_call(
        _qkv_kernel,
        grid=(t // tm,),
        in_specs=[
            pl.BlockSpec((tm, Q_LORA), lambda i: (i, 0)),
            pl.BlockSpec((tm, 2 * LANES), lambda i: (i, 0)),
            pl.BlockSpec((tm, HEAD_PAD), lambda i: (i % tpb, 0)),
            pl.BlockSpec((Q_LORA, N_HEADS * HEAD_PAD), lambda i: (0, 0)),
            pl.BlockSpec((N_HEADS, HEAD_PAD, 2 * LANES), lambda i: (0, 0, 0)),
            pl.BlockSpec((KV_LORA, N_HEADS * HEAD_PAD), lambda i: (0, 0)),
        ],
        out_specs=[
            pl.BlockSpec((1, N_HEADS, tm, HEAD_PAD), lambda i: (i // tpb, 0, i % tpb, 0)),
            pl.BlockSpec((1, N_HEADS, HEAD_PAD, tm), lambda i: (i // tpb, 0, 0, i % tpb)),
            pl.BlockSpec((1, N_HEADS, tm, HEAD_PAD), lambda i: (i // tpb, 0, i % tpb, 0)),
        ],
        out_shape=[
            jax.ShapeDtypeStruct((nb, N_HEADS, n, HEAD_PAD), BF16),
            jax.ShapeDtypeStruct((nb, N_HEADS, HEAD_PAD, n), BF16),
            jax.ShapeDtypeStruct((nb, N_HEADS, n, HEAD_PAD), BF16),
        ],
        compiler_params=_cp(("parallel",)),
        name="qkv",
    )(qlat, ckr, tq, wq, wkt, wv)


def _attn_kernel(q_ref, ktc_ref, vc_ref, *rest, n_chunks, tk, unroll):
    if n_chunks:
        kt_ref, v_ref, o_ref = rest
    else:
        (o_ref,) = rest

    def step(q, kt, v, m, acc):
        s = _dot(q, kt)
        m_new = jnp.max(s, axis=1, keepdims=True)
        if m is not None:
            m_new = jnp.maximum(m, m_new)
        p = jnp.exp2((s - m_new).astype(BF16))
        pv = _dot(p, v)
        if m is not None:
            pv = jnp.exp2(m - m_new) * acc + pv
        return m_new, pv

    qs = [q_ref[0, hh] for hh in range(2)]
    carry = []
    for hh in range(2):
        carry += step(qs[hh], ktc_ref[0, hh], vc_ref[0, hh], None, None)

    if n_chunks:
        def body(j, carry):
            off = pl.multiple_of(j * tk, tk)
            out = []
            for hh in range(2):
                out += step(qs[hh], kt_ref[0, hh, :, pl.ds(off, tk)], v_ref[0, hh, pl.ds(off, tk), :],
                            carry[2 * hh], carry[2 * hh + 1])
            return tuple(out)

        carry = lax.fori_loop(0, n_chunks, body, tuple(carry), unroll=unroll)

    lane = lax.broadcasted_iota(jnp.int32, carry[1].shape, 1)
    outs = []
    for hh in range(2):
        acc = carry[2 * hh + 1]
        l = jnp.sum(jnp.where(lane == V_ONES_LANE[hh], acc, 0.0), axis=1, keepdims=True)
        outs.append(acc * (1.0 / l))
    o_ref[0] = jnp.where(lane < V_HEAD, outs[0], outs[1]).astype(BF16)


def _attention(q, kt_c, v_c, kt, v, tq, tk, unroll=1):
    nb, _, n, _ = q.shape
    nc = kt_c.shape[-1]
    n_chunks = 0 if kt is None else kt.shape[-1] // tk
    in_specs = [
        pl.BlockSpec((1, 2, tq, HEAD_PAD), lambda b, p, i: (b, p, i, 0)),
        pl.BlockSpec((1, 2, HEAD_PAD, nc), lambda b, p, i: (b, p, 0, 0)),
        pl.BlockSpec((1, 2, nc, HEAD_PAD), lambda b, p, i: (b, p, 0, 0)),
    ]
    args = [q, kt_c, v_c]
    if n_chunks:
        nk = kt.shape[-1]
        in_specs += [
            pl.BlockSpec((1, 2, HEAD_PAD, nk), lambda b, p, i: (b, p, 0, 0)),
            pl.BlockSpec((1, 2, nk, HEAD_PAD), lambda b, p, i: (b, p, 0, 0)),
        ]
        args += [kt, v]
    return pl.pallas_call(
        functools.partial(_attn_kernel, n_chunks=n_chunks, tk=tk, unroll=unroll),
        grid=(nb, N_HEADS // 2, n // tq),
        in_specs=in_specs,
        out_specs=pl.BlockSpec((1, tq, 2 * V_HEAD), lambda b, p, i: (b, i, p)),
        out_shape=jax.ShapeDtypeStruct((nb, n, N_HEADS * V_HEAD), BF16),
        compiler_params=_cp(("parallel", "parallel", "arbitrary")),
        name="attn",
    )(*args)


def _fft1_kernel(x_ref, c_ref, s_ref, tc_ref, ts_ref, o_ref, *, n2t):
    x = x_ref[0]
    cx = _dot(c_ref[...], x)
    sx = _dot(s_ref[...], x)
    w = 2 * FOURIER_DIM
    for t in range(n2t):
        re = slice(t * w, t * w + FOURIER_DIM)
        im = slice(t * w + FOURIER_DIM, (t + 1) * w)
        yr = cx[:, re] + sx[:, im]
        yi = cx[:, im] - sx[:, re]
        tc = jnp.concatenate([tc_ref[:, t * LANES:(t + 1) * LANES]] * F_GROUPS, axis=1)
        ts = jnp.concatenate([ts_ref[:, t * LANES:(t + 1) * LANES]] * F_GROUPS, axis=1)
        o_ref[0, :, re] = (yr * tc + yi * ts).astype(BF16)
        o_ref[0, :, im] = (yi * tc - yr * ts).astype(BF16)


def _fft2_kernel(y_ref, c_ref, s_ref, o_ref, *, k1t):
    for t in range(k1t):
        y = y_ref[0, t]
        zr = _dot(c_ref[...], y[:, :FOURIER_DIM]) + _dot(s_ref[...], y[:, FOURIER_DIM:])
        o_ref[0, :, t * FOURIER_DIM:(t + 1) * FOURIER_DIM] = zr.astype(BF16)


def _fourier_latent(wf, cmat, smat, tc, ts):
    nb = wf.shape[0]
    w = 2 * FOURIER_DIM
    n2t = 8
    y = pl.pallas_call(
        functools.partial(_fft1_kernel, n2t=n2t),
        grid=(nb, FFT_R // n2t),
        in_specs=[
            pl.BlockSpec((1, FFT_R, n2t * w), lambda b, j: (b, 0, j)),
            pl.BlockSpec((FFT_R, FFT_R), lambda b, j: (0, 0)),
            pl.BlockSpec((FFT_R, FFT_R), lambda b, j: (0, 0)),
            pl.BlockSpec((FFT_R, n2t * LANES), lambda b, j: (0, j)),
            pl.BlockSpec((FFT_R, n2t * LANES), lambda b, j: (0, j)),
        ],
        out_specs=pl.BlockSpec((1, FFT_R, n2t * w), lambda b, j: (b, 0, j)),
        out_shape=jax.ShapeDtypeStruct((nb, FFT_R, FFT_R * w), BF16),
        compiler_params=_cp(("parallel", "parallel")),
        name="fft1",
    )(wf.reshape(nb, FFT_R, FFT_R * w), cmat, smat, tc, ts)
    k1t = 8
    four = pl.pallas_call(
        functools.partial(_fft2_kernel, k1t=k1t),
        grid=(nb, FFT_R // k1t),
        in_specs=[
            pl.BlockSpec((1, k1t, FFT_R, w), lambda b, j: (b, j, 0, 0)),
            pl.BlockSpec((FFT_R, FFT_R), lambda b, j: (0, 0)),
            pl.BlockSpec((FFT_R, FFT_R), lambda b, j: (0, 0)),
        ],
        out_specs=pl.BlockSpec((1, FFT_R, k1t * FOURIER_DIM), lambda b, j: (b, 0, j)),
        out_shape=jax.ShapeDtypeStruct((nb, FFT_R, FFT_R * FOURIER_DIM), BF16),
        compiler_params=_cp(("parallel", "parallel")),
        name="fft2",
    )(y.reshape(nb, FFT_R, FFT_R, w), cmat, smat)
    return four.reshape(nb, SEQ, FOURIER_DIM)


def _dft_ctx_kernel(x_ref, c_ref, s_ref, o_ref):
    x = x_ref[0]
    zr = _dot(c_ref[...], x[:, :FOURIER_DIM]) + _dot(s_ref[...], x[:, FOURIER_DIM:])
    o_ref[0] = zr.astype(BF16)


def _fourier_ctx(wf, cmat, smat):
    nb, n, w = wf.shape
    return pl.pallas_call(
        _dft_ctx_kernel,
        grid=(nb,),
        in_specs=[
            pl.BlockSpec((1, n, w), lambda b: (b, 0, 0)),
            pl.BlockSpec((n, n), lambda b: (0, 0)),
            pl.BlockSpec((n, n), lambda b: (0, 0)),
        ],
        out_specs=pl.BlockSpec((1, n, FOURIER_DIM), lambda b: (b, 0, 0)),
        out_shape=jax.ShapeDtypeStruct((nb, n, FOURIER_DIM), BF16),
        compiler_params=_cp(("parallel",)),
        name="dft_ctx",
    )(wf, cmat, smat)


def _mixer_kernel(attn_ref, four_ref, gates_ref, x_ref, mod_ref, g_ref, b_ref,
                  wao_ref, wfo_ref, wout_ref, *rest, route):
    if route:
        wr_ref, br_ref, x1_ref, h2_ref, rinfo_ref = rest
    else:
        x1_ref, h2_ref = rest
    mod = mod_ref[0]
    a = _dot(attn_ref[...], wao_ref[...])
    f = _dot(four_ref[...], wfo_ref[...])
    merged = gates_ref[:, :D_MODEL].astype(F32) * a + gates_ref[:, D_MODEL:].astype(F32) * f
    y = _dot(merged.astype(BF16), wout_ref[...])
    x1 = _ln(ALPHA * x_ref[...] + mod[2:3] * y) * g_ref[...] + b_ref[...]
    x1_ref[...] = x1
    h2 = _ln(x1) * (1.0 + mod[4:5]) + mod[3:4]
    h2_ref[...] = h2.astype(h2_ref.dtype)
    if route:
        logits = jnp.dot(h2, wr_ref[...], preferred_element_type=F32,
                         precision=lax.Precision.HIGHEST) + br_ref[...]
        lane = lax.broadcasted_iota(jnp.int32, logits.shape, 1)
        neg = jnp.float32(-jnp.inf)
        lg = jnp.where(lane < N_EXPERTS, logits, neg)
        m1 = jnp.max(lg, axis=1, keepdims=True)
        i1 = jnp.min(jnp.where(lg == m1, lane, LANES), axis=1, keepdims=True)
        lg2 = jnp.where(lane == i1, neg, lg)
        m2 = jnp.max(lg2, axis=1, keepdims=True)
        i2 = jnp.min(jnp.where(lg2 == m2, lane, LANES), axis=1, keepdims=True)
        e2 = jnp.exp(m2 - m1)
        w1 = 1.0 / (1.0 + e2)
        w2 = e2 * w1
        rinfo = jnp.where((lane == i1) | (lane == i2), 1.0, 0.0)
        rinfo = jnp.where(lane == RI_E1, i1.astype(F32), rinfo)
        rinfo = jnp.where(lane == RI_E2, i2.astype(F32), rinfo)
        rinfo = jnp.where(lane == RI_W1, w1, rinfo)
        rinfo_ref[...] = jnp.where(lane == RI_W2, w2, rinfo)


def _mixer(attn, four, gates, x2d, mod, g, b, wao, wfo, wout, tm, router=None):
    t = x2d.shape[0]
    nb = mod.shape[0]
    tpb = t // nb // tm
    row = lambda i: (i, 0)
    const = lambda i: (0, 0)
    in_specs = [
        pl.BlockSpec((tm, D_MODEL), row),
        pl.BlockSpec((tm, FOURIER_DIM), row),
        pl.BlockSpec((tm, 2 * D_MODEL), row),
        pl.BlockSpec((tm, D_MODEL), row),
        pl.BlockSpec((1, 8, D_MODEL), lambda i: (i // tpb, 0, 0)),
        pl.BlockSpec((1, D_MODEL), const),
        pl.BlockSpec((1, D_MODEL), const),
        pl.BlockSpec((D_MODEL, D_MODEL), const),
        pl.BlockSpec((FOURIER_DIM, D_MODEL), const),
        pl.BlockSpec((D_MODEL, D_MODEL), const),
    ]
    args = [attn, four, gates, x2d, mod, g, b, wao, wfo, wout]
    out_specs = [pl.BlockSpec((tm, D_MODEL), row), pl.BlockSpec((tm, D_MODEL), row)]
    h2_dtype = BF16 if router is None else F32
    out_shape = [jax.ShapeDtypeStruct((t, D_MODEL), F32), jax.ShapeDtypeStruct((t, D_MODEL), h2_dtype)]
    if router is not None:
        in_specs += [pl.BlockSpec((D_MODEL, LANES), const), pl.BlockSpec((1, LANES), const)]
        args += list(router)
        out_specs.append(pl.BlockSpec((tm, LANES), row))
        out_shape.append(jax.ShapeDtypeStruct((t, LANES), F32))
    return pl.pallas_call(
        functools.partial(_mixer_kernel, route=router is not None),
        grid=(t // tm,),
        in_specs=in_specs,
        out_specs=out_specs,
        out_shape=out_shape,
        compiler_params=_cp(("parallel",)),
        name="mixer",
    )(*args)


def _ffn_kernel(h_ref, x1_ref, mod_ref, g_ref, b_ref, wg_ref, wu_ref, wd_ref, o_ref):
    h = h_ref[...]
    gate = _dot(h, wg_ref[...])
    up = _dot(h, wu_ref[...])
    act = (gate * _sigmoid(gate) * up).astype(BF16)
    f = _dot(act, wd_ref[...])
    mod = mod_ref[0]
    o_ref[...] = _ln(ALPHA * x1_ref[...] + mod[5:6] * f) * g_ref[...] + b_ref[...]


def _ffn(h2, x1, mod, g, b, wg, wu, wd, tm):
    t = h2.shape[0]
    nb = mod.shape[0]
    tpb = t // nb // tm
    dff = wg.shape[1]
    row = lambda i: (i, 0)
    const = lambda i: (0, 0)
    resident = pl.Buffered(1)
    return pl.pallas_call(
        _ffn_kernel,
        grid=(t // tm,),
        in_specs=[
            pl.BlockSpec((tm, D_MODEL), row),
            pl.BlockSpec((tm, D_MODEL), row),
            pl.BlockSpec((1, 8, D_MODEL), lambda i: (i // tpb, 0, 0)),
            pl.BlockSpec((1, D_MODEL), const),
            pl.BlockSpec((1, D_MODEL), const),
            pl.BlockSpec((D_MODEL, dff), const, pipeline_mode=resident),
            pl.BlockSpec((D_MODEL, dff), const, pipeline_mode=resident),
            pl.BlockSpec((dff, D_MODEL), const, pipeline_mode=resident),
        ],
        out_specs=pl.BlockSpec((tm, D_MODEL), row),
        out_shape=jax.ShapeDtypeStruct((t, D_MODEL), F32),
        compiler_params=_cp(("parallel",)),
        name="ffn",
    )(h2, x1, mod, g, b, wg, wu, wd)


RI_E1, RI_E2, RI_W1, RI_W2 = N_EXPERTS, N_EXPERTS + 1, N_EXPERTS + 2, N_EXPERTS + 3
MOE_TM = 512
MOE_TT = 512
DMA_ISSUE_UNROLL = 8


def _moe_rows(n_tokens):
    return 2 * n_tokens + N_EXPERTS * MOE_TM


def _scan_kernel(r_ref, o_ref, cnt_ref, run_ref):
    @pl.when(pl.program_id(0) == 0)
    def _():
        run_ref[...] = jnp.zeros_like(run_ref)

    r = r_ref[...]
    ts = r.shape[0]
    lane = lax.broadcasted_iota(jnp.int32, r.shape, 1)
    sel = jnp.where(lane < N_EXPERTS, r, 0.0)
    rows = lax.broadcasted_iota(jnp.int32, (ts, ts), 0)
    cols = lax.broadcasted_iota(jnp.int32, (ts, ts), 1)
    ltri = jnp.where(rows > cols, 1.0, 0.0).astype(BF16)
    prefix = _dot(ltri, sel.astype(BF16)) + run_ref[0:1, :]
    lane_f = lane.astype(F32)
    pick = lambda k: jnp.sum(jnp.where(lane == k, r, 0.0), axis=1, keepdims=True)
    e1, e2 = pick(RI_E1), pick(RI_E2)
    p1 = jnp.sum(jnp.where(lane_f == e1, prefix, 0.0), axis=1, keepdims=True)
    p2 = jnp.sum(jnp.where(lane_f == e2, prefix, 0.0), axis=1, keepdims=True)
    out = jnp.where(lane == 0, e1, jnp.where(lane == 1, e2, jnp.where(lane == 2, p1, jnp.where(lane == 3, p2, 0.0))))
    o_ref[...] = out.astype(jnp.int32)
    run_ref[...] = run_ref[...] + jnp.sum(sel, axis=0, keepdims=True)
    cnt_ref[...] = run_ref[...]


def _route_scan(rinfo, ts=512):
    t = rinfo.shape[0]
    return pl.pallas_call(
        _scan_kernel,
        grid=(t // ts,),
        in_specs=[pl.BlockSpec((ts, LANES), lambda i: (i, 0))],
        out_specs=[pl.BlockSpec((ts, LANES), lambda i: (i, 0)), pl.BlockSpec((8, LANES), lambda i: (0, 0))],
        out_shape=[jax.ShapeDtypeStruct((t, LANES), jnp.int32), jax.ShapeDtypeStruct((8, LANES), F32)],
        scratch_shapes=[pltpu.VMEM((8, LANES), F32)],
        compiler_params=_cp(("arbitrary",)),
        name="route_scan",
    )(rinfo)


def _row_copy(src_ref, src_row, dst_ref, dst_row, sem):
    return pltpu.make_async_copy(src_ref.at[pl.ds(src_row, 1)], dst_ref.at[pl.ds(dst_row, 1)], sem)


def _dispatch_kernel(dest_ref, h_ref, xs_in_ref, xs_ref, sem):
    del xs_in_ref

    def issue(t, c):
        for k in range(2):
            _row_copy(h_ref, t, xs_ref, dest_ref[0, k, t], sem).start(priority=k)
        return c

    lax.fori_loop(0, MOE_TT, issue, 0, unroll=DMA_ISSUE_UNROLL)
    for k in range(2):
        pltpu.make_async_copy(h_ref, xs_ref.at[pl.ds(0, MOE_TT)], sem).wait()


def _dispatch(dest, h2):
    t = h2.shape[0]
    xs0 = jnp.zeros((_moe_rows(t), D_MODEL), F32)
    return pl.pallas_call(
        _dispatch_kernel,
        grid=(t // MOE_TT,),
        in_specs=[
            pl.BlockSpec((1, 2, MOE_TT), lambda i: (i, 0, 0), memory_space=pltpu.SMEM),
            pl.BlockSpec((MOE_TT, D_MODEL), lambda i: (i, 0)),
            pl.BlockSpec(memory_space=pl.ANY),
        ],
        out_specs=pl.BlockSpec(memory_space=pl.ANY),
        out_shape=jax.ShapeDtypeStruct(xs0.shape, F32),
        scratch_shapes=[pltpu.SemaphoreType.DMA(())],
        input_output_aliases={2: 0},
        compiler_params=_cp(("arbitrary",)),
        name="moe_dispatch",
    )(dest, h2, xs0)


def _experts_kernel(te_ref, nt_ref, xs_ref, wg_ref, wu_ref, wd_ref, y_ref, acc_ref):
    j = pl.program_id(1)

    @pl.when(pl.program_id(0) < nt_ref[0])
    def _():
        h = xs_ref[...].astype(BF16)
        gate = _dot(h, wg_ref[0])
        up = _dot(h, wu_ref[0])
        act = (gate * _sigmoid(gate) * up).astype(BF16)
        part = _dot(act, wd_ref[0])

        @pl.when(j == 0)
        def _():
            acc_ref[...] = part

        @pl.when(j > 0)
        def _():
            acc_ref[...] += part

        @pl.when(j == pl.num_programs(1) - 1)
        def _():
            y_ref[...] = acc_ref[...]

    @pl.when((pl.program_id(0) >= nt_ref[0]) & (j == pl.num_programs(1) - 1))
    def _():
        y_ref[...] = jnp.zeros_like(y_ref)


def _experts(tile_expert, n_tiles_used, xs, wg, wu, wd, fc):
    ne, _, dff = wg.shape
    grid_spec = pltpu.PrefetchScalarGridSpec(
        num_scalar_prefetch=2,
        grid=(xs.shape[0] // MOE_TM, dff // fc),
        in_specs=[
            pl.BlockSpec((MOE_TM, D_MODEL), lambda i, j, te, nt: (i, 0)),
            pl.BlockSpec((1, D_MODEL, fc), lambda i, j, te, nt: (te[i], 0, j)),
            pl.BlockSpec((1, D_MODEL, fc), lambda i, j, te, nt: (te[i], 0, j)),
            pl.BlockSpec((1, fc, D_MODEL), lambda i, j, te, nt: (te[i], j, 0)),
        ],
        out_specs=pl.BlockSpec((MOE_TM, D_MODEL), lambda i, j, te, nt: (i, 0)),
        scratch_shapes=[pltpu.VMEM((MOE_TM, D_MODEL), F32)],
    )
    return pl.pallas_call(
        _experts_kernel,
        grid_spec=grid_spec,
        out_shape=jax.ShapeDtypeStruct(xs.shape, F32),
        compiler_params=_cp(("arbitrary", "arbitrary")),
        name="moe_experts",
    )(tile_expert, n_tiles_used, xs, wg, wu, wd)


def _combine_kernel(dest_ref, dest_next_ref, y_ref, x1_ref, rinfo_ref, mod_ref, g_ref, b_ref, o_ref,
                    ybuf_ref, sems):
    i = pl.program_id(0)
    slot = i % 2

    def gather(d_ref, s):
        def issue(t, c):
            for k in range(2):
                _row_copy(y_ref, d_ref[0, k, t], ybuf_ref.at[s, k], t, sems.at[s]).start(priority=k)
            return c
        lax.fori_loop(0, MOE_TT, issue, 0, unroll=DMA_ISSUE_UNROLL)

    @pl.when(i == 0)
    def _():
        gather(dest_ref, slot)

    @pl.when(i + 1 < pl.num_programs(0))
    def _():
        gather(dest_next_ref, 1 - slot)

    for k in range(2):
        pltpu.make_async_copy(y_ref.at[pl.ds(0, MOE_TT)], ybuf_ref.at[slot, k], sems.at[slot]).wait()

    r = rinfo_ref[...]
    lane = lax.broadcasted_iota(jnp.int32, r.shape, 1)
    w1 = jnp.sum(jnp.where(lane == RI_W1, r, 0.0), axis=1, keepdims=True)
    w2 = jnp.sum(jnp.where(lane == RI_W2, r, 0.0), axis=1, keepdims=True)
    f = w1 * ybuf_ref[slot, 0] + w2 * ybuf_ref[slot, 1]
    mod = mod_ref[0]
    o_ref[...] = _ln(ALPHA * x1_ref[...] + mod[5:6] * f) * g_ref[...] + b_ref[...]


def _combine(dest, y, x1, rinfo, mod, g, b):
    t = x1.shape[0]
    nb = mod.shape[0]
    tpb = t // nb // MOE_TT
    row = lambda i: (i, 0)
    const = lambda i: (0, 0)
    n_steps = t // MOE_TT
    return pl.pallas_call(
        _combine_kernel,
        grid=(n_steps,),
        in_specs=[
            pl.BlockSpec((1, 2, MOE_TT), lambda i: (i, 0, 0), memory_space=pltpu.SMEM),
            pl.BlockSpec((1, 2, MOE_TT), lambda i: (jnp.minimum(i + 1, n_steps - 1), 0, 0),
                         memory_space=pltpu.SMEM),
            pl.BlockSpec(memory_space=pl.ANY),
            pl.BlockSpec((MOE_TT, D_MODEL), row),
            pl.BlockSpec((MOE_TT, LANES), row),
            pl.BlockSpec((1, 8, D_MODEL), lambda i: (i // tpb, 0, 0)),
            pl.BlockSpec((1, D_MODEL), const),
            pl.BlockSpec((1, D_MODEL), const),
        ],
        out_specs=pl.BlockSpec((MOE_TT, D_MODEL), row),
        out_shape=jax.ShapeDtypeStruct((t, D_MODEL), F32),
        scratch_shapes=[pltpu.VMEM((2, 2, MOE_TT, D_MODEL), F32), pltpu.SemaphoreType.DMA((2,))],
        compiler_params=_cp(("arbitrary",)),
        name="moe_combine",
    )(dest, dest, y, x1, rinfo, mod, g, b)


def _moe(h2, x1, rinfo, mod, g, b, wg, wu, wd):
    t = h2.shape[0]
    route, counts = _route_scan(rinfo)
    cnt = counts[0, :N_EXPERTS].astype(jnp.int32)
    padded = (cnt + MOE_TM - 1) // MOE_TM * MOE_TM
    ends = jnp.cumsum(padded)
    starts = ends - padded
    dest = jnp.stack([starts[route[:, 0]] + route[:, 2], starts[route[:, 1]] + route[:, 3]])
    dest = dest.reshape(2, t // MOE_TT, MOE_TT).transpose(1, 0, 2)
    tile_row0 = jnp.arange(_moe_rows(t) // MOE_TM, dtype=jnp.int32) * MOE_TM
    tile_expert = jnp.minimum(jnp.sum(tile_row0[:, None] >= ends[None, :], axis=1), N_EXPERTS - 1).astype(jnp.int32)
    n_tiles_used = (ends[-1:] // MOE_TM).astype(jnp.int32)
    xs = _dispatch(dest, h2)
    y = _experts(tile_expert, n_tiles_used, xs, wg, wu, wd, fc=D_FF_EXPERT // 2)
    return _combine(dest, y, x1, rinfo, mod, g, b)


def _rope_np(n_tokens):
    rows = np.repeat(np.arange(n_tokens // GRID_W), GRID_W)
    cols = np.tile(np.arange(GRID_W), n_tokens // GRID_W)
    pos = np.stack([rows, cols], axis=-1).astype(np.float64)
    inv_freq = 1.0 / (ROPE_THETA ** (np.arange(ROPE_FREQS, dtype=np.float64) / ROPE_FREQS))
    ang = pos[..., None] * inv_freq
    cos, sin = np.cos(ang), np.sin(ang)
    cos32 = np.broadcast_to(cos[:, :, None, :], (n_tokens, 2, 2, ROPE_FREQS)).reshape(n_tokens, QK_ROPE)
    sgn = np.array([-1.0, 1.0])[None, None, :, None]
    sin32 = (np.broadcast_to(sin[:, :, None, :], (n_tokens, 2, 2, ROPE_FREQS)) * sgn).reshape(n_tokens, QK_ROPE)
    return cos32, sin32


def _half_swap_perm():
    p = np.arange(QK_ROPE).reshape(2, 2, ROPE_FREQS)
    return p[:, ::-1, :].reshape(QK_ROPE)


def _tables():
    c0 = SM_SCALE * math.log2(math.e)
    cos32, sin32 = _rope_np(SEQ)
    one64 = np.ones((SEQ, QK_NOPE))
    tq_lat = c0 * np.concatenate([one64, cos32, sin32], axis=1)
    tk_lat = np.concatenate([cos32, sin32, np.zeros((SEQ, LANES - 2 * QK_ROPE))], axis=1)
    tq_ctx = c0 * np.concatenate([np.ones((CTX_LEN, QK_NOPE + QK_ROPE)), np.zeros((CTX_LEN, QK_ROPE))], axis=1)
    tk_ctx = np.concatenate([np.ones((CTX_LEN, QK_ROPE)), np.zeros((CTX_LEN, LANES - QK_ROPE))], axis=1)
    f32 = lambda a: jnp.asarray(a, F32)
    bf = lambda a: jnp.asarray(a, BF16)

    def dft(n):
        k = np.arange(n)
        ang = 2.0 * np.pi * ((k[:, None] * k[None, :]) % n) / n
        return np.cos(ang), np.sin(ang)

    c128, s128 = dft(FFT_R)
    c256, s256 = dft(CTX_LEN)
    bd_lat = np.concatenate([c128, -s128], axis=1) / math.sqrt(SEQ * F_GROUP_DIM)
    bd_ctx = np.concatenate([c128, -s128], axis=1) / math.sqrt(CTX_LEN * F_GROUP_DIM)
    k1 = np.arange(FFT_R)
    tw = 2.0 * np.pi * ((k1[:, None] * k1[None, :]) % SEQ) / SEQ
    tc = jnp.broadcast_to(f32(np.cos(tw))[:, :, None], (FFT_R, FFT_R, LANES)).reshape(FFT_R, FFT_R * LANES)
    ts = jnp.broadcast_to(f32(np.sin(tw))[:, :, None], (FFT_R, FFT_R, LANES)).reshape(FFT_R, FFT_R * LANES)
    return dict(tq_lat=f32(tq_lat), tk_lat=f32(tk_lat), tq_ctx=f32(tq_ctx), tk_ctx=f32(tk_ctx),
                c128=bf(c128), s128=bf(s128), c256=bf(c256), s256=bf(s256),
                bd_lat=bf(bd_lat), bd_ctx=bf(bd_ctx), tc=tc, ts=ts)


def _arrange_weights(l, w_in, w_uq, w_uk, w_uv):
    sw = _half_swap_perm()
    wi = w_in[l]
    kr = wi[:, OFF_KR:OFF_F]
    w_ip = jnp.concatenate([
        wi[:, :OFF_KR], kr, kr[:, sw], jnp.zeros((D_MODEL, LANES - 2 * QK_ROPE), F32),
        wi[:, OFF_F:]], axis=1).astype(BF16)
    uq = w_uq[l].reshape(Q_LORA, N_HEADS, QK_NOPE + QK_ROPE)
    qr = uq[:, :, QK_NOPE:]
    wq = jnp.concatenate([uq, qr[:, :, sw]], axis=-1).reshape(Q_LORA, N_HEADS * HEAD_PAD).astype(BF16)
    uk = w_uk[l].reshape(KV_LORA, N_HEADS, QK_NOPE)
    top = jnp.concatenate([uk, jnp.zeros((KV_LORA, N_HEADS, HEAD_PAD - QK_NOPE), F32)], axis=-1)
    rmat = np.zeros((LANES, HEAD_PAD), np.float32)
    for i in range(QK_ROPE):
        for r in (i, QK_ROPE + i):
            rmat[r, QK_NOPE + i] = 1.0
            rmat[r, QK_NOPE + QK_ROPE + i] = 1.0
    bot = jnp.broadcast_to(jnp.asarray(rmat)[:, None, :], (LANES, N_HEADS, HEAD_PAD))
    wkt = jnp.transpose(jnp.concatenate([top, bot], axis=0), (1, 2, 0)).astype(BF16)
    uv = w_uv[l].reshape(KV_LORA, N_HEADS // 2, 2, V_HEAD)
    z = jnp.zeros_like(uv[:, :, 0])
    even = jnp.concatenate([uv[:, :, 0], z], axis=-1)
    odd = jnp.concatenate([z, uv[:, :, 1]], axis=-1)
    wv = jnp.stack([even, odd], axis=2).reshape(KV_LORA, N_HEADS * HEAD_PAD).astype(BF16)
    return w_ip, wq, wkt, wv


def kernel(x, c, ctx, c_ctx, w_mod, b_mod, w_in, g_q, w_uq, g_kv, w_uk, w_uv, w_ao, w_fo, w_out,
           ln1_g, ln1_b, ln2_g, ln2_b, w_ff_gate, w_ff_up, w_ff_down,
           w_router, b_router, w_e_gate, w_e_up, w_e_down):
    tb = _tables()
    t_lat = BATCH * SEQ
    t_ctx = BATCH * CTX_LEN

    cs = jnp.concatenate([c, c_ctx[None, :], jnp.zeros((8 - BATCH - 1, D_MODEL), F32)], axis=0)
    mods = _modulation(cs, w_mod.astype(BF16), b_mod).reshape(DEPTH, 8, 6, D_MODEL)
    pad2 = jnp.zeros((DEPTH, 8, 2, D_MODEL), F32)
    mods = jnp.concatenate([mods, pad2], axis=2)

    xl = x.reshape(t_lat, D_MODEL)
    xc = ctx.reshape(t_ctx, D_MODEL)

    for l in range(DEPTH):
        last = l == DEPTH - 1
        mod_x = mods[l, :BATCH]
        mod_c = jnp.broadcast_to(mods[l, BATCH:BATCH + 1], (BATCH, 8, D_MODEL))
        w_ip, wq, wkt, wv = _arrange_weights(l, w_in, w_uq, w_uk, w_uv)
        gq = g_q[l][None, :]
        gkv = g_kv[l][None, :]
        wao = w_ao[l].astype(BF16)
        wfo = w_fo[l].astype(BF16)
        wout = w_out[l].astype(BF16)
        g1, b1 = ln1_g[l][None, :], ln1_b[l][None, :]
        g2, b2 = ln2_g[l][None, :], ln2_b[l][None, :]

        qlat_c, ckr_c, wf_c, gates_c = _inproj(xc, mod_c, w_ip, gq, gkv, tb["tk_ctx"], tb["bd_ctx"], tm=CTX_LEN)
        q_c, kt_c, v_c = _qkv(qlat_c, ckr_c, tb["tq_ctx"], wq, wkt, wv, nb=BATCH, tm=CTX_LEN)

        qlat, ckr, wf, gates = _inproj(xl, mod_x, w_ip, gq, gkv, tb["tk_lat"], tb["bd_lat"], tm=512)
        q, kt, v = _qkv(qlat, ckr, tb["tq_lat"], wq, wkt, wv, nb=BATCH, tm=512)
        attn = _attention(q, kt_c, v_c, kt, v, tq=ATTN_TQ, tk=ATTN_TK, unroll=ATTN_UNROLL)
        four = _fourier_latent(wf.reshape(BATCH, SEQ, 2 * FOURIER_DIM), tb["c128"], tb["s128"], tb["tc"], tb["ts"])

        if l % 2 == 0:
            i = l // 2
            x1, h2 = _mixer(attn.reshape(t_lat, -1), four.reshape(t_lat, -1), gates, xl, mod_x, g1, b1,
                            wao, wfo, wout, tm=512)
            wg = w_ff_gate[i].astype(BF16)
            wu = w_ff_up[i].astype(BF16)
            wd = w_ff_down[i].astype(BF16)
            xl_new = _ffn(h2, x1, mod_x, g2, b2, wg, wu, wd, tm=512)
        else:
            i = l // 2
            wr = jnp.concatenate([w_router[i], jnp.zeros((D_MODEL, LANES - N_EXPERTS), F32)], axis=1)
            br = jnp.concatenate([b_router[i], jnp.zeros((LANES - N_EXPERTS,), F32)])[None, :]
            x1, h2, rinfo = _mixer(attn.reshape(t_lat, -1), four.reshape(t_lat, -1), gates, xl, mod_x, g1, b1,
                                   wao, wfo, wout, tm=512, router=(wr, br))
            wg = w_e_gate[i].astype(BF16)
            wu = w_e_up[i].astype(BF16)
            wd = w_e_down[i].astype(BF16)
            xl_new = _moe(h2, x1, rinfo, mod_x, g2, b2, wg, wu, wd)

        if not last:
            assert l % 2 == 0
            attn_c = _attention(q_c, kt_c, v_c, None, None, tq=CTX_LEN, tk=CTX_LEN)
            four_c = _fourier_ctx(wf_c.reshape(BATCH, CTX_LEN, 2 * FOURIER_DIM), tb["c256"], tb["s256"])
            x1c, h2c = _mixer(attn_c.reshape(t_ctx, -1), four_c.reshape(t_ctx, -1), gates_c, xc, mod_c,
                              g1, b1, wao, wfo, wout, tm=CTX_LEN)
            xc = _ffn(h2c, x1c, mod_c, g2, b2, wg, wu, wd, tm=CTX_LEN)
        xl = xl_new

    return xl.reshape(BATCH, SEQ, D_MODEL)
```

```python
import functools
import math

import numpy as np
import jax
import jax.numpy as jnp
from jax import lax
from jax.experimental import pallas as pl
from jax.experimental.pallas import tpu as pltpu

D_MODEL = 1024
BATCH = 2
SEQ = 16384
DEPTH = 2
GRID_W = 64
CTX_LEN = 256
N_HEADS = 16
QK_NOPE = 64
QK_ROPE = 32
ROPE_FREQS = QK_ROPE // 4
V_HEAD = 64
Q_LORA = 256
KV_LORA = 128
ROPE_THETA = 10000.0
SM_SCALE = (QK_NOPE + QK_ROPE) ** -0.5
F_GROUPS = 4
F_GROUP_DIM = 128
FOURIER_DIM = F_GROUPS * F_GROUP_DIM
OFF_KV = Q_LORA
OFF_KR = OFF_KV + KV_LORA
OFF_F = OFF_KR + QK_ROPE
OFF_G = OFF_F + FOURIER_DIM
D_FF = 2816
N_EXPERTS = 8
D_FF_EXPERT = 3584
ALPHA = (2 * DEPTH) ** 0.25
LN_EPS = 1e-6
RMS_EPS = 1e-6

LANES = 128
HEAD_PAD = 128
FFT_R = 128
V_ONES_LANE = (V_HEAD, 0)
VMEM_LIMIT = 56 * 1024 * 1024
ATTN_TQ = 1024
ATTN_TK = 512
ATTN_UNROLL = 8

BF16 = jnp.bfloat16
F32 = jnp.float32


def _cp(sem, vmem=VMEM_LIMIT):
    return pltpu.CompilerParams(dimension_semantics=sem, vmem_limit_bytes=vmem)


def _dot(a, b):
    return jnp.dot(a, b, preferred_element_type=F32)


def _ln(x):
    mu = jnp.mean(x, axis=-1, keepdims=True)
    xc = x - mu
    var = jnp.mean(xc * xc, axis=-1, keepdims=True)
    return xc * lax.rsqrt(var + LN_EPS)


def _rms(x, g):
    return x * lax.rsqrt(jnp.mean(x * x, axis=-1, keepdims=True) + RMS_EPS) * g


def _sigmoid(x):
    return 1.0 / (1.0 + jnp.exp(-x))


def _mod_kernel(cs_ref, w_ref, b_ref, o_ref):
    cs = cs_ref[...]
    a = (cs * _sigmoid(cs)).astype(BF16)
    o_ref[0] = _dot(a, w_ref[0]) + b_ref[0]


def _modulation(cs, w_mod, b_mod):
    n_chunk = 1024
    n_out = w_mod.shape[-1]
    return pl.pallas_call(
        _mod_kernel,
        grid=(DEPTH, n_out // n_chunk),
        in_specs=[
            pl.BlockSpec((8, D_MODEL), lambda l, j: (0, 0)),
            pl.BlockSpec((1, D_MODEL, n_chunk), lambda l, j: (l, 0, j)),
            pl.BlockSpec((1, 1, n_chunk), lambda l, j: (l, 0, j)),
        ],
        out_specs=pl.BlockSpec((1, 8, n_chunk), lambda l, j: (l, 0, j)),
        out_shape=jax.ShapeDtypeStruct((DEPTH, 8, n_out), F32),
        compiler_params=_cp(("parallel", "parallel")),
        name="mod",
    )(cs, w_mod, b_mod.reshape(DEPTH, 1, n_out))


IP_Q = 0
IP_KV = Q_LORA
IP_KR = IP_KV + KV_LORA
IP_F = IP_KR + LANES
IP_G = IP_F + FOURIER_DIM
IP_COLS = IP_G + 2 * D_MODEL


def _inproj_kernel(x_ref, mod_ref, w_ref, gq_ref, gkv_ref, tk_ref, bd_ref,
                   qlat_ref, ckr_ref, wf_ref, gates_ref):
    x = x_ref[...]
    mod = mod_ref[0]
    h = (_ln(x) * (1.0 + mod[1:2]) + mod[0:1]).astype(BF16)
    p0 = _dot(h, w_ref[:, IP_Q:IP_F])
    qlat_ref[...] = _rms(p0[:, IP_Q:IP_KV], gq_ref[...]).astype(BF16)
    ckr_ref[:, 0:KV_LORA] = _rms(p0[:, IP_KV:IP_KR], gkv_ref[...]).astype(BF16)
    ckr_ref[:, KV_LORA:] = (p0[:, IP_KR:IP_F] * tk_ref[...]).astype(BF16)
    uf = _dot(h, w_ref[:, IP_F:IP_G]).astype(BF16)
    for g in range(F_GROUPS):
        r = _dot(uf[:, g * LANES:(g + 1) * LANES], bd_ref[...])
        wf_ref[:, g * LANES:(g + 1) * LANES] = r[:, :LANES].astype(BF16)
        wf_ref[:, FOURIER_DIM + g * LANES:FOURIER_DIM + (g + 1) * LANES] = r[:, LANES:].astype(BF16)
    gc = 512
    for c in range(2 * D_MODEL // gc):
        gl = _dot(h, w_ref[:, IP_G + c * gc:IP_G + (c + 1) * gc])
        gates_ref[:, c * gc:(c + 1) * gc] = _sigmoid(gl).astype(BF16)


def _inproj(x2d, mod, w, gq, gkv, tk, bd, tm):
    t = x2d.shape[0]
    nb = mod.shape[0]
    tpb = t // nb // tm
    return pl.pallas_call(
        _inproj_kernel,
        grid=(t // tm,),
        in_specs=[
            pl.BlockSpec((tm, D_MODEL), lambda i: (i, 0)),
            pl.BlockSpec((1, 8, D_MODEL), lambda i: (i // tpb, 0, 0)),
            pl.BlockSpec((D_MODEL, IP_COLS), lambda i: (0, 0)),
            pl.BlockSpec((1, Q_LORA), lambda i: (0, 0)),
            pl.BlockSpec((1, KV_LORA), lambda i: (0, 0)),
            pl.BlockSpec((tm, LANES), lambda i: (i % tpb, 0)),
            pl.BlockSpec((LANES, 2 * LANES), lambda i: (0, 0)),
        ],
        out_specs=[
            pl.BlockSpec((tm, Q_LORA), lambda i: (i, 0)),
            pl.BlockSpec((tm, 2 * LANES), lambda i: (i, 0)),
            pl.BlockSpec((tm, 2 * FOURIER_DIM), lambda i: (i, 0)),
            pl.BlockSpec((tm, 2 * D_MODEL), lambda i: (i, 0)),
        ],
        out_shape=[
            jax.ShapeDtypeStruct((t, Q_LORA), BF16),
            jax.ShapeDtypeStruct((t, 2 * LANES), BF16),
            jax.ShapeDtypeStruct((t, 2 * FOURIER_DIM), BF16),
            jax.ShapeDtypeStruct((t, 2 * D_MODEL), BF16),
        ],
        compiler_params=_cp(("parallel",)),
        name="inproj",
    )(x2d, mod, w, gq, gkv, tk, bd)


def _qkv_kernel(qlat_ref, ckr_ref, tq_ref, wq_ref, wkt_ref, wv_ref, q_ref, kt_ref, v_ref):
    ql = qlat_ref[...]
    ckr = ckr_ref[...]
    tq = tq_ref[...]
    ckv = ckr[:, :KV_LORA]
    lane = lax.broadcasted_iota(jnp.int32, (ckr.shape[0], HEAD_PAD), 1)
    q_all = _dot(ql, wq_ref[...])
    v_all = _dot(ckv, wv_ref[...])
    for h in range(N_HEADS):
        cols = slice(h * HEAD_PAD, (h + 1) * HEAD_PAD)
        q_ref[0, h] = (q_all[:, cols] * tq).astype(BF16)
        kt_ref[0, h] = lax.dot_general(
            wkt_ref[h], ckr, (((1,), (1,)), ((), ())), preferred_element_type=F32).astype(BF16)
        v_ref[0, h] = jnp.where(lane == V_ONES_LANE[h % 2], 1.0, v_all[:, cols]).astype(BF16)


def _qkv(qlat, ckr, tq, wq, wkt, wv, nb, tm):
    t = qlat.shape[0]
    n = t // nb
    tpb = n // tm
    return pl.pallas_call(
        _qkv_kernel,
        grid=(t // tm,),
        in_specs=[
            pl.BlockSpec((tm, Q_LORA), lambda i: (i, 0)),
            pl.BlockSpec((tm, 2 * LANES), lambda i: (i, 0)),
            pl.BlockSpec((tm, HEAD_PAD), lambda i: (i % tpb, 0)),
            pl.BlockSpec((Q_LORA, N_HEADS * HEAD_PAD), lambda i: (0, 0)),
            pl.BlockSpec((N_HEADS, HEAD_PAD, 2 * LANES), lambda i: (0, 0, 0)),
            pl.BlockSpec((KV_LORA, N_HEADS * HEAD_PAD), lambda i: (0, 0)),
        ],
        out_specs=[
            pl.BlockSpec((1, N_HEADS, tm, HEAD_PAD), lambda i: (i // tpb, 0, i % tpb, 0)),
            pl.BlockSpec((1, N_HEADS, HEAD_PAD, tm), lambda i: (i // tpb, 0, 0, i % tpb)),
            pl.BlockSpec((1, N_HEADS, tm, HEAD_PAD), lambda i: (i // tpb, 0, i % tpb, 0)),
        ],
        out_shape=[
            jax.ShapeDtypeStruct((nb, N_HEADS, n, HEAD_PAD), BF16),
            jax.ShapeDtypeStruct((nb, N_HEADS, HEAD_PAD, n), BF16),
            jax.ShapeDtypeStruct((nb, N_HEADS, n, HEAD_PAD), BF16),
        ],
        compiler_params=_cp(("parallel",)),
        name="qkv",
    )(qlat, ckr, tq, wq, wkt, wv)


def _attn_kernel(q_ref, ktc_ref, vc_ref, *rest, n_chunks, tk, unroll):
    if n_chunks:
        kt_ref, v_ref, o_ref = rest
    else:
        (o_ref,) = rest

    def step(q, kt, v, m, acc):
        s = _dot(q, kt)
        m_new = jnp.max(s, axis=1, keepdims=True)
        if m is not None:
            m_new = jnp.maximum(m, m_new)
        p = jnp.exp2((s - m_new).astype(BF16))
        pv = _dot(p, v)
        if m is not None:
            pv = jnp.exp2(m - m_new) * acc + pv
        return m_new, pv

    qs = [q_ref[0, hh] for hh in range(2)]
    carry = []
    for hh in range(2):
        carry += step(qs[hh], ktc_ref[0, hh], vc_ref[0, hh], None, None)

    if n_chunks:
        span = unroll * tk

        def group(base, carry):
            for u in range(unroll):
                off = base + u * tk
                if not isinstance(off, int):
                    off = pl.multiple_of(off, tk)
                out = []
                for hh in range(2):
                    out += step(qs[hh], kt_ref[0, hh, :, pl.ds(off, tk)], v_ref[0, hh, pl.ds(off, tk), :],
                                carry[2 * hh], carry[2 * hh + 1])
                carry = out
            return tuple(carry)

        carry = group(0, carry)
        carry = lax.fori_loop(1, n_chunks // unroll,
                              lambda g, c: group(pl.multiple_of(g * span, span), c), carry)

    lane = lax.broadcasted_iota(jnp.int32, carry[1].shape, 1)
    outs = []
    for hh in range(2):
        acc = carry[2 * hh + 1]
        l = jnp.sum(jnp.where(lane == V_ONES_LANE[hh], acc, 0.0), axis=1, keepdims=True)
        outs.append(acc * (1.0 / l))
    o_ref[0] = jnp.where(lane < V_HEAD, outs[0], outs[1]).astype(BF16)


def _attention(q, kt_c, v_c, kt, v, tq, tk, unroll=1):
    nb, _, n, _ = q.shape
    nc = kt_c.shape[-1]
    n_chunks = 0 if kt is None else kt.shape[-1] // tk
    in_specs = [
        pl.BlockSpec((1, 2, tq, HEAD_PAD), lambda b, p, i: (b, p, i, 0)),
        pl.BlockSpec((1, 2, HEAD_PAD, nc), lambda b, p, i: (b, p, 0, 0)),
        pl.BlockSpec((1, 2, nc, HEAD_PAD), lambda b, p, i: (b, p, 0, 0)),
    ]
    args = [q, kt_c, v_c]
    if n_chunks:
        nk = kt.shape[-1]
        in_specs += [
            pl.BlockSpec((1, 2, HEAD_PAD, nk), lambda b, p, i: (b, p, 0, 0)),
            pl.BlockSpec((1, 2, nk, HEAD_PAD), lambda b, p, i: (b, p, 0, 0)),
        ]
        args += [kt, v]
    return pl.pallas_call(
        functools.partial(_attn_kernel, n_chunks=n_chunks, tk=tk, unroll=unroll),
        grid=(nb, N_HEADS // 2, n // tq),
        in_specs=in_specs,
        out_specs=pl.BlockSpec((1, tq, 2 * V_HEAD), lambda b, p, i: (b, i, p)),
        out_shape=jax.ShapeDtypeStruct((nb, n, N_HEADS * V_HEAD), BF16),
        compiler_params=_cp(("parallel", "parallel", "arbitrary")),
        name="attn",
    )(*args)


def _fft1_kernel(x_ref, c_ref, s_ref, tc_ref, ts_ref, o_ref, *, n2t):
    x = x_ref[0]
    cx = _dot(c_ref[...], x)
    sx = _dot(s_ref[...], x)
    w = 2 * FOURIER_DIM
    for t in range(n2t):
        re = slice(t * w, t * w + FOURIER_DIM)
        im = slice(t * w + FOURIER_DIM, (t + 1) * w)
        yr = cx[:, re] + sx[:, im]
        yi = cx[:, im] - sx[:, re]
        tc = jnp.concatenate([tc_ref[:, t * LANES:(t + 1) * LANES]] * F_GROUPS, axis=1)
        ts = jnp.concatenate([ts_ref[:, t * LANES:(t + 1) * LANES]] * F_GROUPS, axis=1)
        o_ref[0, :, re] = (yr * tc + yi * ts).astype(BF16)
        o_ref[0, :, im] = (yi * tc - yr * ts).astype(BF16)


def _fft2_kernel(y_ref, c_ref, s_ref, o_ref, *, k1t):
    for t in range(k1t):
        y = y_ref[0, t]
        zr = _dot(c_ref[...], y[:, :FOURIER_DIM]) + _dot(s_ref[...], y[:, FOURIER_DIM:])
        o_ref[0, :, t * FOURIER_DIM:(t + 1) * FOURIER_DIM] = zr.astype(BF16)


def _fourier_latent(wf, cmat, smat, tc, ts):
    nb = wf.shape[0]
    w = 2 * FOURIER_DIM
    n2t = 8
    y = pl.pallas_call(
        functools.partial(_fft1_kernel, n2t=n2t),
        grid=(nb, FFT_R // n2t),
        in_specs=[
            pl.BlockSpec((1, FFT_R, n2t * w), lambda b, j: (b, 0, j)),
            pl.BlockSpec((FFT_R, FFT_R), lambda b, j: (0, 0)),
            pl.BlockSpec((FFT_R, FFT_R), lambda b, j: (0, 0)),
            pl.BlockSpec((FFT_R, n2t * LANES), lambda b, j: (0, j)),
            pl.BlockSpec((FFT_R, n2t * LANES), lambda b, j: (0, j)),
        ],
        out_specs=pl.BlockSpec((1, FFT_R, n2t * w), lambda b, j: (b, 0, j)),
        out_shape=jax.ShapeDtypeStruct((nb, FFT_R, FFT_R * w), BF16),
        compiler_params=_cp(("parallel", "parallel")),
        name="fft1",
    )(wf.reshape(nb, FFT_R, FFT_R * w), cmat, smat, tc, ts)
    k1t = 8
    four = pl.pallas_call(
        functools.partial(_fft2_kernel, k1t=k1t),
        grid=(nb, FFT_R // k1t),
        in_specs=[
            pl.BlockSpec((1, k1t, FFT_R, w), lambda b, j: (b, j, 0, 0)),
            pl.BlockSpec((FFT_R, FFT_R), lambda b, j: (0, 0)),
            pl.BlockSpec((FFT_R, FFT_R), lambda b, j: (0, 0)),
        ],
        out_specs=pl.BlockSpec((1, FFT_R, k1t * FOURIER_DIM), lambda b, j: (b, 0, j)),
        out_shape=jax.ShapeDtypeStruct((nb, FFT_R, FFT_R * FOURIER_DIM), BF16),
        compiler_params=_cp(("parallel", "parallel")),
        name="fft2",
    )(y.reshape(nb, FFT_R, FFT_R, w), cmat, smat)
    return four.reshape(nb, SEQ, FOURIER_DIM)


def _dft_ctx_kernel(x_ref, c_ref, s_ref, o_ref):
    x = x_ref[0]
    zr = _dot(c_ref[...], x[:, :FOURIER_DIM]) + _dot(s_ref[...], x[:, FOURIER_DIM:])
    o_ref[0] = zr.astype(BF16)


def _fourier_ctx(wf, cmat, smat):
    nb, n, w = wf.shape
    return pl.pallas_call(
        _dft_ctx_kernel,
        grid=(nb,),
        in_specs=[
            pl.BlockSpec((1, n, w), lambda b: (b, 0, 0)),
            pl.BlockSpec((n, n), lambda b: (0, 0)),
            pl.BlockSpec((n, n), lambda b: (0, 0)),
        ],
        out_specs=pl.BlockSpec((1, n, FOURIER_DIM), lambda b: (b, 0, 0)),
        out_shape=jax.ShapeDtypeStruct((nb, n, FOURIER_DIM), BF16),
        compiler_params=_cp(("parallel",)),
        name="dft_ctx",
    )(wf, cmat, smat)


def _mixer_kernel(attn_ref, four_ref, gates_ref, x_ref, mod_ref, g_ref, b_ref,
                  wao_ref, wfo_ref, wout_ref, *rest, route):
    if route:
        wr_ref, br_ref, x1_ref, h2_ref, rinfo_ref = rest
    else:
        x1_ref, h2_ref = rest
    mod = mod_ref[0]
    a = _dot(attn_ref[...], wao_ref[...])
    f = _dot(four_ref[...], wfo_ref[...])
    merged = gates_ref[:, :D_MODEL].astype(F32) * a + gates_ref[:, D_MODEL:].astype(F32) * f
    y = _dot(merged.astype(BF16), wout_ref[...])
    x1 = _ln(ALPHA * x_ref[...] + mod[2:3] * y) * g_ref[...] + b_ref[...]
    x1_ref[...] = x1
    h2 = _ln(x1) * (1.0 + mod[4:5]) + mod[3:4]
    h2_ref[...] = h2.astype(h2_ref.dtype)
    if route:
        logits = jnp.dot(h2, wr_ref[...], preferred_element_type=F32,
                         precision=lax.Precision.HIGHEST) + br_ref[...]
        lane = lax.broadcasted_iota(jnp.int32, logits.shape, 1)
        neg = jnp.float32(-jnp.inf)
        lg = jnp.where(lane < N_EXPERTS, logits, neg)
        m1 = jnp.max(lg, axis=1, keepdims=True)
        i1 = jnp.min(jnp.where(lg == m1, lane, LANES), axis=1, keepdims=True)
        lg2 = jnp.where(lane == i1, neg, lg)
        m2 = jnp.max(lg2, axis=1, keepdims=True)
        i2 = jnp.min(jnp.where(lg2 == m2, lane, LANES), axis=1, keepdims=True)
        e2 = jnp.exp(m2 - m1)
        w1 = 1.0 / (1.0 + e2)
        w2 = e2 * w1
        rinfo = jnp.where((lane == i1) | (lane == i2), 1.0, 0.0)
        rinfo = jnp.where(lane == RI_E1, i1.astype(F32), rinfo)
        rinfo = jnp.where(lane == RI_E2, i2.astype(F32), rinfo)
        rinfo = jnp.where(lane == RI_W1, w1, rinfo)
        rinfo_ref[...] = jnp.where(lane == RI_W2, w2, rinfo)


def _mixer(attn, four, gates, x2d, mod, g, b, wao, wfo, wout, tm, router=None):
    t = x2d.shape[0]
    nb = mod.shape[0]
    tpb = t // nb // tm
    row = lambda i: (i, 0)
    const = lambda i: (0, 0)
    in_specs = [
        pl.BlockSpec((tm, D_MODEL), row),
        pl.BlockSpec((tm, FOURIER_DIM), row),
        pl.BlockSpec((tm, 2 * D_MODEL), row),
        pl.BlockSpec((tm, D_MODEL), row),
        pl.BlockSpec((1, 8, D_MODEL), lambda i: (i // tpb, 0, 0)),
        pl.BlockSpec((1, D_MODEL), const),
        pl.BlockSpec((1, D_MODEL), const),
        pl.BlockSpec((D_MODEL, D_MODEL), const),
        pl.BlockSpec((FOURIER_DIM, D_MODEL), const),
        pl.BlockSpec((D_MODEL, D_MODEL), const),
    ]
    args = [attn, four, gates, x2d, mod, g, b, wao, wfo, wout]
    out_specs = [pl.BlockSpec((tm, D_MODEL), row), pl.BlockSpec((tm, D_MODEL), row)]
    h2_dtype = BF16 if router is None else F32
    out_shape = [jax.ShapeDtypeStruct((t, D_MODEL), F32), jax.ShapeDtypeStruct((t, D_MODEL), h2_dtype)]
    if router is not None:
        in_specs += [pl.BlockSpec((D_MODEL, LANES), const), pl.BlockSpec((1, LANES), const)]
        args += list(router)
        out_specs.append(pl.BlockSpec((tm, LANES), row))
        out_shape.append(jax.ShapeDtypeStruct((t, LANES), F32))
    return pl.pallas_call(
        functools.partial(_mixer_kernel, route=router is not None),
        grid=(t // tm,),
        in_specs=in_specs,
        out_specs=out_specs,
        out_shape=out_shape,
        compiler_params=_cp(("parallel",)),
        name="mixer",
    )(*args)


def _ffn_kernel(h_ref, x1_ref, mod_ref, g_ref, b_ref, wg_ref, wu_ref, wd_ref, o_ref):
    h = h_ref[...]
    gate = _dot(h, wg_ref[...])
    up = _dot(h, wu_ref[...])
    act = (gate * _sigmoid(gate) * up).astype(BF16)
    f = _dot(act, wd_ref[...])
    mod = mod_ref[0]
    o_ref[...] = _ln(ALPHA * x1_ref[...] + mod[5:6] * f) * g_ref[...] + b_ref[...]


def _ffn(h2, x1, mod, g, b, wg, wu, wd, tm):
    t = h2.shape[0]
    nb = mod.shape[0]
    tpb = t // nb // tm
    dff = wg.shape[1]
    row = lambda i: (i, 0)
    const = lambda i: (0, 0)
    resident = pl.Buffered(1)
    return pl.pallas_call(
        _ffn_kernel,
        grid=(t // tm,),
        in_specs=[
            pl.BlockSpec((tm, D_MODEL), row),
            pl.BlockSpec((tm, D_MODEL), row),
            pl.BlockSpec((1, 8, D_MODEL), lambda i: (i // tpb, 0, 0)),
            pl.BlockSpec((1, D_MODEL), const),
            pl.BlockSpec((1, D_MODEL), const),
            pl.BlockSpec((D_MODEL, dff), const, pipeline_mode=resident),
            pl.BlockSpec((D_MODEL, dff), const, pipeline_mode=resident),
            pl.BlockSpec((dff, D_MODEL), const, pipeline_mode=resident),
        ],
        out_specs=pl.BlockSpec((tm, D_MODEL), row),
        out_shape=jax.ShapeDtypeStruct((t, D_MODEL), F32),
        compiler_params=_cp(("parallel",)),
        name="ffn",
    )(h2, x1, mod, g, b, wg, wu, wd)


RI_E1, RI_E2, RI_W1, RI_W2 = N_EXPERTS, N_EXPERTS + 1, N_EXPERTS + 2, N_EXPERTS + 3
MOE_TM = 512
MOE_TT = 512
DMA_ISSUE_UNROLL = 8


def _moe_rows(n_tokens):
    return 2 * n_tokens + N_EXPERTS * MOE_TM


def _scan_kernel(r_ref, o_ref, cnt_ref, run_ref):
    @pl.when(pl.program_id(0) == 0)
    def _():
        run_ref[...] = jnp.zeros_like(run_ref)

    r = r_ref[...]
    ts = r.shape[0]
    lane = lax.broadcasted_iota(jnp.int32, r.shape, 1)
    sel = jnp.where(lane < N_EXPERTS, r, 0.0)
    rows = lax.broadcasted_iota(jnp.int32, (ts, ts), 0)
    cols = lax.broadcasted_iota(jnp.int32, (ts, ts), 1)
    ltri = jnp.where(rows > cols, 1.0, 0.0).astype(BF16)
    prefix = _dot(ltri, sel.astype(BF16)) + run_ref[0:1, :]
    lane_f = lane.astype(F32)
    pick = lambda k: jnp.sum(jnp.where(lane == k, r, 0.0), axis=1, keepdims=True)
    e1, e2 = pick(RI_E1), pick(RI_E2)
    p1 = jnp.sum(jnp.where(lane_f == e1, prefix, 0.0), axis=1, keepdims=True)
    p2 = jnp.sum(jnp.where(lane_f == e2, prefix, 0.0), axis=1, keepdims=True)
    out = jnp.where(lane == 0, e1, jnp.where(lane == 1, e2, jnp.where(lane == 2, p1, jnp.where(lane == 3, p2, 0.0))))
    o_ref[...] = out.astype(jnp.int32)
    run_ref[...] = run_ref[...] + jnp.sum(sel, axis=0, keepdims=True)
    cnt_ref[...] = run_ref[...]


def _route_scan(rinfo, ts=512):
    t = rinfo.shape[0]
    return pl.pallas_call(
        _scan_kernel,
        grid=(t // ts,),
        in_specs=[pl.BlockSpec((ts, LANES), lambda i: (i, 0))],
        out_specs=[pl.BlockSpec((ts, LANES), lambda i: (i, 0)), pl.BlockSpec((8, LANES), lambda i: (0, 0))],
        out_shape=[jax.ShapeDtypeStruct((t, LANES), jnp.int32), jax.ShapeDtypeStruct((8, LANES), F32)],
        scratch_shapes=[pltpu.VMEM((8, LANES), F32)],
        compiler_params=_cp(("arbitrary",)),
        name="route_scan",
    )(rinfo)


def _row_copy(src_ref, src_row, dst_ref, dst_row, sem):
    return pltpu.make_async_copy(src_ref.at[pl.ds(src_row, 1)], dst_ref.at[pl.ds(dst_row, 1)], sem)


def _dispatch_kernel(dest_ref, h_ref, xs_ref, sem):
    def issue(t, c):
        for k in range(2):
            _row_copy(h_ref, t, xs_ref, dest_ref[0, k, t], sem).start(priority=k)
        return c

    lax.fori_loop(0, MOE_TT, issue, 0, unroll=DMA_ISSUE_UNROLL)
    for k in range(2):
        pltpu.make_async_copy(h_ref, xs_ref.at[pl.ds(0, MOE_TT)], sem).wait()


def _dispatch(dest, h2):
    t = h2.shape[0]
    return pl.pallas_call(
        _dispatch_kernel,
        grid=(t // MOE_TT,),
        in_specs=[
            pl.BlockSpec((1, 2, MOE_TT), lambda i: (i, 0, 0), memory_space=pltpu.SMEM),
            pl.BlockSpec((MOE_TT, D_MODEL), lambda i: (i, 0)),
        ],
        out_specs=pl.BlockSpec(memory_space=pl.ANY),
        out_shape=jax.ShapeDtypeStruct((_moe_rows(t), D_MODEL), F32),
        scratch_shapes=[pltpu.SemaphoreType.DMA(())],
        compiler_params=_cp(("arbitrary",)),
        name="moe_dispatch",
    )(dest, h2)


def _experts_kernel(te_ref, tv_ref, nt_ref, xs_ref, wg_ref, wu_ref, wd_ref, y_ref, acc_ref):
    j = pl.program_id(1)

    @pl.when(pl.program_id(0) < nt_ref[0])
    def _():
        rows = lax.broadcasted_iota(jnp.int32, xs_ref.shape, 0)
        h = jnp.where(rows < tv_ref[pl.program_id(0)], xs_ref[...], 0.0).astype(BF16)
        gate = _dot(h, wg_ref[0])
        up = _dot(h, wu_ref[0])
        act = (gate * _sigmoid(gate) * up).astype(BF16)
        part = _dot(act, wd_ref[0])

        @pl.when(j == 0)
        def _():
            acc_ref[...] = part

        @pl.when(j > 0)
        def _():
            acc_ref[...] += part

        @pl.when(j == pl.num_programs(1) - 1)
        def _():
            y_ref[...] = acc_ref[...]

    @pl.when((pl.program_id(0) >= nt_ref[0]) & (j == pl.num_programs(1) - 1))
    def _():
        y_ref[...] = jnp.zeros_like(y_ref)


def _experts(tile_expert, tile_valid, n_tiles_used, xs, wg, wu, wd, fc):
    ne, _, dff = wg.shape
    grid_spec = pltpu.PrefetchScalarGridSpec(
        num_scalar_prefetch=3,
        grid=(xs.shape[0] // MOE_TM, dff // fc),
        in_specs=[
            pl.BlockSpec((MOE_TM, D_MODEL), lambda i, j, te, tv, nt: (i, 0)),
            pl.BlockSpec((1, D_MODEL, fc), lambda i, j, te, tv, nt: (te[i], 0, j)),
            pl.BlockSpec((1, D_MODEL, fc), lambda i, j, te, tv, nt: (te[i], 0, j)),
            pl.BlockSpec((1, fc, D_MODEL), lambda i, j, te, tv, nt: (te[i], j, 0)),
        ],
        out_specs=pl.BlockSpec((MOE_TM, D_MODEL), lambda i, j, te, tv, nt: (i, 0)),
        scratch_shapes=[pltpu.VMEM((MOE_TM, D_MODEL), F32)],
    )
    return pl.pallas_call(
        _experts_kernel,
        grid_spec=grid_spec,
        out_shape=jax.ShapeDtypeStruct(xs.shape, F32),
        compiler_params=_cp(("arbitrary", "arbitrary")),
        name="moe_experts",
    )(tile_expert, tile_valid, n_tiles_used, xs, wg, wu, wd)


def _combine_kernel(dest_ref, dest_next_ref, y_ref, x1_ref, rinfo_ref, mod_ref, g_ref, b_ref, o_ref,
                    ybuf_ref, sems):
    i = pl.program_id(0)
    slot = i % 2

    def gather(d_ref, s):
        def issue(t, c):
            for k in range(2):
                _row_copy(y_ref, d_ref[0, k, t], ybuf_ref.at[s, k], t, sems.at[s]).start(priority=k)
            return c
        lax.fori_loop(0, MOE_TT, issue, 0, unroll=DMA_ISSUE_UNROLL)

    @pl.when(i == 0)
    def _():
        gather(dest_ref, slot)

    @pl.when(i + 1 < pl.num_programs(0))
    def _():
        gather(dest_next_ref, 1 - slot)

    for k in range(2):
        pltpu.make_async_copy(y_ref.at[pl.ds(0, MOE_TT)], ybuf_ref.at[slot, k], sems.at[slot]).wait()

    r = rinfo_ref[...]
    lane = lax.broadcasted_iota(jnp.int32, r.shape, 1)
    w1 = jnp.sum(jnp.where(lane == RI_W1, r, 0.0), axis=1, keepdims=True)
    w2 = jnp.sum(jnp.where(lane == RI_W2, r, 0.0), axis=1, keepdims=True)
    f = w1 * ybuf_ref[slot, 0] + w2 * ybuf_ref[slot, 1]
    mod = mod_ref[0]
    o_ref[...] = _ln(ALPHA * x1_ref[...] + mod[5:6] * f) * g_ref[...] + b_ref[...]


def _combine(dest, y, x1, rinfo, mod, g, b):
    t = x1.shape[0]
    nb = mod.shape[0]
    tpb = t // nb // MOE_TT
    row = lambda i: (i, 0)
    const = lambda i: (0, 0)
    n_steps = t // MOE_TT
    return pl.pallas_call(
        _combine_kernel,
        grid=(n_steps,),
        in_specs=[
            pl.BlockSpec((1, 2, MOE_TT), lambda i: (i, 0, 0), memory_space=pltpu.SMEM),
            pl.BlockSpec((1, 2, MOE_TT), lambda i: (jnp.minimum(i + 1, n_steps - 1), 0, 0),
                         memory_space=pltpu.SMEM),
            pl.BlockSpec(memory_space=pl.ANY),
            pl.BlockSpec((MOE_TT, D_MODEL), row),
            pl.BlockSpec((MOE_TT, LANES), row),
            pl.BlockSpec((1, 8, D_MODEL), lambda i: (i // tpb, 0, 0)),
            pl.BlockSpec((1, D_MODEL), const),
            pl.BlockSpec((1, D_MODEL), const),
        ],
        out_specs=pl.BlockSpec((MOE_TT, D_MODEL), row),
        out_shape=jax.ShapeDtypeStruct((t, D_MODEL), F32),
        scratch_shapes=[pltpu.VMEM((2, 2, MOE_TT, D_MODEL), F32), pltpu.SemaphoreType.DMA((2,))],
        compiler_params=_cp(("arbitrary",)),
        name="moe_combine",
    )(dest, dest, y, x1, rinfo, mod, g, b)


def _moe(h2, x1, rinfo, mod, g, b, wg, wu, wd):
    t = h2.shape[0]
    route, counts = _route_scan(rinfo)
    cnt = counts[0, :N_EXPERTS].astype(jnp.int32)
    padded = (cnt + MOE_TM - 1) // MOE_TM * MOE_TM
    ends = jnp.cumsum(padded)
    starts = ends - padded
    dest = jnp.stack([starts[route[:, 0]] + route[:, 2], starts[route[:, 1]] + route[:, 3]])
    dest = dest.reshape(2, t // MOE_TT, MOE_TT).transpose(1, 0, 2)
    tile_row0 = jnp.arange(_moe_rows(t) // MOE_TM, dtype=jnp.int32) * MOE_TM
    tile_expert = jnp.minimum(jnp.sum(tile_row0[:, None] >= ends[None, :], axis=1), N_EXPERTS - 1).astype(jnp.int32)
    tile_valid = jnp.clip(starts[tile_expert] + cnt[tile_expert] - tile_row0, 0, MOE_TM).astype(jnp.int32)
    n_tiles_used = (ends[-1:] // MOE_TM).astype(jnp.int32)
    xs = _dispatch(dest, h2)
    y = _experts(tile_expert, tile_valid, n_tiles_used, xs, wg, wu, wd, fc=D_FF_EXPERT // 2)
    return _combine(dest, y, x1, rinfo, mod, g, b)


def _rope_np(n_tokens):
    rows = np.repeat(np.arange(n_tokens // GRID_W), GRID_W)
    cols = np.tile(np.arange(GRID_W), n_tokens // GRID_W)
    pos = np.stack([rows, cols], axis=-1).astype(np.float64)
    inv_freq = 1.0 / (ROPE_THETA ** (np.arange(ROPE_FREQS, dtype=np.float64) / ROPE_FREQS))
    ang = pos[..., None] * inv_freq
    cos, sin = np.cos(ang), np.sin(ang)
    cos32 = np.broadcast_to(cos[:, :, None, :], (n_tokens, 2, 2, ROPE_FREQS)).reshape(n_tokens, QK_ROPE)
    sgn = np.array([-1.0, 1.0])[None, None, :, None]
    sin32 = (np.broadcast_to(sin[:, :, None, :], (n_tokens, 2, 2, ROPE_FREQS)) * sgn).reshape(n_tokens, QK_ROPE)
    return cos32, sin32


def _half_swap_perm():
    p = np.arange(QK_ROPE).reshape(2, 2, ROPE_FREQS)
    return p[:, ::-1, :].reshape(QK_ROPE)


def _tables():
    c0 = SM_SCALE * math.log2(math.e)
    cos32, sin32 = _rope_np(SEQ)
    one64 = np.ones((SEQ, QK_NOPE))
    tq_lat = c0 * np.concatenate([one64, cos32, sin32], axis=1)
    tk_lat = np.concatenate([cos32, sin32, np.zeros((SEQ, LANES - 2 * QK_ROPE))], axis=1)
    tq_ctx = c0 * np.concatenate([np.ones((CTX_LEN, QK_NOPE + QK_ROPE)), np.zeros((CTX_LEN, QK_ROPE))], axis=1)
    tk_ctx = np.concatenate([np.ones((CTX_LEN, QK_ROPE)), np.zeros((CTX_LEN, LANES - QK_ROPE))], axis=1)
    f32 = lambda a: jnp.asarray(a, F32)
    bf = lambda a: jnp.asarray(a, BF16)

    def dft(n):
        k = np.arange(n)
        ang = 2.0 * np.pi * ((k[:, None] * k[None, :]) % n) / n
        return np.cos(ang), np.sin(ang)

    c128, s128 = dft(FFT_R)
    c256, s256 = dft(CTX_LEN)
    bd_lat = np.concatenate([c128, -s128], axis=1) / math.sqrt(SEQ * F_GROUP_DIM)
    bd_ctx = np.concatenate([c128, -s128], axis=1) / math.sqrt(CTX_LEN * F_GROUP_DIM)
    k1 = np.arange(FFT_R)
    tw = 2.0 * np.pi * ((k1[:, None] * k1[None, :]) % SEQ) / SEQ
    tc = jnp.broadcast_to(f32(np.cos(tw))[:, :, None], (FFT_R, FFT_R, LANES)).reshape(FFT_R, FFT_R * LANES)
    ts = jnp.broadcast_to(f32(np.sin(tw))[:, :, None], (FFT_R, FFT_R, LANES)).reshape(FFT_R, FFT_R * LANES)
    return dict(tq_lat=f32(tq_lat), tk_lat=f32(tk_lat), tq_ctx=f32(tq_ctx), tk_ctx=f32(tk_ctx),
                c128=bf(c128), s128=bf(s128), c256=bf(c256), s256=bf(s256),
                bd_lat=bf(bd_lat), bd_ctx=bf(bd_ctx), tc=tc, ts=ts)


def _arrange_weights(l, w_in, w_uq, w_uk, w_uv):
    sw = _half_swap_perm()
    wi = w_in[l]
    kr = wi[:, OFF_KR:OFF_F]
    w_ip = jnp.concatenate([
        wi[:, :OFF_KR], kr, kr[:, sw], jnp.zeros((D_MODEL, LANES - 2 * QK_ROPE), F32),
        wi[:, OFF_F:]], axis=1).astype(BF16)
    uq = w_uq[l].reshape(Q_LORA, N_HEADS, QK_NOPE + QK_ROPE)
    qr = uq[:, :, QK_NOPE:]
    wq = jnp.concatenate([uq, qr[:, :, sw]], axis=-1).reshape(Q_LORA, N_HEADS * HEAD_PAD).astype(BF16)
    uk = w_uk[l].reshape(KV_LORA, N_HEADS, QK_NOPE)
    top = jnp.concatenate([uk, jnp.zeros((KV_LORA, N_HEADS, HEAD_PAD - QK_NOPE), F32)], axis=-1)
    rmat = np.zeros((LANES, HEAD_PAD), np.float32)
    for i in range(QK_ROPE):
        for r in (i, QK_ROPE + i):
            rmat[r, QK_NOPE + i] = 1.0
            rmat[r, QK_NOPE + QK_ROPE + i] = 1.0
    bot = jnp.broadcast_to(jnp.asarray(rmat)[:, None, :], (LANES, N_HEADS, HEAD_PAD))
    wkt = jnp.transpose(jnp.concatenate([top, bot], axis=0), (1, 2, 0)).astype(BF16)
    uv = w_uv[l].reshape(KV_LORA, N_HEADS // 2, 2, V_HEAD)
    z = jnp.zeros_like(uv[:, :, 0])
    even = jnp.concatenate([uv[:, :, 0], z], axis=-1)
    odd = jnp.concatenate([z, uv[:, :, 1]], axis=-1)
    wv = jnp.stack([even, odd], axis=2).reshape(KV_LORA, N_HEADS * HEAD_PAD).astype(BF16)
    return w_ip, wq, wkt, wv


def kernel(x, c, ctx, c_ctx, w_mod, b_mod, w_in, g_q, w_uq, g_kv, w_uk, w_uv, w_ao, w_fo, w_out,
           ln1_g, ln1_b, ln2_g, ln2_b, w_ff_gate, w_ff_up, w_ff_down,
           w_router, b_router, w_e_gate, w_e_up, w_e_down):
    tb = _tables()
    t_lat = BATCH * SEQ
    t_ctx = BATCH * CTX_LEN

    cs = jnp.concatenate([c, c_ctx[None, :], jnp.zeros((8 - BATCH - 1, D_MODEL), F32)], axis=0)
    mods = _modulation(cs, w_mod.astype(BF16), b_mod).reshape(DEPTH, 8, 6, D_MODEL)
    pad2 = jnp.zeros((DEPTH, 8, 2, D_MODEL), F32)
    mods = jnp.concatenate([mods, pad2], axis=2)

    xl = x.reshape(t_lat, D_MODEL)
    xc = ctx.reshape(t_ctx, D_MODEL)

    for l in range(DEPTH):
        last = l == DEPTH - 1
        mod_x = mods[l, :BATCH]
        mod_c = jnp.broadcast_to(mods[l, BATCH:BATCH + 1], (BATCH, 8, D_MODEL))
        w_ip, wq, wkt, wv = _arrange_weights(l, w_in, w_uq, w_uk, w_uv)
        gq = g_q[l][None, :]
        gkv = g_kv[l][None, :]
        wao = w_ao[l].astype(BF16)
        wfo = w_fo[l].astype(BF16)
        wout = w_out[l].astype(BF16)
        g1, b1 = ln1_g[l][None, :], ln1_b[l][None, :]
        g2, b2 = ln2_g[l][None, :], ln2_b[l][None, :]

        qlat_c, ckr_c, wf_c, gates_c = _inproj(xc, mod_c, w_ip, gq, gkv, tb["tk_ctx"], tb["bd_ctx"], tm=CTX_LEN)
        q_c, kt_c, v_c = _qkv(qlat_c, ckr_c, tb["tq_ctx"], wq, wkt, wv, nb=BATCH, tm=CTX_LEN)

        qlat, ckr, wf, gates = _inproj(xl, mod_x, w_ip, gq, gkv, tb["tk_lat"], tb["bd_lat"], tm=1024)
        q, kt, v = _qkv(qlat, ckr, tb["tq_lat"], wq, wkt, wv, nb=BATCH, tm=512)
        attn = _attention(q, kt_c, v_c, kt, v, tq=ATTN_TQ, tk=ATTN_TK, unroll=ATTN_UNROLL)
        four = _fourier_latent(wf.reshape(BATCH, SEQ, 2 * FOURIER_DIM), tb["c128"], tb["s128"], tb["tc"], tb["ts"])

        if l % 2 == 0:
            i = l // 2
            x1, h2 = _mixer(attn.reshape(t_lat, -1), four.reshape(t_lat, -1), gates, xl, mod_x, g1, b1,
                            wao, wfo, wout, tm=512)
            wg = w_ff_gate[i].astype(BF16)
            wu = w_ff_up[i].astype(BF16)
            wd = w_ff_down[i].astype(BF16)
            xl_new = _ffn(h2, x1, mod_x, g2, b2, wg, wu, wd, tm=1024)
        else:
            i = l // 2
            wr = jnp.concatenate([w_router[i], jnp.zeros((D_MODEL, LANES - N_EXPERTS), F32)], axis=1)
            br = jnp.concatenate([b_router[i], jnp.zeros((LANES - N_EXPERTS,), F32)])[None, :]
            x1, h2, rinfo = _mixer(attn.reshape(t_lat, -1), four.reshape(t_lat, -1), gates, xl, mod_x, g1, b1,
                                   wao, wfo, wout, tm=512, router=(wr, br))
            wg = w_e_gate[i].astype(BF16)
            wu = w_e_up[i].astype(BF16)
            wd = w_e_down[i].astype(BF16)
            xl_new = _moe(h2, x1, rinfo, mod_x, g2, b2, wg, wu, wd)

        if not last:
            assert l % 2 == 0
            attn_c = _attention(q_c, kt_c, v_c, None, None, tq=CTX_LEN, tk=CTX_LEN)
            four_c = _fourier_ctx(wf_c.reshape(BATCH, CTX_LEN, 2 * FOURIER_DIM), tb["c256"], tb["s256"])
            x1c, h2c = _mixer(attn_c.reshape(t_ctx, -1), four_c.reshape(t_ctx, -1), gates_c, xc, mod_c,
                              g1, b1, wao, wfo, wout, tm=CTX_LEN)
            xc = _ffn(h2c, x1c, mod_c, g2, b2, wg, wu, wd, tm=CTX_LEN)
        xl = xl_new

    return xl.reshape(BATCH, SEQ, D_MODEL)
```

```python
import functools
import math

import numpy as np
import jax
import jax.numpy as jnp
from jax import lax
from jax.experimental import pallas as pl
from jax.experimental.pallas import tpu as pltpu

D_MODEL = 1024
BATCH = 2
SEQ = 16384
DEPTH = 2
GRID_W = 64
CTX_LEN = 256
N_HEADS = 16
QK_NOPE = 64
QK_ROPE = 32
ROPE_FREQS = QK_ROPE // 4
V_HEAD = 64
Q_LORA = 256
KV_LORA = 128
ROPE_THETA = 10000.0
SM_SCALE = (QK_NOPE + QK_ROPE) ** -0.5
F_GROUPS = 4
F_GROUP_DIM = 128
FOURIER_DIM = F_GROUPS * F_GROUP_DIM
OFF_KV = Q_LORA
OFF_KR = OFF_KV + KV_LORA
OFF_F = OFF_KR + QK_ROPE
OFF_G = OFF_F + FOURIER_DIM
D_FF = 2816
N_EXPERTS = 8
D_FF_EXPERT = 3584
ALPHA = (2 * DEPTH) ** 0.25
LN_EPS = 1e-6
RMS_EPS = 1e-6

LANES = 128
HEAD_PAD = 128
FFT_R = 128
V_ONES_LANE = (V_HEAD, 0)
VMEM_LIMIT = 56 * 1024 * 1024
ATTN_TQ = 1024
ATTN_TK = 512
ATTN_UNROLL = 16

BF16 = jnp.bfloat16
F32 = jnp.float32


def _cp(sem, vmem=VMEM_LIMIT):
    return pltpu.CompilerParams(dimension_semantics=sem, vmem_limit_bytes=vmem)


def _dot(a, b):
    return jnp.dot(a, b, preferred_element_type=F32)


def _ln(x):
    mu = jnp.mean(x, axis=-1, keepdims=True)
    xc = x - mu
    var = jnp.mean(xc * xc, axis=-1, keepdims=True)
    return xc * lax.rsqrt(var + LN_EPS)


def _rms(x, g):
    return x * lax.rsqrt(jnp.mean(x * x, axis=-1, keepdims=True) + RMS_EPS) * g


def _sigmoid(x):
    return 1.0 / (1.0 + jnp.exp(-x))


def _mod_kernel(cs_ref, w_ref, b_ref, o_ref):
    cs = cs_ref[...]
    a = (cs * _sigmoid(cs)).astype(BF16)
    o_ref[0] = _dot(a, w_ref[0]) + b_ref[0]


def _modulation(cs, w_mod, b_mod):
    n_chunk = 1024
    n_out = w_mod.shape[-1]
    return pl.pallas_call(
        _mod_kernel,
        grid=(DEPTH, n_out // n_chunk),
        in_specs=[
            pl.BlockSpec((8, D_MODEL), lambda l, j: (0, 0)),
            pl.BlockSpec((1, D_MODEL, n_chunk), lambda l, j: (l, 0, j)),
            pl.BlockSpec((1, 1, n_chunk), lambda l, j: (l, 0, j)),
        ],
        out_specs=pl.BlockSpec((1, 8, n_chunk), lambda l, j: (l, 0, j)),
        out_shape=jax.ShapeDtypeStruct((DEPTH, 8, n_out), F32),
        compiler_params=_cp(("parallel", "parallel")),
        name="mod",
    )(cs, w_mod, b_mod.reshape(DEPTH, 1, n_out))


IP_Q = 0
IP_KV = Q_LORA
IP_KR = IP_KV + KV_LORA
IP_F = IP_KR + LANES
IP_G = IP_F + FOURIER_DIM
IP_COLS = IP_G + 2 * D_MODEL


def _inproj_kernel(x_ref, mod_ref, w_ref, gq_ref, gkv_ref, tk_ref, bd_ref,
                   qlat_ref, ckr_ref, wf_ref, gates_ref):
    x = x_ref[...]
    mod = mod_ref[0]
    h = (_ln(x) * (1.0 + mod[1:2]) + mod[0:1]).astype(BF16)
    p0 = _dot(h, w_ref[:, IP_Q:IP_F])
    qlat_ref[...] = _rms(p0[:, IP_Q:IP_KV], gq_ref[...]).astype(BF16)
    ckr_ref[:, 0:KV_LORA] = _rms(p0[:, IP_KV:IP_KR], gkv_ref[...]).astype(BF16)
    ckr_ref[:, KV_LORA:] = (p0[:, IP_KR:IP_F] * tk_ref[...]).astype(BF16)
    uf = _dot(h, w_ref[:, IP_F:IP_G]).astype(BF16)
    for g in range(F_GROUPS):
        r = _dot(uf[:, g * LANES:(g + 1) * LANES], bd_ref[...])
        wf_ref[:, g * LANES:(g + 1) * LANES] = r[:, :LANES].astype(BF16)
        wf_ref[:, FOURIER_DIM + g * LANES:FOURIER_DIM + (g + 1) * LANES] = r[:, LANES:].astype(BF16)
    gc = 512
    for c in range(2 * D_MODEL // gc):
        gl = _dot(h, w_ref[:, IP_G + c * gc:IP_G + (c + 1) * gc])
        gates_ref[:, c * gc:(c + 1) * gc] = _sigmoid(gl).astype(BF16)


def _inproj(x2d, mod, w, gq, gkv, tk, bd, tm):
    t = x2d.shape[0]
    nb = mod.shape[0]
    tpb = t // nb // tm
    return pl.pallas_call(
        _inproj_kernel,
        grid=(t // tm,),
        in_specs=[
            pl.BlockSpec((tm, D_MODEL), lambda i: (i, 0)),
            pl.BlockSpec((1, 8, D_MODEL), lambda i: (i // tpb, 0, 0)),
            pl.BlockSpec((D_MODEL, IP_COLS), lambda i: (0, 0)),
            pl.BlockSpec((1, Q_LORA), lambda i: (0, 0)),
            pl.BlockSpec((1, KV_LORA), lambda i: (0, 0)),
            pl.BlockSpec((tm, LANES), lambda i: (i % tpb, 0)),
            pl.BlockSpec((LANES, 2 * LANES), lambda i: (0, 0)),
        ],
        out_specs=[
            pl.BlockSpec((tm, Q_LORA), lambda i: (i, 0)),
            pl.BlockSpec((tm, 2 * LANES), lambda i: (i, 0)),
            pl.BlockSpec((tm, 2 * FOURIER_DIM), lambda i: (i, 0)),
            pl.BlockSpec((tm, 2 * D_MODEL), lambda i: (i, 0)),
        ],
        out_shape=[
            jax.ShapeDtypeStruct((t, Q_LORA), BF16),
            jax.ShapeDtypeStruct((t, 2 * LANES), BF16),
            jax.ShapeDtypeStruct((t, 2 * FOURIER_DIM), BF16),
            jax.ShapeDtypeStruct((t, 2 * D_MODEL), BF16),
        ],
        compiler_params=_cp(("parallel",)),
        name="inproj",
    )(x2d, mod, w, gq, gkv, tk, bd)


def _qkv_kernel(qlat_ref, ckr_ref, tq_ref, wq_ref, wkt_ref, wv_ref, q_ref, kt_ref, v_ref):
    ql = qlat_ref[...]
    ckr = ckr_ref[...]
    tq = tq_ref[...]
    ckv = ckr[:, :KV_LORA]
    lane = lax.broadcasted_iota(jnp.int32, (ckr.shape[0], HEAD_PAD), 1)
    q_all = _dot(ql, wq_ref[...])
    v_all = _dot(ckv, wv_ref[...])
    for h in range(N_HEADS):
        cols = slice(h * HEAD_PAD, (h + 1) * HEAD_PAD)
        q_ref[0, h] = (q_all[:, cols] * tq).astype(BF16)
        kt_ref[0, h] = lax.dot_general(
            wkt_ref[h], ckr, (((1,), (1,)), ((), ())), preferred_element_type=F32).astype(BF16)
        v_ref[0, h] = jnp.where(lane == V_ONES_LANE[h % 2], 1.0, v_all[:, cols]).astype(BF16)


def _qkv(qlat, ckr, tq, wq, wkt, wv, nb, tm):
    t = qlat.shape[0]
    n = t // nb
    tpb = n // tm
    return pl.pallas_call(
        _qkv_kernel,
        grid=(t // tm,),
        in_specs=[
            pl.BlockSpec((tm, Q_LORA), lambda i: (i, 0)),
            pl.BlockSpec((tm, 2 * LANES), lambda i: (i, 0)),
            pl.BlockSpec((tm, HEAD_PAD), lambda i: (i % tpb, 0)),
            pl.BlockSpec((Q_LORA, N_HEADS * HEAD_PAD), lambda i: (0, 0)),
            pl.BlockSpec((N_HEADS, HEAD_PAD, 2 * LANES), lambda i: (0, 0, 0)),
            pl.BlockSpec((KV_LORA, N_HEADS * HEAD_PAD), lambda i: (0, 0)),
        ],
        out_specs=[
            pl.BlockSpec((1, N_HEADS, tm, HEAD_PAD), lambda i: (i // tpb, 0, i % tpb, 0)),
            pl.BlockSpec((1, N_HEADS, HEAD_PAD, tm), lambda i: (i // tpb, 0, 0, i % tpb)),
            pl.BlockSpec((1, N_HEADS, tm, HEAD_PAD), lambda i: (i // tpb, 0, i % tpb, 0)),
        ],
        out_shape=[
            jax.ShapeDtypeStruct((nb, N_HEADS, n, HEAD_PAD), BF16),
            jax.ShapeDtypeStruct((nb, N_HEADS, HEAD_PAD, n), BF16),
            jax.ShapeDtypeStruct((nb, N_HEADS, n, HEAD_PAD), BF16),
        ],
        compiler_params=_cp(("parallel",)),
        name="qkv",
    )(qlat, ckr, tq, wq, wkt, wv)


def _attn_kernel(q_ref, ktc_ref, vc_ref, *rest, n_chunks, tk, unroll):
    if n_chunks:
        kt_ref, v_ref, o_ref = rest
    else:
        (o_ref,) = rest

    def step(q, kt, v, m, acc):
        s = _dot(q, kt)
        m_new = jnp.max(s, axis=1, keepdims=True)
        if m is not None:
            m_new = jnp.maximum(m, m_new)
        p = jnp.exp2((s - m_new).astype(BF16))
        pv = _dot(p, v)
        if m is not None:
            pv = jnp.exp2(m - m_new) * acc + pv
        return m_new, pv

    qs = [q_ref[0, hh] for hh in range(2)]
    carry = []
    for hh in range(2):
        carry += step(qs[hh], ktc_ref[0, hh], vc_ref[0, hh], None, None)

    if n_chunks:
        span = unroll * tk

        def group(base, carry):
            for u in range(unroll):
                off = base + u * tk
                if not isinstance(off, int):
                    off = pl.multiple_of(off, tk)
                out = []
                for hh in range(2):
                    out += step(qs[hh], kt_ref[0, hh, :, pl.ds(off, tk)], v_ref[0, hh, pl.ds(off, tk), :],
                                carry[2 * hh], carry[2 * hh + 1])
                carry = out
            return tuple(carry)

        carry = group(0, carry)
        carry = lax.fori_loop(1, n_chunks // unroll,
                              lambda g, c: group(pl.multiple_of(g * span, span), c), carry)

    lane = lax.broadcasted_iota(jnp.int32, carry[1].shape, 1)
    outs = []
    for hh in range(2):
        acc = carry[2 * hh + 1]
        l = jnp.sum(jnp.where(lane == V_ONES_LANE[hh], acc, 0.0), axis=1, keepdims=True)
        outs.append(acc * (1.0 / l))
    o_ref[0] = jnp.where(lane < V_HEAD, outs[0], outs[1]).astype(BF16)


def _attention(q, kt_c, v_c, kt, v, tq, tk, unroll=1):
    nb, _, n, _ = q.shape
    nc = kt_c.shape[-1]
    n_chunks = 0 if kt is None else kt.shape[-1] // tk
    in_specs = [
        pl.BlockSpec((1, 2, tq, HEAD_PAD), lambda b, p, i: (b, p, i, 0)),
        pl.BlockSpec((1, 2, HEAD_PAD, nc), lambda b, p, i: (b, p, 0, 0)),
        pl.BlockSpec((1, 2, nc, HEAD_PAD), lambda b, p, i: (b, p, 0, 0)),
    ]
    args = [q, kt_c, v_c]
    if n_chunks:
        nk = kt.shape[-1]
        in_specs += [
            pl.BlockSpec((1, 2, HEAD_PAD, nk), lambda b, p, i: (b, p, 0, 0)),
            pl.BlockSpec((1, 2, nk, HEAD_PAD), lambda b, p, i: (b, p, 0, 0)),
        ]
        args += [kt, v]
    return pl.pallas_call(
        functools.partial(_attn_kernel, n_chunks=n_chunks, tk=tk, unroll=unroll),
        grid=(nb, N_HEADS // 2, n // tq),
        in_specs=in_specs,
        out_specs=pl.BlockSpec((1, tq, 2 * V_HEAD), lambda b, p, i: (b, i, p)),
        out_shape=jax.ShapeDtypeStruct((nb, n, N_HEADS * V_HEAD), BF16),
        compiler_params=_cp(("parallel", "parallel", "arbitrary")),
        name="attn",
    )(*args)


def _fft1_kernel(x_ref, c_ref, s_ref, tc_ref, ts_ref, o_ref, *, n2t):
    x = x_ref[0]
    cx = _dot(c_ref[...], x)
    sx = _dot(s_ref[...], x)
    w = 2 * FOURIER_DIM
    for t in range(n2t):
        re = slice(t * w, t * w + FOURIER_DIM)
        im = slice(t * w + FOURIER_DIM, (t + 1) * w)
        yr = cx[:, re] + sx[:, im]
        yi = cx[:, im] - sx[:, re]
        tc = jnp.concatenate([tc_ref[:, t * LANES:(t + 1) * LANES]] * F_GROUPS, axis=1)
        ts = jnp.concatenate([ts_ref[:, t * LANES:(t + 1) * LANES]] * F_GROUPS, axis=1)
        o_ref[0, :, re] = (yr * tc + yi * ts).astype(BF16)
        o_ref[0, :, im] = (yi * tc - yr * ts).astype(BF16)


def _fft2_kernel(y_ref, c_ref, s_ref, o_ref, *, k1t):
    for t in range(k1t):
        y = y_ref[0, t]
        zr = _dot(c_ref[...], y[:, :FOURIER_DIM]) + _dot(s_ref[...], y[:, FOURIER_DIM:])
        o_ref[0, :, t * FOURIER_DIM:(t + 1) * FOURIER_DIM] = zr.astype(BF16)


def _fourier_latent(wf, cmat, smat, tc, ts):
    nb = wf.shape[0]
    w = 2 * FOURIER_DIM
    n2t = 8
    y = pl.pallas_call(
        functools.partial(_fft1_kernel, n2t=n2t),
        grid=(nb, FFT_R // n2t),
        in_specs=[
            pl.BlockSpec((1, FFT_R, n2t * w), lambda b, j: (b, 0, j)),
            pl.BlockSpec((FFT_R, FFT_R), lambda b, j: (0, 0)),
            pl.BlockSpec((FFT_R, FFT_R), lambda b, j: (0, 0)),
            pl.BlockSpec((FFT_R, n2t * LANES), lambda b, j: (0, j)),
            pl.BlockSpec((FFT_R, n2t * LANES), lambda b, j: (0, j)),
        ],
        out_specs=pl.BlockSpec((1, FFT_R, n2t * w), lambda b, j: (b, 0, j)),
        out_shape=jax.ShapeDtypeStruct((nb, FFT_R, FFT_R * w), BF16),
        compiler_params=_cp(("parallel", "parallel")),
        name="fft1",
    )(wf.reshape(nb, FFT_R, FFT_R * w), cmat, smat, tc, ts)
    k1t = 8
    four = pl.pallas_call(
        functools.partial(_fft2_kernel, k1t=k1t),
        grid=(nb, FFT_R // k1t),
        in_specs=[
            pl.BlockSpec((1, k1t, FFT_R, w), lambda b, j: (b, j, 0, 0)),
            pl.BlockSpec((FFT_R, FFT_R), lambda b, j: (0, 0)),
            pl.BlockSpec((FFT_R, FFT_R), lambda b, j: (0, 0)),
        ],
        out_specs=pl.BlockSpec((1, FFT_R, k1t * FOURIER_DIM), lambda b, j: (b, 0, j)),
        out_shape=jax.ShapeDtypeStruct((nb, FFT_R, FFT_R * FOURIER_DIM), BF16),
        compiler_params=_cp(("parallel", "parallel")),
        name="fft2",
    )(y.reshape(nb, FFT_R, FFT_R, w), cmat, smat)
    return four.reshape(nb, SEQ, FOURIER_DIM)


def _dft_ctx_kernel(x_ref, c_ref, s_ref, o_ref):
    x = x_ref[0]
    zr = _dot(c_ref[...], x[:, :FOURIER_DIM]) + _dot(s_ref[...], x[:, FOURIER_DIM:])
    o_ref[0] = zr.astype(BF16)


def _fourier_ctx(wf, cmat, smat):
    nb, n, w = wf.shape
    return pl.pallas_call(
        _dft_ctx_kernel,
        grid=(nb,),
        in_specs=[
            pl.BlockSpec((1, n, w), lambda b: (b, 0, 0)),
            pl.BlockSpec((n, n), lambda b: (0, 0)),
            pl.BlockSpec((n, n), lambda b: (0, 0)),
        ],
        out_specs=pl.BlockSpec((1, n, FOURIER_DIM), lambda b: (b, 0, 0)),
        out_shape=jax.ShapeDtypeStruct((nb, n, FOURIER_DIM), BF16),
        compiler_params=_cp(("parallel",)),
        name="dft_ctx",
    )(wf, cmat, smat)


def _mixer_kernel(attn_ref, four_ref, gates_ref, x_ref, mod_ref, g_ref, b_ref,
                  wao_ref, wfo_ref, wout_ref, *rest, route):
    if route:
        wr_ref, br_ref, x1_ref, h2_ref, rinfo_ref = rest
    else:
        x1_ref, h2_ref = rest
    mod = mod_ref[0]
    a = _dot(attn_ref[...], wao_ref[...])
    f = _dot(four_ref[...], wfo_ref[...])
    merged = gates_ref[:, :D_MODEL].astype(F32) * a + gates_ref[:, D_MODEL:].astype(F32) * f
    y = _dot(merged.astype(BF16), wout_ref[...])
    x1 = _ln(ALPHA * x_ref[...] + mod[2:3] * y) * g_ref[...] + b_ref[...]
    x1_ref[...] = x1
    h2 = _ln(x1) * (1.0 + mod[4:5]) + mod[3:4]
    h2_ref[...] = h2.astype(h2_ref.dtype)
    if route:
        logits = jnp.dot(h2, wr_ref[...], preferred_element_type=F32,
                         precision=lax.Precision.HIGHEST) + br_ref[...]
        lane = lax.broadcasted_iota(jnp.int32, logits.shape, 1)
        neg = jnp.float32(-jnp.inf)
        lg = jnp.where(lane < N_EXPERTS, logits, neg)
        m1 = jnp.max(lg, axis=1, keepdims=True)
        i1 = jnp.min(jnp.where(lg == m1, lane, LANES), axis=1, keepdims=True)
        lg2 = jnp.where(lane == i1, neg, lg)
        m2 = jnp.max(lg2, axis=1, keepdims=True)
        i2 = jnp.min(jnp.where(lg2 == m2, lane, LANES), axis=1, keepdims=True)
        e2 = jnp.exp(m2 - m1)
        w1 = 1.0 / (1.0 + e2)
        w2 = e2 * w1
        rinfo = jnp.where((lane == i1) | (lane == i2), 1.0, 0.0)
        rinfo = jnp.where(lane == RI_E1, i1.astype(F32), rinfo)
        rinfo = jnp.where(lane == RI_E2, i2.astype(F32), rinfo)
        rinfo = jnp.where(lane == RI_W1, w1, rinfo)
        rinfo_ref[...] = jnp.where(lane == RI_W2, w2, rinfo)


def _mixer(attn, four, gates, x2d, mod, g, b, wao, wfo, wout, tm, router=None):
    t = x2d.shape[0]
    nb = mod.shape[0]
    tpb = t // nb // tm
    row = lambda i: (i, 0)
    const = lambda i: (0, 0)
    in_specs = [
        pl.BlockSpec((tm, D_MODEL), row),
        pl.BlockSpec((tm, FOURIER_DIM), row),
        pl.BlockSpec((tm, 2 * D_MODEL), row),
        pl.BlockSpec((tm, D_MODEL), row),
        pl.BlockSpec((1, 8, D_MODEL), lambda i: (i // tpb, 0, 0)),
        pl.BlockSpec((1, D_MODEL), const),
        pl.BlockSpec((1, D_MODEL), const),
        pl.BlockSpec((D_MODEL, D_MODEL), const),
        pl.BlockSpec((FOURIER_DIM, D_MODEL), const),
        pl.BlockSpec((D_MODEL, D_MODEL), const),
    ]
    args = [attn, four, gates, x2d, mod, g, b, wao, wfo, wout]
    out_specs = [pl.BlockSpec((tm, D_MODEL), row), pl.BlockSpec((tm, D_MODEL), row)]
    h2_dtype = BF16 if router is None else F32
    out_shape = [jax.ShapeDtypeStruct((t, D_MODEL), F32), jax.ShapeDtypeStruct((t, D_MODEL), h2_dtype)]
    if router is not None:
        in_specs += [pl.BlockSpec((D_MODEL, LANES), const), pl.BlockSpec((1, LANES), const)]
        args += list(router)
        out_specs.append(pl.BlockSpec((tm, LANES), row))
        out_shape.append(jax.ShapeDtypeStruct((t, LANES), F32))
    return pl.pallas_call(
        functools.partial(_mixer_kernel, route=router is not None),
        grid=(t // tm,),
        in_specs=in_specs,
        out_specs=out_specs,
        out_shape=out_shape,
        compiler_params=_cp(("parallel",)),
        name="mixer",
    )(*args)


def _ffn_kernel(h_ref, x1_ref, mod_ref, g_ref, b_ref, wg_ref, wu_ref, wd_ref, o_ref):
    h = h_ref[...]
    gate = _dot(h, wg_ref[...])
    up = _dot(h, wu_ref[...])
    act = (gate * _sigmoid(gate) * up).astype(BF16)
    f = _dot(act, wd_ref[...])
    mod = mod_ref[0]
    o_ref[...] = _ln(ALPHA * x1_ref[...] + mod[5:6] * f) * g_ref[...] + b_ref[...]


def _ffn(h2, x1, mod, g, b, wg, wu, wd, tm):
    t = h2.shape[0]
    nb = mod.shape[0]
    tpb = t // nb // tm
    dff = wg.shape[1]
    row = lambda i: (i, 0)
    const = lambda i: (0, 0)
    resident = pl.Buffered(1)
    return pl.pallas_call(
        _ffn_kernel,
        grid=(t // tm,),
        in_specs=[
            pl.BlockSpec((tm, D_MODEL), row),
            pl.BlockSpec((tm, D_MODEL), row),
            pl.BlockSpec((1, 8, D_MODEL), lambda i: (i // tpb, 0, 0)),
            pl.BlockSpec((1, D_MODEL), const),
            pl.BlockSpec((1, D_MODEL), const),
            pl.BlockSpec((D_MODEL, dff), const, pipeline_mode=resident),
            pl.BlockSpec((D_MODEL, dff), const, pipeline_mode=resident),
            pl.BlockSpec((dff, D_MODEL), const, pipeline_mode=resident),
        ],
        out_specs=pl.BlockSpec((tm, D_MODEL), row),
        out_shape=jax.ShapeDtypeStruct((t, D_MODEL), F32),
        compiler_params=_cp(("parallel",)),
        name="ffn",
    )(h2, x1, mod, g, b, wg, wu, wd)


RI_E1, RI_E2, RI_W1, RI_W2 = N_EXPERTS, N_EXPERTS + 1, N_EXPERTS + 2, N_EXPERTS + 3
MOE_TM = 512
MOE_TT = 512
DMA_ISSUE_UNROLL = 8


def _moe_rows(n_tokens):
    return 2 * n_tokens + N_EXPERTS * MOE_TM


def _scan_kernel(r_ref, o_ref, cnt_ref, run_ref):
    @pl.when(pl.program_id(0) == 0)
    def _():
        run_ref[...] = jnp.zeros_like(run_ref)

    r = r_ref[...]
    ts = r.shape[0]
    lane = lax.broadcasted_iota(jnp.int32, r.shape, 1)
    sel = jnp.where(lane < N_EXPERTS, r, 0.0)
    rows = lax.broadcasted_iota(jnp.int32, (ts, ts), 0)
    cols = lax.broadcasted_iota(jnp.int32, (ts, ts), 1)
    ltri = jnp.where(rows > cols, 1.0, 0.0).astype(BF16)
    prefix = _dot(ltri, sel.astype(BF16)) + run_ref[0:1, :]
    lane_f = lane.astype(F32)
    pick = lambda k: jnp.sum(jnp.where(lane == k, r, 0.0), axis=1, keepdims=True)
    e1, e2 = pick(RI_E1), pick(RI_E2)
    p1 = jnp.sum(jnp.where(lane_f == e1, prefix, 0.0), axis=1, keepdims=True)
    p2 = jnp.sum(jnp.where(lane_f == e2, prefix, 0.0), axis=1, keepdims=True)
    out = jnp.where(lane == 0, e1, jnp.where(lane == 1, e2, jnp.where(lane == 2, p1, jnp.where(lane == 3, p2, 0.0))))
    o_ref[...] = out.astype(jnp.int32)
    run_ref[...] = run_ref[...] + jnp.sum(sel, axis=0, keepdims=True)
    cnt_ref[...] = run_ref[...]


def _route_scan(rinfo, ts=512):
    t = rinfo.shape[0]
    return pl.pallas_call(
        _scan_kernel,
        grid=(t // ts,),
        in_specs=[pl.BlockSpec((ts, LANES), lambda i: (i, 0))],
        out_specs=[pl.BlockSpec((ts, LANES), lambda i: (i, 0)), pl.BlockSpec((8, LANES), lambda i: (0, 0))],
        out_shape=[jax.ShapeDtypeStruct((t, LANES), jnp.int32), jax.ShapeDtypeStruct((8, LANES), F32)],
        scratch_shapes=[pltpu.VMEM((8, LANES), F32)],
        compiler_params=_cp(("arbitrary",)),
        name="route_scan",
    )(rinfo)


def _row_copy(src_ref, src_row, dst_ref, dst_row, sem):
    return pltpu.make_async_copy(src_ref.at[pl.ds(src_row, 1)], dst_ref.at[pl.ds(dst_row, 1)], sem)


def _dispatch_kernel(dest_ref, h_ref, xs_ref, sem):
    def issue(t, c):
        for k in range(2):
            _row_copy(h_ref, t, xs_ref, dest_ref[0, k, t], sem).start(priority=k)
        return c

    lax.fori_loop(0, MOE_TT, issue, 0, unroll=DMA_ISSUE_UNROLL)
    for k in range(2):
        pltpu.make_async_copy(h_ref, xs_ref.at[pl.ds(0, MOE_TT)], sem).wait()


def _dispatch(dest, h2):
    t = h2.shape[0]
    return pl.pallas_call(
        _dispatch_kernel,
        grid=(t // MOE_TT,),
        in_specs=[
            pl.BlockSpec((1, 2, MOE_TT), lambda i: (i, 0, 0), memory_space=pltpu.SMEM),
            pl.BlockSpec((MOE_TT, D_MODEL), lambda i: (i, 0)),
        ],
        out_specs=pl.BlockSpec(memory_space=pl.ANY),
        out_shape=jax.ShapeDtypeStruct((_moe_rows(t), D_MODEL), F32),
        scratch_shapes=[pltpu.SemaphoreType.DMA(())],
        compiler_params=_cp(("arbitrary",)),
        name="moe_dispatch",
    )(dest, h2)


def _experts_kernel(te_ref, tv_ref, nt_ref, xs_ref, wg_ref, wu_ref, wd_ref, y_ref, acc_ref):
    j = pl.program_id(1)

    @pl.when(pl.program_id(0) < nt_ref[0])
    def _():
        rows = lax.broadcasted_iota(jnp.int32, xs_ref.shape, 0)
        h = jnp.where(rows < tv_ref[pl.program_id(0)], xs_ref[...], 0.0).astype(BF16)
        gate = _dot(h, wg_ref[0])
        up = _dot(h, wu_ref[0])
        act = (gate * _sigmoid(gate) * up).astype(BF16)
        part = _dot(act, wd_ref[0])

        @pl.when(j == 0)
        def _():
            acc_ref[...] = part

        @pl.when(j > 0)
        def _():
            acc_ref[...] += part

        @pl.when(j == pl.num_programs(1) - 1)
        def _():
            y_ref[...] = acc_ref[...]

    @pl.when((pl.program_id(0) >= nt_ref[0]) & (j == pl.num_programs(1) - 1))
    def _():
        y_ref[...] = jnp.zeros_like(y_ref)


def _experts(tile_expert, tile_valid, n_tiles_used, xs, wg, wu, wd, fc):
    ne, _, dff = wg.shape
    grid_spec = pltpu.PrefetchScalarGridSpec(
        num_scalar_prefetch=3,
        grid=(xs.shape[0] // MOE_TM, dff // fc),
        in_specs=[
            pl.BlockSpec((MOE_TM, D_MODEL), lambda i, j, te, tv, nt: (i, 0)),
            pl.BlockSpec((1, D_MODEL, fc), lambda i, j, te, tv, nt: (te[i], 0, j)),
            pl.BlockSpec((1, D_MODEL, fc), lambda i, j, te, tv, nt: (te[i], 0, j)),
            pl.BlockSpec((1, fc, D_MODEL), lambda i, j, te, tv, nt: (te[i], j, 0)),
        ],
        out_specs=pl.BlockSpec((MOE_TM, D_MODEL), lambda i, j, te, tv, nt: (i, 0)),
        scratch_shapes=[pltpu.VMEM((MOE_TM, D_MODEL), F32)],
    )
    return pl.pallas_call(
        _experts_kernel,
        grid_spec=grid_spec,
        out_shape=jax.ShapeDtypeStruct(xs.shape, F32),
        compiler_params=_cp(("arbitrary", "arbitrary")),
        name="moe_experts",
    )(tile_expert, tile_valid, n_tiles_used, xs, wg, wu, wd)


def _combine_kernel(dest_ref, dest_next_ref, y_ref, x1_ref, rinfo_ref, mod_ref, g_ref, b_ref, o_ref,
                    ybuf_ref, sems):
    i = pl.program_id(0)
    slot = i % 2

    def gather(d_ref, s):
        def issue(t, c):
            for k in range(2):
                _row_copy(y_ref, d_ref[0, k, t], ybuf_ref.at[s, k], t, sems.at[s]).start(priority=k)
            return c
        lax.fori_loop(0, MOE_TT, issue, 0, unroll=DMA_ISSUE_UNROLL)

    @pl.when(i == 0)
    def _():
        gather(dest_ref, slot)

    @pl.when(i + 1 < pl.num_programs(0))
    def _():
        gather(dest_next_ref, 1 - slot)

    for k in range(2):
        pltpu.make_async_copy(y_ref.at[pl.ds(0, MOE_TT)], ybuf_ref.at[slot, k], sems.at[slot]).wait()

    r = rinfo_ref[...]
    lane = lax.broadcasted_iota(jnp.int32, r.shape, 1)
    w1 = jnp.sum(jnp.where(lane == RI_W1, r, 0.0), axis=1, keepdims=True)
    w2 = jnp.sum(jnp.where(lane == RI_W2, r, 0.0), axis=1, keepdims=True)
    f = w1 * ybuf_ref[slot, 0] + w2 * ybuf_ref[slot, 1]
    mod = mod_ref[0]
    o_ref[...] = _ln(ALPHA * x1_ref[...] + mod[5:6] * f) * g_ref[...] + b_ref[...]


def _combine(dest, y, x1, rinfo, mod, g, b):
    t = x1.shape[0]
    nb = mod.shape[0]
    tpb = t // nb // MOE_TT
    row = lambda i: (i, 0)
    const = lambda i: (0, 0)
    n_steps = t // MOE_TT
    return pl.pallas_call(
        _combine_kernel,
        grid=(n_steps,),
        in_specs=[
            pl.BlockSpec((1, 2, MOE_TT), lambda i: (i, 0, 0), memory_space=pltpu.SMEM),
            pl.BlockSpec((1, 2, MOE_TT), lambda i: (jnp.minimum(i + 1, n_steps - 1), 0, 0),
                         memory_space=pltpu.SMEM),
            pl.BlockSpec(memory_space=pl.ANY),
            pl.BlockSpec((MOE_TT, D_MODEL), row),
            pl.BlockSpec((MOE_TT, LANES), row),
            pl.BlockSpec((1, 8, D_MODEL), lambda i: (i // tpb, 0, 0)),
            pl.BlockSpec((1, D_MODEL), const),
            pl.BlockSpec((1, D_MODEL), const),
        ],
        out_specs=pl.BlockSpec((MOE_TT, D_MODEL), row),
        out_shape=jax.ShapeDtypeStruct((t, D_MODEL), F32),
        scratch_shapes=[pltpu.VMEM((2, 2, MOE_TT, D_MODEL), F32), pltpu.SemaphoreType.DMA((2,))],
        compiler_params=_cp(("arbitrary",)),
        name="moe_combine",
    )(dest, dest, y, x1, rinfo, mod, g, b)


def _moe(h2, x1, rinfo, mod, g, b, wg, wu, wd):
    t = h2.shape[0]
    route, counts = _route_scan(rinfo)
    cnt = counts[0, :N_EXPERTS].astype(jnp.int32)
    padded = (cnt + MOE_TM - 1) // MOE_TM * MOE_TM
    ends = jnp.cumsum(padded)
    starts = ends - padded
    dest = jnp.stack([starts[route[:, 0]] + route[:, 2], starts[route[:, 1]] + route[:, 3]])
    dest = dest.reshape(2, t // MOE_TT, MOE_TT).transpose(1, 0, 2)
    tile_row0 = jnp.arange(_moe_rows(t) // MOE_TM, dtype=jnp.int32) * MOE_TM
    tile_expert = jnp.minimum(jnp.sum(tile_row0[:, None] >= ends[None, :], axis=1), N_EXPERTS - 1).astype(jnp.int32)
    tile_valid = jnp.clip(starts[tile_expert] + cnt[tile_expert] - tile_row0, 0, MOE_TM).astype(jnp.int32)
    n_tiles_used = (ends[-1:] // MOE_TM).astype(jnp.int32)
    xs = _dispatch(dest, h2)
    y = _experts(tile_expert, tile_valid, n_tiles_used, xs, wg, wu, wd, fc=D_FF_EXPERT // 2)
    return _combine(dest, y, x1, rinfo, mod, g, b)


def _rope_np(n_tokens):
    rows = np.repeat(np.arange(n_tokens // GRID_W), GRID_W)
    cols = np.tile(np.arange(GRID_W), n_tokens // GRID_W)
    pos = np.stack([rows, cols], axis=-1).astype(np.float64)
    inv_freq = 1.0 / (ROPE_THETA ** (np.arange(ROPE_FREQS, dtype=np.float64) / ROPE_FREQS))
    ang = pos[..., None] * inv_freq
    cos, sin = np.cos(ang), np.sin(ang)
    cos32 = np.broadcast_to(cos[:, :, None, :], (n_tokens, 2, 2, ROPE_FREQS)).reshape(n_tokens, QK_ROPE)
    sgn = np.array([-1.0, 1.0])[None, None, :, None]
    sin32 = (np.broadcast_to(sin[:, :, None, :], (n_tokens, 2, 2, ROPE_FREQS)) * sgn).reshape(n_tokens, QK_ROPE)
    return cos32, sin32


def _half_swap_perm():
    p = np.arange(QK_ROPE).reshape(2, 2, ROPE_FREQS)
    return p[:, ::-1, :].reshape(QK_ROPE)


def _tables():
    c0 = SM_SCALE * math.log2(math.e)
    cos32, sin32 = _rope_np(SEQ)
    one64 = np.ones((SEQ, QK_NOPE))
    tq_lat = c0 * np.concatenate([one64, cos32, sin32], axis=1)
    tk_lat = np.concatenate([cos32, sin32, np.zeros((SEQ, LANES - 2 * QK_ROPE))], axis=1)
    tq_ctx = c0 * np.concatenate([np.ones((CTX_LEN, QK_NOPE + QK_ROPE)), np.zeros((CTX_LEN, QK_ROPE))], axis=1)
    tk_ctx = np.concatenate([np.ones((CTX_LEN, QK_ROPE)), np.zeros((CTX_LEN, LANES - QK_ROPE))], axis=1)
    f32 = lambda a: jnp.asarray(a, F32)
    bf = lambda a: jnp.asarray(a, BF16)

    def dft(n):
        k = np.arange(n)
        ang = 2.0 * np.pi * ((k[:, None] * k[None, :]) % n) / n
        return np.cos(ang), np.sin(ang)

    c128, s128 = dft(FFT_R)
    c256, s256 = dft(CTX_LEN)
    bd_lat = np.concatenate([c128, -s128], axis=1) / math.sqrt(SEQ * F_GROUP_DIM)
    bd_ctx = np.concatenate([c128, -s128], axis=1) / math.sqrt(CTX_LEN * F_GROUP_DIM)
    k1 = np.arange(FFT_R)
    tw = 2.0 * np.pi * ((k1[:, None] * k1[None, :]) % SEQ) / SEQ
    tc = jnp.broadcast_to(f32(np.cos(tw))[:, :, None], (FFT_R, FFT_R, LANES)).reshape(FFT_R, FFT_R * LANES)
    ts = jnp.broadcast_to(f32(np.sin(tw))[:, :, None], (FFT_R, FFT_R, LANES)).reshape(FFT_R, FFT_R * LANES)
    return dict(tq_lat=f32(tq_lat), tk_lat=f32(tk_lat), tq_ctx=f32(tq_ctx), tk_ctx=f32(tk_ctx),
                c128=bf(c128), s128=bf(s128), c256=bf(c256), s256=bf(s256),
                bd_lat=bf(bd_lat), bd_ctx=bf(bd_ctx), tc=tc, ts=ts)


def _arrange_weights(l, w_in, w_uq, w_uk, w_uv):
    sw = _half_swap_perm()
    wi = w_in[l]
    kr = wi[:, OFF_KR:OFF_F]
    w_ip = jnp.concatenate([
        wi[:, :OFF_KR], kr, kr[:, sw], jnp.zeros((D_MODEL, LANES - 2 * QK_ROPE), F32),
        wi[:, OFF_F:]], axis=1).astype(BF16)
    uq = w_uq[l].reshape(Q_LORA, N_HEADS, QK_NOPE + QK_ROPE)
    qr = uq[:, :, QK_NOPE:]
    wq = jnp.concatenate([uq, qr[:, :, sw]], axis=-1).reshape(Q_LORA, N_HEADS * HEAD_PAD).astype(BF16)
    uk = w_uk[l].reshape(KV_LORA, N_HEADS, QK_NOPE)
    top = jnp.concatenate([uk, jnp.zeros((KV_LORA, N_HEADS, HEAD_PAD - QK_NOPE), F32)], axis=-1)
    rmat = np.zeros((LANES, HEAD_PAD), np.float32)
    for i in range(QK_ROPE):
        for r in (i, QK_ROPE + i):
            rmat[r, QK_NOPE + i] = 1.0
            rmat[r, QK_NOPE + QK_ROPE + i] = 1.0
    bot = jnp.broadcast_to(jnp.asarray(rmat)[:, None, :], (LANES, N_HEADS, HEAD_PAD))
    wkt = jnp.transpose(jnp.concatenate([top, bot], axis=0), (1, 2, 0)).astype(BF16)
    uv = w_uv[l].reshape(KV_LORA, N_HEADS // 2, 2, V_HEAD)
    z = jnp.zeros_like(uv[:, :, 0])
    even = jnp.concatenate([uv[:, :, 0], z], axis=-1)
    odd = jnp.concatenate([z, uv[:, :, 1]], axis=-1)
    wv = jnp.stack([even, odd], axis=2).reshape(KV_LORA, N_HEADS * HEAD_PAD).astype(BF16)
    return w_ip, wq, wkt, wv


def kernel(x, c, ctx, c_ctx, w_mod, b_mod, w_in, g_q, w_uq, g_kv, w_uk, w_uv, w_ao, w_fo, w_out,
           ln1_g, ln1_b, ln2_g, ln2_b, w_ff_gate, w_ff_up, w_ff_down,
           w_router, b_router, w_e_gate, w_e_up, w_e_down):
    tb = _tables()
    t_lat = BATCH * SEQ
    t_ctx = BATCH * CTX_LEN

    cs = jnp.concatenate([c, c_ctx[None, :], jnp.zeros((8 - BATCH - 1, D_MODEL), F32)], axis=0)
    mods = _modulation(cs, w_mod.astype(BF16), b_mod).reshape(DEPTH, 8, 6, D_MODEL)
    pad2 = jnp.zeros((DEPTH, 8, 2, D_MODEL), F32)
    mods = jnp.concatenate([mods, pad2], axis=2)

    xl = x.reshape(t_lat, D_MODEL)
    xc = ctx.reshape(t_ctx, D_MODEL)

    for l in range(DEPTH):
        last = l == DEPTH - 1
        mod_x = mods[l, :BATCH]
        mod_c = jnp.broadcast_to(mods[l, BATCH:BATCH + 1], (BATCH, 8, D_MODEL))
        w_ip, wq, wkt, wv = _arrange_weights(l, w_in, w_uq, w_uk, w_uv)
        gq = g_q[l][None, :]
        gkv = g_kv[l][None, :]
        wao = w_ao[l].astype(BF16)
        wfo = w_fo[l].astype(BF16)
        wout = w_out[l].astype(BF16)
        g1, b1 = ln1_g[l][None, :], ln1_b[l][None, :]
        g2, b2 = ln2_g[l][None, :], ln2_b[l][None, :]

        qlat_c, ckr_c, wf_c, gates_c = _inproj(xc, mod_c, w_ip, gq, gkv, tb["tk_ctx"], tb["bd_ctx"], tm=CTX_LEN)
        q_c, kt_c, v_c = _qkv(qlat_c, ckr_c, tb["tq_ctx"], wq, wkt, wv, nb=BATCH, tm=CTX_LEN)

        qlat, ckr, wf, gates = _inproj(xl, mod_x, w_ip, gq, gkv, tb["tk_lat"], tb["bd_lat"], tm=1024)
        q, kt, v = _qkv(qlat, ckr, tb["tq_lat"], wq, wkt, wv, nb=BATCH, tm=512)
        attn = _attention(q, kt_c, v_c, kt, v, tq=ATTN_TQ, tk=ATTN_TK, unroll=ATTN_UNROLL)
        four = _fourier_latent(wf.reshape(BATCH, SEQ, 2 * FOURIER_DIM), tb["c128"], tb["s128"], tb["tc"], tb["ts"])

        if l % 2 == 0:
            i = l // 2
            x1, h2 = _mixer(attn.reshape(t_lat, -1), four.reshape(t_lat, -1), gates, xl, mod_x, g1, b1,
                            wao, wfo, wout, tm=1024)
            wg = w_ff_gate[i].astype(BF16)
            wu = w_ff_up[i].astype(BF16)
            wd = w_ff_down[i].astype(BF16)
            xl_new = _ffn(h2, x1, mod_x, g2, b2, wg, wu, wd, tm=1024)
        else:
            i = l // 2
            wr = jnp.concatenate([w_router[i], jnp.zeros((D_MODEL, LANES - N_EXPERTS), F32)], axis=1)
            br = jnp.concatenate([b_router[i], jnp.zeros((LANES - N_EXPERTS,), F32)])[None, :]
            x1, h2, rinfo = _mixer(attn.reshape(t_lat, -1), four.reshape(t_lat, -1), gates, xl, mod_x, g1, b1,
                                   wao, wfo, wout, tm=512, router=(wr, br))
            wg = w_e_gate[i].astype(BF16)
            wu = w_e_up[i].astype(BF16)
            wd = w_e_down[i].astype(BF16)
            xl_new = _moe(h2, x1, rinfo, mod_x, g2, b2, wg, wu, wd)

        if not last:
            assert l % 2 == 0
            attn_c = _attention(q_c, kt_c, v_c, None, None, tq=CTX_LEN, tk=CTX_LEN)
            four_c = _fourier_ctx(wf_c.reshape(BATCH, CTX_LEN, 2 * FOURIER_DIM), tb["c256"], tb["s256"])
            x1c, h2c = _mixer(attn_c.reshape(t_ctx, -1), four_c.reshape(t_ctx, -1), gates_c, xc, mod_c,
                              g1, b1, wao, wfo, wout, tm=CTX_LEN)
            xc = _ffn(h2c, x1c, mod_c, g2, b2, wg, wu, wd, tm=CTX_LEN)
        xl = xl_new

    return xl.reshape(BATCH, SEQ, D_MODEL)
```

```python
import functools
import math

import numpy as np
import jax
import jax.numpy as jnp
from jax import lax
from jax.experimental import pallas as pl
from jax.experimental.pallas import tpu as pltpu

D_MODEL = 1024
BATCH = 2
SEQ = 16384
DEPTH = 2
GRID_W = 64
CTX_LEN = 256
N_HEADS = 16
QK_NOPE = 64
QK_ROPE = 32
ROPE_FREQS = QK_ROPE // 4
V_HEAD = 64
Q_LORA = 256
KV_LORA = 128
ROPE_THETA = 10000.0
SM_SCALE = (QK_NOPE + QK_ROPE) ** -0.5
F_GROUPS = 4
F_GROUP_DIM = 128
FOURIER_DIM = F_GROUPS * F_GROUP_DIM
OFF_KV = Q_LORA
OFF_KR = OFF_KV + KV_LORA
OFF_F = OFF_KR + QK_ROPE
OFF_G = OFF_F + FOURIER_DIM
D_FF = 2816
N_EXPERTS = 8
D_FF_EXPERT = 3584
ALPHA = (2 * DEPTH) ** 0.25
LN_EPS = 1e-6
RMS_EPS = 1e-6

LANES = 128
HEAD_PAD = 128
FFT_R = 128
V_ONES_LANE = (V_HEAD, 0)
VMEM_LIMIT = 56 * 1024 * 1024
ATTN_TQ = 1024
ATTN_TK = 512
ATTN_PEEL = 16
ATTN_UNROLL = 8

BF16 = jnp.bfloat16
F32 = jnp.float32


def _cp(sem, vmem=VMEM_LIMIT):
    return pltpu.CompilerParams(dimension_semantics=sem, vmem_limit_bytes=vmem)


def _dot(a, b):
    return jnp.dot(a, b, preferred_element_type=F32)


def _ln(x):
    mu = jnp.mean(x, axis=-1, keepdims=True)
    xc = x - mu
    var = jnp.mean(xc * xc, axis=-1, keepdims=True)
    return xc * lax.rsqrt(var + LN_EPS)


def _rms(x, g):
    return x * lax.rsqrt(jnp.mean(x * x, axis=-1, keepdims=True) + RMS_EPS) * g


def _sigmoid(x):
    return 1.0 / (1.0 + jnp.exp(-x))


def _mod_kernel(cs_ref, w_ref, b_ref, o_ref):
    cs = cs_ref[...]
    a = (cs * _sigmoid(cs)).astype(BF16)
    o_ref[0] = _dot(a, w_ref[0]) + b_ref[0]


def _modulation(cs, w_mod, b_mod):
    n_chunk = 1024
    n_out = w_mod.shape[-1]
    return pl.pallas_call(
        _mod_kernel,
        grid=(DEPTH, n_out // n_chunk),
        in_specs=[
            pl.BlockSpec((8, D_MODEL), lambda l, j: (0, 0)),
            pl.BlockSpec((1, D_MODEL, n_chunk), lambda l, j: (l, 0, j)),
            pl.BlockSpec((1, 1, n_chunk), lambda l, j: (l, 0, j)),
        ],
        out_specs=pl.BlockSpec((1, 8, n_chunk), lambda l, j: (l, 0, j)),
        out_shape=jax.ShapeDtypeStruct((DEPTH, 8, n_out), F32),
        compiler_params=_cp(("parallel", "parallel")),
        name="mod",
    )(cs, w_mod, b_mod.reshape(DEPTH, 1, n_out))


IP_Q = 0
IP_KV = Q_LORA
IP_KR = IP_KV + KV_LORA
IP_F = IP_KR + LANES
IP_G = IP_F + FOURIER_DIM
IP_COLS = IP_G + 2 * D_MODEL


def _inproj_kernel(x_ref, mod_ref, w_ref, gq_ref, gkv_ref, tk_ref, bd_ref,
                   qlat_ref, ckr_ref, wf_ref, gates_ref):
    x = x_ref[...]
    mod = mod_ref[0]
    h = (_ln(x) * (1.0 + mod[1:2]) + mod[0:1]).astype(BF16)
    p0 = _dot(h, w_ref[:, IP_Q:IP_F])
    qlat_ref[...] = _rms(p0[:, IP_Q:IP_KV], gq_ref[...]).astype(BF16)
    ckr_ref[:, 0:KV_LORA] = _rms(p0[:, IP_KV:IP_KR], gkv_ref[...]).astype(BF16)
    ckr_ref[:, KV_LORA:] = (p0[:, IP_KR:IP_F] * tk_ref[...]).astype(BF16)
    uf = _dot(h, w_ref[:, IP_F:IP_G]).astype(BF16)
    for g in range(F_GROUPS):
        r = _dot(uf[:, g * LANES:(g + 1) * LANES], bd_ref[...])
        wf_ref[:, g * LANES:(g + 1) * LANES] = r[:, :LANES].astype(BF16)
        wf_ref[:, FOURIER_DIM + g * LANES:FOURIER_DIM + (g + 1) * LANES] = r[:, LANES:].astype(BF16)
    gc = 512
    for c in range(2 * D_MODEL // gc):
        gl = _dot(h, w_ref[:, IP_G + c * gc:IP_G + (c + 1) * gc])
        gates_ref[:, c * gc:(c + 1) * gc] = _sigmoid(gl).astype(BF16)


def _inproj(x2d, mod, w, gq, gkv, tk, bd, tm):
    t = x2d.shape[0]
    nb = mod.shape[0]
    tpb = t // nb // tm
    return pl.pallas_call(
        _inproj_kernel,
        grid=(t // tm,),
        in_specs=[
            pl.BlockSpec((tm, D_MODEL), lambda i: (i, 0)),
            pl.BlockSpec((1, 8, D_MODEL), lambda i: (i // tpb, 0, 0)),
            pl.BlockSpec((D_MODEL, IP_COLS), lambda i: (0, 0)),
            pl.BlockSpec((1, Q_LORA), lambda i: (0, 0)),
            pl.BlockSpec((1, KV_LORA), lambda i: (0, 0)),
            pl.BlockSpec((tm, LANES), lambda i: (i % tpb, 0)),
            pl.BlockSpec((LANES, 2 * LANES), lambda i: (0, 0)),
        ],
        out_specs=[
            pl.BlockSpec((tm, Q_LORA), lambda i: (i, 0)),
            pl.BlockSpec((tm, 2 * LANES), lambda i: (i, 0)),
            pl.BlockSpec((tm, 2 * FOURIER_DIM), lambda i: (i, 0)),
            pl.BlockSpec((tm, 2 * D_MODEL), lambda i: (i, 0)),
        ],
        out_shape=[
            jax.ShapeDtypeStruct((t, Q_LORA), BF16),
            jax.ShapeDtypeStruct((t, 2 * LANES), BF16),
            jax.ShapeDtypeStruct((t, 2 * FOURIER_DIM), BF16),
            jax.ShapeDtypeStruct((t, 2 * D_MODEL), BF16),
        ],
        compiler_params=_cp(("parallel",)),
        name="inproj",
    )(x2d, mod, w, gq, gkv, tk, bd)


def _qkv_kernel(qlat_ref, ckr_ref, tq_ref, wq_ref, wkt_ref, wv_ref, q_ref, kt_ref, v_ref):
    ql = qlat_ref[...]
    ckr = ckr_ref[...]
    tq = tq_ref[...]
    ckv = ckr[:, :KV_LORA]
    lane = lax.broadcasted_iota(jnp.int32, (ckr.shape[0], HEAD_PAD), 1)
    q_all = _dot(ql, wq_ref[...])
    v_all = _dot(ckv, wv_ref[...])
    for h in range(N_HEADS):
        cols = slice(h * HEAD_PAD, (h + 1) * HEAD_PAD)
        q_ref[0, h] = (q_all[:, cols] * tq).astype(BF16)
        kt_ref[0, h] = lax.dot_general(
            wkt_ref[h], ckr, (((1,), (1,)), ((), ())), preferred_element_type=F32).astype(BF16)
        v_ref[0, h] = jnp.where(lane == V_ONES_LANE[h % 2], 1.0, v_all[:, cols]).astype(BF16)


def _qkv(qlat, ckr, tq, wq, wkt, wv, nb, tm):
    t = qlat.shape[0]
    n = t // nb
    tpb = n // tm
    return pl.pallas_call(
        _qkv_kernel,
        grid=(t // tm,),
        in_specs=[
            pl.BlockSpec((tm, Q_LORA), lambda i: (i, 0)),
            pl.BlockSpec((tm, 2 * LANES), lambda i: (i, 0)),
            pl.BlockSpec((tm, HEAD_PAD), lambda i: (i % tpb, 0)),
            pl.BlockSpec((Q_LORA, N_HEADS * HEAD_PAD), lambda i: (0, 0)),
            pl.BlockSpec((N_HEADS, HEAD_PAD, 2 * LANES), lambda i: (0, 0, 0)),
            pl.BlockSpec((KV_LORA, N_HEADS * HEAD_PAD), lambda i: (0, 0)),
        ],
        out_specs=[
            pl.BlockSpec((1, N_HEADS, tm, HEAD_PAD), lambda i: (i // tpb, 0, i % tpb, 0)),
            pl.BlockSpec((1, N_HEADS, HEAD_PAD, tm), lambda i: (i // tpb, 0, 0, i % tpb)),
            pl.BlockSpec((1, N_HEADS, tm, HEAD_PAD), lambda i: (i // tpb, 0, i % tpb, 0)),
        ],
        out_shape=[
            jax.ShapeDtypeStruct((nb, N_HEADS, n, HEAD_PAD), BF16),
            jax.ShapeDtypeStruct((nb, N_HEADS, HEAD_PAD, n), BF16),
            jax.ShapeDtypeStruct((nb, N_HEADS, n, HEAD_PAD), BF16),
        ],
        compiler_params=_cp(("parallel",)),
        name="qkv",
    )(qlat, ckr, tq, wq, wkt, wv)


def _attn_kernel(q_ref, ktc_ref, vc_ref, *rest, n_chunks, tk, peel, unroll):
    if n_chunks:
        kt_ref, v_ref, o_ref = rest
    else:
        (o_ref,) = rest

    def step(q, kt, v, m, acc):
        s = _dot(q, kt)
        m_new = jnp.max(s, axis=1, keepdims=True)
        if m is not None:
            m_new = jnp.maximum(m, m_new)
        p = jnp.exp2((s - m_new).astype(BF16))
        pv = _dot(p, v)
        if m is not None:
            pv = jnp.exp2(m - m_new) * acc + pv
        return m_new, pv

    qs = [q_ref[0, hh] for hh in range(2)]
    carry = []
    for hh in range(2):
        carry += step(qs[hh], ktc_ref[0, hh], vc_ref[0, hh], None, None)

    if n_chunks:
        span = unroll * tk

        def group(base, count, carry):
            for u in range(count):
                off = base + u * tk
                if not isinstance(off, int):
                    off = pl.multiple_of(off, tk)
                out = []
                for hh in range(2):
                    out += step(qs[hh], kt_ref[0, hh, :, pl.ds(off, tk)], v_ref[0, hh, pl.ds(off, tk), :],
                                carry[2 * hh], carry[2 * hh + 1])
                carry = out
            return tuple(carry)

        carry = group(0, peel, carry)
        carry = lax.fori_loop(0, (n_chunks - peel) // unroll,
                              lambda g, c: group(pl.multiple_of(peel * tk + g * span, tk), unroll, c), carry)

    lane = lax.broadcasted_iota(jnp.int32, carry[1].shape, 1)
    outs = []
    for hh in range(2):
        acc = carry[2 * hh + 1]
        l = jnp.sum(jnp.where(lane == V_ONES_LANE[hh], acc, 0.0), axis=1, keepdims=True)
        outs.append(acc * (1.0 / l))
    o_ref[0] = jnp.where(lane < V_HEAD, outs[0], outs[1]).astype(BF16)


def _attention(q, kt_c, v_c, kt, v, tq, tk, peel=0, unroll=1):
    nb, _, n, _ = q.shape
    nc = kt_c.shape[-1]
    n_chunks = 0 if kt is None else kt.shape[-1] // tk
    in_specs = [
        pl.BlockSpec((1, 2, tq, HEAD_PAD), lambda b, p, i: (b, p, i, 0)),
        pl.BlockSpec((1, 2, HEAD_PAD, nc), lambda b, p, i: (b, p, 0, 0)),
        pl.BlockSpec((1, 2, nc, HEAD_PAD), lambda b, p, i: (b, p, 0, 0)),
    ]
    args = [q, kt_c, v_c]
    if n_chunks:
        nk = kt.shape[-1]
        in_specs += [
            pl.BlockSpec((1, 2, HEAD_PAD, nk), lambda b, p, i: (b, p, 0, 0)),
            pl.BlockSpec((1, 2, nk, HEAD_PAD), lambda b, p, i: (b, p, 0, 0)),
        ]
        args += [kt, v]
    return pl.pallas_call(
        functools.partial(_attn_kernel, n_chunks=n_chunks, tk=tk, peel=peel, unroll=unroll),
        grid=(nb, N_HEADS // 2, n // tq),
        in_specs=in_specs,
        out_specs=pl.BlockSpec((1, tq, 2 * V_HEAD), lambda b, p, i: (b, i, p)),
        out_shape=jax.ShapeDtypeStruct((nb, n, N_HEADS * V_HEAD), BF16),
        compiler_params=_cp(("parallel", "parallel", "arbitrary")),
        name="attn",
    )(*args)


def _fft1_kernel(x_ref, c_ref, s_ref, tc_ref, ts_ref, o_ref, *, n2t):
    x = x_ref[0]
    cx = _dot(c_ref[...], x)
    sx = _dot(s_ref[...], x)
    w = 2 * FOURIER_DIM
    for t in range(n2t):
        re = slice(t * w, t * w + FOURIER_DIM)
        im = slice(t * w + FOURIER_DIM, (t + 1) * w)
        yr = cx[:, re] + sx[:, im]
        yi = cx[:, im] - sx[:, re]
        tc = jnp.concatenate([tc_ref[:, t * LANES:(t + 1) * LANES]] * F_GROUPS, axis=1)
        ts = jnp.concatenate([ts_ref[:, t * LANES:(t + 1) * LANES]] * F_GROUPS, axis=1)
        o_ref[0, :, re] = (yr * tc + yi * ts).astype(BF16)
        o_ref[0, :, im] = (yi * tc - yr * ts).astype(BF16)


def _fft2_kernel(y_ref, c_ref, s_ref, o_ref, *, k1t):
    for t in range(k1t):
        y = y_ref[0, t]
        zr = _dot(c_ref[...], y[:, :FOURIER_DIM]) + _dot(s_ref[...], y[:, FOURIER_DIM:])
        o_ref[0, :, t * FOURIER_DIM:(t + 1) * FOURIER_DIM] = zr.astype(BF16)


def _fourier_latent(wf, cmat, smat, tc, ts):
    nb = wf.shape[0]
    w = 2 * FOURIER_DIM
    n2t = 8
    y = pl.pallas_call(
        functools.partial(_fft1_kernel, n2t=n2t),
        grid=(nb, FFT_R // n2t),
        in_specs=[
            pl.BlockSpec((1, FFT_R, n2t * w), lambda b, j: (b, 0, j)),
            pl.BlockSpec((FFT_R, FFT_R), lambda b, j: (0, 0)),
            pl.BlockSpec((FFT_R, FFT_R), lambda b, j: (0, 0)),
            pl.BlockSpec((FFT_R, n2t * LANES), lambda b, j: (0, j)),
            pl.BlockSpec((FFT_R, n2t * LANES), lambda b, j: (0, j)),
        ],
        out_specs=pl.BlockSpec((1, FFT_R, n2t * w), lambda b, j: (b, 0, j)),
        out_shape=jax.ShapeDtypeStruct((nb, FFT_R, FFT_R * w), BF16),
        compiler_params=_cp(("parallel", "parallel")),
        name="fft1",
    )(wf.reshape(nb, FFT_R, FFT_R * w), cmat, smat, tc, ts)
    k1t = 8
    four = pl.pallas_call(
        functools.partial(_fft2_kernel, k1t=k1t),
        grid=(nb, FFT_R // k1t),
        in_specs=[
            pl.BlockSpec((1, k1t, FFT_R, w), lambda b, j: (b, j, 0, 0)),
            pl.BlockSpec((FFT_R, FFT_R), lambda b, j: (0, 0)),
            pl.BlockSpec((FFT_R, FFT_R), lambda b, j: (0, 0)),
        ],
        out_specs=pl.BlockSpec((1, FFT_R, k1t * FOURIER_DIM), lambda b, j: (b, 0, j)),
        out_shape=jax.ShapeDtypeStruct((nb, FFT_R, FFT_R * FOURIER_DIM), BF16),
        compiler_params=_cp(("parallel", "parallel")),
        name="fft2",
    )(y.reshape(nb, FFT_R, FFT_R, w), cmat, smat)
    return four.reshape(nb, SEQ, FOURIER_DIM)


def _dft_ctx_kernel(x_ref, c_ref, s_ref, o_ref):
    x = x_ref[0]
    zr = _dot(c_ref[...], x[:, :FOURIER_DIM]) + _dot(s_ref[...], x[:, FOURIER_DIM:])
    o_ref[0] = zr.astype(BF16)


def _fourier_ctx(wf, cmat, smat):
    nb, n, w = wf.shape
    return pl.pallas_call(
        _dft_ctx_kernel,
        grid=(nb,),
        in_specs=[
            pl.BlockSpec((1, n, w), lambda b: (b, 0, 0)),
            pl.BlockSpec((n, n), lambda b: (0, 0)),
            pl.BlockSpec((n, n), lambda b: (0, 0)),
        ],
        out_specs=pl.BlockSpec((1, n, FOURIER_DIM), lambda b: (b, 0, 0)),
        out_shape=jax.ShapeDtypeStruct((nb, n, FOURIER_DIM), BF16),
        compiler_params=_cp(("parallel",)),
        name="dft_ctx",
    )(wf, cmat, smat)


def _mixer_kernel(attn_ref, four_ref, gates_ref, x_ref, mod_ref, g_ref, b_ref,
                  wao_ref, wfo_ref, wout_ref, *rest, route):
    if route:
        wr_ref, br_ref, x1_ref, h2_ref, rinfo_ref = rest
    else:
        x1_ref, h2_ref = rest
    mod = mod_ref[0]
    a = _dot(attn_ref[...], wao_ref[...])
    f = _dot(four_ref[...], wfo_ref[...])
    merged = gates_ref[:, :D_MODEL].astype(F32) * a + gates_ref[:, D_MODEL:].astype(F32) * f
    y = _dot(merged.astype(BF16), wout_ref[...])
    x1 = _ln(ALPHA * x_ref[...] + mod[2:3] * y) * g_ref[...] + b_ref[...]
    x1_ref[...] = x1
    h2 = _ln(x1) * (1.0 + mod[4:5]) + mod[3:4]
    h2_ref[...] = h2.astype(h2_ref.dtype)
    if route:
        logits = jnp.dot(h2, wr_ref[...], preferred_element_type=F32,
                         precision=lax.Precision.HIGHEST) + br_ref[...]
        lane = lax.broadcasted_iota(jnp.int32, logits.shape, 1)
        neg = jnp.float32(-jnp.inf)
        lg = jnp.where(lane < N_EXPERTS, logits, neg)
        m1 = jnp.max(lg, axis=1, keepdims=True)
        i1 = jnp.min(jnp.where(lg == m1, lane, LANES), axis=1, keepdims=True)
        lg2 = jnp.where(lane == i1, neg, lg)
        m2 = jnp.max(lg2, axis=1, keepdims=True)
        i2 = jnp.min(jnp.where(lg2 == m2, lane, LANES), axis=1, keepdims=True)
        e2 = jnp.exp(m2 - m1)
        w1 = 1.0 / (1.0 + e2)
        w2 = e2 * w1
        rinfo = jnp.where((lane == i1) | (lane == i2), 1.0, 0.0)
        rinfo = jnp.where(lane == RI_E1, i1.astype(F32), rinfo)
        rinfo = jnp.where(lane == RI_E2, i2.astype(F32), rinfo)
        rinfo = jnp.where(lane == RI_W1, w1, rinfo)
        rinfo_ref[...] = jnp.where(lane == RI_W2, w2, rinfo)


def _mixer(attn, four, gates, x2d, mod, g, b, wao, wfo, wout, tm, router=None):
    t = x2d.shape[0]
    nb = mod.shape[0]
    tpb = t // nb // tm
    row = lambda i: (i, 0)
    const = lambda i: (0, 0)
    in_specs = [
        pl.BlockSpec((tm, D_MODEL), row),
        pl.BlockSpec((tm, FOURIER_DIM), row),
        pl.BlockSpec((tm, 2 * D_MODEL), row),
        pl.BlockSpec((tm, D_MODEL), row),
        pl.BlockSpec((1, 8, D_MODEL), lambda i: (i // tpb, 0, 0)),
        pl.BlockSpec((1, D_MODEL), const),
        pl.BlockSpec((1, D_MODEL), const),
        pl.BlockSpec((D_MODEL, D_MODEL), const),
        pl.BlockSpec((FOURIER_DIM, D_MODEL), const),
        pl.BlockSpec((D_MODEL, D_MODEL), const),
    ]
    args = [attn, four, gates, x2d, mod, g, b, wao, wfo, wout]
    out_specs = [pl.BlockSpec((tm, D_MODEL), row), pl.BlockSpec((tm, D_MODEL), row)]
    h2_dtype = BF16 if router is None else F32
    out_shape = [jax.ShapeDtypeStruct((t, D_MODEL), F32), jax.ShapeDtypeStruct((t, D_MODEL), h2_dtype)]
    if router is not None:
        in_specs += [pl.BlockSpec((D_MODEL, LANES), const), pl.BlockSpec((1, LANES), const)]
        args += list(router)
        out_specs.append(pl.BlockSpec((tm, LANES), row))
        out_shape.append(jax.ShapeDtypeStruct((t, LANES), F32))
    return pl.pallas_call(
        functools.partial(_mixer_kernel, route=router is not None),
        grid=(t // tm,),
        in_specs=in_specs,
        out_specs=out_specs,
        out_shape=out_shape,
        compiler_params=_cp(("parallel",)),
        name="mixer",
    )(*args)


def _ffn_kernel(h_ref, x1_ref, mod_ref, g_ref, b_ref, wg_ref, wu_ref, wd_ref, o_ref):
    h = h_ref[...]
    gate = _dot(h, wg_ref[...])
    up = _dot(h, wu_ref[...])
    act = (gate * _sigmoid(gate) * up).astype(BF16)
    f = _dot(act, wd_ref[...])
    mod = mod_ref[0]
    o_ref[...] = _ln(ALPHA * x1_ref[...] + mod[5:6] * f) * g_ref[...] + b_ref[...]


def _ffn(h2, x1, mod, g, b, wg, wu, wd, tm):
    t = h2.shape[0]
    nb = mod.shape[0]
    tpb = t // nb // tm
    dff = wg.shape[1]
    row = lambda i: (i, 0)
    const = lambda i: (0, 0)
    resident = pl.Buffered(1)
    return pl.pallas_call(
        _ffn_kernel,
        grid=(t // tm,),
        in_specs=[
            pl.BlockSpec((tm, D_MODEL), row),
            pl.BlockSpec((tm, D_MODEL), row),
            pl.BlockSpec((1, 8, D_MODEL), lambda i: (i // tpb, 0, 0)),
            pl.BlockSpec((1, D_MODEL), const),
            pl.BlockSpec((1, D_MODEL), const),
            pl.BlockSpec((D_MODEL, dff), const, pipeline_mode=resident),
            pl.BlockSpec((D_MODEL, dff), const, pipeline_mode=resident),
            pl.BlockSpec((dff, D_MODEL), const, pipeline_mode=resident),
        ],
        out_specs=pl.BlockSpec((tm, D_MODEL), row),
        out_shape=jax.ShapeDtypeStruct((t, D_MODEL), F32),
        compiler_params=_cp(("parallel",)),
        name="ffn",
    )(h2, x1, mod, g, b, wg, wu, wd)


RI_E1, RI_E2, RI_W1, RI_W2 = N_EXPERTS, N_EXPERTS + 1, N_EXPERTS + 2, N_EXPERTS + 3
MOE_TM = 512
MOE_TT = 512
DMA_ISSUE_UNROLL = 8


def _moe_rows(n_tokens):
    return 2 * n_tokens + N_EXPERTS * MOE_TM


def _scan_kernel(r_ref, o_ref, cnt_ref, run_ref):
    @pl.when(pl.program_id(0) == 0)
    def _():
        run_ref[...] = jnp.zeros_like(run_ref)

    r = r_ref[...]
    ts = r.shape[0]
    lane = lax.broadcasted_iota(jnp.int32, r.shape, 1)
    sel = jnp.where(lane < N_EXPERTS, r, 0.0)
    rows = lax.broadcasted_iota(jnp.int32, (ts, ts), 0)
    cols = lax.broadcasted_iota(jnp.int32, (ts, ts), 1)
    ltri = jnp.where(rows > cols, 1.0, 0.0).astype(BF16)
    prefix = _dot(ltri, sel.astype(BF16)) + run_ref[0:1, :]
    lane_f = lane.astype(F32)
    pick = lambda k: jnp.sum(jnp.where(lane == k, r, 0.0), axis=1, keepdims=True)
    e1, e2 = pick(RI_E1), pick(RI_E2)
    p1 = jnp.sum(jnp.where(lane_f == e1, prefix, 0.0), axis=1, keepdims=True)
    p2 = jnp.sum(jnp.where(lane_f == e2, prefix, 0.0), axis=1, keepdims=True)
    out = jnp.where(lane == 0, e1, jnp.where(lane == 1, e2, jnp.where(lane == 2, p1, jnp.where(lane == 3, p2, 0.0))))
    o_ref[...] = out.astype(jnp.int32)
    run_ref[...] = run_ref[...] + jnp.sum(sel, axis=0, keepdims=True)
    cnt_ref[...] = run_ref[...]


def _route_scan(rinfo, ts=512):
    t = rinfo.shape[0]
    return pl.pallas_call(
        _scan_kernel,
        grid=(t // ts,),
        in_specs=[pl.BlockSpec((ts, LANES), lambda i: (i, 0))],
        out_specs=[pl.BlockSpec((ts, LANES), lambda i: (i, 0)), pl.BlockSpec((8, LANES), lambda i: (0, 0))],
        out_shape=[jax.ShapeDtypeStruct((t, LANES), jnp.int32), jax.ShapeDtypeStruct((8, LANES), F32)],
        scratch_shapes=[pltpu.VMEM((8, LANES), F32)],
        compiler_params=_cp(("arbitrary",)),
        name="route_scan",
    )(rinfo)


def _row_copy(src_ref, src_row, dst_ref, dst_row, sem):
    return pltpu.make_async_copy(src_ref.at[pl.ds(src_row, 1)], dst_ref.at[pl.ds(dst_row, 1)], sem)


def _dispatch_kernel(dest_ref, h_ref, xs_ref, sem):
    def issue(t, c):
        for k in range(2):
            _row_copy(h_ref, t, xs_ref, dest_ref[0, k, t], sem).start(priority=k)
        return c

    lax.fori_loop(0, MOE_TT, issue, 0, unroll=DMA_ISSUE_UNROLL)
    for k in range(2):
        pltpu.make_async_copy(h_ref, xs_ref.at[pl.ds(0, MOE_TT)], sem).wait()


def _dispatch(dest, h2):
    t = h2.shape[0]
    return pl.pallas_call(
        _dispatch_kernel,
        grid=(t // MOE_TT,),
        in_specs=[
            pl.BlockSpec((1, 2, MOE_TT), lambda i: (i, 0, 0), memory_space=pltpu.SMEM),
            pl.BlockSpec((MOE_TT, D_MODEL), lambda i: (i, 0)),
        ],
        out_specs=pl.BlockSpec(memory_space=pl.ANY),
        out_shape=jax.ShapeDtypeStruct((_moe_rows(t), D_MODEL), F32),
        scratch_shapes=[pltpu.SemaphoreType.DMA(())],
        compiler_params=_cp(("arbitrary",)),
        name="moe_dispatch",
    )(dest, h2)


def _experts_kernel(te_ref, tv_ref, nt_ref, xs_ref, wg_ref, wu_ref, wd_ref, y_ref, acc_ref):
    j = pl.program_id(1)

    @pl.when(pl.program_id(0) < nt_ref[0])
    def _():
        rows = lax.broadcasted_iota(jnp.int32, xs_ref.shape, 0)
        h = jnp.where(rows < tv_ref[pl.program_id(0)], xs_ref[...], 0.0).astype(BF16)
        gate = _dot(h, wg_ref[0])
        up = _dot(h, wu_ref[0])
        act = (gate * _sigmoid(gate) * up).astype(BF16)
        part = _dot(act, wd_ref[0])

        @pl.when(j == 0)
        def _():
            acc_ref[...] = part

        @pl.when(j > 0)
        def _():
            acc_ref[...] += part

        @pl.when(j == pl.num_programs(1) - 1)
        def _():
            y_ref[...] = acc_ref[...]

    @pl.when((pl.program_id(0) >= nt_ref[0]) & (j == pl.num_programs(1) - 1))
    def _():
        y_ref[...] = jnp.zeros_like(y_ref)


def _experts(tile_expert, tile_valid, n_tiles_used, xs, wg, wu, wd, fc):
    ne, _, dff = wg.shape
    grid_spec = pltpu.PrefetchScalarGridSpec(
        num_scalar_prefetch=3,
        grid=(xs.shape[0] // MOE_TM, dff // fc),
        in_specs=[
            pl.BlockSpec((MOE_TM, D_MODEL), lambda i, j, te, tv, nt: (i, 0)),
            pl.BlockSpec((1, D_MODEL, fc), lambda i, j, te, tv, nt: (te[i], 0, j)),
            pl.BlockSpec((1, D_MODEL, fc), lambda i, j, te, tv, nt: (te[i], 0, j)),
            pl.BlockSpec((1, fc, D_MODEL), lambda i, j, te, tv, nt: (te[i], j, 0)),
        ],
        out_specs=pl.BlockSpec((MOE_TM, D_MODEL), lambda i, j, te, tv, nt: (i, 0)),
        scratch_shapes=[pltpu.VMEM((MOE_TM, D_MODEL), F32)],
    )
    return pl.pallas_call(
        _experts_kernel,
        grid_spec=grid_spec,
        out_shape=jax.ShapeDtypeStruct(xs.shape, F32),
        compiler_params=_cp(("arbitrary", "arbitrary")),
        name="moe_experts",
    )(tile_expert, tile_valid, n_tiles_used, xs, wg, wu, wd)


def _combine_kernel(dest_ref, dest_next_ref, y_ref, x1_ref, rinfo_ref, mod_ref, g_ref, b_ref, o_ref,
                    ybuf_ref, sems):
    i = pl.program_id(0)
    slot = i % 2

    def gather(d_ref, s):
        def issue(t, c):
            for k in range(2):
                _row_copy(y_ref, d_ref[0, k, t], ybuf_ref.at[s, k], t, sems.at[s]).start(priority=k)
            return c
        lax.fori_loop(0, MOE_TT, issue, 0, unroll=DMA_ISSUE_UNROLL)

    @pl.when(i == 0)
    def _():
        gather(dest_ref, slot)

    @pl.when(i + 1 < pl.num_programs(0))
    def _():
        gather(dest_next_ref, 1 - slot)

    for k in range(2):
        pltpu.make_async_copy(y_ref.at[pl.ds(0, MOE_TT)], ybuf_ref.at[slot, k], sems.at[slot]).wait()

    r = rinfo_ref[...]
    lane = lax.broadcasted_iota(jnp.int32, r.shape, 1)
    w1 = jnp.sum(jnp.where(lane == RI_W1, r, 0.0), axis=1, keepdims=True)
    w2 = jnp.sum(jnp.where(lane == RI_W2, r, 0.0), axis=1, keepdims=True)
    f = w1 * ybuf_ref[slot, 0] + w2 * ybuf_ref[slot, 1]
    mod = mod_ref[0]
    o_ref[...] = _ln(ALPHA * x1_ref[...] + mod[5:6] * f) * g_ref[...] + b_ref[...]


def _combine(dest, y, x1, rinfo, mod, g, b):
    t = x1.shape[0]
    nb = mod.shape[0]
    tpb = t // nb // MOE_TT
    row = lambda i: (i, 0)
    const = lambda i: (0, 0)
    n_steps = t // MOE_TT
    return pl.pallas_call(
        _combine_kernel,
        grid=(n_steps,),
        in_specs=[
            pl.BlockSpec((1, 2, MOE_TT), lambda i: (i, 0, 0), memory_space=pltpu.SMEM),
            pl.BlockSpec((1, 2, MOE_TT), lambda i: (jnp.minimum(i + 1, n_steps - 1), 0, 0),
                         memory_space=pltpu.SMEM),
            pl.BlockSpec(memory_space=pl.ANY),
            pl.BlockSpec((MOE_TT, D_MODEL), row),
            pl.BlockSpec((MOE_TT, LANES), row),
            pl.BlockSpec((1, 8, D_MODEL), lambda i: (i // tpb, 0, 0)),
            pl.BlockSpec((1, D_MODEL), const),
            pl.BlockSpec((1, D_MODEL), const),
        ],
        out_specs=pl.BlockSpec((MOE_TT, D_MODEL), row),
        out_shape=jax.ShapeDtypeStruct((t, D_MODEL), F32),
        scratch_shapes=[pltpu.VMEM((2, 2, MOE_TT, D_MODEL), F32), pltpu.SemaphoreType.DMA((2,))],
        compiler_params=_cp(("arbitrary",)),
        name="moe_combine",
    )(dest, dest, y, x1, rinfo, mod, g, b)


def _moe(h2, x1, rinfo, mod, g, b, wg, wu, wd):
    t = h2.shape[0]
    route, counts = _route_scan(rinfo)
    cnt = counts[0, :N_EXPERTS].astype(jnp.int32)
    padded = (cnt + MOE_TM - 1) // MOE_TM * MOE_TM
    ends = jnp.cumsum(padded)
    starts = ends - padded
    dest = jnp.stack([starts[route[:, 0]] + route[:, 2], starts[route[:, 1]] + route[:, 3]])
    dest = dest.reshape(2, t // MOE_TT, MOE_TT).transpose(1, 0, 2)
    tile_row0 = jnp.arange(_moe_rows(t) // MOE_TM, dtype=jnp.int32) * MOE_TM
    tile_expert = jnp.minimum(jnp.sum(tile_row0[:, None] >= ends[None, :], axis=1), N_EXPERTS - 1).astype(jnp.int32)
    tile_valid = jnp.clip(starts[tile_expert] + cnt[tile_expert] - tile_row0, 0, MOE_TM).astype(jnp.int32)
    n_tiles_used = (ends[-1:] // MOE_TM).astype(jnp.int32)
    xs = _dispatch(dest, h2)
    y = _experts(tile_expert, tile_valid, n_tiles_used, xs, wg, wu, wd, fc=D_FF_EXPERT // 2)
    return _combine(dest, y, x1, rinfo, mod, g, b)


def _rope_np(n_tokens):
    rows = np.repeat(np.arange(n_tokens // GRID_W), GRID_W)
    cols = np.tile(np.arange(GRID_W), n_tokens // GRID_W)
    pos = np.stack([rows, cols], axis=-1).astype(np.float64)
    inv_freq = 1.0 / (ROPE_THETA ** (np.arange(ROPE_FREQS, dtype=np.float64) / ROPE_FREQS))
    ang = pos[..., None] * inv_freq
    cos, sin = np.cos(ang), np.sin(ang)
    cos32 = np.broadcast_to(cos[:, :, None, :], (n_tokens, 2, 2, ROPE_FREQS)).reshape(n_tokens, QK_ROPE)
    sgn = np.array([-1.0, 1.0])[None, None, :, None]
    sin32 = (np.broadcast_to(sin[:, :, None, :], (n_tokens, 2, 2, ROPE_FREQS)) * sgn).reshape(n_tokens, QK_ROPE)
    return cos32, sin32


def _half_swap_perm():
    p = np.arange(QK_ROPE).reshape(2, 2, ROPE_FREQS)
    return p[:, ::-1, :].reshape(QK_ROPE)


def _tables():
    c0 = SM_SCALE * math.log2(math.e)
    cos32, sin32 = _rope_np(SEQ)
    one64 = np.ones((SEQ, QK_NOPE))
    tq_lat = c0 * np.concatenate([one64, cos32, sin32], axis=1)
    tk_lat = np.concatenate([cos32, sin32, np.zeros((SEQ, LANES - 2 * QK_ROPE))], axis=1)
    tq_ctx = c0 * np.concatenate([np.ones((CTX_LEN, QK_NOPE + QK_ROPE)), np.zeros((CTX_LEN, QK_ROPE))], axis=1)
    tk_ctx = np.concatenate([np.ones((CTX_LEN, QK_ROPE)), np.zeros((CTX_LEN, LANES - QK_ROPE))], axis=1)
    f32 = lambda a: jnp.asarray(a, F32)
    bf = lambda a: jnp.asarray(a, BF16)

    def dft(n):
        k = np.arange(n)
        ang = 2.0 * np.pi * ((k[:, None] * k[None, :]) % n) / n
        return np.cos(ang), np.sin(ang)

    c128, s128 = dft(FFT_R)
    c256, s256 = dft(CTX_LEN)
    bd_lat = np.concatenate([c128, -s128], axis=1) / math.sqrt(SEQ * F_GROUP_DIM)
    bd_ctx = np.concatenate([c128, -s128], axis=1) / math.sqrt(CTX_LEN * F_GROUP_DIM)
    k1 = np.arange(FFT_R)
    tw = 2.0 * np.pi * ((k1[:, None] * k1[None, :]) % SEQ) / SEQ
    tc = jnp.broadcast_to(f32(np.cos(tw))[:, :, None], (FFT_R, FFT_R, LANES)).reshape(FFT_R, FFT_R * LANES)
    ts = jnp.broadcast_to(f32(np.sin(tw))[:, :, None], (FFT_R, FFT_R, LANES)).reshape(FFT_R, FFT_R * LANES)
    return dict(tq_lat=f32(tq_lat), tk_lat=f32(tk_lat), tq_ctx=f32(tq_ctx), tk_ctx=f32(tk_ctx),
                c128=bf(c128), s128=bf(s128), c256=bf(c256), s256=bf(s256),
                bd_lat=bf(bd_lat), bd_ctx=bf(bd_ctx), tc=tc, ts=ts)


def _arrange_weights(l, w_in, w_uq, w_uk, w_uv):
    sw = _half_swap_perm()
    wi = w_in[l]
    kr = wi[:, OFF_KR:OFF_F]
    w_ip = jnp.concatenate([
        wi[:, :OFF_KR], kr, kr[:, sw], jnp.zeros((D_MODEL, LANES - 2 * QK_ROPE), F32),
        wi[:, OFF_F:]], axis=1).astype(BF16)
    uq = w_uq[l].reshape(Q_LORA, N_HEADS, QK_NOPE + QK_ROPE)
    qr = uq[:, :, QK_NOPE:]
    wq = jnp.concatenate([uq, qr[:, :, sw]], axis=-1).reshape(Q_LORA, N_HEADS * HEAD_PAD).astype(BF16)
    uk = w_uk[l].reshape(KV_LORA, N_HEADS, QK_NOPE)
    top = jnp.concatenate([uk, jnp.zeros((KV_LORA, N_HEADS, HEAD_PAD - QK_NOPE), F32)], axis=-1)
    rmat = np.zeros((LANES, HEAD_PAD), np.float32)
    for i in range(QK_ROPE):
        for r in (i, QK_ROPE + i):
            rmat[r, QK_NOPE + i] = 1.0
            rmat[r, QK_NOPE + QK_ROPE + i] = 1.0
    bot = jnp.broadcast_to(jnp.asarray(rmat)[:, None, :], (LANES, N_HEADS, HEAD_PAD))
    wkt = jnp.transpose(jnp.concatenate([top, bot], axis=0), (1, 2, 0)).astype(BF16)
    uv = w_uv[l].reshape(KV_LORA, N_HEADS // 2, 2, V_HEAD)
    z = jnp.zeros_like(uv[:, :, 0])
    even = jnp.concatenate([uv[:, :, 0], z], axis=-1)
    odd = jnp.concatenate([z, uv[:, :, 1]], axis=-1)
    wv = jnp.stack([even, odd], axis=2).reshape(KV_LORA, N_HEADS * HEAD_PAD).astype(BF16)
    return w_ip, wq, wkt, wv


def kernel(x, c, ctx, c_ctx, w_mod, b_mod, w_in, g_q, w_uq, g_kv, w_uk, w_uv, w_ao, w_fo, w_out,
           ln1_g, ln1_b, ln2_g, ln2_b, w_ff_gate, w_ff_up, w_ff_down,
           w_router, b_router, w_e_gate, w_e_up, w_e_down):
    tb = _tables()
    t_lat = BATCH * SEQ
    t_ctx = BATCH * CTX_LEN

    cs = jnp.concatenate([c, c_ctx[None, :], jnp.zeros((8 - BATCH - 1, D_MODEL), F32)], axis=0)
    mods = _modulation(cs, w_mod.astype(BF16), b_mod).reshape(DEPTH, 8, 6, D_MODEL)
    pad2 = jnp.zeros((DEPTH, 8, 2, D_MODEL), F32)
    mods = jnp.concatenate([mods, pad2], axis=2)

    xl = x.reshape(t_lat, D_MODEL)
    xc = ctx.reshape(t_ctx, D_MODEL)

    for l in range(DEPTH):
        last = l == DEPTH - 1
        mod_x = mods[l, :BATCH]
        mod_c = jnp.broadcast_to(mods[l, BATCH:BATCH + 1], (BATCH, 8, D_MODEL))
        w_ip, wq, wkt, wv = _arrange_weights(l, w_in, w_uq, w_uk, w_uv)
        gq = g_q[l][None, :]
        gkv = g_kv[l][None, :]
        wao = w_ao[l].astype(BF16)
        wfo = w_fo[l].astype(BF16)
        wout = w_out[l].astype(BF16)
        g1, b1 = ln1_g[l][None, :], ln1_b[l][None, :]
        g2, b2 = ln2_g[l][None, :], ln2_b[l][None, :]

        qlat_c, ckr_c, wf_c, gates_c = _inproj(xc, mod_c, w_ip, gq, gkv, tb["tk_ctx"], tb["bd_ctx"], tm=CTX_LEN)
        q_c, kt_c, v_c = _qkv(qlat_c, ckr_c, tb["tq_ctx"], wq, wkt, wv, nb=BATCH, tm=CTX_LEN)

        qlat, ckr, wf, gates = _inproj(xl, mod_x, w_ip, gq, gkv, tb["tk_lat"], tb["bd_lat"], tm=1024)
        q, kt, v = _qkv(qlat, ckr, tb["tq_lat"], wq, wkt, wv, nb=BATCH, tm=512)
        attn = _attention(q, kt_c, v_c, kt, v, tq=ATTN_TQ, tk=ATTN_TK, peel=ATTN_PEEL, unroll=ATTN_UNROLL)
        four = _fourier_latent(wf.reshape(BATCH, SEQ, 2 * FOURIER_DIM), tb["c128"], tb["s128"], tb["tc"], tb["ts"])

        if l % 2 == 0:
            i = l // 2
            x1, h2 = _mixer(attn.reshape(t_lat, -1), four.reshape(t_lat, -1), gates, xl, mod_x, g1, b1,
                            wao, wfo, wout, tm=1024)
            wg = w_ff_gate[i].astype(BF16)
            wu = w_ff_up[i].astype(BF16)
            wd = w_ff_down[i].astype(BF16)
            xl_new = _ffn(h2, x1, mod_x, g2, b2, wg, wu, wd, tm=1024)
        else:
            i = l // 2
            wr = jnp.concatenate([w_router[i], jnp.zeros((D_MODEL, LANES - N_EXPERTS), F32)], axis=1)
            br = jnp.concatenate([b_router[i], jnp.zeros((LANES - N_EXPERTS,), F32)])[None, :]
            x1, h2, rinfo = _mixer(attn.reshape(t_lat, -1), four.reshape(t_lat, -1), gates, xl, mod_x, g1, b1,
                                   wao, wfo, wout, tm=512, router=(wr, br))
            wg = w_e_gate[i].astype(BF16)
            wu = w_e_up[i].astype(BF16)
            wd = w_e_down[i].astype(BF16)
            xl_new = _moe(h2, x1, rinfo, mod_x, g2, b2, wg, wu, wd)

        if not last:
            assert l % 2 == 0
            attn_c = _attention(q_c, kt_c, v_c, None, None, tq=CTX_LEN, tk=CTX_LEN)
            four_c = _fourier_ctx(wf_c.reshape(BATCH, CTX_LEN, 2 * FOURIER_DIM), tb["c256"], tb["s256"])
            x1c, h2c = _mixer(attn_c.reshape(t_ctx, -1), four_c.reshape(t_ctx, -1), gates_c, xc, mod_c,
                              g1, b1, wao, wfo, wout, tm=CTX_LEN)
            xc = _ffn(h2c, x1c, mod_c, g2, b2, wg, wu, wd, tm=CTX_LEN)
        xl = xl_new

    return xl.reshape(BATCH, SEQ, D_MODEL)
```

```python
import functools
import math

import numpy as np
import jax
import jax.numpy as jnp
from jax import lax
from jax.experimental import pallas as pl
from jax.experimental.pallas import tpu as pltpu

D_MODEL = 1024
BATCH = 2
SEQ = 16384
DEPTH = 2
GRID_W = 64
CTX_LEN = 256
N_HEADS = 16
QK_NOPE = 64
QK_ROPE = 32
ROPE_FREQS = QK_ROPE // 4
V_HEAD = 64
Q_LORA = 256
KV_LORA = 128
ROPE_THETA = 10000.0
SM_SCALE = (QK_NOPE + QK_ROPE) ** -0.5
F_GROUPS = 4
F_GROUP_DIM = 128
FOURIER_DIM = F_GROUPS * F_GROUP_DIM
OFF_KV = Q_LORA
OFF_KR = OFF_KV + KV_LORA
OFF_F = OFF_KR + QK_ROPE
OFF_G = OFF_F + FOURIER_DIM
D_FF = 2816
N_EXPERTS = 8
D_FF_EXPERT = 3584
ALPHA = (2 * DEPTH) ** 0.25
LN_EPS = 1e-6
RMS_EPS = 1e-6

LANES = 128
SUBLANES = 8
N_MOD = 6
HEAD_PAD = 128
FFT_R = 128
V_ONES_LANE = (V_HEAD, 0)
RI_E1, RI_E2, RI_W1, RI_W2 = N_EXPERTS, N_EXPERTS + 1, N_EXPERTS + 2, N_EXPERTS + 3
VMEM_LIMIT = 56 * 1024 * 1024
ROW_TM = 1024
ROUTE_TM = 512
QKV_TM = 512
ATTN_TQ = 1024
ATTN_TK = 512
ATTN_PEEL = 16
ATTN_UNROLL = 8

BF16 = jnp.bfloat16
F32 = jnp.float32


def _cp(sem, vmem=VMEM_LIMIT):
    return pltpu.CompilerParams(dimension_semantics=sem, vmem_limit_bytes=vmem)


def _dot(a, b):
    return jnp.dot(a, b, preferred_element_type=F32)


def _ln(x):
    mu = jnp.mean(x, axis=-1, keepdims=True)
    xc = x - mu
    var = jnp.mean(xc * xc, axis=-1, keepdims=True)
    return xc * lax.rsqrt(var + LN_EPS)


def _rms(x, g):
    return x * lax.rsqrt(jnp.mean(x * x, axis=-1, keepdims=True) + RMS_EPS) * g


def _sigmoid(x):
    return 1.0 / (1.0 + jnp.exp(-x))


def _mod_kernel(cs_ref, w_ref, b_ref, o_ref):
    cs = cs_ref[...]
    a = (cs * _sigmoid(cs)).astype(BF16)
    o_ref[0] = _dot(a, w_ref[0]) + b_ref[0]


def _modulation(cs, w_mod, b_mod):
    n_chunk = 1024
    n_out = w_mod.shape[-1]
    return pl.pallas_call(
        _mod_kernel,
        grid=(DEPTH, n_out // n_chunk),
        in_specs=[
            pl.BlockSpec((SUBLANES, D_MODEL), lambda l, j: (0, 0)),
            pl.BlockSpec((1, D_MODEL, n_chunk), lambda l, j: (l, 0, j)),
            pl.BlockSpec((1, 1, n_chunk), lambda l, j: (l, 0, j)),
        ],
        out_specs=pl.BlockSpec((1, SUBLANES, n_chunk), lambda l, j: (l, 0, j)),
        out_shape=jax.ShapeDtypeStruct((DEPTH, SUBLANES, n_out), F32),
        compiler_params=_cp(("parallel", "parallel")),
        name="mod",
    )(cs, w_mod, b_mod.reshape(DEPTH, 1, n_out))


IP_Q = 0
IP_KV = Q_LORA
IP_KR = IP_KV + KV_LORA
IP_F = IP_KR + LANES
IP_G = IP_F + FOURIER_DIM
IP_COLS = IP_G + 2 * D_MODEL


def _inproj_kernel(x_ref, mod_ref, w_ref, gq_ref, gkv_ref, tk_ref, bd_ref,
                   qlat_ref, ckr_ref, wf_ref, gates_ref):
    x = x_ref[...]
    mod = mod_ref[0]
    h = (_ln(x) * (1.0 + mod[1:2]) + mod[0:1]).astype(BF16)
    p0 = _dot(h, w_ref[:, IP_Q:IP_F])
    qlat_ref[...] = _rms(p0[:, IP_Q:IP_KV], gq_ref[...]).astype(BF16)
    ckr_ref[:, 0:KV_LORA] = _rms(p0[:, IP_KV:IP_KR], gkv_ref[...]).astype(BF16)
    ckr_ref[:, KV_LORA:] = (p0[:, IP_KR:IP_F] * tk_ref[...]).astype(BF16)
    uf = _dot(h, w_ref[:, IP_F:IP_G]).astype(BF16)
    for g in range(F_GROUPS):
        r = _dot(uf[:, g * LANES:(g + 1) * LANES], bd_ref[...])
        wf_ref[:, g * LANES:(g + 1) * LANES] = r[:, :LANES].astype(BF16)
        wf_ref[:, FOURIER_DIM + g * LANES:FOURIER_DIM + (g + 1) * LANES] = r[:, LANES:].astype(BF16)
    gc = 512
    for c in range(2 * D_MODEL // gc):
        gl = _dot(h, w_ref[:, IP_G + c * gc:IP_G + (c + 1) * gc])
        gates_ref[:, c * gc:(c + 1) * gc] = _sigmoid(gl).astype(BF16)


def _inproj(x2d, mod, w, gq, gkv, tk, bd, tm):
    t = x2d.shape[0]
    nb = mod.shape[0]
    tpb = t // nb // tm
    return pl.pallas_call(
        _inproj_kernel,
        grid=(t // tm,),
        in_specs=[
            pl.BlockSpec((tm, D_MODEL), lambda i: (i, 0)),
            pl.BlockSpec((1, SUBLANES, D_MODEL), lambda i: (i // tpb, 0, 0)),
            pl.BlockSpec((D_MODEL, IP_COLS), lambda i: (0, 0)),
            pl.BlockSpec((1, Q_LORA), lambda i: (0, 0)),
            pl.BlockSpec((1, KV_LORA), lambda i: (0, 0)),
            pl.BlockSpec((tm, LANES), lambda i: (i % tpb, 0)),
            pl.BlockSpec((LANES, 2 * LANES), lambda i: (0, 0)),
        ],
        out_specs=[
            pl.BlockSpec((tm, Q_LORA), lambda i: (i, 0)),
            pl.BlockSpec((tm, 2 * LANES), lambda i: (i, 0)),
            pl.BlockSpec((tm, 2 * FOURIER_DIM), lambda i: (i, 0)),
            pl.BlockSpec((tm, 2 * D_MODEL), lambda i: (i, 0)),
        ],
        out_shape=[
            jax.ShapeDtypeStruct((t, Q_LORA), BF16),
            jax.ShapeDtypeStruct((t, 2 * LANES), BF16),
            jax.ShapeDtypeStruct((t, 2 * FOURIER_DIM), BF16),
            jax.ShapeDtypeStruct((t, 2 * D_MODEL), BF16),
        ],
        compiler_params=_cp(("parallel",)),
        name="inproj",
    )(x2d, mod, w, gq, gkv, tk, bd)


def _qkv_kernel(qlat_ref, ckr_ref, tq_ref, wq_ref, wkt_ref, wv_ref, q_ref, kt_ref, v_ref):
    ql = qlat_ref[...]
    ckr = ckr_ref[...]
    tq = tq_ref[...]
    ckv = ckr[:, :KV_LORA]
    lane = lax.broadcasted_iota(jnp.int32, (ckr.shape[0], HEAD_PAD), 1)
    q_all = _dot(ql, wq_ref[...])
    v_all = _dot(ckv, wv_ref[...])
    for h in range(N_HEADS):
        cols = slice(h * HEAD_PAD, (h + 1) * HEAD_PAD)
        q_ref[0, h] = (q_all[:, cols] * tq).astype(BF16)
        kt_ref[0, h] = lax.dot_general(
            wkt_ref[h], ckr, (((1,), (1,)), ((), ())), preferred_element_type=F32).astype(BF16)
        v_ref[0, h] = jnp.where(lane == V_ONES_LANE[h % 2], 1.0, v_all[:, cols]).astype(BF16)


def _qkv(qlat, ckr, tq, wq, wkt, wv, nb, tm):
    t = qlat.shape[0]
    n = t // nb
    tpb = n // tm
    return pl.pallas_call(
        _qkv_kernel,
        grid=(t // tm,),
        in_specs=[
            pl.BlockSpec((tm, Q_LORA), lambda i: (i, 0)),
            pl.BlockSpec((tm, 2 * LANES), lambda i: (i, 0)),
            pl.BlockSpec((tm, HEAD_PAD), lambda i: (i % tpb, 0)),
            pl.BlockSpec((Q_LORA, N_HEADS * HEAD_PAD), lambda i: (0, 0)),
            pl.BlockSpec((N_HEADS, HEAD_PAD, 2 * LANES), lambda i: (0, 0, 0)),
            pl.BlockSpec((KV_LORA, N_HEADS * HEAD_PAD), lambda i: (0, 0)),
        ],
        out_specs=[
            pl.BlockSpec((1, N_HEADS, tm, HEAD_PAD), lambda i: (i // tpb, 0, i % tpb, 0)),
            pl.BlockSpec((1, N_HEADS, HEAD_PAD, tm), lambda i: (i // tpb, 0, 0, i % tpb)),
            pl.BlockSpec((1, N_HEADS, tm, HEAD_PAD), lambda i: (i // tpb, 0, i % tpb, 0)),
        ],
        out_shape=[
            jax.ShapeDtypeStruct((nb, N_HEADS, n, HEAD_PAD), BF16),
            jax.ShapeDtypeStruct((nb, N_HEADS, HEAD_PAD, n), BF16),
            jax.ShapeDtypeStruct((nb, N_HEADS, n, HEAD_PAD), BF16),
        ],
        compiler_params=_cp(("parallel",)),
        name="qkv",
    )(qlat, ckr, tq, wq, wkt, wv)


def _attn_kernel(q_ref, ktc_ref, vc_ref, *rest, n_chunks, tk, peel, unroll):
    if n_chunks:
        kt_ref, v_ref, o_ref = rest
    else:
        (o_ref,) = rest

    def step(q, kt, v, m, acc):
        s = _dot(q, kt)
        m_new = jnp.max(s, axis=1, keepdims=True)
        if m is not None:
            m_new = jnp.maximum(m, m_new)
        p = jnp.exp2((s - m_new).astype(BF16))
        pv = _dot(p, v)
        if m is not None:
            pv = jnp.exp2(m - m_new) * acc + pv
        return m_new, pv

    qs = [q_ref[0, hh] for hh in range(2)]
    carry = []
    for hh in range(2):
        carry += step(qs[hh], ktc_ref[0, hh], vc_ref[0, hh], None, None)

    if n_chunks:
        span = unroll * tk

        def group(base, count, carry):
            for u in range(count):
                off = base + u * tk
                if not isinstance(off, int):
                    off = pl.multiple_of(off, tk)
                out = []
                for hh in range(2):
                    out += step(qs[hh], kt_ref[0, hh, :, pl.ds(off, tk)], v_ref[0, hh, pl.ds(off, tk), :],
                                carry[2 * hh], carry[2 * hh + 1])
                carry = out
            return tuple(carry)

        carry = group(0, peel, carry)
        carry = lax.fori_loop(0, (n_chunks - peel) // unroll,
                              lambda g, c: group(pl.multiple_of(peel * tk + g * span, tk), unroll, c), carry)

    lane = lax.broadcasted_iota(jnp.int32, carry[1].shape, 1)
    outs = []
    for hh in range(2):
        acc = carry[2 * hh + 1]
        l = jnp.sum(jnp.where(lane == V_ONES_LANE[hh], acc, 0.0), axis=1, keepdims=True)
        outs.append(acc * (1.0 / l))
    o_ref[0] = jnp.where(lane < V_HEAD, outs[0], outs[1]).astype(BF16)


def _attention(q, kt_c, v_c, kt, v, tq, tk, peel=0, unroll=1):
    nb, _, n, _ = q.shape
    nc = kt_c.shape[-1]
    n_chunks = 0 if kt is None else kt.shape[-1] // tk
    in_specs = [
        pl.BlockSpec((1, 2, tq, HEAD_PAD), lambda b, p, i: (b, p, i, 0)),
        pl.BlockSpec((1, 2, HEAD_PAD, nc), lambda b, p, i: (b, p, 0, 0)),
        pl.BlockSpec((1, 2, nc, HEAD_PAD), lambda b, p, i: (b, p, 0, 0)),
    ]
    args = [q, kt_c, v_c]
    if n_chunks:
        nk = kt.shape[-1]
        in_specs += [
            pl.BlockSpec((1, 2, HEAD_PAD, nk), lambda b, p, i: (b, p, 0, 0)),
            pl.BlockSpec((1, 2, nk, HEAD_PAD), lambda b, p, i: (b, p, 0, 0)),
        ]
        args += [kt, v]
    return pl.pallas_call(
        functools.partial(_attn_kernel, n_chunks=n_chunks, tk=tk, peel=peel, unroll=unroll),
        grid=(nb, N_HEADS // 2, n // tq),
        in_specs=in_specs,
        out_specs=pl.BlockSpec((1, tq, 2 * V_HEAD), lambda b, p, i: (b, i, p)),
        out_shape=jax.ShapeDtypeStruct((nb, n, N_HEADS * V_HEAD), BF16),
        compiler_params=_cp(("parallel", "parallel", "arbitrary")),
        name="attn",
    )(*args)


def _fft1_kernel(x_ref, c_ref, s_ref, tc_ref, ts_ref, o_ref, *, n2t):
    x = x_ref[0]
    cx = _dot(c_ref[...], x)
    sx = _dot(s_ref[...], x)
    w = 2 * FOURIER_DIM
    for t in range(n2t):
        re = slice(t * w, t * w + FOURIER_DIM)
        im = slice(t * w + FOURIER_DIM, (t + 1) * w)
        yr = cx[:, re] + sx[:, im]
        yi = cx[:, im] - sx[:, re]
        tc = jnp.concatenate([tc_ref[:, t * LANES:(t + 1) * LANES]] * F_GROUPS, axis=1)
        ts = jnp.concatenate([ts_ref[:, t * LANES:(t + 1) * LANES]] * F_GROUPS, axis=1)
        o_ref[0, :, re] = (yr * tc + yi * ts).astype(BF16)
        o_ref[0, :, im] = (yi * tc - yr * ts).astype(BF16)


def _fft2_kernel(y_ref, c_ref, s_ref, o_ref, *, k1t):
    for t in range(k1t):
        y = y_ref[0, t]
        zr = _dot(c_ref[...], y[:, :FOURIER_DIM]) + _dot(s_ref[...], y[:, FOURIER_DIM:])
        o_ref[0, :, t * FOURIER_DIM:(t + 1) * FOURIER_DIM] = zr.astype(BF16)


def _fourier_latent(wf, cmat, smat, tc, ts):
    nb = wf.shape[0]
    w = 2 * FOURIER_DIM
    n2t = 8
    y = pl.pallas_call(
        functools.partial(_fft1_kernel, n2t=n2t),
        grid=(nb, FFT_R // n2t),
        in_specs=[
            pl.BlockSpec((1, FFT_R, n2t * w), lambda b, j: (b, 0, j)),
            pl.BlockSpec((FFT_R, FFT_R), lambda b, j: (0, 0)),
            pl.BlockSpec((FFT_R, FFT_R), lambda b, j: (0, 0)),
            pl.BlockSpec((FFT_R, n2t * LANES), lambda b, j: (0, j)),
            pl.BlockSpec((FFT_R, n2t * LANES), lambda b, j: (0, j)),
        ],
        out_specs=pl.BlockSpec((1, FFT_R, n2t * w), lambda b, j: (b, 0, j)),
        out_shape=jax.ShapeDtypeStruct((nb, FFT_R, FFT_R * w), BF16),
        compiler_params=_cp(("parallel", "parallel")),
        name="fft1",
    )(wf.reshape(nb, FFT_R, FFT_R * w), cmat, smat, tc, ts)
    k1t = 8
    four = pl.pallas_call(
        functools.partial(_fft2_kernel, k1t=k1t),
        grid=(nb, FFT_R // k1t),
        in_specs=[
            pl.BlockSpec((1, k1t, FFT_R, w), lambda b, j: (b, j, 0, 0)),
            pl.BlockSpec((FFT_R, FFT_R), lambda b, j: (0, 0)),
            pl.BlockSpec((FFT_R, FFT_R), lambda b, j: (0, 0)),
        ],
        out_specs=pl.BlockSpec((1, FFT_R, k1t * FOURIER_DIM), lambda b, j: (b, 0, j)),
        out_shape=jax.ShapeDtypeStruct((nb, FFT_R, FFT_R * FOURIER_DIM), BF16),
        compiler_params=_cp(("parallel", "parallel")),
        name="fft2",
    )(y.reshape(nb, FFT_R, FFT_R, w), cmat, smat)
    return four.reshape(nb, SEQ, FOURIER_DIM)


def _dft_ctx_kernel(x_ref, c_ref, s_ref, o_ref):
    x = x_ref[0]
    zr = _dot(c_ref[...], x[:, :FOURIER_DIM]) + _dot(s_ref[...], x[:, FOURIER_DIM:])
    o_ref[0] = zr.astype(BF16)


def _fourier_ctx(wf, cmat, smat):
    nb, n, w = wf.shape
    return pl.pallas_call(
        _dft_ctx_kernel,
        grid=(nb,),
        in_specs=[
            pl.BlockSpec((1, n, w), lambda b: (b, 0, 0)),
            pl.BlockSpec((n, n), lambda b: (0, 0)),
            pl.BlockSpec((n, n), lambda b: (0, 0)),
        ],
        out_specs=pl.BlockSpec((1, n, FOURIER_DIM), lambda b: (b, 0, 0)),
        out_shape=jax.ShapeDtypeStruct((nb, n, FOURIER_DIM), BF16),
        compiler_params=_cp(("parallel",)),
        name="dft_ctx",
    )(wf, cmat, smat)


def _mixer_kernel(attn_ref, four_ref, gates_ref, x_ref, mod_ref, g_ref, b_ref,
                  wao_ref, wfo_ref, wout_ref, *rest, route):
    if route:
        wr_ref, br_ref, x1_ref, h2_ref, rinfo_ref = rest
    else:
        x1_ref, h2_ref = rest
    mod = mod_ref[0]
    a = _dot(attn_ref[...], wao_ref[...])
    f = _dot(four_ref[...], wfo_ref[...])
    merged = gates_ref[:, :D_MODEL].astype(F32) * a + gates_ref[:, D_MODEL:].astype(F32) * f
    y = _dot(merged.astype(BF16), wout_ref[...])
    x1 = _ln(ALPHA * x_ref[...] + mod[2:3] * y) * g_ref[...] + b_ref[...]
    x1_ref[...] = x1
    h2 = _ln(x1) * (1.0 + mod[4:5]) + mod[3:4]
    h2_ref[...] = h2.astype(h2_ref.dtype)
    if route:
        logits = jnp.dot(h2, wr_ref[...], preferred_element_type=F32,
                         precision=lax.Precision.HIGHEST) + br_ref[...]
        lane = lax.broadcasted_iota(jnp.int32, logits.shape, 1)
        neg = jnp.float32(-jnp.inf)
        lg = jnp.where(lane < N_EXPERTS, logits, neg)
        m1 = jnp.max(lg, axis=1, keepdims=True)
        i1 = jnp.min(jnp.where(lg == m1, lane, LANES), axis=1, keepdims=True)
        lg2 = jnp.where(lane == i1, neg, lg)
        m2 = jnp.max(lg2, axis=1, keepdims=True)
        i2 = jnp.min(jnp.where(lg2 == m2, lane, LANES), axis=1, keepdims=True)
        e2 = jnp.exp(m2 - m1)
        w1 = 1.0 / (1.0 + e2)
        w2 = e2 * w1
        rinfo = jnp.where((lane == i1) | (lane == i2), 1.0, 0.0)
        rinfo = jnp.where(lane == RI_E1, i1.astype(F32), rinfo)
        rinfo = jnp.where(lane == RI_E2, i2.astype(F32), rinfo)
        rinfo = jnp.where(lane == RI_W1, w1, rinfo)
        rinfo_ref[...] = jnp.where(lane == RI_W2, w2, rinfo)


def _mixer(attn, four, gates, x2d, mod, g, b, wao, wfo, wout, tm, router=None):
    t = x2d.shape[0]
    nb = mod.shape[0]
    tpb = t // nb // tm
    row = lambda i: (i, 0)
    const = lambda i: (0, 0)
    in_specs = [
        pl.BlockSpec((tm, D_MODEL), row),
        pl.BlockSpec((tm, FOURIER_DIM), row),
        pl.BlockSpec((tm, 2 * D_MODEL), row),
        pl.BlockSpec((tm, D_MODEL), row),
        pl.BlockSpec((1, SUBLANES, D_MODEL), lambda i: (i // tpb, 0, 0)),
        pl.BlockSpec((1, D_MODEL), const),
        pl.BlockSpec((1, D_MODEL), const),
        pl.BlockSpec((D_MODEL, D_MODEL), const),
        pl.BlockSpec((FOURIER_DIM, D_MODEL), const),
        pl.BlockSpec((D_MODEL, D_MODEL), const),
    ]
    args = [attn, four, gates, x2d, mod, g, b, wao, wfo, wout]
    out_specs = [pl.BlockSpec((tm, D_MODEL), row), pl.BlockSpec((tm, D_MODEL), row)]
    h2_dtype = BF16 if router is None else F32
    out_shape = [jax.ShapeDtypeStruct((t, D_MODEL), F32), jax.ShapeDtypeStruct((t, D_MODEL), h2_dtype)]
    if router is not None:
        in_specs += [pl.BlockSpec((D_MODEL, LANES), const), pl.BlockSpec((1, LANES), const)]
        args += list(router)
        out_specs.append(pl.BlockSpec((tm, LANES), row))
        out_shape.append(jax.ShapeDtypeStruct((t, LANES), F32))
    return pl.pallas_call(
        functools.partial(_mixer_kernel, route=router is not None),
        grid=(t // tm,),
        in_specs=in_specs,
        out_specs=out_specs,
        out_shape=out_shape,
        compiler_params=_cp(("parallel",)),
        name="mixer",
    )(*args)


def _ffn_kernel(h_ref, x1_ref, mod_ref, g_ref, b_ref, wg_ref, wu_ref, wd_ref, o_ref):
    h = h_ref[...]
    gate = _dot(h, wg_ref[...])
    up = _dot(h, wu_ref[...])
    act = (gate * _sigmoid(gate) * up).astype(BF16)
    f = _dot(act, wd_ref[...])
    mod = mod_ref[0]
    o_ref[...] = _ln(ALPHA * x1_ref[...] + mod[5:6] * f) * g_ref[...] + b_ref[...]


def _ffn(h2, x1, mod, g, b, wg, wu, wd, tm):
    t = h2.shape[0]
    nb = mod.shape[0]
    tpb = t // nb // tm
    dff = wg.shape[1]
    row = lambda i: (i, 0)
    const = lambda i: (0, 0)
    resident = pl.Buffered(1)
    return pl.pallas_call(
        _ffn_kernel,
        grid=(t // tm,),
        in_specs=[
            pl.BlockSpec((tm, D_MODEL), row),
            pl.BlockSpec((tm, D_MODEL), row),
            pl.BlockSpec((1, SUBLANES, D_MODEL), lambda i: (i // tpb, 0, 0)),
            pl.BlockSpec((1, D_MODEL), const),
            pl.BlockSpec((1, D_MODEL), const),
            pl.BlockSpec((D_MODEL, dff), const, pipeline_mode=resident),
            pl.BlockSpec((D_MODEL, dff), const, pipeline_mode=resident),
            pl.BlockSpec((dff, D_MODEL), const, pipeline_mode=resident),
        ],
        out_specs=pl.BlockSpec((tm, D_MODEL), row),
        out_shape=jax.ShapeDtypeStruct((t, D_MODEL), F32),
        compiler_params=_cp(("parallel",)),
        name="ffn",
    )(h2, x1, mod, g, b, wg, wu, wd)


MOE_TM = 512
MOE_TT = 512
DMA_ISSUE_UNROLL = 8


def _moe_rows(n_tokens):
    return 2 * n_tokens + N_EXPERTS * MOE_TM


def _scan_kernel(r_ref, o_ref, cnt_ref, run_ref):
    @pl.when(pl.program_id(0) == 0)
    def _():
        run_ref[...] = jnp.zeros_like(run_ref)

    r = r_ref[...]
    ts = r.shape[0]
    lane = lax.broadcasted_iota(jnp.int32, r.shape, 1)
    sel = jnp.where(lane < N_EXPERTS, r, 0.0)
    rows = lax.broadcasted_iota(jnp.int32, (ts, ts), 0)
    cols = lax.broadcasted_iota(jnp.int32, (ts, ts), 1)
    ltri = jnp.where(rows > cols, 1.0, 0.0).astype(BF16)
    prefix = _dot(ltri, sel.astype(BF16)) + run_ref[0:1, :]
    lane_f = lane.astype(F32)
    pick = lambda k: jnp.sum(jnp.where(lane == k, r, 0.0), axis=1, keepdims=True)
    e1, e2 = pick(RI_E1), pick(RI_E2)
    p1 = jnp.sum(jnp.where(lane_f == e1, prefix, 0.0), axis=1, keepdims=True)
    p2 = jnp.sum(jnp.where(lane_f == e2, prefix, 0.0), axis=1, keepdims=True)
    out = jnp.where(lane == 0, e1, jnp.where(lane == 1, e2, jnp.where(lane == 2, p1, jnp.where(lane == 3, p2, 0.0))))
    o_ref[...] = out.astype(jnp.int32)
    run_ref[...] = run_ref[...] + jnp.sum(sel, axis=0, keepdims=True)
    cnt_ref[...] = run_ref[...]


def _route_scan(rinfo, ts=512):
    t = rinfo.shape[0]
    return pl.pallas_call(
        _scan_kernel,
        grid=(t // ts,),
        in_specs=[pl.BlockSpec((ts, LANES), lambda i: (i, 0))],
        out_specs=[pl.BlockSpec((ts, LANES), lambda i: (i, 0)), pl.BlockSpec((SUBLANES, LANES), lambda i: (0, 0))],
        out_shape=[jax.ShapeDtypeStruct((t, LANES), jnp.int32), jax.ShapeDtypeStruct((SUBLANES, LANES), F32)],
        scratch_shapes=[pltpu.VMEM((SUBLANES, LANES), F32)],
        compiler_params=_cp(("arbitrary",)),
        name="route_scan",
    )(rinfo)


def _row_copy(src_ref, src_row, dst_ref, dst_row, sem):
    return pltpu.make_async_copy(src_ref.at[pl.ds(src_row, 1)], dst_ref.at[pl.ds(dst_row, 1)], sem)


def _dispatch_kernel(dest_ref, h_ref, xs_ref, sem):
    def issue(t, c):
        for k in range(2):
            _row_copy(h_ref, t, xs_ref, dest_ref[0, k, t], sem).start(priority=k)
        return c

    lax.fori_loop(0, MOE_TT, issue, 0, unroll=DMA_ISSUE_UNROLL)
    for k in range(2):
        pltpu.make_async_copy(h_ref, xs_ref.at[pl.ds(0, MOE_TT)], sem).wait()


def _dispatch(dest, h2):
    t = h2.shape[0]
    return pl.pallas_call(
        _dispatch_kernel,
        grid=(t // MOE_TT,),
        in_specs=[
            pl.BlockSpec((1, 2, MOE_TT), lambda i: (i, 0, 0), memory_space=pltpu.SMEM),
            pl.BlockSpec((MOE_TT, D_MODEL), lambda i: (i, 0)),
        ],
        out_specs=pl.BlockSpec(memory_space=pl.ANY),
        out_shape=jax.ShapeDtypeStruct((_moe_rows(t), D_MODEL), F32),
        scratch_shapes=[pltpu.SemaphoreType.DMA(())],
        compiler_params=_cp(("arbitrary",)),
        name="moe_dispatch",
    )(dest, h2)


def _experts_kernel(te_ref, tv_ref, nt_ref, xs_ref, wg_ref, wu_ref, wd_ref, y_ref, acc_ref):
    j = pl.program_id(1)

    @pl.when(pl.program_id(0) < nt_ref[0])
    def _():
        rows = lax.broadcasted_iota(jnp.int32, xs_ref.shape, 0)
        h = jnp.where(rows < tv_ref[pl.program_id(0)], xs_ref[...], 0.0).astype(BF16)
        gate = _dot(h, wg_ref[0])
        up = _dot(h, wu_ref[0])
        act = (gate * _sigmoid(gate) * up).astype(BF16)
        part = _dot(act, wd_ref[0])

        @pl.when(j == 0)
        def _():
            acc_ref[...] = part

        @pl.when(j > 0)
        def _():
            acc_ref[...] += part

        @pl.when(j == pl.num_programs(1) - 1)
        def _():
            y_ref[...] = acc_ref[...]

    @pl.when((pl.program_id(0) >= nt_ref[0]) & (j == pl.num_programs(1) - 1))
    def _():
        y_ref[...] = jnp.zeros_like(y_ref)


def _experts(tile_expert, tile_valid, n_tiles_used, xs, wg, wu, wd, fc):
    ne, _, dff = wg.shape
    grid_spec = pltpu.PrefetchScalarGridSpec(
        num_scalar_prefetch=3,
        grid=(xs.shape[0] // MOE_TM, dff // fc),
        in_specs=[
            pl.BlockSpec((MOE_TM, D_MODEL), lambda i, j, te, tv, nt: (i, 0)),
            pl.BlockSpec((1, D_MODEL, fc), lambda i, j, te, tv, nt: (te[i], 0, j)),
            pl.BlockSpec((1, D_MODEL, fc), lambda i, j, te, tv, nt: (te[i], 0, j)),
            pl.BlockSpec((1, fc, D_MODEL), lambda i, j, te, tv, nt: (te[i], j, 0)),
        ],
        out_specs=pl.BlockSpec((MOE_TM, D_MODEL), lambda i, j, te, tv, nt: (i, 0)),
        scratch_shapes=[pltpu.VMEM((MOE_TM, D_MODEL), F32)],
    )
    return pl.pallas_call(
        _experts_kernel,
        grid_spec=grid_spec,
        out_shape=jax.ShapeDtypeStruct(xs.shape, F32),
        compiler_params=_cp(("arbitrary", "arbitrary")),
        name="moe_experts",
    )(tile_expert, tile_valid, n_tiles_used, xs, wg, wu, wd)


def _combine_kernel(dest_ref, dest_next_ref, y_ref, x1_ref, rinfo_ref, mod_ref, g_ref, b_ref, o_ref,
                    ybuf_ref, sems):
    i = pl.program_id(0)
    slot = i % 2

    def gather(d_ref, s):
        def issue(t, c):
            for k in range(2):
                _row_copy(y_ref, d_ref[0, k, t], ybuf_ref.at[s, k], t, sems.at[s]).start(priority=k)
            return c
        lax.fori_loop(0, MOE_TT, issue, 0, unroll=DMA_ISSUE_UNROLL)

    @pl.when(i == 0)
    def _():
        gather(dest_ref, slot)

    @pl.when(i + 1 < pl.num_programs(0))
    def _():
        gather(dest_next_ref, 1 - slot)

    for k in range(2):
        pltpu.make_async_copy(y_ref.at[pl.ds(0, MOE_TT)], ybuf_ref.at[slot, k], sems.at[slot]).wait()

    r = rinfo_ref[...]
    lane = lax.broadcasted_iota(jnp.int32, r.shape, 1)
    w1 = jnp.sum(jnp.where(lane == RI_W1, r, 0.0), axis=1, keepdims=True)
    w2 = jnp.sum(jnp.where(lane == RI_W2, r, 0.0), axis=1, keepdims=True)
    f = w1 * ybuf_ref[slot, 0] + w2 * ybuf_ref[slot, 1]
    mod = mod_ref[0]
    o_ref[...] = _ln(ALPHA * x1_ref[...] + mod[5:6] * f) * g_ref[...] + b_ref[...]


def _combine(dest, y, x1, rinfo, mod, g, b):
    t = x1.shape[0]
    nb = mod.shape[0]
    tpb = t // nb // MOE_TT
    row = lambda i: (i, 0)
    const = lambda i: (0, 0)
    n_steps = t // MOE_TT
    return pl.pallas_call(
        _combine_kernel,
        grid=(n_steps,),
        in_specs=[
            pl.BlockSpec((1, 2, MOE_TT), lambda i: (i, 0, 0), memory_space=pltpu.SMEM),
            pl.BlockSpec((1, 2, MOE_TT), lambda i: (jnp.minimum(i + 1, n_steps - 1), 0, 0),
                         memory_space=pltpu.SMEM),
            pl.BlockSpec(memory_space=pl.ANY),
            pl.BlockSpec((MOE_TT, D_MODEL), row),
            pl.BlockSpec((MOE_TT, LANES), row),
            pl.BlockSpec((1, SUBLANES, D_MODEL), lambda i: (i // tpb, 0, 0)),
            pl.BlockSpec((1, D_MODEL), const),
            pl.BlockSpec((1, D_MODEL), const),
        ],
        out_specs=pl.BlockSpec((MOE_TT, D_MODEL), row),
        out_shape=jax.ShapeDtypeStruct((t, D_MODEL), F32),
        scratch_shapes=[pltpu.VMEM((2, 2, MOE_TT, D_MODEL), F32), pltpu.SemaphoreType.DMA((2,))],
        compiler_params=_cp(("arbitrary",)),
        name="moe_combine",
    )(dest, dest, y, x1, rinfo, mod, g, b)


def _moe(h2, x1, rinfo, mod, g, b, wg, wu, wd):
    t = h2.shape[0]
    route, counts = _route_scan(rinfo)
    cnt = counts[0, :N_EXPERTS].astype(jnp.int32)
    padded = (cnt + MOE_TM - 1) // MOE_TM * MOE_TM
    ends = jnp.cumsum(padded)
    starts = ends - padded
    dest = jnp.stack([starts[route[:, 0]] + route[:, 2], starts[route[:, 1]] + route[:, 3]])
    dest = dest.reshape(2, t // MOE_TT, MOE_TT).transpose(1, 0, 2)
    tile_row0 = jnp.arange(_moe_rows(t) // MOE_TM, dtype=jnp.int32) * MOE_TM
    tile_expert = jnp.minimum(jnp.sum(tile_row0[:, None] >= ends[None, :], axis=1), N_EXPERTS - 1).astype(jnp.int32)
    tile_valid = jnp.clip(starts[tile_expert] + cnt[tile_expert] - tile_row0, 0, MOE_TM).astype(jnp.int32)
    n_tiles_used = (ends[-1:] // MOE_TM).astype(jnp.int32)
    xs = _dispatch(dest, h2)
    y = _experts(tile_expert, tile_valid, n_tiles_used, xs, wg, wu, wd, fc=D_FF_EXPERT // 2)
    return _combine(dest, y, x1, rinfo, mod, g, b)


def _rope_np(n_tokens):
    rows = np.repeat(np.arange(n_tokens // GRID_W), GRID_W)
    cols = np.tile(np.arange(GRID_W), n_tokens // GRID_W)
    pos = np.stack([rows, cols], axis=-1).astype(np.float64)
    inv_freq = 1.0 / (ROPE_THETA ** (np.arange(ROPE_FREQS, dtype=np.float64) / ROPE_FREQS))
    ang = pos[..., None] * inv_freq
    cos, sin = np.cos(ang), np.sin(ang)
    cos32 = np.broadcast_to(cos[:, :, None, :], (n_tokens, 2, 2, ROPE_FREQS)).reshape(n_tokens, QK_ROPE)
    sgn = np.array([-1.0, 1.0])[None, None, :, None]
    sin32 = (np.broadcast_to(sin[:, :, None, :], (n_tokens, 2, 2, ROPE_FREQS)) * sgn).reshape(n_tokens, QK_ROPE)
    return cos32, sin32


def _half_swap_perm():
    p = np.arange(QK_ROPE).reshape(2, 2, ROPE_FREQS)
    return p[:, ::-1, :].reshape(QK_ROPE)


def _tables():
    c0 = SM_SCALE * math.log2(math.e)
    cos32, sin32 = _rope_np(SEQ)
    one64 = np.ones((SEQ, QK_NOPE))
    tq_lat = c0 * np.concatenate([one64, cos32, sin32], axis=1)
    tk_lat = np.concatenate([cos32, sin32, np.zeros((SEQ, LANES - 2 * QK_ROPE))], axis=1)
    tq_ctx = c0 * np.concatenate([np.ones((CTX_LEN, QK_NOPE + QK_ROPE)), np.zeros((CTX_LEN, QK_ROPE))], axis=1)
    tk_ctx = np.concatenate([np.ones((CTX_LEN, QK_ROPE)), np.zeros((CTX_LEN, LANES - QK_ROPE))], axis=1)
    f32 = lambda a: jnp.asarray(a, F32)
    bf = lambda a: jnp.asarray(a, BF16)

    def dft(n):
        k = np.arange(n)
        ang = 2.0 * np.pi * ((k[:, None] * k[None, :]) % n) / n
        return np.cos(ang), np.sin(ang)

    c128, s128 = dft(FFT_R)
    c256, s256 = dft(CTX_LEN)
    bd_lat = np.concatenate([c128, -s128], axis=1) / math.sqrt(SEQ * F_GROUP_DIM)
    bd_ctx = np.concatenate([c128, -s128], axis=1) / math.sqrt(CTX_LEN * F_GROUP_DIM)
    k1 = np.arange(FFT_R)
    tw = 2.0 * np.pi * ((k1[:, None] * k1[None, :]) % SEQ) / SEQ
    tc = jnp.broadcast_to(f32(np.cos(tw))[:, :, None], (FFT_R, FFT_R, LANES)).reshape(FFT_R, FFT_R * LANES)
    ts = jnp.broadcast_to(f32(np.sin(tw))[:, :, None], (FFT_R, FFT_R, LANES)).reshape(FFT_R, FFT_R * LANES)
    return dict(tq_lat=f32(tq_lat), tk_lat=f32(tk_lat), tq_ctx=f32(tq_ctx), tk_ctx=f32(tk_ctx),
                c128=bf(c128), s128=bf(s128), c256=bf(c256), s256=bf(s256),
                bd_lat=bf(bd_lat), bd_ctx=bf(bd_ctx), tc=tc, ts=ts)


def _arrange_weights(l, w_in, w_uq, w_uk, w_uv):
    sw = _half_swap_perm()
    wi = w_in[l]
    kr = wi[:, OFF_KR:OFF_F]
    w_ip = jnp.concatenate([
        wi[:, :OFF_KR], kr, kr[:, sw], jnp.zeros((D_MODEL, LANES - 2 * QK_ROPE), F32),
        wi[:, OFF_F:]], axis=1).astype(BF16)
    uq = w_uq[l].reshape(Q_LORA, N_HEADS, QK_NOPE + QK_ROPE)
    qr = uq[:, :, QK_NOPE:]
    wq = jnp.concatenate([uq, qr[:, :, sw]], axis=-1).reshape(Q_LORA, N_HEADS * HEAD_PAD).astype(BF16)
    uk = w_uk[l].reshape(KV_LORA, N_HEADS, QK_NOPE)
    top = jnp.concatenate([uk, jnp.zeros((KV_LORA, N_HEADS, HEAD_PAD - QK_NOPE), F32)], axis=-1)
    rmat = np.zeros((LANES, HEAD_PAD), np.float32)
    for i in range(QK_ROPE):
        for r in (i, QK_ROPE + i):
            rmat[r, QK_NOPE + i] = 1.0
            rmat[r, QK_NOPE + QK_ROPE + i] = 1.0
    bot = jnp.broadcast_to(jnp.asarray(rmat)[:, None, :], (LANES, N_HEADS, HEAD_PAD))
    wkt = jnp.transpose(jnp.concatenate([top, bot], axis=0), (1, 2, 0)).astype(BF16)
    uv = w_uv[l].reshape(KV_LORA, N_HEADS // 2, 2, V_HEAD)
    z = jnp.zeros_like(uv[:, :, 0])
    even = jnp.concatenate([uv[:, :, 0], z], axis=-1)
    odd = jnp.concatenate([z, uv[:, :, 1]], axis=-1)
    wv = jnp.stack([even, odd], axis=2).reshape(KV_LORA, N_HEADS * HEAD_PAD).astype(BF16)
    return w_ip, wq, wkt, wv


def kernel(x, c, ctx, c_ctx, w_mod, b_mod, w_in, g_q, w_uq, g_kv, w_uk, w_uv, w_ao, w_fo, w_out,
           ln1_g, ln1_b, ln2_g, ln2_b, w_ff_gate, w_ff_up, w_ff_down,
           w_router, b_router, w_e_gate, w_e_up, w_e_down):
    tb = _tables()
    t_lat = BATCH * SEQ
    t_ctx = BATCH * CTX_LEN

    cs = jnp.concatenate([c, c_ctx[None, :], jnp.zeros((SUBLANES - BATCH - 1, D_MODEL), F32)], axis=0)
    mods = _modulation(cs, w_mod.astype(BF16), b_mod).reshape(DEPTH, SUBLANES, N_MOD, D_MODEL)
    mods = jnp.concatenate([mods, jnp.zeros((DEPTH, SUBLANES, SUBLANES - N_MOD, D_MODEL), F32)], axis=2)

    xl = x.reshape(t_lat, D_MODEL)
    xc = ctx.reshape(t_ctx, D_MODEL)

    for l in range(DEPTH):
        last = l == DEPTH - 1
        mod_x = mods[l, :BATCH]
        mod_c = jnp.broadcast_to(mods[l, BATCH:BATCH + 1], (BATCH, SUBLANES, D_MODEL))
        w_ip, wq, wkt, wv = _arrange_weights(l, w_in, w_uq, w_uk, w_uv)
        gq = g_q[l][None, :]
        gkv = g_kv[l][None, :]
        wao = w_ao[l].astype(BF16)
        wfo = w_fo[l].astype(BF16)
        wout = w_out[l].astype(BF16)
        g1, b1 = ln1_g[l][None, :], ln1_b[l][None, :]
        g2, b2 = ln2_g[l][None, :], ln2_b[l][None, :]

        qlat_c, ckr_c, wf_c, gates_c = _inproj(xc, mod_c, w_ip, gq, gkv, tb["tk_ctx"], tb["bd_ctx"], tm=CTX_LEN)
        q_c, kt_c, v_c = _qkv(qlat_c, ckr_c, tb["tq_ctx"], wq, wkt, wv, nb=BATCH, tm=CTX_LEN)

        qlat, ckr, wf, gates = _inproj(xl, mod_x, w_ip, gq, gkv, tb["tk_lat"], tb["bd_lat"], tm=ROW_TM)
        q, kt, v = _qkv(qlat, ckr, tb["tq_lat"], wq, wkt, wv, nb=BATCH, tm=QKV_TM)
        attn = _attention(q, kt_c, v_c, kt, v, tq=ATTN_TQ, tk=ATTN_TK, peel=ATTN_PEEL, unroll=ATTN_UNROLL)
        four = _fourier_latent(wf.reshape(BATCH, SEQ, 2 * FOURIER_DIM), tb["c128"], tb["s128"], tb["tc"], tb["ts"])

        if l % 2 == 0:
            i = l // 2
            x1, h2 = _mixer(attn.reshape(t_lat, -1), four.reshape(t_lat, -1), gates, xl, mod_x, g1, b1,
                            wao, wfo, wout, tm=ROW_TM)
            wg = w_ff_gate[i].astype(BF16)
            wu = w_ff_up[i].astype(BF16)
            wd = w_ff_down[i].astype(BF16)
            xl_new = _ffn(h2, x1, mod_x, g2, b2, wg, wu, wd, tm=ROW_TM)
        else:
            i = l // 2
            wr = jnp.concatenate([w_router[i], jnp.zeros((D_MODEL, LANES - N_EXPERTS), F32)], axis=1)
            br = jnp.concatenate([b_router[i], jnp.zeros((LANES - N_EXPERTS,), F32)])[None, :]
            x1, h2, rinfo = _mixer(attn.reshape(t_lat, -1), four.reshape(t_lat, -1), gates, xl, mod_x, g1, b1,
                                   wao, wfo, wout, tm=ROUTE_TM, router=(wr, br))
            wg = w_e_gate[i].astype(BF16)
            wu = w_e_up[i].astype(BF16)
            wd = w_e_down[i].astype(BF16)
            xl_new = _moe(h2, x1, rinfo, mod_x, g2, b2, wg, wu, wd)

        if not last:
            assert l % 2 == 0
            attn_c = _attention(q_c, kt_c, v_c, None, None, tq=CTX_LEN, tk=CTX_LEN)
            four_c = _fourier_ctx(wf_c.reshape(BATCH, CTX_LEN, 2 * FOURIER_DIM), tb["c256"], tb["s256"])
            x1c, h2c = _mixer(attn_c.reshape(t_ctx, -1), four_c.reshape(t_ctx, -1), gates_c, xc, mod_c,
                              g1, b1, wao, wfo, wout, tm=CTX_LEN)
            xc = _ffn(h2c, x1c, mod_c, g2, b2, wg, wu, wd, tm=CTX_LEN)
        xl = xl_new

    return xl.reshape(BATCH, SEQ, D_MODEL)
```

```python
import functools
import math

import numpy as np
import jax
import jax.numpy as jnp
from jax import lax
from jax.experimental import pallas as pl
from jax.experimental.pallas import tpu as pltpu

D_MODEL = 1024
BATCH = 2
SEQ = 16384
DEPTH = 2
GRID_W = 64
CTX_LEN = 256
N_HEADS = 16
QK_NOPE = 64
QK_ROPE = 32
ROPE_FREQS = QK_ROPE // 4
V_HEAD = 64
Q_LORA = 256
KV_LORA = 128
ROPE_THETA = 10000.0
SM_SCALE = (QK_NOPE + QK_ROPE) ** -0.5
F_GROUPS = 4
F_GROUP_DIM = 128
FOURIER_DIM = F_GROUPS * F_GROUP_DIM
OFF_KV = Q_LORA
OFF_KR = OFF_KV + KV_LORA
OFF_F = OFF_KR + QK_ROPE
OFF_G = OFF_F + FOURIER_DIM
D_FF = 2816
N_EXPERTS = 8
D_FF_EXPERT = 3584
ALPHA = (2 * DEPTH) ** 0.25
LN_EPS = 1e-6
RMS_EPS = 1e-6

LANES = 128
SUBLANES = 8
N_MOD = 6
HEAD_PAD = 128
FFT_R = 128
V_ONES_LANE = (V_HEAD, 0)
RI_E1, RI_E2, RI_W1, RI_W2 = N_EXPERTS, N_EXPERTS + 1, N_EXPERTS + 2, N_EXPERTS + 3
VMEM_LIMIT = 56 * 1024 * 1024
ROW_TM = 1024
ROUTE_TM = 512
QKV_TM = 512
ATTN_TQ = 1024
ATTN_TK = 256
ATTN_PEEL = 32
ATTN_UNROLL = 16

BF16 = jnp.bfloat16
F32 = jnp.float32


def _cp(sem, vmem=VMEM_LIMIT):
    return pltpu.CompilerParams(dimension_semantics=sem, vmem_limit_bytes=vmem)


def _dot(a, b):
    return jnp.dot(a, b, preferred_element_type=F32)


def _ln(x):
    mu = jnp.mean(x, axis=-1, keepdims=True)
    xc = x - mu
    var = jnp.mean(xc * xc, axis=-1, keepdims=True)
    return xc * lax.rsqrt(var + LN_EPS)


def _rms(x, g):
    return x * lax.rsqrt(jnp.mean(x * x, axis=-1, keepdims=True) + RMS_EPS) * g


def _sigmoid(x):
    return 1.0 / (1.0 + jnp.exp(-x))


def _mod_kernel(cs_ref, w_ref, b_ref, o_ref):
    cs = cs_ref[...]
    a = (cs * _sigmoid(cs)).astype(BF16)
    o_ref[0] = _dot(a, w_ref[0]) + b_ref[0]


def _modulation(cs, w_mod, b_mod):
    n_chunk = 1024
    n_out = w_mod.shape[-1]
    return pl.pallas_call(
        _mod_kernel,
        grid=(DEPTH, n_out // n_chunk),
        in_specs=[
            pl.BlockSpec((SUBLANES, D_MODEL), lambda l, j: (0, 0)),
            pl.BlockSpec((1, D_MODEL, n_chunk), lambda l, j: (l, 0, j)),
            pl.BlockSpec((1, 1, n_chunk), lambda l, j: (l, 0, j)),
        ],
        out_specs=pl.BlockSpec((1, SUBLANES, n_chunk), lambda l, j: (l, 0, j)),
        out_shape=jax.ShapeDtypeStruct((DEPTH, SUBLANES, n_out), F32),
        compiler_params=_cp(("parallel", "parallel")),
        name="mod",
    )(cs, w_mod, b_mod.reshape(DEPTH, 1, n_out))


IP_Q = 0
IP_KV = Q_LORA
IP_KR = IP_KV + KV_LORA
IP_F = IP_KR + LANES
IP_G = IP_F + FOURIER_DIM
IP_COLS = IP_G + 2 * D_MODEL


def _inproj_kernel(x_ref, mod_ref, w_ref, gq_ref, gkv_ref, tk_ref, bd_ref,
                   qlat_ref, ckr_ref, wf_ref, gates_ref):
    x = x_ref[...]
    mod = mod_ref[0]
    h = (_ln(x) * (1.0 + mod[1:2]) + mod[0:1]).astype(BF16)
    p0 = _dot(h, w_ref[:, IP_Q:IP_F])
    qlat_ref[...] = _rms(p0[:, IP_Q:IP_KV], gq_ref[...]).astype(BF16)
    ckr_ref[:, 0:KV_LORA] = _rms(p0[:, IP_KV:IP_KR], gkv_ref[...]).astype(BF16)
    ckr_ref[:, KV_LORA:] = (p0[:, IP_KR:IP_F] * tk_ref[...]).astype(BF16)
    uf = _dot(h, w_ref[:, IP_F:IP_G]).astype(BF16)
    for g in range(F_GROUPS):
        r = _dot(uf[:, g * LANES:(g + 1) * LANES], bd_ref[...])
        wf_ref[:, g * LANES:(g + 1) * LANES] = r[:, :LANES].astype(BF16)
        wf_ref[:, FOURIER_DIM + g * LANES:FOURIER_DIM + (g + 1) * LANES] = r[:, LANES:].astype(BF16)
    gc = 512
    for c in range(2 * D_MODEL // gc):
        gl = _dot(h, w_ref[:, IP_G + c * gc:IP_G + (c + 1) * gc])
        gates_ref[:, c * gc:(c + 1) * gc] = _sigmoid(gl).astype(BF16)


def _inproj(x2d, mod, w, gq, gkv, tk, bd, tm):
    t = x2d.shape[0]
    nb = mod.shape[0]
    tpb = t // nb // tm
    return pl.pallas_call(
        _inproj_kernel,
        grid=(t // tm,),
        in_specs=[
            pl.BlockSpec((tm, D_MODEL), lambda i: (i, 0)),
            pl.BlockSpec((1, SUBLANES, D_MODEL), lambda i: (i // tpb, 0, 0)),
            pl.BlockSpec((D_MODEL, IP_COLS), lambda i: (0, 0)),
            pl.BlockSpec((1, Q_LORA), lambda i: (0, 0)),
            pl.BlockSpec((1, KV_LORA), lambda i: (0, 0)),
            pl.BlockSpec((tm, LANES), lambda i: (i % tpb, 0)),
            pl.BlockSpec((LANES, 2 * LANES), lambda i: (0, 0)),
        ],
        out_specs=[
            pl.BlockSpec((tm, Q_LORA), lambda i: (i, 0)),
            pl.BlockSpec((tm, 2 * LANES), lambda i: (i, 0)),
            pl.BlockSpec((tm, 2 * FOURIER_DIM), lambda i: (i, 0)),
            pl.BlockSpec((tm, 2 * D_MODEL), lambda i: (i, 0)),
        ],
        out_shape=[
            jax.ShapeDtypeStruct((t, Q_LORA), BF16),
            jax.ShapeDtypeStruct((t, 2 * LANES), BF16),
            jax.ShapeDtypeStruct((t, 2 * FOURIER_DIM), BF16),
            jax.ShapeDtypeStruct((t, 2 * D_MODEL), BF16),
        ],
        compiler_params=_cp(("parallel",)),
        name="inproj",
    )(x2d, mod, w, gq, gkv, tk, bd)


def _qkv_kernel(qlat_ref, ckr_ref, tq_ref, wq_ref, wkt_ref, wv_ref, q_ref, kt_ref, v_ref):
    ql = qlat_ref[...]
    ckr = ckr_ref[...]
    tq = tq_ref[...]
    ckv = ckr[:, :KV_LORA]
    lane = lax.broadcasted_iota(jnp.int32, (ckr.shape[0], HEAD_PAD), 1)
    q_all = _dot(ql, wq_ref[...])
    v_all = _dot(ckv, wv_ref[...])
    for h in range(N_HEADS):
        cols = slice(h * HEAD_PAD, (h + 1) * HEAD_PAD)
        q_ref[0, h] = (q_all[:, cols] * tq).astype(BF16)
        kt_ref[0, h] = lax.dot_general(
            wkt_ref[h], ckr, (((1,), (1,)), ((), ())), preferred_element_type=F32).astype(BF16)
        v_ref[0, h] = jnp.where(lane == V_ONES_LANE[h % 2], 1.0, v_all[:, cols]).astype(BF16)


def _qkv(qlat, ckr, tq, wq, wkt, wv, nb, tm):
    t = qlat.shape[0]
    n = t // nb
    tpb = n // tm
    return pl.pallas_call(
        _qkv_kernel,
        grid=(t // tm,),
        in_specs=[
            pl.BlockSpec((tm, Q_LORA), lambda i: (i, 0)),
            pl.BlockSpec((tm, 2 * LANES), lambda i: (i, 0)),
            pl.BlockSpec((tm, HEAD_PAD), lambda i: (i % tpb, 0)),
            pl.BlockSpec((Q_LORA, N_HEADS * HEAD_PAD), lambda i: (0, 0)),
            pl.BlockSpec((N_HEADS, HEAD_PAD, 2 * LANES), lambda i: (0, 0, 0)),
            pl.BlockSpec((KV_LORA, N_HEADS * HEAD_PAD), lambda i: (0, 0)),
        ],
        out_specs=[
            pl.BlockSpec((1, N_HEADS, tm, HEAD_PAD), lambda i: (i // tpb, 0, i % tpb, 0)),
            pl.BlockSpec((1, N_HEADS, HEAD_PAD, tm), lambda i: (i // tpb, 0, 0, i % tpb)),
            pl.BlockSpec((1, N_HEADS, tm, HEAD_PAD), lambda i: (i // tpb, 0, i % tpb, 0)),
        ],
        out_shape=[
            jax.ShapeDtypeStruct((nb, N_HEADS, n, HEAD_PAD), BF16),
            jax.ShapeDtypeStruct((nb, N_HEADS, HEAD_PAD, n), BF16),
            jax.ShapeDtypeStruct((nb, N_HEADS, n, HEAD_PAD), BF16),
        ],
        compiler_params=_cp(("parallel",)),
        name="qkv",
    )(qlat, ckr, tq, wq, wkt, wv)


def _attn_kernel(q_ref, ktc_ref, vc_ref, *rest, n_chunks, tk, peel, unroll):
    if n_chunks:
        kt_ref, v_ref, o_ref = rest
    else:
        (o_ref,) = rest

    def step(q, kt, v, m, acc):
        s = _dot(q, kt)
        m_new = jnp.max(s, axis=1, keepdims=True)
        if m is not None:
            m_new = jnp.maximum(m, m_new)
        p = jnp.exp2((s - m_new).astype(BF16))
        pv = _dot(p, v)
        if m is not None:
            pv = jnp.exp2(m - m_new) * acc + pv
        return m_new, pv

    qs = [q_ref[0, hh] for hh in range(2)]
    carry = []
    for hh in range(2):
        carry += step(qs[hh], ktc_ref[0, hh], vc_ref[0, hh], None, None)

    if n_chunks:
        span = unroll * tk

        def group(base, count, carry):
            for u in range(count):
                off = base + u * tk
                if not isinstance(off, int):
                    off = pl.multiple_of(off, tk)
                out = []
                for hh in range(2):
                    out += step(qs[hh], kt_ref[0, hh, :, pl.ds(off, tk)], v_ref[0, hh, pl.ds(off, tk), :],
                                carry[2 * hh], carry[2 * hh + 1])
                carry = out
            return tuple(carry)

        carry = group(0, peel, carry)
        carry = lax.fori_loop(0, (n_chunks - peel) // unroll,
                              lambda g, c: group(pl.multiple_of(peel * tk + g * span, tk), unroll, c), carry)

    lane = lax.broadcasted_iota(jnp.int32, carry[1].shape, 1)
    outs = []
    for hh in range(2):
        acc = carry[2 * hh + 1]
        l = jnp.sum(jnp.where(lane == V_ONES_LANE[hh], acc, 0.0), axis=1, keepdims=True)
        outs.append(acc * (1.0 / l))
    o_ref[0] = jnp.where(lane < V_HEAD, outs[0], outs[1]).astype(BF16)


def _attention(q, kt_c, v_c, kt, v, tq, tk, peel=0, unroll=1):
    nb, _, n, _ = q.shape
    nc = kt_c.shape[-1]
    n_chunks = 0 if kt is None else kt.shape[-1] // tk
    in_specs = [
        pl.BlockSpec((1, 2, tq, HEAD_PAD), lambda b, p, i: (b, p, i, 0)),
        pl.BlockSpec((1, 2, HEAD_PAD, nc), lambda b, p, i: (b, p, 0, 0)),
        pl.BlockSpec((1, 2, nc, HEAD_PAD), lambda b, p, i: (b, p, 0, 0)),
    ]
    args = [q, kt_c, v_c]
    if n_chunks:
        nk = kt.shape[-1]
        in_specs += [
            pl.BlockSpec((1, 2, HEAD_PAD, nk), lambda b, p, i: (b, p, 0, 0)),
            pl.BlockSpec((1, 2, nk, HEAD_PAD), lambda b, p, i: (b, p, 0, 0)),
        ]
        args += [kt, v]
    return pl.pallas_call(
        functools.partial(_attn_kernel, n_chunks=n_chunks, tk=tk, peel=peel, unroll=unroll),
        grid=(nb, N_HEADS // 2, n // tq),
        in_specs=in_specs,
        out_specs=pl.BlockSpec((1, tq, 2 * V_HEAD), lambda b, p, i: (b, i, p)),
        out_shape=jax.ShapeDtypeStruct((nb, n, N_HEADS * V_HEAD), BF16),
        compiler_params=_cp(("parallel", "parallel", "arbitrary")),
        name="attn",
    )(*args)


def _fft1_kernel(x_ref, c_ref, s_ref, tc_ref, ts_ref, o_ref, *, n2t):
    x = x_ref[0]
    cx = _dot(c_ref[...], x)
    sx = _dot(s_ref[...], x)
    w = 2 * FOURIER_DIM
    for t in range(n2t):
        re = slice(t * w, t * w + FOURIER_DIM)
        im = slice(t * w + FOURIER_DIM, (t + 1) * w)
        yr = cx[:, re] + sx[:, im]
        yi = cx[:, im] - sx[:, re]
        tc = jnp.concatenate([tc_ref[:, t * LANES:(t + 1) * LANES]] * F_GROUPS, axis=1)
        ts = jnp.concatenate([ts_ref[:, t * LANES:(t + 1) * LANES]] * F_GROUPS, axis=1)
        o_ref[0, :, re] = (yr * tc + yi * ts).astype(BF16)
        o_ref[0, :, im] = (yi * tc - yr * ts).astype(BF16)


def _fft2_kernel(y_ref, c_ref, s_ref, o_ref, *, k1t):
    for t in range(k1t):
        y = y_ref[0, t]
        zr = _dot(c_ref[...], y[:, :FOURIER_DIM]) + _dot(s_ref[...], y[:, FOURIER_DIM:])
        o_ref[0, :, t * FOURIER_DIM:(t + 1) * FOURIER_DIM] = zr.astype(BF16)


def _fourier_latent(wf, cmat, smat, tc, ts):
    nb = wf.shape[0]
    w = 2 * FOURIER_DIM
    n2t = 8
    y = pl.pallas_call(
        functools.partial(_fft1_kernel, n2t=n2t),
        grid=(nb, FFT_R // n2t),
        in_specs=[
            pl.BlockSpec((1, FFT_R, n2t * w), lambda b, j: (b, 0, j)),
            pl.BlockSpec((FFT_R, FFT_R), lambda b, j: (0, 0)),
            pl.BlockSpec((FFT_R, FFT_R), lambda b, j: (0, 0)),
            pl.BlockSpec((FFT_R, n2t * LANES), lambda b, j: (0, j)),
            pl.BlockSpec((FFT_R, n2t * LANES), lambda b, j: (0, j)),
        ],
        out_specs=pl.BlockSpec((1, FFT_R, n2t * w), lambda b, j: (b, 0, j)),
        out_shape=jax.ShapeDtypeStruct((nb, FFT_R, FFT_R * w), BF16),
        compiler_params=_cp(("parallel", "parallel")),
        name="fft1",
    )(wf.reshape(nb, FFT_R, FFT_R * w), cmat, smat, tc, ts)
    k1t = 8
    four = pl.pallas_call(
        functools.partial(_fft2_kernel, k1t=k1t),
        grid=(nb, FFT_R // k1t),
        in_specs=[
            pl.BlockSpec((1, k1t, FFT_R, w), lambda b, j: (b, j, 0, 0)),
            pl.BlockSpec((FFT_R, FFT_R), lambda b, j: (0, 0)),
            pl.BlockSpec((FFT_R, FFT_R), lambda b, j: (0, 0)),
        ],
        out_specs=pl.BlockSpec((1, FFT_R, k1t * FOURIER_DIM), lambda b, j: (b, 0, j)),
        out_shape=jax.ShapeDtypeStruct((nb, FFT_R, FFT_R * FOURIER_DIM), BF16),
        compiler_params=_cp(("parallel", "parallel")),
        name="fft2",
    )(y.reshape(nb, FFT_R, FFT_R, w), cmat, smat)
    return four.reshape(nb, SEQ, FOURIER_DIM)


def _dft_ctx_kernel(x_ref, c_ref, s_ref, o_ref):
    x = x_ref[0]
    zr = _dot(c_ref[...], x[:, :FOURIER_DIM]) + _dot(s_ref[...], x[:, FOURIER_DIM:])
    o_ref[0] = zr.astype(BF16)


def _fourier_ctx(wf, cmat, smat):
    nb, n, w = wf.shape
    return pl.pallas_call(
        _dft_ctx_kernel,
        grid=(nb,),
        in_specs=[
            pl.BlockSpec((1, n, w), lambda b: (b, 0, 0)),
            pl.BlockSpec((n, n), lambda b: (0, 0)),
            pl.BlockSpec((n, n), lambda b: (0, 0)),
        ],
        out_specs=pl.BlockSpec((1, n, FOURIER_DIM), lambda b: (b, 0, 0)),
        out_shape=jax.ShapeDtypeStruct((nb, n, FOURIER_DIM), BF16),
        compiler_params=_cp(("parallel",)),
        name="dft_ctx",
    )(wf, cmat, smat)


def _mixer_kernel(attn_ref, four_ref, gates_ref, x_ref, mod_ref, g_ref, b_ref,
                  wao_ref, wfo_ref, wout_ref, *rest, route):
    if route:
        wr_ref, br_ref, x1_ref, h2_ref, rinfo_ref = rest
    else:
        x1_ref, h2_ref = rest
    mod = mod_ref[0]
    a = _dot(attn_ref[...], wao_ref[...])
    f = _dot(four_ref[...], wfo_ref[...])
    merged = gates_ref[:, :D_MODEL].astype(F32) * a + gates_ref[:, D_MODEL:].astype(F32) * f
    y = _dot(merged.astype(BF16), wout_ref[...])
    x1 = _ln(ALPHA * x_ref[...] + mod[2:3] * y) * g_ref[...] + b_ref[...]
    x1_ref[...] = x1
    h2 = _ln(x1) * (1.0 + mod[4:5]) + mod[3:4]
    h2_ref[...] = h2.astype(h2_ref.dtype)
    if route:
        logits = jnp.dot(h2, wr_ref[...], preferred_element_type=F32,
                         precision=lax.Precision.HIGHEST) + br_ref[...]
        lane = lax.broadcasted_iota(jnp.int32, logits.shape, 1)
        neg = jnp.float32(-jnp.inf)
        lg = jnp.where(lane < N_EXPERTS, logits, neg)
        m1 = jnp.max(lg, axis=1, keepdims=True)
        i1 = jnp.min(jnp.where(lg == m1, lane, LANES), axis=1, keepdims=True)
        lg2 = jnp.where(lane == i1, neg, lg)
        m2 = jnp.max(lg2, axis=1, keepdims=True)
        i2 = jnp.min(jnp.where(lg2 == m2, lane, LANES), axis=1, keepdims=True)
        e2 = jnp.exp(m2 - m1)
        w1 = 1.0 / (1.0 + e2)
        w2 = e2 * w1
        rinfo = jnp.where((lane == i1) | (lane == i2), 1.0, 0.0)
        rinfo = jnp.where(lane == RI_E1, i1.astype(F32), rinfo)
        rinfo = jnp.where(lane == RI_E2, i2.astype(F32), rinfo)
        rinfo = jnp.where(lane == RI_W1, w1, rinfo)
        rinfo_ref[...] = jnp.where(lane == RI_W2, w2, rinfo)


def _mixer(attn, four, gates, x2d, mod, g, b, wao, wfo, wout, tm, router=None):
    t = x2d.shape[0]
    nb = mod.shape[0]
    tpb = t // nb // tm
    row = lambda i: (i, 0)
    const = lambda i: (0, 0)
    in_specs = [
        pl.BlockSpec((tm, D_MODEL), row),
        pl.BlockSpec((tm, FOURIER_DIM), row),
        pl.BlockSpec((tm, 2 * D_MODEL), row),
        pl.BlockSpec((tm, D_MODEL), row),
        pl.BlockSpec((1, SUBLANES, D_MODEL), lambda i: (i // tpb, 0, 0)),
        pl.BlockSpec((1, D_MODEL), const),
        pl.BlockSpec((1, D_MODEL), const),
        pl.BlockSpec((D_MODEL, D_MODEL), const),
        pl.BlockSpec((FOURIER_DIM, D_MODEL), const),
        pl.BlockSpec((D_MODEL, D_MODEL), const),
    ]
    args = [attn, four, gates, x2d, mod, g, b, wao, wfo, wout]
    out_specs = [pl.BlockSpec((tm, D_MODEL), row), pl.BlockSpec((tm, D_MODEL), row)]
    h2_dtype = BF16 if router is None else F32
    out_shape = [jax.ShapeDtypeStruct((t, D_MODEL), F32), jax.ShapeDtypeStruct((t, D_MODEL), h2_dtype)]
    if router is not None:
        in_specs += [pl.BlockSpec((D_MODEL, LANES), const), pl.BlockSpec((1, LANES), const)]
        args += list(router)
        out_specs.append(pl.BlockSpec((tm, LANES), row))
        out_shape.append(jax.ShapeDtypeStruct((t, LANES), F32))
    return pl.pallas_call(
        functools.partial(_mixer_kernel, route=router is not None),
        grid=(t // tm,),
        in_specs=in_specs,
        out_specs=out_specs,
        out_shape=out_shape,
        compiler_params=_cp(("parallel",)),
        name="mixer",
    )(*args)


def _ffn_kernel(h_ref, x1_ref, mod_ref, g_ref, b_ref, wg_ref, wu_ref, wd_ref, o_ref):
    h = h_ref[...]
    gate = _dot(h, wg_ref[...])
    up = _dot(h, wu_ref[...])
    act = (gate * _sigmoid(gate) * up).astype(BF16)
    f = _dot(act, wd_ref[...])
    mod = mod_ref[0]
    o_ref[...] = _ln(ALPHA * x1_ref[...] + mod[5:6] * f) * g_ref[...] + b_ref[...]


def _ffn(h2, x1, mod, g, b, wg, wu, wd, tm):
    t = h2.shape[0]
    nb = mod.shape[0]
    tpb = t // nb // tm
    dff = wg.shape[1]
    row = lambda i: (i, 0)
    const = lambda i: (0, 0)
    resident = pl.Buffered(1)
    return pl.pallas_call(
        _ffn_kernel,
        grid=(t // tm,),
        in_specs=[
            pl.BlockSpec((tm, D_MODEL), row),
            pl.BlockSpec((tm, D_MODEL), row),
            pl.BlockSpec((1, SUBLANES, D_MODEL), lambda i: (i // tpb, 0, 0)),
            pl.BlockSpec((1, D_MODEL), const),
            pl.BlockSpec((1, D_MODEL), const),
            pl.BlockSpec((D_MODEL, dff), const, pipeline_mode=resident),
            pl.BlockSpec((D_MODEL, dff), const, pipeline_mode=resident),
            pl.BlockSpec((dff, D_MODEL), const, pipeline_mode=resident),
        ],
        out_specs=pl.BlockSpec((tm, D_MODEL), row),
        out_shape=jax.ShapeDtypeStruct((t, D_MODEL), F32),
        compiler_params=_cp(("parallel",)),
        name="ffn",
    )(h2, x1, mod, g, b, wg, wu, wd)


MOE_TM = 512
MOE_TT = 512
DMA_ISSUE_UNROLL = 8


def _moe_rows(n_tokens):
    return 2 * n_tokens + N_EXPERTS * MOE_TM


def _scan_kernel(r_ref, o_ref, cnt_ref, run_ref):
    @pl.when(pl.program_id(0) == 0)
    def _():
        run_ref[...] = jnp.zeros_like(run_ref)

    r = r_ref[...]
    ts = r.shape[0]
    lane = lax.broadcasted_iota(jnp.int32, r.shape, 1)
    sel = jnp.where(lane < N_EXPERTS, r, 0.0)
    rows = lax.broadcasted_iota(jnp.int32, (ts, ts), 0)
    cols = lax.broadcasted_iota(jnp.int32, (ts, ts), 1)
    ltri = jnp.where(rows > cols, 1.0, 0.0).astype(BF16)
    prefix = _dot(ltri, sel.astype(BF16)) + run_ref[0:1, :]
    lane_f = lane.astype(F32)
    pick = lambda k: jnp.sum(jnp.where(lane == k, r, 0.0), axis=1, keepdims=True)
    e1, e2 = pick(RI_E1), pick(RI_E2)
    p1 = jnp.sum(jnp.where(lane_f == e1, prefix, 0.0), axis=1, keepdims=True)
    p2 = jnp.sum(jnp.where(lane_f == e2, prefix, 0.0), axis=1, keepdims=True)
    out = jnp.where(lane == 0, e1, jnp.where(lane == 1, e2, jnp.where(lane == 2, p1, jnp.where(lane == 3, p2, 0.0))))
    o_ref[...] = out.astype(jnp.int32)
    run_ref[...] = run_ref[...] + jnp.sum(sel, axis=0, keepdims=True)
    cnt_ref[...] = run_ref[...]


def _route_scan(rinfo, ts=512):
    t = rinfo.shape[0]
    return pl.pallas_call(
        _scan_kernel,
        grid=(t // ts,),
        in_specs=[pl.BlockSpec((ts, LANES), lambda i: (i, 0))],
        out_specs=[pl.BlockSpec((ts, LANES), lambda i: (i, 0)), pl.BlockSpec((SUBLANES, LANES), lambda i: (0, 0))],
        out_shape=[jax.ShapeDtypeStruct((t, LANES), jnp.int32), jax.ShapeDtypeStruct((SUBLANES, LANES), F32)],
        scratch_shapes=[pltpu.VMEM((SUBLANES, LANES), F32)],
        compiler_params=_cp(("arbitrary",)),
        name="route_scan",
    )(rinfo)


def _row_copy(src_ref, src_row, dst_ref, dst_row, sem):
    return pltpu.make_async_copy(src_ref.at[pl.ds(src_row, 1)], dst_ref.at[pl.ds(dst_row, 1)], sem)


def _dispatch_kernel(dest_ref, h_ref, xs_ref, sem):
    def issue(t, c):
        for k in range(2):
            _row_copy(h_ref, t, xs_ref, dest_ref[0, k, t], sem).start(priority=k)
        return c

    lax.fori_loop(0, MOE_TT, issue, 0, unroll=DMA_ISSUE_UNROLL)
    for k in range(2):
        pltpu.make_async_copy(h_ref, xs_ref.at[pl.ds(0, MOE_TT)], sem).wait()


def _dispatch(dest, h2):
    t = h2.shape[0]
    return pl.pallas_call(
        _dispatch_kernel,
        grid=(t // MOE_TT,),
        in_specs=[
            pl.BlockSpec((1, 2, MOE_TT), lambda i: (i, 0, 0), memory_space=pltpu.SMEM),
            pl.BlockSpec((MOE_TT, D_MODEL), lambda i: (i, 0)),
        ],
        out_specs=pl.BlockSpec(memory_space=pl.ANY),
        out_shape=jax.ShapeDtypeStruct((_moe_rows(t), D_MODEL), F32),
        scratch_shapes=[pltpu.SemaphoreType.DMA(())],
        compiler_params=_cp(("arbitrary",)),
        name="moe_dispatch",
    )(dest, h2)


def _experts_kernel(te_ref, tv_ref, nt_ref, xs_ref, wg_ref, wu_ref, wd_ref, y_ref, acc_ref):
    j = pl.program_id(1)

    @pl.when(pl.program_id(0) < nt_ref[0])
    def _():
        rows = lax.broadcasted_iota(jnp.int32, xs_ref.shape, 0)
        h = jnp.where(rows < tv_ref[pl.program_id(0)], xs_ref[...], 0.0).astype(BF16)
        gate = _dot(h, wg_ref[0])
        up = _dot(h, wu_ref[0])
        act = (gate * _sigmoid(gate) * up).astype(BF16)
        part = _dot(act, wd_ref[0])

        @pl.when(j == 0)
        def _():
            acc_ref[...] = part

        @pl.when(j > 0)
        def _():
            acc_ref[...] += part

        @pl.when(j == pl.num_programs(1) - 1)
        def _():
            y_ref[...] = acc_ref[...]

    @pl.when((pl.program_id(0) >= nt_ref[0]) & (j == pl.num_programs(1) - 1))
    def _():
        y_ref[...] = jnp.zeros_like(y_ref)


def _experts(tile_expert, tile_valid, n_tiles_used, xs, wg, wu, wd, fc):
    ne, _, dff = wg.shape
    grid_spec = pltpu.PrefetchScalarGridSpec(
        num_scalar_prefetch=3,
        grid=(xs.shape[0] // MOE_TM, dff // fc),
        in_specs=[
            pl.BlockSpec((MOE_TM, D_MODEL), lambda i, j, te, tv, nt: (i, 0)),
            pl.BlockSpec((1, D_MODEL, fc), lambda i, j, te, tv, nt: (te[i], 0, j)),
            pl.BlockSpec((1, D_MODEL, fc), lambda i, j, te, tv, nt: (te[i], 0, j)),
            pl.BlockSpec((1, fc, D_MODEL), lambda i, j, te, tv, nt: (te[i], j, 0)),
        ],
        out_specs=pl.BlockSpec((MOE_TM, D_MODEL), lambda i, j, te, tv, nt: (i, 0)),
        scratch_shapes=[pltpu.VMEM((MOE_TM, D_MODEL), F32)],
    )
    return pl.pallas_call(
        _experts_kernel,
        grid_spec=grid_spec,
        out_shape=jax.ShapeDtypeStruct(xs.shape, F32),
        compiler_params=_cp(("arbitrary", "arbitrary")),
        name="moe_experts",
    )(tile_expert, tile_valid, n_tiles_used, xs, wg, wu, wd)


def _combine_kernel(dest_ref, dest_next_ref, y_ref, x1_ref, rinfo_ref, mod_ref, g_ref, b_ref, o_ref,
                    ybuf_ref, sems):
    i = pl.program_id(0)
    slot = i % 2

    def gather(d_ref, s):
        def issue(t, c):
            for k in range(2):
                _row_copy(y_ref, d_ref[0, k, t], ybuf_ref.at[s, k], t, sems.at[s]).start(priority=k)
            return c
        lax.fori_loop(0, MOE_TT, issue, 0, unroll=DMA_ISSUE_UNROLL)

    @pl.when(i == 0)
    def _():
        gather(dest_ref, slot)

    @pl.when(i + 1 < pl.num_programs(0))
    def _():
        gather(dest_next_ref, 1 - slot)

    for k in range(2):
        pltpu.make_async_copy(y_ref.at[pl.ds(0, MOE_TT)], ybuf_ref.at[slot, k], sems.at[slot]).wait()

    r = rinfo_ref[...]
    lane = lax.broadcasted_iota(jnp.int32, r.shape, 1)
    w1 = jnp.sum(jnp.where(lane == RI_W1, r, 0.0), axis=1, keepdims=True)
    w2 = jnp.sum(jnp.where(lane == RI_W2, r, 0.0), axis=1, keepdims=True)
    f = w1 * ybuf_ref[slot, 0] + w2 * ybuf_ref[slot, 1]
    mod = mod_ref[0]
    o_ref[...] = _ln(ALPHA * x1_ref[...] + mod[5:6] * f) * g_ref[...] + b_ref[...]


def _combine(dest, y, x1, rinfo, mod, g, b):
    t = x1.shape[0]
    nb = mod.shape[0]
    tpb = t // nb // MOE_TT
    row = lambda i: (i, 0)
    const = lambda i: (0, 0)
    n_steps = t // MOE_TT
    return pl.pallas_call(
        _combine_kernel,
        grid=(n_steps,),
        in_specs=[
            pl.BlockSpec((1, 2, MOE_TT), lambda i: (i, 0, 0), memory_space=pltpu.SMEM),
            pl.BlockSpec((1, 2, MOE_TT), lambda i: (jnp.minimum(i + 1, n_steps - 1), 0, 0),
                         memory_space=pltpu.SMEM),
            pl.BlockSpec(memory_space=pl.ANY),
            pl.BlockSpec((MOE_TT, D_MODEL), row),
            pl.BlockSpec((MOE_TT, LANES), row),
            pl.BlockSpec((1, SUBLANES, D_MODEL), lambda i: (i // tpb, 0, 0)),
            pl.BlockSpec((1, D_MODEL), const),
            pl.BlockSpec((1, D_MODEL), const),
        ],
        out_specs=pl.BlockSpec((MOE_TT, D_MODEL), row),
        out_shape=jax.ShapeDtypeStruct((t, D_MODEL), F32),
        scratch_shapes=[pltpu.VMEM((2, 2, MOE_TT, D_MODEL), F32), pltpu.SemaphoreType.DMA((2,))],
        compiler_params=_cp(("arbitrary",)),
        name="moe_combine",
    )(dest, dest, y, x1, rinfo, mod, g, b)


def _moe(h2, x1, rinfo, mod, g, b, wg, wu, wd):
    t = h2.shape[0]
    route, counts = _route_scan(rinfo)
    cnt = counts[0, :N_EXPERTS].astype(jnp.int32)
    padded = (cnt + MOE_TM - 1) // MOE_TM * MOE_TM
    ends = jnp.cumsum(padded)
    starts = ends - padded
    dest = jnp.stack([starts[route[:, 0]] + route[:, 2], starts[route[:, 1]] + route[:, 3]])
    dest = dest.reshape(2, t // MOE_TT, MOE_TT).transpose(1, 0, 2)
    tile_row0 = jnp.arange(_moe_rows(t) // MOE_TM, dtype=jnp.int32) * MOE_TM
    tile_expert = jnp.minimum(jnp.sum(tile_row0[:, None] >= ends[None, :], axis=1), N_EXPERTS - 1).astype(jnp.int32)
    tile_valid = jnp.clip(starts[tile_expert] + cnt[tile_expert] - tile_row0, 0, MOE_TM).astype(jnp.int32)
    n_tiles_used = (ends[-1:] // MOE_TM).astype(jnp.int32)
    xs = _dispatch(dest, h2)
    y = _experts(tile_expert, tile_valid, n_tiles_used, xs, wg, wu, wd, fc=D_FF_EXPERT // 2)
    return _combine(dest, y, x1, rinfo, mod, g, b)


def _rope_np(n_tokens):
    rows = np.repeat(np.arange(n_tokens // GRID_W), GRID_W)
    cols = np.tile(np.arange(GRID_W), n_tokens // GRID_W)
    pos = np.stack([rows, cols], axis=-1).astype(np.float64)
    inv_freq = 1.0 / (ROPE_THETA ** (np.arange(ROPE_FREQS, dtype=np.float64) / ROPE_FREQS))
    ang = pos[..., None] * inv_freq
    cos, sin = np.cos(ang), np.sin(ang)
    cos32 = np.broadcast_to(cos[:, :, None, :], (n_tokens, 2, 2, ROPE_FREQS)).reshape(n_tokens, QK_ROPE)
    sgn = np.array([-1.0, 1.0])[None, None, :, None]
    sin32 = (np.broadcast_to(sin[:, :, None, :], (n_tokens, 2, 2, ROPE_FREQS)) * sgn).reshape(n_tokens, QK_ROPE)
    return cos32, sin32


def _half_swap_perm():
    p = np.arange(QK_ROPE).reshape(2, 2, ROPE_FREQS)
    return p[:, ::-1, :].reshape(QK_ROPE)


def _tables():
    c0 = SM_SCALE * math.log2(math.e)
    cos32, sin32 = _rope_np(SEQ)
    one64 = np.ones((SEQ, QK_NOPE))
    tq_lat = c0 * np.concatenate([one64, cos32, sin32], axis=1)
    tk_lat = np.concatenate([cos32, sin32, np.zeros((SEQ, LANES - 2 * QK_ROPE))], axis=1)
    tq_ctx = c0 * np.concatenate([np.ones((CTX_LEN, QK_NOPE + QK_ROPE)), np.zeros((CTX_LEN, QK_ROPE))], axis=1)
    tk_ctx = np.concatenate([np.ones((CTX_LEN, QK_ROPE)), np.zeros((CTX_LEN, LANES - QK_ROPE))], axis=1)
    f32 = lambda a: jnp.asarray(a, F32)
    bf = lambda a: jnp.asarray(a, BF16)

    def dft(n):
        k = np.arange(n)
        ang = 2.0 * np.pi * ((k[:, None] * k[None, :]) % n) / n
        return np.cos(ang), np.sin(ang)

    c128, s128 = dft(FFT_R)
    c256, s256 = dft(CTX_LEN)
    bd_lat = np.concatenate([c128, -s128], axis=1) / math.sqrt(SEQ * F_GROUP_DIM)
    bd_ctx = np.concatenate([c128, -s128], axis=1) / math.sqrt(CTX_LEN * F_GROUP_DIM)
    k1 = np.arange(FFT_R)
    tw = 2.0 * np.pi * ((k1[:, None] * k1[None, :]) % SEQ) / SEQ
    tc = jnp.broadcast_to(f32(np.cos(tw))[:, :, None], (FFT_R, FFT_R, LANES)).reshape(FFT_R, FFT_R * LANES)
    ts = jnp.broadcast_to(f32(np.sin(tw))[:, :, None], (FFT_R, FFT_R, LANES)).reshape(FFT_R, FFT_R * LANES)
    return dict(tq_lat=f32(tq_lat), tk_lat=f32(tk_lat), tq_ctx=f32(tq_ctx), tk_ctx=f32(tk_ctx),
                c128=bf(c128), s128=bf(s128), c256=bf(c256), s256=bf(s256),
                bd_lat=bf(bd_lat), bd_ctx=bf(bd_ctx), tc=tc, ts=ts)


def _arrange_weights(l, w_in, w_uq, w_uk, w_uv):
    sw = _half_swap_perm()
    wi = w_in[l]
    kr = wi[:, OFF_KR:OFF_F]
    w_ip = jnp.concatenate([
        wi[:, :OFF_KR], kr, kr[:, sw], jnp.zeros((D_MODEL, LANES - 2 * QK_ROPE), F32),
        wi[:, OFF_F:]], axis=1).astype(BF16)
    uq = w_uq[l].reshape(Q_LORA, N_HEADS, QK_NOPE + QK_ROPE)
    qr = uq[:, :, QK_NOPE:]
    wq = jnp.concatenate([uq, qr[:, :, sw]], axis=-1).reshape(Q_LORA, N_HEADS * HEAD_PAD).astype(BF16)
    uk = w_uk[l].reshape(KV_LORA, N_HEADS, QK_NOPE)
    top = jnp.concatenate([uk, jnp.zeros((KV_LORA, N_HEADS, HEAD_PAD - QK_NOPE), F32)], axis=-1)
    rmat = np.zeros((LANES, HEAD_PAD), np.float32)
    for i in range(QK_ROPE):
        for r in (i, QK_ROPE + i):
            rmat[r, QK_NOPE + i] = 1.0
            rmat[r, QK_NOPE + QK_ROPE + i] = 1.0
    bot = jnp.broadcast_to(jnp.asarray(rmat)[:, None, :], (LANES, N_HEADS, HEAD_PAD))
    wkt = jnp.transpose(jnp.concatenate([top, bot], axis=0), (1, 2, 0)).astype(BF16)
    uv = w_uv[l].reshape(KV_LORA, N_HEADS // 2, 2, V_HEAD)
    z = jnp.zeros_like(uv[:, :, 0])
    even = jnp.concatenate([uv[:, :, 0], z], axis=-1)
    odd = jnp.concatenate([z, uv[:, :, 1]], axis=-1)
    wv = jnp.stack([even, odd], axis=2).reshape(KV_LORA, N_HEADS * HEAD_PAD).astype(BF16)
    return w_ip, wq, wkt, wv


def kernel(x, c, ctx, c_ctx, w_mod, b_mod, w_in, g_q, w_uq, g_kv, w_uk, w_uv, w_ao, w_fo, w_out,
           ln1_g, ln1_b, ln2_g, ln2_b, w_ff_gate, w_ff_up, w_ff_down,
           w_router, b_router, w_e_gate, w_e_up, w_e_down):
    tb = _tables()
    t_lat = BATCH * SEQ
    t_ctx = BATCH * CTX_LEN

    cs = jnp.concatenate([c, c_ctx[None, :], jnp.zeros((SUBLANES - BATCH - 1, D_MODEL), F32)], axis=0)
    mods = _modulation(cs, w_mod.astype(BF16), b_mod).reshape(DEPTH, SUBLANES, N_MOD, D_MODEL)
    mods = jnp.concatenate([mods, jnp.zeros((DEPTH, SUBLANES, SUBLANES - N_MOD, D_MODEL), F32)], axis=2)

    xl = x.reshape(t_lat, D_MODEL)
    xc = ctx.reshape(t_ctx, D_MODEL)

    for l in range(DEPTH):
        last = l == DEPTH - 1
        mod_x = mods[l, :BATCH]
        mod_c = jnp.broadcast_to(mods[l, BATCH:BATCH + 1], (BATCH, SUBLANES, D_MODEL))
        w_ip, wq, wkt, wv = _arrange_weights(l, w_in, w_uq, w_uk, w_uv)
        gq = g_q[l][None, :]
        gkv = g_kv[l][None, :]
        wao = w_ao[l].astype(BF16)
        wfo = w_fo[l].astype(BF16)
        wout = w_out[l].astype(BF16)
        g1, b1 = ln1_g[l][None, :], ln1_b[l][None, :]
        g2, b2 = ln2_g[l][None, :], ln2_b[l][None, :]

        qlat_c, ckr_c, wf_c, gates_c = _inproj(xc, mod_c, w_ip, gq, gkv, tb["tk_ctx"], tb["bd_ctx"], tm=CTX_LEN)
        q_c, kt_c, v_c = _qkv(qlat_c, ckr_c, tb["tq_ctx"], wq, wkt, wv, nb=BATCH, tm=CTX_LEN)

        qlat, ckr, wf, gates = _inproj(xl, mod_x, w_ip, gq, gkv, tb["tk_lat"], tb["bd_lat"], tm=ROW_TM)
        q, kt, v = _qkv(qlat, ckr, tb["tq_lat"], wq, wkt, wv, nb=BATCH, tm=QKV_TM)
        attn = _attention(q, kt_c, v_c, kt, v, tq=ATTN_TQ, tk=ATTN_TK, peel=ATTN_PEEL, unroll=ATTN_UNROLL)
        four = _fourier_latent(wf.reshape(BATCH, SEQ, 2 * FOURIER_DIM), tb["c128"], tb["s128"], tb["tc"], tb["ts"])

        if l % 2 == 0:
            i = l // 2
            x1, h2 = _mixer(attn.reshape(t_lat, -1), four.reshape(t_lat, -1), gates, xl, mod_x, g1, b1,
                            wao, wfo, wout, tm=ROW_TM)
            wg = w_ff_gate[i].astype(BF16)
            wu = w_ff_up[i].astype(BF16)
            wd = w_ff_down[i].astype(BF16)
            xl_new = _ffn(h2, x1, mod_x, g2, b2, wg, wu, wd, tm=ROW_TM)
        else:
            i = l // 2
            wr = jnp.concatenate([w_router[i], jnp.zeros((D_MODEL, LANES - N_EXPERTS), F32)], axis=1)
            br = jnp.concatenate([b_router[i], jnp.zeros((LANES - N_EXPERTS,), F32)])[None, :]
            x1, h2, rinfo = _mixer(attn.reshape(t_lat, -1), four.reshape(t_lat, -1), gates, xl, mod_x, g1, b1,
                                   wao, wfo, wout, tm=ROUTE_TM, router=(wr, br))
            wg = w_e_gate[i].astype(BF16)
            wu = w_e_up[i].astype(BF16)
            wd = w_e_down[i].astype(BF16)
            xl_new = _moe(h2, x1, rinfo, mod_x, g2, b2, wg, wu, wd)

        if not last:
            assert l % 2 == 0
            attn_c = _attention(q_c, kt_c, v_c, None, None, tq=CTX_LEN, tk=CTX_LEN)
            four_c = _fourier_ctx(wf_c.reshape(BATCH, CTX_LEN, 2 * FOURIER_DIM), tb["c256"], tb["s256"])
            x1c, h2c = _mixer(attn_c.reshape(t_ctx, -1), four_c.reshape(t_ctx, -1), gates_c, xc, mod_c,
                              g1, b1, wao, wfo, wout, tm=CTX_LEN)
            xc = _ffn(h2c, x1c, mod_c, g2, b2, wg, wu, wd, tm=CTX_LEN)
        xl = xl_new

    return xl.reshape(BATCH, SEQ, D_MODEL)
```

```python
import functools
import math

import numpy as np
import jax
import jax.numpy as jnp
from jax import lax
from jax.experimental import pallas as pl
from jax.experimental.pallas import tpu as pltpu

D_MODEL = 1024
BATCH = 2
SEQ = 16384
DEPTH = 2
GRID_W = 64
CTX_LEN = 256
N_HEADS = 16
QK_NOPE = 64
QK_ROPE = 32
ROPE_FREQS = QK_ROPE // 4
V_HEAD = 64
Q_LORA = 256
KV_LORA = 128
ROPE_THETA = 10000.0
SM_SCALE = (QK_NOPE + QK_ROPE) ** -0.5
F_GROUPS = 4
F_GROUP_DIM = 128
FOURIER_DIM = F_GROUPS * F_GROUP_DIM
OFF_KV = Q_LORA
OFF_KR = OFF_KV + KV_LORA
OFF_F = OFF_KR + QK_ROPE
OFF_G = OFF_F + FOURIER_DIM
D_FF = 2816
N_EXPERTS = 8
D_FF_EXPERT = 3584
ALPHA = (2 * DEPTH) ** 0.25
LN_EPS = 1e-6
RMS_EPS = 1e-6

LANES = 128
SUBLANES = 8
N_MOD = 6
HEAD_PAD = 128
FFT_R = 128
V_ONES_LANE = (V_HEAD, 0)
RI_E1, RI_E2, RI_W1, RI_W2 = N_EXPERTS, N_EXPERTS + 1, N_EXPERTS + 2, N_EXPERTS + 3
VMEM_LIMIT = 56 * 1024 * 1024
ROW_TM = 1024
ROUTE_TM = 512
QKV_TM = 512
ATTN_TQ = 1024
ATTN_TK = 1024
ATTN_PEEL = 8
ATTN_UNROLL = 4

BF16 = jnp.bfloat16
F32 = jnp.float32


def _cp(sem, vmem=VMEM_LIMIT):
    return pltpu.CompilerParams(dimension_semantics=sem, vmem_limit_bytes=vmem)


def _dot(a, b):
    return jnp.dot(a, b, preferred_element_type=F32)


def _ln(x):
    mu = jnp.mean(x, axis=-1, keepdims=True)
    xc = x - mu
    var = jnp.mean(xc * xc, axis=-1, keepdims=True)
    return xc * lax.rsqrt(var + LN_EPS)


def _rms(x, g):
    return x * lax.rsqrt(jnp.mean(x * x, axis=-1, keepdims=True) + RMS_EPS) * g


def _sigmoid(x):
    return 1.0 / (1.0 + jnp.exp(-x))


def _mod_kernel(cs_ref, w_ref, b_ref, o_ref):
    cs = cs_ref[...]
    a = (cs * _sigmoid(cs)).astype(BF16)
    o_ref[0] = _dot(a, w_ref[0]) + b_ref[0]


def _modulation(cs, w_mod, b_mod):
    n_chunk = 1024
    n_out = w_mod.shape[-1]
    return pl.pallas_call(
        _mod_kernel,
        grid=(DEPTH, n_out // n_chunk),
        in_specs=[
            pl.BlockSpec((SUBLANES, D_MODEL), lambda l, j: (0, 0)),
            pl.BlockSpec((1, D_MODEL, n_chunk), lambda l, j: (l, 0, j)),
            pl.BlockSpec((1, 1, n_chunk), lambda l, j: (l, 0, j)),
        ],
        out_specs=pl.BlockSpec((1, SUBLANES, n_chunk), lambda l, j: (l, 0, j)),
        out_shape=jax.ShapeDtypeStruct((DEPTH, SUBLANES, n_out), F32),
        compiler_params=_cp(("parallel", "parallel")),
        name="mod",
    )(cs, w_mod, b_mod.reshape(DEPTH, 1, n_out))


IP_Q = 0
IP_KV = Q_LORA
IP_KR = IP_KV + KV_LORA
IP_F = IP_KR + LANES
IP_G = IP_F + FOURIER_DIM
IP_COLS = IP_G + 2 * D_MODEL


def _inproj_kernel(x_ref, mod_ref, w_ref, gq_ref, gkv_ref, tk_ref, bd_ref,
                   qlat_ref, ckr_ref, wf_ref, gates_ref):
    x = x_ref[...]
    mod = mod_ref[0]
    h = (_ln(x) * (1.0 + mod[1:2]) + mod[0:1]).astype(BF16)
    p0 = _dot(h, w_ref[:, IP_Q:IP_F])
    qlat_ref[...] = _rms(p0[:, IP_Q:IP_KV], gq_ref[...]).astype(BF16)
    ckr_ref[:, 0:KV_LORA] = _rms(p0[:, IP_KV:IP_KR], gkv_ref[...]).astype(BF16)
    ckr_ref[:, KV_LORA:] = (p0[:, IP_KR:IP_F] * tk_ref[...]).astype(BF16)
    uf = _dot(h, w_ref[:, IP_F:IP_G]).astype(BF16)
    for g in range(F_GROUPS):
        r = _dot(uf[:, g * LANES:(g + 1) * LANES], bd_ref[...])
        wf_ref[:, g * LANES:(g + 1) * LANES] = r[:, :LANES].astype(BF16)
        wf_ref[:, FOURIER_DIM + g * LANES:FOURIER_DIM + (g + 1) * LANES] = r[:, LANES:].astype(BF16)
    gc = 512
    for c in range(2 * D_MODEL // gc):
        gl = _dot(h, w_ref[:, IP_G + c * gc:IP_G + (c + 1) * gc])
        gates_ref[:, c * gc:(c + 1) * gc] = _sigmoid(gl).astype(BF16)


def _inproj(x2d, mod, w, gq, gkv, tk, bd, tm):
    t = x2d.shape[0]
    nb = mod.shape[0]
    tpb = t // nb // tm
    return pl.pallas_call(
        _inproj_kernel,
        grid=(t // tm,),
        in_specs=[
            pl.BlockSpec((tm, D_MODEL), lambda i: (i, 0)),
            pl.BlockSpec((1, SUBLANES, D_MODEL), lambda i: (i // tpb, 0, 0)),
            pl.BlockSpec((D_MODEL, IP_COLS), lambda i: (0, 0)),
            pl.BlockSpec((1, Q_LORA), lambda i: (0, 0)),
            pl.BlockSpec((1, KV_LORA), lambda i: (0, 0)),
            pl.BlockSpec((tm, LANES), lambda i: (i % tpb, 0)),
            pl.BlockSpec((LANES, 2 * LANES), lambda i: (0, 0)),
        ],
        out_specs=[
            pl.BlockSpec((tm, Q_LORA), lambda i: (i, 0)),
            pl.BlockSpec((tm, 2 * LANES), lambda i: (i, 0)),
            pl.BlockSpec((tm, 2 * FOURIER_DIM), lambda i: (i, 0)),
            pl.BlockSpec((tm, 2 * D_MODEL), lambda i: (i, 0)),
        ],
        out_shape=[
            jax.ShapeDtypeStruct((t, Q_LORA), BF16),
            jax.ShapeDtypeStruct((t, 2 * LANES), BF16),
            jax.ShapeDtypeStruct((t, 2 * FOURIER_DIM), BF16),
            jax.ShapeDtypeStruct((t, 2 * D_MODEL), BF16),
        ],
        compiler_params=_cp(("parallel",)),
        name="inproj",
    )(x2d, mod, w, gq, gkv, tk, bd)


def _qkv_kernel(qlat_ref, ckr_ref, tq_ref, wq_ref, wkt_ref, wv_ref, q_ref, kt_ref, v_ref):
    ql = qlat_ref[...]
    ckr = ckr_ref[...]
    tq = tq_ref[...]
    ckv = ckr[:, :KV_LORA]
    lane = lax.broadcasted_iota(jnp.int32, (ckr.shape[0], HEAD_PAD), 1)
    q_all = _dot(ql, wq_ref[...])
    v_all = _dot(ckv, wv_ref[...])
    for h in range(N_HEADS):
        cols = slice(h * HEAD_PAD, (h + 1) * HEAD_PAD)
        q_ref[0, h] = (q_all[:, cols] * tq).astype(BF16)
        kt_ref[0, h] = lax.dot_general(
            wkt_ref[h], ckr, (((1,), (1,)), ((), ())), preferred_element_type=F32).astype(BF16)
        v_ref[0, h] = jnp.where(lane == V_ONES_LANE[h % 2], 1.0, v_all[:, cols]).astype(BF16)


def _qkv(qlat, ckr, tq, wq, wkt, wv, nb, tm):
    t = qlat.shape[0]
    n = t // nb
    tpb = n // tm
    return pl.pallas_call(
        _qkv_kernel,
        grid=(t // tm,),
        in_specs=[
            pl.BlockSpec((tm, Q_LORA), lambda i: (i, 0)),
            pl.BlockSpec((tm, 2 * LANES), lambda i: (i, 0)),
            pl.BlockSpec((tm, HEAD_PAD), lambda i: (i % tpb, 0)),
            pl.BlockSpec((Q_LORA, N_HEADS * HEAD_PAD), lambda i: (0, 0)),
            pl.BlockSpec((N_HEADS, HEAD_PAD, 2 * LANES), lambda i: (0, 0, 0)),
            pl.BlockSpec((KV_LORA, N_HEADS * HEAD_PAD), lambda i: (0, 0)),
        ],
        out_specs=[
            pl.BlockSpec((1, N_HEADS, tm, HEAD_PAD), lambda i: (i // tpb, 0, i % tpb, 0)),
            pl.BlockSpec((1, N_HEADS, HEAD_PAD, tm), lambda i: (i // tpb, 0, 0, i % tpb)),
            pl.BlockSpec((1, N_HEADS, tm, HEAD_PAD), lambda i: (i // tpb, 0, i % tpb, 0)),
        ],
        out_shape=[
            jax.ShapeDtypeStruct((nb, N_HEADS, n, HEAD_PAD), BF16),
            jax.ShapeDtypeStruct((nb, N_HEADS, HEAD_PAD, n), BF16),
            jax.ShapeDtypeStruct((nb, N_HEADS, n, HEAD_PAD), BF16),
        ],
        compiler_params=_cp(("parallel",)),
        name="qkv",
    )(qlat, ckr, tq, wq, wkt, wv)


def _attn_kernel(q_ref, ktc_ref, vc_ref, *rest, n_chunks, tk, peel, unroll):
    if n_chunks:
        kt_ref, v_ref, o_ref = rest
    else:
        (o_ref,) = rest

    def step(q, kt, v, m, acc):
        s = _dot(q, kt)
        m_new = jnp.max(s, axis=1, keepdims=True)
        if m is not None:
            m_new = jnp.maximum(m, m_new)
        p = jnp.exp2((s - m_new).astype(BF16))
        pv = _dot(p, v)
        if m is not None:
            pv = jnp.exp2(m - m_new) * acc + pv
        return m_new, pv

    qs = [q_ref[0, hh] for hh in range(2)]
    carry = []
    for hh in range(2):
        carry += step(qs[hh], ktc_ref[0, hh], vc_ref[0, hh], None, None)

    if n_chunks:
        span = unroll * tk

        def group(base, count, carry):
            for u in range(count):
                off = base + u * tk
                if not isinstance(off, int):
                    off = pl.multiple_of(off, tk)
                out = []
                for hh in range(2):
                    out += step(qs[hh], kt_ref[0, hh, :, pl.ds(off, tk)], v_ref[0, hh, pl.ds(off, tk), :],
                                carry[2 * hh], carry[2 * hh + 1])
                carry = out
            return tuple(carry)

        carry = group(0, peel, carry)
        carry = lax.fori_loop(0, (n_chunks - peel) // unroll,
                              lambda g, c: group(pl.multiple_of(peel * tk + g * span, tk), unroll, c), carry)

    lane = lax.broadcasted_iota(jnp.int32, carry[1].shape, 1)
    outs = []
    for hh in range(2):
        acc = carry[2 * hh + 1]
        l = jnp.sum(jnp.where(lane == V_ONES_LANE[hh], acc, 0.0), axis=1, keepdims=True)
        outs.append(acc * (1.0 / l))
    o_ref[0] = jnp.where(lane < V_HEAD, outs[0], outs[1]).astype(BF16)


def _attention(q, kt_c, v_c, kt, v, tq, tk, peel=0, unroll=1):
    nb, _, n, _ = q.shape
    nc = kt_c.shape[-1]
    n_chunks = 0 if kt is None else kt.shape[-1] // tk
    in_specs = [
        pl.BlockSpec((1, 2, tq, HEAD_PAD), lambda b, p, i: (b, p, i, 0)),
        pl.BlockSpec((1, 2, HEAD_PAD, nc), lambda b, p, i: (b, p, 0, 0)),
        pl.BlockSpec((1, 2, nc, HEAD_PAD), lambda b, p, i: (b, p, 0, 0)),
    ]
    args = [q, kt_c, v_c]
    if n_chunks:
        nk = kt.shape[-1]
        in_specs += [
            pl.BlockSpec((1, 2, HEAD_PAD, nk), lambda b, p, i: (b, p, 0, 0)),
            pl.BlockSpec((1, 2, nk, HEAD_PAD), lambda b, p, i: (b, p, 0, 0)),
        ]
        args += [kt, v]
    return pl.pallas_call(
        functools.partial(_attn_kernel, n_chunks=n_chunks, tk=tk, peel=peel, unroll=unroll),
        grid=(nb, N_HEADS // 2, n // tq),
        in_specs=in_specs,
        out_specs=pl.BlockSpec((1, tq, 2 * V_HEAD), lambda b, p, i: (b, i, p)),
        out_shape=jax.ShapeDtypeStruct((nb, n, N_HEADS * V_HEAD), BF16),
        compiler_params=_cp(("parallel", "parallel", "arbitrary")),
        name="attn",
    )(*args)


def _fft1_kernel(x_ref, c_ref, s_ref, tc_ref, ts_ref, o_ref, *, n2t):
    x = x_ref[0]
    cx = _dot(c_ref[...], x)
    sx = _dot(s_ref[...], x)
    w = 2 * FOURIER_DIM
    for t in range(n2t):
        re = slice(t * w, t * w + FOURIER_DIM)
        im = slice(t * w + FOURIER_DIM, (t + 1) * w)
        yr = cx[:, re] + sx[:, im]
        yi = cx[:, im] - sx[:, re]
        tc = jnp.concatenate([tc_ref[:, t * LANES:(t + 1) * LANES]] * F_GROUPS, axis=1)
        ts = jnp.concatenate([ts_ref[:, t * LANES:(t + 1) * LANES]] * F_GROUPS, axis=1)
        o_ref[0, :, re] = (yr * tc + yi * ts).astype(BF16)
        o_ref[0, :, im] = (yi * tc - yr * ts).astype(BF16)


def _fft2_kernel(y_ref, c_ref, s_ref, o_ref, *, k1t):
    for t in range(k1t):
        y = y_ref[0, t]
        zr = _dot(c_ref[...], y[:, :FOURIER_DIM]) + _dot(s_ref[...], y[:, FOURIER_DIM:])
        o_ref[0, :, t * FOURIER_DIM:(t + 1) * FOURIER_DIM] = zr.astype(BF16)


def _fourier_latent(wf, cmat, smat, tc, ts):
    nb = wf.shape[0]
    w = 2 * FOURIER_DIM
    n2t = 8
    y = pl.pallas_call(
        functools.partial(_fft1_kernel, n2t=n2t),
        grid=(nb, FFT_R // n2t),
        in_specs=[
            pl.BlockSpec((1, FFT_R, n2t * w), lambda b, j: (b, 0, j)),
            pl.BlockSpec((FFT_R, FFT_R), lambda b, j: (0, 0)),
            pl.BlockSpec((FFT_R, FFT_R), lambda b, j: (0, 0)),
            pl.BlockSpec((FFT_R, n2t * LANES), lambda b, j: (0, j)),
            pl.BlockSpec((FFT_R, n2t * LANES), lambda b, j: (0, j)),
        ],
        out_specs=pl.BlockSpec((1, FFT_R, n2t * w), lambda b, j: (b, 0, j)),
        out_shape=jax.ShapeDtypeStruct((nb, FFT_R, FFT_R * w), BF16),
        compiler_params=_cp(("parallel", "parallel")),
        name="fft1",
    )(wf.reshape(nb, FFT_R, FFT_R * w), cmat, smat, tc, ts)
    k1t = 8
    four = pl.pallas_call(
        functools.partial(_fft2_kernel, k1t=k1t),
        grid=(nb, FFT_R // k1t),
        in_specs=[
            pl.BlockSpec((1, k1t, FFT_R, w), lambda b, j: (b, j, 0, 0)),
            pl.BlockSpec((FFT_R, FFT_R), lambda b, j: (0, 0)),
            pl.BlockSpec((FFT_R, FFT_R), lambda b, j: (0, 0)),
        ],
        out_specs=pl.BlockSpec((1, FFT_R, k1t * FOURIER_DIM), lambda b, j: (b, 0, j)),
        out_shape=jax.ShapeDtypeStruct((nb, FFT_R, FFT_R * FOURIER_DIM), BF16),
        compiler_params=_cp(("parallel", "parallel")),
        name="fft2",
    )(y.reshape(nb, FFT_R, FFT_R, w), cmat, smat)
    return four.reshape(nb, SEQ, FOURIER_DIM)


def _dft_ctx_kernel(x_ref, c_ref, s_ref, o_ref):
    x = x_ref[0]
    zr = _dot(c_ref[...], x[:, :FOURIER_DIM]) + _dot(s_ref[...], x[:, FOURIER_DIM:])
    o_ref[0] = zr.astype(BF16)


def _fourier_ctx(wf, cmat, smat):
    nb, n, w = wf.shape
    return pl.pallas_call(
        _dft_ctx_kernel,
        grid=(nb,),
        in_specs=[
            pl.BlockSpec((1, n, w), lambda b: (b, 0, 0)),
            pl.BlockSpec((n, n), lambda b: (0, 0)),
            pl.BlockSpec((n, n), lambda b: (0, 0)),
        ],
        out_specs=pl.BlockSpec((1, n, FOURIER_DIM), lambda b: (b, 0, 0)),
        out_shape=jax.ShapeDtypeStruct((nb, n, FOURIER_DIM), BF16),
        compiler_params=_cp(("parallel",)),
        name="dft_ctx",
    )(wf, cmat, smat)


def _mixer_kernel(attn_ref, four_ref, gates_ref, x_ref, mod_ref, g_ref, b_ref,
                  wao_ref, wfo_ref, wout_ref, *rest, route):
    if route:
        wr_ref, br_ref, x1_ref, h2_ref, rinfo_ref = rest
    else:
        x1_ref, h2_ref = rest
    mod = mod_ref[0]
    a = _dot(attn_ref[...], wao_ref[...])
    f = _dot(four_ref[...], wfo_ref[...])
    merged = gates_ref[:, :D_MODEL].astype(F32) * a + gates_ref[:, D_MODEL:].astype(F32) * f
    y = _dot(merged.astype(BF16), wout_ref[...])
    x1 = _ln(ALPHA * x_ref[...] + mod[2:3] * y) * g_ref[...] + b_ref[...]
    x1_ref[...] = x1
    h2 = _ln(x1) * (1.0 + mod[4:5]) + mod[3:4]
    h2_ref[...] = h2.astype(h2_ref.dtype)
    if route:
        logits = jnp.dot(h2, wr_ref[...], preferred_element_type=F32,
                         precision=lax.Precision.HIGHEST) + br_ref[...]
        lane = lax.broadcasted_iota(jnp.int32, logits.shape, 1)
        neg = jnp.float32(-jnp.inf)
        lg = jnp.where(lane < N_EXPERTS, logits, neg)
        m1 = jnp.max(lg, axis=1, keepdims=True)
        i1 = jnp.min(jnp.where(lg == m1, lane, LANES), axis=1, keepdims=True)
        lg2 = jnp.where(lane == i1, neg, lg)
        m2 = jnp.max(lg2, axis=1, keepdims=True)
        i2 = jnp.min(jnp.where(lg2 == m2, lane, LANES), axis=1, keepdims=True)
        e2 = jnp.exp(m2 - m1)
        w1 = 1.0 / (1.0 + e2)
        w2 = e2 * w1
        rinfo = jnp.where((lane == i1) | (lane == i2), 1.0, 0.0)
        rinfo = jnp.where(lane == RI_E1, i1.astype(F32), rinfo)
        rinfo = jnp.where(lane == RI_E2, i2.astype(F32), rinfo)
        rinfo = jnp.where(lane == RI_W1, w1, rinfo)
        rinfo_ref[...] = jnp.where(lane == RI_W2, w2, rinfo)


def _mixer(attn, four, gates, x2d, mod, g, b, wao, wfo, wout, tm, router=None):
    t = x2d.shape[0]
    nb = mod.shape[0]
    tpb = t // nb // tm
    row = lambda i: (i, 0)
    const = lambda i: (0, 0)
    in_specs = [
        pl.BlockSpec((tm, D_MODEL), row),
        pl.BlockSpec((tm, FOURIER_DIM), row),
        pl.BlockSpec((tm, 2 * D_MODEL), row),
        pl.BlockSpec((tm, D_MODEL), row),
        pl.BlockSpec((1, SUBLANES, D_MODEL), lambda i: (i // tpb, 0, 0)),
        pl.BlockSpec((1, D_MODEL), const),
        pl.BlockSpec((1, D_MODEL), const),
        pl.BlockSpec((D_MODEL, D_MODEL), const),
        pl.BlockSpec((FOURIER_DIM, D_MODEL), const),
        pl.BlockSpec((D_MODEL, D_MODEL), const),
    ]
    args = [attn, four, gates, x2d, mod, g, b, wao, wfo, wout]
    out_specs = [pl.BlockSpec((tm, D_MODEL), row), pl.BlockSpec((tm, D_MODEL), row)]
    h2_dtype = BF16 if router is None else F32
    out_shape = [jax.ShapeDtypeStruct((t, D_MODEL), F32), jax.ShapeDtypeStruct((t, D_MODEL), h2_dtype)]
    if router is not None:
        in_specs += [pl.BlockSpec((D_MODEL, LANES), const), pl.BlockSpec((1, LANES), const)]
        args += list(router)
        out_specs.append(pl.BlockSpec((tm, LANES), row))
        out_shape.append(jax.ShapeDtypeStruct((t, LANES), F32))
    return pl.pallas_call(
        functools.partial(_mixer_kernel, route=router is not None),
        grid=(t // tm,),
        in_specs=in_specs,
        out_specs=out_specs,
        out_shape=out_shape,
        compiler_params=_cp(("parallel",)),
        name="mixer",
    )(*args)


def _ffn_kernel(h_ref, x1_ref, mod_ref, g_ref, b_ref, wg_ref, wu_ref, wd_ref, o_ref):
    h = h_ref[...]
    gate = _dot(h, wg_ref[...])
    up = _dot(h, wu_ref[...])
    act = (gate * _sigmoid(gate) * up).astype(BF16)
    f = _dot(act, wd_ref[...])
    mod = mod_ref[0]
    o_ref[...] = _ln(ALPHA * x1_ref[...] + mod[5:6] * f) * g_ref[...] + b_ref[...]


def _ffn(h2, x1, mod, g, b, wg, wu, wd, tm):
    t = h2.shape[0]
    nb = mod.shape[0]
    tpb = t // nb // tm
    dff = wg.shape[1]
    row = lambda i: (i, 0)
    const = lambda i: (0, 0)
    resident = pl.Buffered(1)
    return pl.pallas_call(
        _ffn_kernel,
        grid=(t // tm,),
        in_specs=[
            pl.BlockSpec((tm, D_MODEL), row),
            pl.BlockSpec((tm, D_MODEL), row),
            pl.BlockSpec((1, SUBLANES, D_MODEL), lambda i: (i // tpb, 0, 0)),
            pl.BlockSpec((1, D_MODEL), const),
            pl.BlockSpec((1, D_MODEL), const),
            pl.BlockSpec((D_MODEL, dff), const, pipeline_mode=resident),
            pl.BlockSpec((D_MODEL, dff), const, pipeline_mode=resident),
            pl.BlockSpec((dff, D_MODEL), const, pipeline_mode=resident),
        ],
        out_specs=pl.BlockSpec((tm, D_MODEL), row),
        out_shape=jax.ShapeDtypeStruct((t, D_MODEL), F32),
        compiler_params=_cp(("parallel",)),
        name="ffn",
    )(h2, x1, mod, g, b, wg, wu, wd)


MOE_TM = 512
MOE_TT = 512
DMA_ISSUE_UNROLL = 8


def _moe_rows(n_tokens):
    return 2 * n_tokens + N_EXPERTS * MOE_TM


def _scan_kernel(r_ref, o_ref, cnt_ref, run_ref):
    @pl.when(pl.program_id(0) == 0)
    def _():
        run_ref[...] = jnp.zeros_like(run_ref)

    r = r_ref[...]
    ts = r.shape[0]
    lane = lax.broadcasted_iota(jnp.int32, r.shape, 1)
    sel = jnp.where(lane < N_EXPERTS, r, 0.0)
    rows = lax.broadcasted_iota(jnp.int32, (ts, ts), 0)
    cols = lax.broadcasted_iota(jnp.int32, (ts, ts), 1)
    ltri = jnp.where(rows > cols, 1.0, 0.0).astype(BF16)
    prefix = _dot(ltri, sel.astype(BF16)) + run_ref[0:1, :]
    lane_f = lane.astype(F32)
    pick = lambda k: jnp.sum(jnp.where(lane == k, r, 0.0), axis=1, keepdims=True)
    e1, e2 = pick(RI_E1), pick(RI_E2)
    p1 = jnp.sum(jnp.where(lane_f == e1, prefix, 0.0), axis=1, keepdims=True)
    p2 = jnp.sum(jnp.where(lane_f == e2, prefix, 0.0), axis=1, keepdims=True)
    out = jnp.where(lane == 0, e1, jnp.where(lane == 1, e2, jnp.where(lane == 2, p1, jnp.where(lane == 3, p2, 0.0))))
    o_ref[...] = out.astype(jnp.int32)
    run_ref[...] = run_ref[...] + jnp.sum(sel, axis=0, keepdims=True)
    cnt_ref[...] = run_ref[...]


def _route_scan(rinfo, ts=512):
    t = rinfo.shape[0]
    return pl.pallas_call(
        _scan_kernel,
        grid=(t // ts,),
        in_specs=[pl.BlockSpec((ts, LANES), lambda i: (i, 0))],
        out_specs=[pl.BlockSpec((ts, LANES), lambda i: (i, 0)), pl.BlockSpec((SUBLANES, LANES), lambda i: (0, 0))],
        out_shape=[jax.ShapeDtypeStruct((t, LANES), jnp.int32), jax.ShapeDtypeStruct((SUBLANES, LANES), F32)],
        scratch_shapes=[pltpu.VMEM((SUBLANES, LANES), F32)],
        compiler_params=_cp(("arbitrary",)),
        name="route_scan",
    )(rinfo)


def _row_copy(src_ref, src_row, dst_ref, dst_row, sem):
    return pltpu.make_async_copy(src_ref.at[pl.ds(src_row, 1)], dst_ref.at[pl.ds(dst_row, 1)], sem)


def _dispatch_kernel(dest_ref, h_ref, xs_ref, sem):
    def issue(t, c):
        for k in range(2):
            _row_copy(h_ref, t, xs_ref, dest_ref[0, k, t], sem).start(priority=k)
        return c

    lax.fori_loop(0, MOE_TT, issue, 0, unroll=DMA_ISSUE_UNROLL)
    for k in range(2):
        pltpu.make_async_copy(h_ref, xs_ref.at[pl.ds(0, MOE_TT)], sem).wait()


def _dispatch(dest, h2):
    t = h2.shape[0]
    return pl.pallas_call(
        _dispatch_kernel,
        grid=(t // MOE_TT,),
        in_specs=[
            pl.BlockSpec((1, 2, MOE_TT), lambda i: (i, 0, 0), memory_space=pltpu.SMEM),
            pl.BlockSpec((MOE_TT, D_MODEL), lambda i: (i, 0)),
        ],
        out_specs=pl.BlockSpec(memory_space=pl.ANY),
        out_shape=jax.ShapeDtypeStruct((_moe_rows(t), D_MODEL), F32),
        scratch_shapes=[pltpu.SemaphoreType.DMA(())],
        compiler_params=_cp(("arbitrary",)),
        name="moe_dispatch",
    )(dest, h2)


def _experts_kernel(te_ref, tv_ref, nt_ref, xs_ref, wg_ref, wu_ref, wd_ref, y_ref, acc_ref):
    j = pl.program_id(1)

    @pl.when(pl.program_id(0) < nt_ref[0])
    def _():
        rows = lax.broadcasted_iota(jnp.int32, xs_ref.shape, 0)
        h = jnp.where(rows < tv_ref[pl.program_id(0)], xs_ref[...], 0.0).astype(BF16)
        gate = _dot(h, wg_ref[0])
        up = _dot(h, wu_ref[0])
        act = (gate * _sigmoid(gate) * up).astype(BF16)
        part = _dot(act, wd_ref[0])

        @pl.when(j == 0)
        def _():
            acc_ref[...] = part

        @pl.when(j > 0)
        def _():
            acc_ref[...] += part

        @pl.when(j == pl.num_programs(1) - 1)
        def _():
            y_ref[...] = acc_ref[...]

    @pl.when((pl.program_id(0) >= nt_ref[0]) & (j == pl.num_programs(1) - 1))
    def _():
        y_ref[...] = jnp.zeros_like(y_ref)


def _experts(tile_expert, tile_valid, n_tiles_used, xs, wg, wu, wd, fc):
    ne, _, dff = wg.shape
    grid_spec = pltpu.PrefetchScalarGridSpec(
        num_scalar_prefetch=3,
        grid=(xs.shape[0] // MOE_TM, dff // fc),
        in_specs=[
            pl.BlockSpec((MOE_TM, D_MODEL), lambda i, j, te, tv, nt: (i, 0)),
            pl.BlockSpec((1, D_MODEL, fc), lambda i, j, te, tv, nt: (te[i], 0, j)),
            pl.BlockSpec((1, D_MODEL, fc), lambda i, j, te, tv, nt: (te[i], 0, j)),
            pl.BlockSpec((1, fc, D_MODEL), lambda i, j, te, tv, nt: (te[i], j, 0)),
        ],
        out_specs=pl.BlockSpec((MOE_TM, D_MODEL), lambda i, j, te, tv, nt: (i, 0)),
        scratch_shapes=[pltpu.VMEM((MOE_TM, D_MODEL), F32)],
    )
    return pl.pallas_call(
        _experts_kernel,
        grid_spec=grid_spec,
        out_shape=jax.ShapeDtypeStruct(xs.shape, F32),
        compiler_params=_cp(("arbitrary", "arbitrary")),
        name="moe_experts",
    )(tile_expert, tile_valid, n_tiles_used, xs, wg, wu, wd)


def _combine_kernel(dest_ref, dest_next_ref, y_ref, x1_ref, rinfo_ref, mod_ref, g_ref, b_ref, o_ref,
                    ybuf_ref, sems):
    i = pl.program_id(0)
    slot = i % 2

    def gather(d_ref, s):
        def issue(t, c):
            for k in range(2):
                _row_copy(y_ref, d_ref[0, k, t], ybuf_ref.at[s, k], t, sems.at[s]).start(priority=k)
            return c
        lax.fori_loop(0, MOE_TT, issue, 0, unroll=DMA_ISSUE_UNROLL)

    @pl.when(i == 0)
    def _():
        gather(dest_ref, slot)

    @pl.when(i + 1 < pl.num_programs(0))
    def _():
        gather(dest_next_ref, 1 - slot)

    for k in range(2):
        pltpu.make_async_copy(y_ref.at[pl.ds(0, MOE_TT)], ybuf_ref.at[slot, k], sems.at[slot]).wait()

    r = rinfo_ref[...]
    lane = lax.broadcasted_iota(jnp.int32, r.shape, 1)
    w1 = jnp.sum(jnp.where(lane == RI_W1, r, 0.0), axis=1, keepdims=True)
    w2 = jnp.sum(jnp.where(lane == RI_W2, r, 0.0), axis=1, keepdims=True)
    f = w1 * ybuf_ref[slot, 0] + w2 * ybuf_ref[slot, 1]
    mod = mod_ref[0]
    o_ref[...] = _ln(ALPHA * x1_ref[...] + mod[5:6] * f) * g_ref[...] + b_ref[...]


def _combine(dest, y, x1, rinfo, mod, g, b):
    t = x1.shape[0]
    nb = mod.shape[0]
    tpb = t // nb // MOE_TT
    row = lambda i: (i, 0)
    const = lambda i: (0, 0)
    n_steps = t // MOE_TT
    return pl.pallas_call(
        _combine_kernel,
        grid=(n_steps,),
        in_specs=[
            pl.BlockSpec((1, 2, MOE_TT), lambda i: (i, 0, 0), memory_space=pltpu.SMEM),
            pl.BlockSpec((1, 2, MOE_TT), lambda i: (jnp.minimum(i + 1, n_steps - 1), 0, 0),
                         memory_space=pltpu.SMEM),
            pl.BlockSpec(memory_space=pl.ANY),
            pl.BlockSpec((MOE_TT, D_MODEL), row),
            pl.BlockSpec((MOE_TT, LANES), row),
            pl.BlockSpec((1, SUBLANES, D_MODEL), lambda i: (i // tpb, 0, 0)),
            pl.BlockSpec((1, D_MODEL), const),
            pl.BlockSpec((1, D_MODEL), const),
        ],
        out_specs=pl.BlockSpec((MOE_TT, D_MODEL), row),
        out_shape=jax.ShapeDtypeStruct((t, D_MODEL), F32),
        scratch_shapes=[pltpu.VMEM((2, 2, MOE_TT, D_MODEL), F32), pltpu.SemaphoreType.DMA((2,))],
        compiler_params=_cp(("arbitrary",)),
        name="moe_combine",
    )(dest, dest, y, x1, rinfo, mod, g, b)


def _moe(h2, x1, rinfo, mod, g, b, wg, wu, wd):
    t = h2.shape[0]
    route, counts = _route_scan(rinfo)
    cnt = counts[0, :N_EXPERTS].astype(jnp.int32)
    padded = (cnt + MOE_TM - 1) // MOE_TM * MOE_TM
    ends = jnp.cumsum(padded)
    starts = ends - padded
    dest = jnp.stack([starts[route[:, 0]] + route[:, 2], starts[route[:, 1]] + route[:, 3]])
    dest = dest.reshape(2, t // MOE_TT, MOE_TT).transpose(1, 0, 2)
    tile_row0 = jnp.arange(_moe_rows(t) // MOE_TM, dtype=jnp.int32) * MOE_TM
    tile_expert = jnp.minimum(jnp.sum(tile_row0[:, None] >= ends[None, :], axis=1), N_EXPERTS - 1).astype(jnp.int32)
    tile_valid = jnp.clip(starts[tile_expert] + cnt[tile_expert] - tile_row0, 0, MOE_TM).astype(jnp.int32)
    n_tiles_used = (ends[-1:] // MOE_TM).astype(jnp.int32)
    xs = _dispatch(dest, h2)
    y = _experts(tile_expert, tile_valid, n_tiles_used, xs, wg, wu, wd, fc=D_FF_EXPERT // 2)
    return _combine(dest, y, x1, rinfo, mod, g, b)


def _rope_np(n_tokens):
    rows = np.repeat(np.arange(n_tokens // GRID_W), GRID_W)
    cols = np.tile(np.arange(GRID_W), n_tokens // GRID_W)
    pos = np.stack([rows, cols], axis=-1).astype(np.float64)
    inv_freq = 1.0 / (ROPE_THETA ** (np.arange(ROPE_FREQS, dtype=np.float64) / ROPE_FREQS))
    ang = pos[..., None] * inv_freq
    cos, sin = np.cos(ang), np.sin(ang)
    cos32 = np.broadcast_to(cos[:, :, None, :], (n_tokens, 2, 2, ROPE_FREQS)).reshape(n_tokens, QK_ROPE)
    sgn = np.array([-1.0, 1.0])[None, None, :, None]
    sin32 = (np.broadcast_to(sin[:, :, None, :], (n_tokens, 2, 2, ROPE_FREQS)) * sgn).reshape(n_tokens, QK_ROPE)
    return cos32, sin32


def _half_swap_perm():
    p = np.arange(QK_ROPE).reshape(2, 2, ROPE_FREQS)
    return p[:, ::-1, :].reshape(QK_ROPE)


def _tables():
    c0 = SM_SCALE * math.log2(math.e)
    cos32, sin32 = _rope_np(SEQ)
    one64 = np.ones((SEQ, QK_NOPE))
    tq_lat = c0 * np.concatenate([one64, cos32, sin32], axis=1)
    tk_lat = np.concatenate([cos32, sin32, np.zeros((SEQ, LANES - 2 * QK_ROPE))], axis=1)
    tq_ctx = c0 * np.concatenate([np.ones((CTX_LEN, QK_NOPE + QK_ROPE)), np.zeros((CTX_LEN, QK_ROPE))], axis=1)
    tk_ctx = np.concatenate([np.ones((CTX_LEN, QK_ROPE)), np.zeros((CTX_LEN, LANES - QK_ROPE))], axis=1)
    f32 = lambda a: jnp.asarray(a, F32)
    bf = lambda a: jnp.asarray(a, BF16)

    def dft(n):
        k = np.arange(n)
        ang = 2.0 * np.pi * ((k[:, None] * k[None, :]) % n) / n
        return np.cos(ang), np.sin(ang)

    c128, s128 = dft(FFT_R)
    c256, s256 = dft(CTX_LEN)
    bd_lat = np.concatenate([c128, -s128], axis=1) / math.sqrt(SEQ * F_GROUP_DIM)
    bd_ctx = np.concatenate([c128, -s128], axis=1) / math.sqrt(CTX_LEN * F_GROUP_DIM)
    k1 = np.arange(FFT_R)
    tw = 2.0 * np.pi * ((k1[:, None] * k1[None, :]) % SEQ) / SEQ
    tc = jnp.broadcast_to(f32(np.cos(tw))[:, :, None], (FFT_R, FFT_R, LANES)).reshape(FFT_R, FFT_R * LANES)
    ts = jnp.broadcast_to(f32(np.sin(tw))[:, :, None], (FFT_R, FFT_R, LANES)).reshape(FFT_R, FFT_R * LANES)
    return dict(tq_lat=f32(tq_lat), tk_lat=f32(tk_lat), tq_ctx=f32(tq_ctx), tk_ctx=f32(tk_ctx),
                c128=bf(c128), s128=bf(s128), c256=bf(c256), s256=bf(s256),
                bd_lat=bf(bd_lat), bd_ctx=bf(bd_ctx), tc=tc, ts=ts)


def _arrange_weights(l, w_in, w_uq, w_uk, w_uv):
    sw = _half_swap_perm()
    wi = w_in[l]
    kr = wi[:, OFF_KR:OFF_F]
    w_ip = jnp.concatenate([
        wi[:, :OFF_KR], kr, kr[:, sw], jnp.zeros((D_MODEL, LANES - 2 * QK_ROPE), F32),
        wi[:, OFF_F:]], axis=1).astype(BF16)
    uq = w_uq[l].reshape(Q_LORA, N_HEADS, QK_NOPE + QK_ROPE)
    qr = uq[:, :, QK_NOPE:]
    wq = jnp.concatenate([uq, qr[:, :, sw]], axis=-1).reshape(Q_LORA, N_HEADS * HEAD_PAD).astype(BF16)
    uk = w_uk[l].reshape(KV_LORA, N_HEADS, QK_NOPE)
    top = jnp.concatenate([uk, jnp.zeros((KV_LORA, N_HEADS, HEAD_PAD - QK_NOPE), F32)], axis=-1)
    rmat = np.zeros((LANES, HEAD_PAD), np.float32)
    for i in range(QK_ROPE):
        for r in (i, QK_ROPE + i):
            rmat[r, QK_NOPE + i] = 1.0
            rmat[r, QK_NOPE + QK_ROPE + i] = 1.0
    bot = jnp.broadcast_to(jnp.asarray(rmat)[:, None, :], (LANES, N_HEADS, HEAD_PAD))
    wkt = jnp.transpose(jnp.concatenate([top, bot], axis=0), (1, 2, 0)).astype(BF16)
    uv = w_uv[l].reshape(KV_LORA, N_HEADS // 2, 2, V_HEAD)
    z = jnp.zeros_like(uv[:, :, 0])
    even = jnp.concatenate([uv[:, :, 0], z], axis=-1)
    odd = jnp.concatenate([z, uv[:, :, 1]], axis=-1)
    wv = jnp.stack([even, odd], axis=2).reshape(KV_LORA, N_HEADS * HEAD_PAD).astype(BF16)
    return w_ip, wq, wkt, wv


def kernel(x, c, ctx, c_ctx, w_mod, b_mod, w_in, g_q, w_uq, g_kv, w_uk, w_uv, w_ao, w_fo, w_out,
           ln1_g, ln1_b, ln2_g, ln2_b, w_ff_gate, w_ff_up, w_ff_down,
           w_router, b_router, w_e_gate, w_e_up, w_e_down):
    tb = _tables()
    t_lat = BATCH * SEQ
    t_ctx = BATCH * CTX_LEN

    cs = jnp.concatenate([c, c_ctx[None, :], jnp.zeros((SUBLANES - BATCH - 1, D_MODEL), F32)], axis=0)
    mods = _modulation(cs, w_mod.astype(BF16), b_mod).reshape(DEPTH, SUBLANES, N_MOD, D_MODEL)
    mods = jnp.concatenate([mods, jnp.zeros((DEPTH, SUBLANES, SUBLANES - N_MOD, D_MODEL), F32)], axis=2)

    xl = x.reshape(t_lat, D_MODEL)
    xc = ctx.reshape(t_ctx, D_MODEL)

    for l in range(DEPTH):
        last = l == DEPTH - 1
        mod_x = mods[l, :BATCH]
        mod_c = jnp.broadcast_to(mods[l, BATCH:BATCH + 1], (BATCH, SUBLANES, D_MODEL))
        w_ip, wq, wkt, wv = _arrange_weights(l, w_in, w_uq, w_uk, w_uv)
        gq = g_q[l][None, :]
        gkv = g_kv[l][None, :]
        wao = w_ao[l].astype(BF16)
        wfo = w_fo[l].astype(BF16)
        wout = w_out[l].astype(BF16)
        g1, b1 = ln1_g[l][None, :], ln1_b[l][None, :]
        g2, b2 = ln2_g[l][None, :], ln2_b[l][None, :]

        qlat_c, ckr_c, wf_c, gates_c = _inproj(xc, mod_c, w_ip, gq, gkv, tb["tk_ctx"], tb["bd_ctx"], tm=CTX_LEN)
        q_c, kt_c, v_c = _qkv(qlat_c, ckr_c, tb["tq_ctx"], wq, wkt, wv, nb=BATCH, tm=CTX_LEN)

        qlat, ckr, wf, gates = _inproj(xl, mod_x, w_ip, gq, gkv, tb["tk_lat"], tb["bd_lat"], tm=ROW_TM)
        q, kt, v = _qkv(qlat, ckr, tb["tq_lat"], wq, wkt, wv, nb=BATCH, tm=QKV_TM)
        attn = _attention(q, kt_c, v_c, kt, v, tq=ATTN_TQ, tk=ATTN_TK, peel=ATTN_PEEL, unroll=ATTN_UNROLL)
        four = _fourier_latent(wf.reshape(BATCH, SEQ, 2 * FOURIER_DIM), tb["c128"], tb["s128"], tb["tc"], tb["ts"])

        if l % 2 == 0:
            i = l // 2
            x1, h2 = _mixer(attn.reshape(t_lat, -1), four.reshape(t_lat, -1), gates, xl, mod_x, g1, b1,
                            wao, wfo, wout, tm=ROW_TM)
            wg = w_ff_gate[i].astype(BF16)
            wu = w_ff_up[i].astype(BF16)
            wd = w_ff_down[i].astype(BF16)
            xl_new = _ffn(h2, x1, mod_x, g2, b2, wg, wu, wd, tm=ROW_TM)
        else:
            i = l // 2
            wr = jnp.concatenate([w_router[i], jnp.zeros((D_MODEL, LANES - N_EXPERTS), F32)], axis=1)
            br = jnp.concatenate([b_router[i], jnp.zeros((LANES - N_EXPERTS,), F32)])[None, :]
            x1, h2, rinfo = _mixer(attn.reshape(t_lat, -1), four.reshape(t_lat, -1), gates, xl, mod_x, g1, b1,
                                   wao, wfo, wout, tm=ROUTE_TM, router=(wr, br))
            wg = w_e_gate[i].astype(BF16)
            wu = w_e_up[i].astype(BF16)
            wd = w_e_down[i].astype(BF16)
            xl_new = _moe(h2, x1, rinfo, mod_x, g2, b2, wg, wu, wd)

        if not last:
            assert l % 2 == 0
            attn_c = _attention(q_c, kt_c, v_c, None, None, tq=CTX_LEN, tk=CTX_LEN)
            four_c = _fourier_ctx(wf_c.reshape(BATCH, CTX_LEN, 2 * FOURIER_DIM), tb["c256"], tb["s256"])
            x1c, h2c = _mixer(attn_c.reshape(t_ctx, -1), four_c.reshape(t_ctx, -1), gates_c, xc, mod_c,
                              g1, b1, wao, wfo, wout, tm=CTX_LEN)
            xc = _ffn(h2c, x1c, mod_c, g2, b2, wg, wu, wd, tm=CTX_LEN)
        xl = xl_new

    return xl.reshape(BATCH, SEQ, D_MODEL)
```

```python
import functools
import math

import numpy as np
import jax
import jax.numpy as jnp
from jax import lax
from jax.experimental import pallas as pl
from jax.experimental.pallas import tpu as pltpu

D_MODEL = 1024
BATCH = 2
SEQ = 16384
DEPTH = 2
GRID_W = 64
CTX_LEN = 256
N_HEADS = 16
QK_NOPE = 64
QK_ROPE = 32
ROPE_FREQS = QK_ROPE // 4
V_HEAD = 64
Q_LORA = 256
KV_LORA = 128
ROPE_THETA = 10000.0
SM_SCALE = (QK_NOPE + QK_ROPE) ** -0.5
F_GROUPS = 4
F_GROUP_DIM = 128
FOURIER_DIM = F_GROUPS * F_GROUP_DIM
OFF_KV = Q_LORA
OFF_KR = OFF_KV + KV_LORA
OFF_F = OFF_KR + QK_ROPE
OFF_G = OFF_F + FOURIER_DIM
D_FF = 2816
N_EXPERTS = 8
D_FF_EXPERT = 3584
ALPHA = (2 * DEPTH) ** 0.25
LN_EPS = 1e-6
RMS_EPS = 1e-6

LANES = 128
SUBLANES = 8
N_MOD = 6
HEAD_PAD = 128
FFT_R = 128
V_ONES_LANE = (V_HEAD, 0)
RI_E1, RI_E2, RI_W1, RI_W2 = N_EXPERTS, N_EXPERTS + 1, N_EXPERTS + 2, N_EXPERTS + 3
VMEM_LIMIT = 56 * 1024 * 1024
ROW_TM = 1024
ROUTE_TM = 512
QKV_TM = 512
ATTN_TQ = 1024
ATTN_TK = 1024
ATTN_PEEL = 8
ATTN_UNROLL = 4

BF16 = jnp.bfloat16
F32 = jnp.float32


def _cp(sem, vmem=VMEM_LIMIT):
    return pltpu.CompilerParams(dimension_semantics=sem, vmem_limit_bytes=vmem)


def _dot(a, b):
    return jnp.dot(a, b, preferred_element_type=F32)


def _ln(x):
    mu = jnp.mean(x, axis=-1, keepdims=True)
    xc = x - mu
    var = jnp.mean(xc * xc, axis=-1, keepdims=True)
    return xc * lax.rsqrt(var + LN_EPS)


def _rms(x, g):
    return x * lax.rsqrt(jnp.mean(x * x, axis=-1, keepdims=True) + RMS_EPS) * g


def _sigmoid(x):
    return 1.0 / (1.0 + jnp.exp(-x))


def _mod_kernel(cs_ref, w_ref, b_ref, o_ref):
    cs = cs_ref[...]
    a = (cs * _sigmoid(cs)).astype(BF16)
    o_ref[0] = _dot(a, w_ref[0]) + b_ref[0]


def _modulation(cs, w_mod, b_mod):
    n_chunk = 1024
    n_out = w_mod.shape[-1]
    return pl.pallas_call(
        _mod_kernel,
        grid=(DEPTH, n_out // n_chunk),
        in_specs=[
            pl.BlockSpec((SUBLANES, D_MODEL), lambda l, j: (0, 0)),
            pl.BlockSpec((1, D_MODEL, n_chunk), lambda l, j: (l, 0, j)),
            pl.BlockSpec((1, 1, n_chunk), lambda l, j: (l, 0, j)),
        ],
        out_specs=pl.BlockSpec((1, SUBLANES, n_chunk), lambda l, j: (l, 0, j)),
        out_shape=jax.ShapeDtypeStruct((DEPTH, SUBLANES, n_out), F32),
        compiler_params=_cp(("parallel", "parallel")),
        name="mod",
    )(cs, w_mod, b_mod.reshape(DEPTH, 1, n_out))


IP_Q = 0
IP_KV = Q_LORA
IP_KR = IP_KV + KV_LORA
IP_F = IP_KR + LANES
IP_G = IP_F + FOURIER_DIM
IP_COLS = IP_G + 2 * D_MODEL


def _inproj_kernel(x_ref, mod_ref, w_ref, gq_ref, gkv_ref, tk_ref, bd_ref,
                   qlat_ref, ckr_ref, wf_ref, gates_ref):
    x = x_ref[...]
    mod = mod_ref[0]
    h = (_ln(x) * (1.0 + mod[1:2]) + mod[0:1]).astype(BF16)
    p0 = _dot(h, w_ref[:, IP_Q:IP_F])
    qlat_ref[...] = _rms(p0[:, IP_Q:IP_KV], gq_ref[...]).astype(BF16)
    ckr_ref[:, 0:KV_LORA] = _rms(p0[:, IP_KV:IP_KR], gkv_ref[...]).astype(BF16)
    ckr_ref[:, KV_LORA:] = (p0[:, IP_KR:IP_F] * tk_ref[...]).astype(BF16)
    uf = _dot(h, w_ref[:, IP_F:IP_G]).astype(BF16)
    for g in range(F_GROUPS):
        r = _dot(uf[:, g * LANES:(g + 1) * LANES], bd_ref[...])
        wf_ref[:, g * LANES:(g + 1) * LANES] = r[:, :LANES].astype(BF16)
        wf_ref[:, FOURIER_DIM + g * LANES:FOURIER_DIM + (g + 1) * LANES] = r[:, LANES:].astype(BF16)
    gc = 512
    for c in range(2 * D_MODEL // gc):
        gl = _dot(h, w_ref[:, IP_G + c * gc:IP_G + (c + 1) * gc])
        gates_ref[:, c * gc:(c + 1) * gc] = _sigmoid(gl).astype(BF16)


def _inproj(x2d, mod, w, gq, gkv, tk, bd, tm):
    t = x2d.shape[0]
    nb = mod.shape[0]
    tpb = t // nb // tm
    return pl.pallas_call(
        _inproj_kernel,
        grid=(t // tm,),
        in_specs=[
            pl.BlockSpec((tm, D_MODEL), lambda i: (i, 0)),
            pl.BlockSpec((1, SUBLANES, D_MODEL), lambda i: (i // tpb, 0, 0)),
            pl.BlockSpec((D_MODEL, IP_COLS), lambda i: (0, 0)),
            pl.BlockSpec((1, Q_LORA), lambda i: (0, 0)),
            pl.BlockSpec((1, KV_LORA), lambda i: (0, 0)),
            pl.BlockSpec((tm, LANES), lambda i: (i % tpb, 0)),
            pl.BlockSpec((LANES, 2 * LANES), lambda i: (0, 0)),
        ],
        out_specs=[
            pl.BlockSpec((tm, Q_LORA), lambda i: (i, 0)),
            pl.BlockSpec((tm, 2 * LANES), lambda i: (i, 0)),
            pl.BlockSpec((tm, 2 * FOURIER_DIM), lambda i: (i, 0)),
            pl.BlockSpec((tm, 2 * D_MODEL), lambda i: (i, 0)),
        ],
        out_shape=[
            jax.ShapeDtypeStruct((t, Q_LORA), BF16),
            jax.ShapeDtypeStruct((t, 2 * LANES), BF16),
            jax.ShapeDtypeStruct((t, 2 * FOURIER_DIM), BF16),
            jax.ShapeDtypeStruct((t, 2 * D_MODEL), BF16),
        ],
        compiler_params=_cp(("parallel",)),
        name="inproj",
    )(x2d, mod, w, gq, gkv, tk, bd)


def _qkv_kernel(qlat_ref, ckr_ref, tq_ref, wq_ref, wkt_ref, wv_ref, q_ref, kt_ref, v_ref):
    ql = qlat_ref[...]
    ckr = ckr_ref[...]
    tq = tq_ref[...]
    ckv = ckr[:, :KV_LORA]
    lane = lax.broadcasted_iota(jnp.int32, (ckr.shape[0], HEAD_PAD), 1)
    q_all = _dot(ql, wq_ref[...])
    v_all = _dot(ckv, wv_ref[...])
    for h in range(N_HEADS):
        cols = slice(h * HEAD_PAD, (h + 1) * HEAD_PAD)
        q_ref[0, h] = (q_all[:, cols] * tq).astype(BF16)
        kt_ref[0, h] = lax.dot_general(
            wkt_ref[h], ckr, (((1,), (1,)), ((), ())), preferred_element_type=F32).astype(BF16)
        v_ref[0, h] = jnp.where(lane == V_ONES_LANE[h % 2], 1.0, v_all[:, cols]).astype(BF16)


def _qkv(qlat, ckr, tq, wq, wkt, wv, nb, tm):
    t = qlat.shape[0]
    n = t // nb
    tpb = n // tm
    return pl.pallas_call(
        _qkv_kernel,
        grid=(t // tm,),
        in_specs=[
            pl.BlockSpec((tm, Q_LORA), lambda i: (i, 0)),
            pl.BlockSpec((tm, 2 * LANES), lambda i: (i, 0)),
            pl.BlockSpec((tm, HEAD_PAD), lambda i: (i % tpb, 0)),
            pl.BlockSpec((Q_LORA, N_HEADS * HEAD_PAD), lambda i: (0, 0)),
            pl.BlockSpec((N_HEADS, HEAD_PAD, 2 * LANES), lambda i: (0, 0, 0)),
            pl.BlockSpec((KV_LORA, N_HEADS * HEAD_PAD), lambda i: (0, 0)),
        ],
        out_specs=[
            pl.BlockSpec((1, N_HEADS, tm, HEAD_PAD), lambda i: (i // tpb, 0, i % tpb, 0)),
            pl.BlockSpec((1, N_HEADS, HEAD_PAD, tm), lambda i: (i // tpb, 0, 0, i % tpb)),
            pl.BlockSpec((1, N_HEADS, tm, HEAD_PAD), lambda i: (i // tpb, 0, i % tpb, 0)),
        ],
        out_shape=[
            jax.ShapeDtypeStruct((nb, N_HEADS, n, HEAD_PAD), BF16),
            jax.ShapeDtypeStruct((nb, N_HEADS, HEAD_PAD, n), BF16),
            jax.ShapeDtypeStruct((nb, N_HEADS, n, HEAD_PAD), BF16),
        ],
        compiler_params=_cp(("parallel",)),
        name="qkv",
    )(qlat, ckr, tq, wq, wkt, wv)


def _attn_kernel(q_ref, ktc_ref, vc_ref, *rest, n_chunks, tk, peel, unroll):
    if n_chunks:
        kt_ref, v_ref, o_ref = rest
    else:
        (o_ref,) = rest

    def step(q, kt, v, m, acc):
        s = _dot(q, kt)
        m_new = jnp.max(s, axis=1, keepdims=True)
        if m is not None:
            m_new = jnp.maximum(m, m_new)
        p = jnp.exp2((s - m_new).astype(BF16))
        pv = _dot(p, v)
        if m is not None:
            pv = jnp.exp2(m - m_new) * acc + pv
        return m_new, pv

    qs = [q_ref[0, hh] for hh in range(2)]
    carry = []
    for hh in range(2):
        carry += step(qs[hh], ktc_ref[0, hh], vc_ref[0, hh], None, None)

    if n_chunks:
        span = unroll * tk

        def group(base, count, carry):
            for u in range(count):
                off = base + u * tk
                if not isinstance(off, int):
                    off = pl.multiple_of(off, tk)
                out = []
                for hh in range(2):
                    out += step(qs[hh], kt_ref[0, hh, :, pl.ds(off, tk)], v_ref[0, hh, pl.ds(off, tk), :],
                                carry[2 * hh], carry[2 * hh + 1])
                carry = out
            return tuple(carry)

        carry = group(0, peel, carry)
        carry = lax.fori_loop(0, (n_chunks - peel) // unroll,
                              lambda g, c: group(pl.multiple_of(peel * tk + g * span, tk), unroll, c), carry)

    lane = lax.broadcasted_iota(jnp.int32, carry[1].shape, 1)
    outs = []
    for hh in range(2):
        acc = carry[2 * hh + 1]
        l = jnp.sum(jnp.where(lane == V_ONES_LANE[hh], acc, 0.0), axis=1, keepdims=True)
        outs.append(acc * (1.0 / l))
    o_ref[0] = jnp.where(lane < V_HEAD, outs[0], outs[1]).astype(BF16)


def _attention(q, kt_c, v_c, kt, v, tq, tk, peel=0, unroll=1):
    nb, _, n, _ = q.shape
    nc = kt_c.shape[-1]
    n_chunks = 0 if kt is None else kt.shape[-1] // tk
    in_specs = [
        pl.BlockSpec((1, 2, tq, HEAD_PAD), lambda b, p, i: (b, p, i, 0)),
        pl.BlockSpec((1, 2, HEAD_PAD, nc), lambda b, p, i: (b, p, 0, 0)),
        pl.BlockSpec((1, 2, nc, HEAD_PAD), lambda b, p, i: (b, p, 0, 0)),
    ]
    args = [q, kt_c, v_c]
    if n_chunks:
        nk = kt.shape[-1]
        in_specs += [
            pl.BlockSpec((1, 2, HEAD_PAD, nk), lambda b, p, i: (b, p, 0, 0)),
            pl.BlockSpec((1, 2, nk, HEAD_PAD), lambda b, p, i: (b, p, 0, 0)),
        ]
        args += [kt, v]
    return pl.pallas_call(
        functools.partial(_attn_kernel, n_chunks=n_chunks, tk=tk, peel=peel, unroll=unroll),
        grid=(nb, N_HEADS // 2, n // tq),
        in_specs=in_specs,
        out_specs=pl.BlockSpec((1, tq, 2 * V_HEAD), lambda b, p, i: (b, i, p)),
        out_shape=jax.ShapeDtypeStruct((nb, n, N_HEADS * V_HEAD), BF16),
        compiler_params=_cp(("parallel", "parallel", "arbitrary")),
        name="attn",
    )(*args)


def _fft1_kernel(x_ref, c_ref, s_ref, tc_ref, ts_ref, o_ref, *, n2t):
    x = x_ref[0]
    cx = _dot(c_ref[...], x)
    sx = _dot(s_ref[...], x)
    w = 2 * FOURIER_DIM
    for t in range(n2t):
        re = slice(t * w, t * w + FOURIER_DIM)
        im = slice(t * w + FOURIER_DIM, (t + 1) * w)
        yr = cx[:, re] + sx[:, im]
        yi = cx[:, im] - sx[:, re]
        tc = jnp.concatenate([tc_ref[:, t * LANES:(t + 1) * LANES]] * F_GROUPS, axis=1)
        ts = jnp.concatenate([ts_ref[:, t * LANES:(t + 1) * LANES]] * F_GROUPS, axis=1)
        o_ref[0, :, re] = (yr * tc + yi * ts).astype(BF16)
        o_ref[0, :, im] = (yi * tc - yr * ts).astype(BF16)


def _fft2_kernel(y_ref, c_ref, s_ref, o_ref, *, k1t):
    for t in range(k1t):
        y = y_ref[0, t]
        zr = _dot(c_ref[...], y[:, :FOURIER_DIM]) + _dot(s_ref[...], y[:, FOURIER_DIM:])
        o_ref[0, :, t * FOURIER_DIM:(t + 1) * FOURIER_DIM] = zr.astype(BF16)


def _fourier_latent(wf, cmat, smat, tc, ts):
    nb = wf.shape[0]
    w = 2 * FOURIER_DIM
    n2t = 8
    y = pl.pallas_call(
        functools.partial(_fft1_kernel, n2t=n2t),
        grid=(nb, FFT_R // n2t),
        in_specs=[
            pl.BlockSpec((1, FFT_R, n2t * w), lambda b, j: (b, 0, j)),
            pl.BlockSpec((FFT_R, FFT_R), lambda b, j: (0, 0)),
            pl.BlockSpec((FFT_R, FFT_R), lambda b, j: (0, 0)),
            pl.BlockSpec((FFT_R, n2t * LANES), lambda b, j: (0, j)),
            pl.BlockSpec((FFT_R, n2t * LANES), lambda b, j: (0, j)),
        ],
        out_specs=pl.BlockSpec((1, FFT_R, n2t * w), lambda b, j: (b, 0, j)),
        out_shape=jax.ShapeDtypeStruct((nb, FFT_R, FFT_R * w), BF16),
        compiler_params=_cp(("parallel", "parallel")),
        name="fft1",
    )(wf.reshape(nb, FFT_R, FFT_R * w), cmat, smat, tc, ts)
    k1t = 8
    four = pl.pallas_call(
        functools.partial(_fft2_kernel, k1t=k1t),
        grid=(nb, FFT_R // k1t),
        in_specs=[
            pl.BlockSpec((1, k1t, FFT_R, w), lambda b, j: (b, j, 0, 0)),
            pl.BlockSpec((FFT_R, FFT_R), lambda b, j: (0, 0)),
            pl.BlockSpec((FFT_R, FFT_R), lambda b, j: (0, 0)),
        ],
        out_specs=pl.BlockSpec((1, FFT_R, k1t * FOURIER_DIM), lambda b, j: (b, 0, j)),
        out_shape=jax.ShapeDtypeStruct((nb, FFT_R, FFT_R * FOURIER_DIM), BF16),
        compiler_params=_cp(("parallel", "parallel")),
        name="fft2",
    )(y.reshape(nb, FFT_R, FFT_R, w), cmat, smat)
    return four.reshape(nb, SEQ, FOURIER_DIM)


def _dft_ctx_kernel(x_ref, c_ref, s_ref, o_ref):
    x = x_ref[0]
    zr = _dot(c_ref[...], x[:, :FOURIER_DIM]) + _dot(s_ref[...], x[:, FOURIER_DIM:])
    o_ref[0] = zr.astype(BF16)


def _fourier_ctx(wf, cmat, smat):
    nb, n, w = wf.shape
    return pl.pallas_call(
        _dft_ctx_kernel,
        grid=(nb,),
        in_specs=[
            pl.BlockSpec((1, n, w), lambda b: (b, 0, 0)),
            pl.BlockSpec((n, n), lambda b: (0, 0)),
            pl.BlockSpec((n, n), lambda b: (0, 0)),
        ],
        out_specs=pl.BlockSpec((1, n, FOURIER_DIM), lambda b: (b, 0, 0)),
        out_shape=jax.ShapeDtypeStruct((nb, n, FOURIER_DIM), BF16),
        compiler_params=_cp(("parallel",)),
        name="dft_ctx",
    )(wf, cmat, smat)


def _mixer_kernel(attn_ref, four_ref, gates_ref, x_ref, mod_ref, g_ref, b_ref,
                  wao_ref, wfo_ref, wout_ref, *rest, route):
    if route:
        wr_ref, br_ref, x1_ref, h2_ref, rinfo_ref = rest
    else:
        x1_ref, h2_ref = rest
    mod = mod_ref[0]
    a = _dot(attn_ref[...], wao_ref[...])
    f = _dot(four_ref[...], wfo_ref[...])
    merged = gates_ref[:, :D_MODEL].astype(F32) * a + gates_ref[:, D_MODEL:].astype(F32) * f
    y = _dot(merged.astype(BF16), wout_ref[...])
    x1 = _ln(ALPHA * x_ref[...] + mod[2:3] * y) * g_ref[...] + b_ref[...]
    x1_ref[...] = x1
    h2 = _ln(x1) * (1.0 + mod[4:5]) + mod[3:4]
    h2_ref[...] = h2.astype(h2_ref.dtype)
    if route:
        logits = jnp.dot(h2, wr_ref[...], preferred_element_type=F32,
                         precision=lax.Precision.HIGHEST) + br_ref[...]
        lane = lax.broadcasted_iota(jnp.int32, logits.shape, 1)
        neg = jnp.float32(-jnp.inf)
        lg = jnp.where(lane < N_EXPERTS, logits, neg)
        m1 = jnp.max(lg, axis=1, keepdims=True)
        i1 = jnp.min(jnp.where(lg == m1, lane, LANES), axis=1, keepdims=True)
        lg2 = jnp.where(lane == i1, neg, lg)
        m2 = jnp.max(lg2, axis=1, keepdims=True)
        i2 = jnp.min(jnp.where(lg2 == m2, lane, LANES), axis=1, keepdims=True)
        e2 = jnp.exp(m2 - m1)
        w1 = 1.0 / (1.0 + e2)
        w2 = e2 * w1
        rinfo = jnp.where((lane == i1) | (lane == i2), 1.0, 0.0)
        rinfo = jnp.where(lane == RI_E1, i1.astype(F32), rinfo)
        rinfo = jnp.where(lane == RI_E2, i2.astype(F32), rinfo)
        rinfo = jnp.where(lane == RI_W1, w1, rinfo)
        rinfo_ref[...] = jnp.where(lane == RI_W2, w2, rinfo)


def _mixer(attn, four, gates, x2d, mod, g, b, wao, wfo, wout, tm, router=None):
    t = x2d.shape[0]
    nb = mod.shape[0]
    tpb = t // nb // tm
    row = lambda i: (i, 0)
    const = lambda i: (0, 0)
    in_specs = [
        pl.BlockSpec((tm, D_MODEL), row),
        pl.BlockSpec((tm, FOURIER_DIM), row),
        pl.BlockSpec((tm, 2 * D_MODEL), row),
        pl.BlockSpec((tm, D_MODEL), row),
        pl.BlockSpec((1, SUBLANES, D_MODEL), lambda i: (i // tpb, 0, 0)),
        pl.BlockSpec((1, D_MODEL), const),
        pl.BlockSpec((1, D_MODEL), const),
        pl.BlockSpec((D_MODEL, D_MODEL), const),
        pl.BlockSpec((FOURIER_DIM, D_MODEL), const),
        pl.BlockSpec((D_MODEL, D_MODEL), const),
    ]
    args = [attn, four, gates, x2d, mod, g, b, wao, wfo, wout]
    out_specs = [pl.BlockSpec((tm, D_MODEL), row), pl.BlockSpec((tm, D_MODEL), row)]
    h2_dtype = BF16 if router is None else F32
    out_shape = [jax.ShapeDtypeStruct((t, D_MODEL), F32), jax.ShapeDtypeStruct((t, D_MODEL), h2_dtype)]
    if router is not None:
        in_specs += [pl.BlockSpec((D_MODEL, LANES), const), pl.BlockSpec((1, LANES), const)]
        args += list(router)
        out_specs.append(pl.BlockSpec((tm, LANES), row))
        out_shape.append(jax.ShapeDtypeStruct((t, LANES), F32))
    return pl.pallas_call(
        functools.partial(_mixer_kernel, route=router is not None),
        grid=(t // tm,),
        in_specs=in_specs,
        out_specs=out_specs,
        out_shape=out_shape,
        compiler_params=_cp(("parallel",)),
        name="mixer",
    )(*args)


def _ffn_kernel(h_ref, x1_ref, mod_ref, g_ref, b_ref, wg_ref, wu_ref, wd_ref, o_ref):
    h = h_ref[...]
    gate = _dot(h, wg_ref[...])
    up = _dot(h, wu_ref[...])
    act = (gate * _sigmoid(gate) * up).astype(BF16)
    f = _dot(act, wd_ref[...])
    mod = mod_ref[0]
    o_ref[...] = _ln(ALPHA * x1_ref[...] + mod[5:6] * f) * g_ref[...] + b_ref[...]


def _ffn(h2, x1, mod, g, b, wg, wu, wd, tm):
    t = h2.shape[0]
    nb = mod.shape[0]
    tpb = t // nb // tm
    dff = wg.shape[1]
    row = lambda i: (i, 0)
    const = lambda i: (0, 0)
    resident = pl.Buffered(1)
    return pl.pallas_call(
        _ffn_kernel,
        grid=(t // tm,),
        in_specs=[
            pl.BlockSpec((tm, D_MODEL), row),
            pl.BlockSpec((tm, D_MODEL), row),
            pl.BlockSpec((1, SUBLANES, D_MODEL), lambda i: (i // tpb, 0, 0)),
            pl.BlockSpec((1, D_MODEL), const),
            pl.BlockSpec((1, D_MODEL), const),
            pl.BlockSpec((D_MODEL, dff), const, pipeline_mode=resident),
            pl.BlockSpec((D_MODEL, dff), const, pipeline_mode=resident),
            pl.BlockSpec((dff, D_MODEL), const, pipeline_mode=resident),
        ],
        out_specs=pl.BlockSpec((tm, D_MODEL), row),
        out_shape=jax.ShapeDtypeStruct((t, D_MODEL), F32),
        compiler_params=_cp(("parallel",)),
        name="ffn",
    )(h2, x1, mod, g, b, wg, wu, wd)


MOE_TM = 512
MOE_TT = 512
DMA_ISSUE_UNROLL = 8


def _moe_rows(n_tokens):
    return 2 * n_tokens + N_EXPERTS * MOE_TM


def _scan_kernel(r_ref, o_ref, cnt_ref, run_ref):
    @pl.when(pl.program_id(0) == 0)
    def _():
        run_ref[...] = jnp.zeros_like(run_ref)

    r = r_ref[...]
    ts = r.shape[0]
    lane = lax.broadcasted_iota(jnp.int32, r.shape, 1)
    sel = jnp.where(lane < N_EXPERTS, r, 0.0)
    rows = lax.broadcasted_iota(jnp.int32, (ts, ts), 0)
    cols = lax.broadcasted_iota(jnp.int32, (ts, ts), 1)
    ltri = jnp.where(rows > cols, 1.0, 0.0).astype(BF16)
    prefix = _dot(ltri, sel.astype(BF16)) + run_ref[0:1, :]
    lane_f = lane.astype(F32)
    pick = lambda k: jnp.sum(jnp.where(lane == k, r, 0.0), axis=1, keepdims=True)
    e1, e2 = pick(RI_E1), pick(RI_E2)
    p1 = jnp.sum(jnp.where(lane_f == e1, prefix, 0.0), axis=1, keepdims=True)
    p2 = jnp.sum(jnp.where(lane_f == e2, prefix, 0.0), axis=1, keepdims=True)
    out = jnp.where(lane == 0, e1, jnp.where(lane == 1, e2, jnp.where(lane == 2, p1, jnp.where(lane == 3, p2, 0.0))))
    o_ref[...] = out.astype(jnp.int32)
    run_ref[...] = run_ref[...] + jnp.sum(sel, axis=0, keepdims=True)
    cnt_ref[...] = run_ref[...]


def _route_scan(rinfo, ts=512):
    t = rinfo.shape[0]
    return pl.pallas_call(
        _scan_kernel,
        grid=(t // ts,),
        in_specs=[pl.BlockSpec((ts, LANES), lambda i: (i, 0))],
        out_specs=[pl.BlockSpec((ts, LANES), lambda i: (i, 0)), pl.BlockSpec((SUBLANES, LANES), lambda i: (0, 0))],
        out_shape=[jax.ShapeDtypeStruct((t, LANES), jnp.int32), jax.ShapeDtypeStruct((SUBLANES, LANES), F32)],
        scratch_shapes=[pltpu.VMEM((SUBLANES, LANES), F32)],
        compiler_params=_cp(("arbitrary",)),
        name="route_scan",
    )(rinfo)


def _row_copy(src_ref, src_row, dst_ref, dst_row, sem):
    return pltpu.make_async_copy(src_ref.at[pl.ds(src_row, 1)], dst_ref.at[pl.ds(dst_row, 1)], sem)


def _dispatch_kernel(dest_ref, h_ref, xs_ref, sem):
    def issue(t, c):
        for k in range(2):
            _row_copy(h_ref, t, xs_ref, dest_ref[0, k, t], sem).start(priority=k)
        return c

    lax.fori_loop(0, MOE_TT, issue, 0, unroll=DMA_ISSUE_UNROLL)
    for k in range(2):
        pltpu.make_async_copy(h_ref, xs_ref.at[pl.ds(0, MOE_TT)], sem).wait()


def _dispatch(dest, h2):
    t = h2.shape[0]
    return pl.pallas_call(
        _dispatch_kernel,
        grid=(t // MOE_TT,),
        in_specs=[
            pl.BlockSpec((1, 2, MOE_TT), lambda i: (i, 0, 0), memory_space=pltpu.SMEM),
            pl.BlockSpec((MOE_TT, D_MODEL), lambda i: (i, 0)),
        ],
        out_specs=pl.BlockSpec(memory_space=pl.ANY),
        out_shape=jax.ShapeDtypeStruct((_moe_rows(t), D_MODEL), F32),
        scratch_shapes=[pltpu.SemaphoreType.DMA(())],
        compiler_params=_cp(("arbitrary",)),
        name="moe_dispatch",
    )(dest, h2)


def _experts_kernel(te_ref, tv_ref, nt_ref, xs_ref, wg_ref, wu_ref, wd_ref, y_ref, *, fc):
    @pl.when(pl.program_id(0) < nt_ref[0])
    def _():
        rows = lax.broadcasted_iota(jnp.int32, xs_ref.shape, 0)
        h = jnp.where(rows < tv_ref[pl.program_id(0)], xs_ref[...], 0.0).astype(BF16)
        acc = None
        for c in range(wg_ref.shape[2] // fc):
            cols = slice(c * fc, (c + 1) * fc)
            gate = _dot(h, wg_ref[0, :, cols])
            up = _dot(h, wu_ref[0, :, cols])
            act = (gate * _sigmoid(gate) * up).astype(BF16)
            part = _dot(act, wd_ref[0, cols, :])
            acc = part if acc is None else acc + part
        y_ref[...] = acc

    @pl.when(pl.program_id(0) >= nt_ref[0])
    def _():
        y_ref[...] = jnp.zeros_like(y_ref)


def _experts(tile_expert, tile_valid, n_tiles_used, xs, wg, wu, wd, fc):
    ne, _, dff = wg.shape
    resident = pl.Buffered(1)
    grid_spec = pltpu.PrefetchScalarGridSpec(
        num_scalar_prefetch=3,
        grid=(xs.shape[0] // MOE_TM,),
        in_specs=[
            pl.BlockSpec((MOE_TM, D_MODEL), lambda i, te, tv, nt: (i, 0)),
            pl.BlockSpec((1, D_MODEL, dff), lambda i, te, tv, nt: (te[i], 0, 0), pipeline_mode=resident),
            pl.BlockSpec((1, D_MODEL, dff), lambda i, te, tv, nt: (te[i], 0, 0), pipeline_mode=resident),
            pl.BlockSpec((1, dff, D_MODEL), lambda i, te, tv, nt: (te[i], 0, 0), pipeline_mode=resident),
        ],
        out_specs=pl.BlockSpec((MOE_TM, D_MODEL), lambda i, te, tv, nt: (i, 0)),
    )
    return pl.pallas_call(
        functools.partial(_experts_kernel, fc=fc),
        grid_spec=grid_spec,
        out_shape=jax.ShapeDtypeStruct(xs.shape, F32),
        compiler_params=_cp(("arbitrary",)),
        name="moe_experts",
    )(tile_expert, tile_valid, n_tiles_used, xs, wg, wu, wd)


def _combine_kernel(dest_ref, dest_next_ref, y_ref, x1_ref, rinfo_ref, mod_ref, g_ref, b_ref, o_ref,
                    ybuf_ref, sems):
    i = pl.program_id(0)
    slot = i % 2

    def gather(d_ref, s):
        def issue(t, c):
            for k in range(2):
                _row_copy(y_ref, d_ref[0, k, t], ybuf_ref.at[s, k], t, sems.at[s]).start(priority=k)
            return c
        lax.fori_loop(0, MOE_TT, issue, 0, unroll=DMA_ISSUE_UNROLL)

    @pl.when(i == 0)
    def _():
        gather(dest_ref, slot)

    @pl.when(i + 1 < pl.num_programs(0))
    def _():
        gather(dest_next_ref, 1 - slot)

    for k in range(2):
        pltpu.make_async_copy(y_ref.at[pl.ds(0, MOE_TT)], ybuf_ref.at[slot, k], sems.at[slot]).wait()

    r = rinfo_ref[...]
    lane = lax.broadcasted_iota(jnp.int32, r.shape, 1)
    w1 = jnp.sum(jnp.where(lane == RI_W1, r, 0.0), axis=1, keepdims=True)
    w2 = jnp.sum(jnp.where(lane == RI_W2, r, 0.0), axis=1, keepdims=True)
    f = w1 * ybuf_ref[slot, 0] + w2 * ybuf_ref[slot, 1]
    mod = mod_ref[0]
    o_ref[...] = _ln(ALPHA * x1_ref[...] + mod[5:6] * f) * g_ref[...] + b_ref[...]


def _combine(dest, y, x1, rinfo, mod, g, b):
    t = x1.shape[0]
    nb = mod.shape[0]
    tpb = t // nb // MOE_TT
    row = lambda i: (i, 0)
    const = lambda i: (0, 0)
    n_steps = t // MOE_TT
    return pl.pallas_call(
        _combine_kernel,
        grid=(n_steps,),
        in_specs=[
            pl.BlockSpec((1, 2, MOE_TT), lambda i: (i, 0, 0), memory_space=pltpu.SMEM),
            pl.BlockSpec((1, 2, MOE_TT), lambda i: (jnp.minimum(i + 1, n_steps - 1), 0, 0),
                         memory_space=pltpu.SMEM),
            pl.BlockSpec(memory_space=pl.ANY),
            pl.BlockSpec((MOE_TT, D_MODEL), row),
            pl.BlockSpec((MOE_TT, LANES), row),
            pl.BlockSpec((1, SUBLANES, D_MODEL), lambda i: (i // tpb, 0, 0)),
            pl.BlockSpec((1, D_MODEL), const),
            pl.BlockSpec((1, D_MODEL), const),
        ],
        out_specs=pl.BlockSpec((MOE_TT, D_MODEL), row),
        out_shape=jax.ShapeDtypeStruct((t, D_MODEL), F32),
        scratch_shapes=[pltpu.VMEM((2, 2, MOE_TT, D_MODEL), F32), pltpu.SemaphoreType.DMA((2,))],
        compiler_params=_cp(("arbitrary",)),
        name="moe_combine",
    )(dest, dest, y, x1, rinfo, mod, g, b)


def _moe(h2, x1, rinfo, mod, g, b, wg, wu, wd):
    t = h2.shape[0]
    route, counts = _route_scan(rinfo)
    cnt = counts[0, :N_EXPERTS].astype(jnp.int32)
    padded = (cnt + MOE_TM - 1) // MOE_TM * MOE_TM
    ends = jnp.cumsum(padded)
    starts = ends - padded
    dest = jnp.stack([starts[route[:, 0]] + route[:, 2], starts[route[:, 1]] + route[:, 3]])
    dest = dest.reshape(2, t // MOE_TT, MOE_TT).transpose(1, 0, 2)
    tile_row0 = jnp.arange(_moe_rows(t) // MOE_TM, dtype=jnp.int32) * MOE_TM
    tile_expert = jnp.minimum(jnp.sum(tile_row0[:, None] >= ends[None, :], axis=1), N_EXPERTS - 1).astype(jnp.int32)
    tile_valid = jnp.clip(starts[tile_expert] + cnt[tile_expert] - tile_row0, 0, MOE_TM).astype(jnp.int32)
    n_tiles_used = (ends[-1:] // MOE_TM).astype(jnp.int32)
    xs = _dispatch(dest, h2)
    y = _experts(tile_expert, tile_valid, n_tiles_used, xs, wg, wu, wd, fc=D_FF_EXPERT // 2)
    return _combine(dest, y, x1, rinfo, mod, g, b)


def _rope_np(n_tokens):
    rows = np.repeat(np.arange(n_tokens // GRID_W), GRID_W)
    cols = np.tile(np.arange(GRID_W), n_tokens // GRID_W)
    pos = np.stack([rows, cols], axis=-1).astype(np.float64)
    inv_freq = 1.0 / (ROPE_THETA ** (np.arange(ROPE_FREQS, dtype=np.float64) / ROPE_FREQS))
    ang = pos[..., None] * inv_freq
    cos, sin = np.cos(ang), np.sin(ang)
    cos32 = np.broadcast_to(cos[:, :, None, :], (n_tokens, 2, 2, ROPE_FREQS)).reshape(n_tokens, QK_ROPE)
    sgn = np.array([-1.0, 1.0])[None, None, :, None]
    sin32 = (np.broadcast_to(sin[:, :, None, :], (n_tokens, 2, 2, ROPE_FREQS)) * sgn).reshape(n_tokens, QK_ROPE)
    return cos32, sin32


def _half_swap_perm():
    p = np.arange(QK_ROPE).reshape(2, 2, ROPE_FREQS)
    return p[:, ::-1, :].reshape(QK_ROPE)


def _tables():
    c0 = SM_SCALE * math.log2(math.e)
    cos32, sin32 = _rope_np(SEQ)
    one64 = np.ones((SEQ, QK_NOPE))
    tq_lat = c0 * np.concatenate([one64, cos32, sin32], axis=1)
    tk_lat = np.concatenate([cos32, sin32, np.zeros((SEQ, LANES - 2 * QK_ROPE))], axis=1)
    tq_ctx = c0 * np.concatenate([np.ones((CTX_LEN, QK_NOPE + QK_ROPE)), np.zeros((CTX_LEN, QK_ROPE))], axis=1)
    tk_ctx = np.concatenate([np.ones((CTX_LEN, QK_ROPE)), np.zeros((CTX_LEN, LANES - QK_ROPE))], axis=1)
    f32 = lambda a: jnp.asarray(a, F32)
    bf = lambda a: jnp.asarray(a, BF16)

    def dft(n):
        k = np.arange(n)
        ang = 2.0 * np.pi * ((k[:, None] * k[None, :]) % n) / n
        return np.cos(ang), np.sin(ang)

    c128, s128 = dft(FFT_R)
    c256, s256 = dft(CTX_LEN)
    bd_lat = np.concatenate([c128, -s128], axis=1) / math.sqrt(SEQ * F_GROUP_DIM)
    bd_ctx = np.concatenate([c128, -s128], axis=1) / math.sqrt(CTX_LEN * F_GROUP_DIM)
    k1 = np.arange(FFT_R)
    tw = 2.0 * np.pi * ((k1[:, None] * k1[None, :]) % SEQ) / SEQ
    tc = jnp.broadcast_to(f32(np.cos(tw))[:, :, None], (FFT_R, FFT_R, LANES)).reshape(FFT_R, FFT_R * LANES)
    ts = jnp.broadcast_to(f32(np.sin(tw))[:, :, None], (FFT_R, FFT_R, LANES)).reshape(FFT_R, FFT_R * LANES)
    return dict(tq_lat=f32(tq_lat), tk_lat=f32(tk_lat), tq_ctx=f32(tq_ctx), tk_ctx=f32(tk_ctx),
                c128=bf(c128), s128=bf(s128), c256=bf(c256), s256=bf(s256),
                bd_lat=bf(bd_lat), bd_ctx=bf(bd_ctx), tc=tc, ts=ts)


def _arrange_weights(l, w_in, w_uq, w_uk, w_uv):
    sw = _half_swap_perm()
    wi = w_in[l]
    kr = wi[:, OFF_KR:OFF_F]
    w_ip = jnp.concatenate([
        wi[:, :OFF_KR], kr, kr[:, sw], jnp.zeros((D_MODEL, LANES - 2 * QK_ROPE), F32),
        wi[:, OFF_F:]], axis=1).astype(BF16)
    uq = w_uq[l].reshape(Q_LORA, N_HEADS, QK_NOPE + QK_ROPE)
    qr = uq[:, :, QK_NOPE:]
    wq = jnp.concatenate([uq, qr[:, :, sw]], axis=-1).reshape(Q_LORA, N_HEADS * HEAD_PAD).astype(BF16)
    uk = w_uk[l].reshape(KV_LORA, N_HEADS, QK_NOPE)
    top = jnp.concatenate([uk, jnp.zeros((KV_LORA, N_HEADS, HEAD_PAD - QK_NOPE), F32)], axis=-1)
    rmat = np.zeros((LANES, HEAD_PAD), np.float32)
    for i in range(QK_ROPE):
        for r in (i, QK_ROPE + i):
            rmat[r, QK_NOPE + i] = 1.0
            rmat[r, QK_NOPE + QK_ROPE + i] = 1.0
    bot = jnp.broadcast_to(jnp.asarray(rmat)[:, None, :], (LANES, N_HEADS, HEAD_PAD))
    wkt = jnp.transpose(jnp.concatenate([top, bot], axis=0), (1, 2, 0)).astype(BF16)
    uv = w_uv[l].reshape(KV_LORA, N_HEADS // 2, 2, V_HEAD)
    z = jnp.zeros_like(uv[:, :, 0])
    even = jnp.concatenate([uv[:, :, 0], z], axis=-1)
    odd = jnp.concatenate([z, uv[:, :, 1]], axis=-1)
    wv = jnp.stack([even, odd], axis=2).reshape(KV_LORA, N_HEADS * HEAD_PAD).astype(BF16)
    return w_ip, wq, wkt, wv


def kernel(x, c, ctx, c_ctx, w_mod, b_mod, w_in, g_q, w_uq, g_kv, w_uk, w_uv, w_ao, w_fo, w_out,
           ln1_g, ln1_b, ln2_g, ln2_b, w_ff_gate, w_ff_up, w_ff_down,
           w_router, b_router, w_e_gate, w_e_up, w_e_down):
    tb = _tables()
    t_lat = BATCH * SEQ
    t_ctx = BATCH * CTX_LEN

    cs = jnp.concatenate([c, c_ctx[None, :], jnp.zeros((SUBLANES - BATCH - 1, D_MODEL), F32)], axis=0)
    mods = _modulation(cs, w_mod.astype(BF16), b_mod).reshape(DEPTH, SUBLANES, N_MOD, D_MODEL)
    mods = jnp.concatenate([mods, jnp.zeros((DEPTH, SUBLANES, SUBLANES - N_MOD, D_MODEL), F32)], axis=2)

    xl = x.reshape(t_lat, D_MODEL)
    xc = ctx.reshape(t_ctx, D_MODEL)

    for l in range(DEPTH):
        last = l == DEPTH - 1
        mod_x = mods[l, :BATCH]
        mod_c = jnp.broadcast_to(mods[l, BATCH:BATCH + 1], (BATCH, SUBLANES, D_MODEL))
        w_ip, wq, wkt, wv = _arrange_weights(l, w_in, w_uq, w_uk, w_uv)
        gq = g_q[l][None, :]
        gkv = g_kv[l][None, :]
        wao = w_ao[l].astype(BF16)
        wfo = w_fo[l].astype(BF16)
        wout = w_out[l].astype(BF16)
        g1, b1 = ln1_g[l][None, :], ln1_b[l][None, :]
        g2, b2 = ln2_g[l][None, :], ln2_b[l][None, :]

        qlat_c, ckr_c, wf_c, gates_c = _inproj(xc, mod_c, w_ip, gq, gkv, tb["tk_ctx"], tb["bd_ctx"], tm=CTX_LEN)
        q_c, kt_c, v_c = _qkv(qlat_c, ckr_c, tb["tq_ctx"], wq, wkt, wv, nb=BATCH, tm=CTX_LEN)

        qlat, ckr, wf, gates = _inproj(xl, mod_x, w_ip, gq, gkv, tb["tk_lat"], tb["bd_lat"], tm=ROW_TM)
        q, kt, v = _qkv(qlat, ckr, tb["tq_lat"], wq, wkt, wv, nb=BATCH, tm=QKV_TM)
        attn = _attention(q, kt_c, v_c, kt, v, tq=ATTN_TQ, tk=ATTN_TK, peel=ATTN_PEEL, unroll=ATTN_UNROLL)
        four = _fourier_latent(wf.reshape(BATCH, SEQ, 2 * FOURIER_DIM), tb["c128"], tb["s128"], tb["tc"], tb["ts"])

        if l % 2 == 0:
            i = l // 2
            x1, h2 = _mixer(attn.reshape(t_lat, -1), four.reshape(t_lat, -1), gates, xl, mod_x, g1, b1,
                            wao, wfo, wout, tm=ROW_TM)
            wg = w_ff_gate[i].astype(BF16)
            wu = w_ff_up[i].astype(BF16)
            wd = w_ff_down[i].astype(BF16)
            xl_new = _ffn(h2, x1, mod_x, g2, b2, wg, wu, wd, tm=ROW_TM)
        else:
            i = l // 2
            wr = jnp.concatenate([w_router[i], jnp.zeros((D_MODEL, LANES - N_EXPERTS), F32)], axis=1)
            br = jnp.concatenate([b_router[i], jnp.zeros((LANES - N_EXPERTS,), F32)])[None, :]
            x1, h2, rinfo = _mixer(attn.reshape(t_lat, -1), four.reshape(t_lat, -1), gates, xl, mod_x, g1, b1,
                                   wao, wfo, wout, tm=ROUTE_TM, router=(wr, br))
            wg = w_e_gate[i].astype(BF16)
            wu = w_e_up[i].astype(BF16)
            wd = w_e_down[i].astype(BF16)
            xl_new = _moe(h2, x1, rinfo, mod_x, g2, b2, wg, wu, wd)

        if not last:
            assert l % 2 == 0
            attn_c = _attention(q_c, kt_c, v_c, None, None, tq=CTX_LEN, tk=CTX_LEN)
            four_c = _fourier_ctx(wf_c.reshape(BATCH, CTX_LEN, 2 * FOURIER_DIM), tb["c256"], tb["s256"])
            x1c, h2c = _mixer(attn_c.reshape(t_ctx, -1), four_c.reshape(t_ctx, -1), gates_c, xc, mod_c,
                              g1, b1, wao, wfo, wout, tm=CTX_LEN)
            xc = _ffn(h2c, x1c, mod_c, g2, b2, wg, wu, wd, tm=CTX_LEN)
        xl = xl_new

    return xl.reshape(BATCH, SEQ, D_MODEL)
```

```python
import functools
import math

import numpy as np
import jax
import jax.numpy as jnp
from jax import lax
from jax.experimental import pallas as pl
from jax.experimental.pallas import tpu as pltpu

D_MODEL = 1024
BATCH = 2
SEQ = 16384
DEPTH = 2
GRID_W = 64
CTX_LEN = 256
N_HEADS = 16
QK_NOPE = 64
QK_ROPE = 32
ROPE_FREQS = QK_ROPE // 4
V_HEAD = 64
Q_LORA = 256
KV_LORA = 128
ROPE_THETA = 10000.0
SM_SCALE = (QK_NOPE + QK_ROPE) ** -0.5
F_GROUPS = 4
F_GROUP_DIM = 128
FOURIER_DIM = F_GROUPS * F_GROUP_DIM
OFF_KV = Q_LORA
OFF_KR = OFF_KV + KV_LORA
OFF_F = OFF_KR + QK_ROPE
OFF_G = OFF_F + FOURIER_DIM
D_FF = 2816
N_EXPERTS = 8
D_FF_EXPERT = 3584
ALPHA = (2 * DEPTH) ** 0.25
LN_EPS = 1e-6
RMS_EPS = 1e-6

LANES = 128
SUBLANES = 8
N_MOD = 6
HEAD_PAD = 128
FFT_R = 128
V_ONES_LANE = (V_HEAD, 0)
RI_E1, RI_E2, RI_W1, RI_W2 = N_EXPERTS, N_EXPERTS + 1, N_EXPERTS + 2, N_EXPERTS + 3
VMEM_LIMIT = 56 * 1024 * 1024
ROW_TM = 1024
ROUTE_TM = 512
QKV_TM = 512
ATTN_TQ = 1024
ATTN_TK = 1024
ATTN_PEEL = 8
ATTN_UNROLL = 4

BF16 = jnp.bfloat16
F32 = jnp.float32


def _cp(sem, vmem=VMEM_LIMIT):
    return pltpu.CompilerParams(dimension_semantics=sem, vmem_limit_bytes=vmem)


def _dot(a, b):
    return jnp.dot(a, b, preferred_element_type=F32)


def _ln(x):
    mu = jnp.mean(x, axis=-1, keepdims=True)
    xc = x - mu
    var = jnp.mean(xc * xc, axis=-1, keepdims=True)
    return xc * lax.rsqrt(var + LN_EPS)


def _rms(x, g):
    return x * lax.rsqrt(jnp.mean(x * x, axis=-1, keepdims=True) + RMS_EPS) * g


def _sigmoid(x):
    return 1.0 / (1.0 + jnp.exp(-x))


def _mod_kernel(cs_ref, w_ref, b_ref, o_ref):
    cs = cs_ref[...]
    a = (cs * _sigmoid(cs)).astype(BF16)
    o_ref[0] = _dot(a, w_ref[0]) + b_ref[0]


def _modulation(cs, w_mod, b_mod):
    n_chunk = 1024
    n_out = w_mod.shape[-1]
    return pl.pallas_call(
        _mod_kernel,
        grid=(DEPTH, n_out // n_chunk),
        in_specs=[
            pl.BlockSpec((SUBLANES, D_MODEL), lambda l, j: (0, 0)),
            pl.BlockSpec((1, D_MODEL, n_chunk), lambda l, j: (l, 0, j)),
            pl.BlockSpec((1, 1, n_chunk), lambda l, j: (l, 0, j)),
        ],
        out_specs=pl.BlockSpec((1, SUBLANES, n_chunk), lambda l, j: (l, 0, j)),
        out_shape=jax.ShapeDtypeStruct((DEPTH, SUBLANES, n_out), F32),
        compiler_params=_cp(("parallel", "parallel")),
        name="mod",
    )(cs, w_mod, b_mod.reshape(DEPTH, 1, n_out))


IP_Q = 0
IP_KV = Q_LORA
IP_KR = IP_KV + KV_LORA
IP_F = IP_KR + LANES
IP_G = IP_F + FOURIER_DIM
IP_COLS = IP_G + 2 * D_MODEL


def _inproj_kernel(x_ref, mod_ref, w_ref, gq_ref, gkv_ref, tk_ref, bd_ref,
                   qlat_ref, ckr_ref, wf_ref, gates_ref):
    x = x_ref[...]
    mod = mod_ref[0]
    h = (_ln(x) * (1.0 + mod[1:2]) + mod[0:1]).astype(BF16)
    p0 = _dot(h, w_ref[:, IP_Q:IP_F])
    qlat_ref[...] = _rms(p0[:, IP_Q:IP_KV], gq_ref[...]).astype(BF16)
    ckr_ref[:, 0:KV_LORA] = _rms(p0[:, IP_KV:IP_KR], gkv_ref[...]).astype(BF16)
    ckr_ref[:, KV_LORA:] = (p0[:, IP_KR:IP_F] * tk_ref[...]).astype(BF16)
    uf = _dot(h, w_ref[:, IP_F:IP_G]).astype(BF16)
    for g in range(F_GROUPS):
        r = _dot(uf[:, g * LANES:(g + 1) * LANES], bd_ref[...])
        wf_ref[:, g * LANES:(g + 1) * LANES] = r[:, :LANES].astype(BF16)
        wf_ref[:, FOURIER_DIM + g * LANES:FOURIER_DIM + (g + 1) * LANES] = r[:, LANES:].astype(BF16)
    gc = 512
    for c in range(2 * D_MODEL // gc):
        gl = _dot(h, w_ref[:, IP_G + c * gc:IP_G + (c + 1) * gc])
        gates_ref[:, c * gc:(c + 1) * gc] = _sigmoid(gl).astype(BF16)


def _inproj(x2d, mod, w, gq, gkv, tk, bd, tm):
    t = x2d.shape[0]
    nb = mod.shape[0]
    tpb = t // nb // tm
    return pl.pallas_call(
        _inproj_kernel,
        grid=(t // tm,),
        in_specs=[
            pl.BlockSpec((tm, D_MODEL), lambda i: (i, 0)),
            pl.BlockSpec((1, SUBLANES, D_MODEL), lambda i: (i // tpb, 0, 0)),
            pl.BlockSpec((D_MODEL, IP_COLS), lambda i: (0, 0)),
            pl.BlockSpec((1, Q_LORA), lambda i: (0, 0)),
            pl.BlockSpec((1, KV_LORA), lambda i: (0, 0)),
            pl.BlockSpec((tm, LANES), lambda i: (i % tpb, 0)),
            pl.BlockSpec((LANES, 2 * LANES), lambda i: (0, 0)),
        ],
        out_specs=[
            pl.BlockSpec((tm, Q_LORA), lambda i: (i, 0)),
            pl.BlockSpec((tm, 2 * LANES), lambda i: (i, 0)),
            pl.BlockSpec((tm, 2 * FOURIER_DIM), lambda i: (i, 0)),
            pl.BlockSpec((tm, 2 * D_MODEL), lambda i: (i, 0)),
        ],
        out_shape=[
            jax.ShapeDtypeStruct((t, Q_LORA), BF16),
            jax.ShapeDtypeStruct((t, 2 * LANES), BF16),
            jax.ShapeDtypeStruct((t, 2 * FOURIER_DIM), BF16),
            jax.ShapeDtypeStruct((t, 2 * D_MODEL), BF16),
        ],
        compiler_params=_cp(("parallel",)),
        name="inproj",
    )(x2d, mod, w, gq, gkv, tk, bd)


def _qkv_kernel(qlat_ref, ckr_ref, tq_ref, wq_ref, wkt_ref, wv_ref, q_ref, kt_ref, v_ref):
    ql = qlat_ref[...]
    ckr = ckr_ref[...]
    tq = tq_ref[...]
    ckv = ckr[:, :KV_LORA]
    lane = lax.broadcasted_iota(jnp.int32, (ckr.shape[0], HEAD_PAD), 1)
    q_all = _dot(ql, wq_ref[...])
    v_all = _dot(ckv, wv_ref[...])
    for h in range(N_HEADS):
        cols = slice(h * HEAD_PAD, (h + 1) * HEAD_PAD)
        q_ref[0, h] = (q_all[:, cols] * tq).astype(BF16)
        kt_ref[0, h] = lax.dot_general(
            wkt_ref[h], ckr, (((1,), (1,)), ((), ())), preferred_element_type=F32).astype(BF16)
        v_ref[0, h] = jnp.where(lane == V_ONES_LANE[h % 2], 1.0, v_all[:, cols]).astype(BF16)


def _qkv(qlat, ckr, tq, wq, wkt, wv, nb, tm):
    t = qlat.shape[0]
    n = t // nb
    tpb = n // tm
    return pl.pallas_call(
        _qkv_kernel,
        grid=(t // tm,),
        in_specs=[
            pl.BlockSpec((tm, Q_LORA), lambda i: (i, 0)),
            pl.BlockSpec((tm, 2 * LANES), lambda i: (i, 0)),
            pl.BlockSpec((tm, HEAD_PAD), lambda i: (i % tpb, 0)),
            pl.BlockSpec((Q_LORA, N_HEADS * HEAD_PAD), lambda i: (0, 0)),
            pl.BlockSpec((N_HEADS, HEAD_PAD, 2 * LANES), lambda i: (0, 0, 0)),
            pl.BlockSpec((KV_LORA, N_HEADS * HEAD_PAD), lambda i: (0, 0)),
        ],
        out_specs=[
            pl.BlockSpec((1, N_HEADS, tm, HEAD_PAD), lambda i: (i // tpb, 0, i % tpb, 0)),
            pl.BlockSpec((1, N_HEADS, HEAD_PAD, tm), lambda i: (i // tpb, 0, 0, i % tpb)),
            pl.BlockSpec((1, N_HEADS, tm, HEAD_PAD), lambda i: (i // tpb, 0, i % tpb, 0)),
        ],
        out_shape=[
            jax.ShapeDtypeStruct((nb, N_HEADS, n, HEAD_PAD), BF16),
            jax.ShapeDtypeStruct((nb, N_HEADS, HEAD_PAD, n), BF16),
            jax.ShapeDtypeStruct((nb, N_HEADS, n, HEAD_PAD), BF16),
        ],
        compiler_params=_cp(("parallel",)),
        name="qkv",
    )(qlat, ckr, tq, wq, wkt, wv)


def _attn_kernel(q_ref, ktc_ref, vc_ref, *rest, n_chunks, tk, peel, unroll):
    if n_chunks:
        kt_ref, v_ref, o_ref = rest
    else:
        (o_ref,) = rest

    def step(q, kt, v, m, acc):
        s = _dot(q, kt)
        m_new = jnp.max(s, axis=1, keepdims=True)
        if m is not None:
            m_new = jnp.maximum(m, m_new)
        p = jnp.exp2((s - m_new).astype(BF16))
        pv = _dot(p, v)
        if m is not None:
            pv = jnp.exp2(m - m_new) * acc + pv
        return m_new, pv

    qs = [q_ref[0, hh] for hh in range(2)]
    carry = []
    for hh in range(2):
        carry += step(qs[hh], ktc_ref[0, hh], vc_ref[0, hh], None, None)

    if n_chunks:
        span = unroll * tk

        def group(base, count, carry):
            for u in range(count):
                off = base + u * tk
                if not isinstance(off, int):
                    off = pl.multiple_of(off, tk)
                out = []
                for hh in range(2):
                    out += step(qs[hh], kt_ref[0, hh, :, pl.ds(off, tk)], v_ref[0, hh, pl.ds(off, tk), :],
                                carry[2 * hh], carry[2 * hh + 1])
                carry = out
            return tuple(carry)

        carry = group(0, peel, carry)
        carry = lax.fori_loop(0, (n_chunks - peel) // unroll,
                              lambda g, c: group(pl.multiple_of(peel * tk + g * span, tk), unroll, c), carry)

    lane = lax.broadcasted_iota(jnp.int32, carry[1].shape, 1)
    outs = []
    for hh in range(2):
        acc = carry[2 * hh + 1]
        l = jnp.sum(jnp.where(lane == V_ONES_LANE[hh], acc, 0.0), axis=1, keepdims=True)
        outs.append(acc * (1.0 / l))
    o_ref[0] = jnp.where(lane < V_HEAD, outs[0], outs[1]).astype(BF16)


def _attention(q, kt_c, v_c, kt, v, tq, tk, peel=0, unroll=1):
    nb, _, n, _ = q.shape
    nc = kt_c.shape[-1]
    n_chunks = 0 if kt is None else kt.shape[-1] // tk
    in_specs = [
        pl.BlockSpec((1, 2, tq, HEAD_PAD), lambda b, p, i: (b, p, i, 0)),
        pl.BlockSpec((1, 2, HEAD_PAD, nc), lambda b, p, i: (b, p, 0, 0)),
        pl.BlockSpec((1, 2, nc, HEAD_PAD), lambda b, p, i: (b, p, 0, 0)),
    ]
    args = [q, kt_c, v_c]
    if n_chunks:
        nk = kt.shape[-1]
        in_specs += [
            pl.BlockSpec((1, 2, HEAD_PAD, nk), lambda b, p, i: (b, p, 0, 0)),
            pl.BlockSpec((1, 2, nk, HEAD_PAD), lambda b, p, i: (b, p, 0, 0)),
        ]
        args += [kt, v]
    return pl.pallas_call(
        functools.partial(_attn_kernel, n_chunks=n_chunks, tk=tk, peel=peel, unroll=unroll),
        grid=(nb, N_HEADS // 2, n // tq),
        in_specs=in_specs,
        out_specs=pl.BlockSpec((1, tq, 2 * V_HEAD), lambda b, p, i: (b, i, p)),
        out_shape=jax.ShapeDtypeStruct((nb, n, N_HEADS * V_HEAD), BF16),
        compiler_params=_cp(("parallel", "parallel", "arbitrary")),
        name="attn",
    )(*args)


def _fft1_kernel(x_ref, c_ref, s_ref, tc_ref, ts_ref, o_ref, *, n2t):
    x = x_ref[0]
    cx = _dot(c_ref[...], x)
    sx = _dot(s_ref[...], x)
    w = 2 * FOURIER_DIM
    for t in range(n2t):
        re = slice(t * w, t * w + FOURIER_DIM)
        im = slice(t * w + FOURIER_DIM, (t + 1) * w)
        yr = cx[:, re] + sx[:, im]
        yi = cx[:, im] - sx[:, re]
        tc = jnp.concatenate([tc_ref[:, t * LANES:(t + 1) * LANES]] * F_GROUPS, axis=1)
        ts = jnp.concatenate([ts_ref[:, t * LANES:(t + 1) * LANES]] * F_GROUPS, axis=1)
        o_ref[0, :, re] = (yr * tc + yi * ts).astype(BF16)
        o_ref[0, :, im] = (yi * tc - yr * ts).astype(BF16)


def _fft2_kernel(y_ref, c_ref, s_ref, o_ref, *, k1t):
    for t in range(k1t):
        y = y_ref[0, t]
        zr = _dot(c_ref[...], y[:, :FOURIER_DIM]) + _dot(s_ref[...], y[:, FOURIER_DIM:])
        o_ref[0, :, t * FOURIER_DIM:(t + 1) * FOURIER_DIM] = zr.astype(BF16)


def _fourier_latent(wf, cmat, smat, tc, ts):
    nb = wf.shape[0]
    w = 2 * FOURIER_DIM
    n2t = 8
    y = pl.pallas_call(
        functools.partial(_fft1_kernel, n2t=n2t),
        grid=(nb, FFT_R // n2t),
        in_specs=[
            pl.BlockSpec((1, FFT_R, n2t * w), lambda b, j: (b, 0, j)),
            pl.BlockSpec((FFT_R, FFT_R), lambda b, j: (0, 0)),
            pl.BlockSpec((FFT_R, FFT_R), lambda b, j: (0, 0)),
            pl.BlockSpec((FFT_R, n2t * LANES), lambda b, j: (0, j)),
            pl.BlockSpec((FFT_R, n2t * LANES), lambda b, j: (0, j)),
        ],
        out_specs=pl.BlockSpec((1, FFT_R, n2t * w), lambda b, j: (b, 0, j)),
        out_shape=jax.ShapeDtypeStruct((nb, FFT_R, FFT_R * w), BF16),
        compiler_params=_cp(("parallel", "parallel")),
        name="fft1",
    )(wf.reshape(nb, FFT_R, FFT_R * w), cmat, smat, tc, ts)
    k1t = 8
    four = pl.pallas_call(
        functools.partial(_fft2_kernel, k1t=k1t),
        grid=(nb, FFT_R // k1t),
        in_specs=[
            pl.BlockSpec((1, k1t, FFT_R, w), lambda b, j: (b, j, 0, 0)),
            pl.BlockSpec((FFT_R, FFT_R), lambda b, j: (0, 0)),
            pl.BlockSpec((FFT_R, FFT_R), lambda b, j: (0, 0)),
        ],
        out_specs=pl.BlockSpec((1, FFT_R, k1t * FOURIER_DIM), lambda b, j: (b, 0, j)),
        out_shape=jax.ShapeDtypeStruct((nb, FFT_R, FFT_R * FOURIER_DIM), BF16),
        compiler_params=_cp(("parallel", "parallel")),
        name="fft2",
    )(y.reshape(nb, FFT_R, FFT_R, w), cmat, smat)
    return four.reshape(nb, SEQ, FOURIER_DIM)


def _dft_ctx_kernel(x_ref, c_ref, s_ref, o_ref):
    x = x_ref[0]
    zr = _dot(c_ref[...], x[:, :FOURIER_DIM]) + _dot(s_ref[...], x[:, FOURIER_DIM:])
    o_ref[0] = zr.astype(BF16)


def _fourier_ctx(wf, cmat, smat):
    nb, n, w = wf.shape
    return pl.pallas_call(
        _dft_ctx_kernel,
        grid=(nb,),
        in_specs=[
            pl.BlockSpec((1, n, w), lambda b: (b, 0, 0)),
            pl.BlockSpec((n, n), lambda b: (0, 0)),
            pl.BlockSpec((n, n), lambda b: (0, 0)),
        ],
        out_specs=pl.BlockSpec((1, n, FOURIER_DIM), lambda b: (b, 0, 0)),
        out_shape=jax.ShapeDtypeStruct((nb, n, FOURIER_DIM), BF16),
        compiler_params=_cp(("parallel",)),
        name="dft_ctx",
    )(wf, cmat, smat)


def _mixer_kernel(attn_ref, four_ref, gates_ref, x_ref, mod_ref, g_ref, b_ref,
                  wao_ref, wfo_ref, wout_ref, *rest, route):
    if route:
        wr_ref, br_ref, x1_ref, h2_ref, rinfo_ref = rest
    else:
        x1_ref, h2_ref = rest
    mod = mod_ref[0]
    a = _dot(attn_ref[...], wao_ref[...])
    f = _dot(four_ref[...], wfo_ref[...])
    merged = gates_ref[:, :D_MODEL].astype(F32) * a + gates_ref[:, D_MODEL:].astype(F32) * f
    y = _dot(merged.astype(BF16), wout_ref[...])
    x1 = _ln(ALPHA * x_ref[...] + mod[2:3] * y) * g_ref[...] + b_ref[...]
    x1_ref[...] = x1
    h2 = _ln(x1) * (1.0 + mod[4:5]) + mod[3:4]
    h2_ref[...] = h2.astype(h2_ref.dtype)
    if route:
        logits = jnp.dot(h2, wr_ref[...], preferred_element_type=F32,
                         precision=lax.Precision.HIGHEST) + br_ref[...]
        lane = lax.broadcasted_iota(jnp.int32, logits.shape, 1)
        neg = jnp.float32(-jnp.inf)
        lg = jnp.where(lane < N_EXPERTS, logits, neg)
        m1 = jnp.max(lg, axis=1, keepdims=True)
        i1 = jnp.min(jnp.where(lg == m1, lane, LANES), axis=1, keepdims=True)
        lg2 = jnp.where(lane == i1, neg, lg)
        m2 = jnp.max(lg2, axis=1, keepdims=True)
        i2 = jnp.min(jnp.where(lg2 == m2, lane, LANES), axis=1, keepdims=True)
        e2 = jnp.exp(m2 - m1)
        w1 = 1.0 / (1.0 + e2)
        w2 = e2 * w1
        rinfo = jnp.where((lane == i1) | (lane == i2), 1.0, 0.0)
        rinfo = jnp.where(lane == RI_E1, i1.astype(F32), rinfo)
        rinfo = jnp.where(lane == RI_E2, i2.astype(F32), rinfo)
        rinfo = jnp.where(lane == RI_W1, w1, rinfo)
        rinfo_ref[...] = jnp.where(lane == RI_W2, w2, rinfo)


def _mixer(attn, four, gates, x2d, mod, g, b, wao, wfo, wout, tm, router=None):
    t = x2d.shape[0]
    nb = mod.shape[0]
    tpb = t // nb // tm
    row = lambda i: (i, 0)
    const = lambda i: (0, 0)
    in_specs = [
        pl.BlockSpec((tm, D_MODEL), row),
        pl.BlockSpec((tm, FOURIER_DIM), row),
        pl.BlockSpec((tm, 2 * D_MODEL), row),
        pl.BlockSpec((tm, D_MODEL), row),
        pl.BlockSpec((1, SUBLANES, D_MODEL), lambda i: (i // tpb, 0, 0)),
        pl.BlockSpec((1, D_MODEL), const),
        pl.BlockSpec((1, D_MODEL), const),
        pl.BlockSpec((D_MODEL, D_MODEL), const),
        pl.BlockSpec((FOURIER_DIM, D_MODEL), const),
        pl.BlockSpec((D_MODEL, D_MODEL), const),
    ]
    args = [attn, four, gates, x2d, mod, g, b, wao, wfo, wout]
    out_specs = [pl.BlockSpec((tm, D_MODEL), row), pl.BlockSpec((tm, D_MODEL), row)]
    h2_dtype = BF16 if router is None else F32
    out_shape = [jax.ShapeDtypeStruct((t, D_MODEL), F32), jax.ShapeDtypeStruct((t, D_MODEL), h2_dtype)]
    if router is not None:
        in_specs += [pl.BlockSpec((D_MODEL, LANES), const), pl.BlockSpec((1, LANES), const)]
        args += list(router)
        out_specs.append(pl.BlockSpec((tm, LANES), row))
        out_shape.append(jax.ShapeDtypeStruct((t, LANES), F32))
    return pl.pallas_call(
        functools.partial(_mixer_kernel, route=router is not None),
        grid=(t // tm,),
        in_specs=in_specs,
        out_specs=out_specs,
        out_shape=out_shape,
        compiler_params=_cp(("parallel",)),
        name="mixer",
    )(*args)


MXU_TILE = 256


def _ffn_chunks(dff, n=2):
    tiles = dff // MXU_TILE
    assert tiles * MXU_TILE == dff
    cuts = [MXU_TILE * ((tiles * c + n - 1) // n) for c in range(n + 1)]
    return list(zip(cuts[:-1], cuts[1:]))


def _ffn_kernel(h_ref, x1_ref, mod_ref, g_ref, b_ref, wg_ref, wu_ref, wd_ref, o_ref):
    h = h_ref[...]
    f = None
    for lo, hi in _ffn_chunks(wg_ref.shape[1]):
        gate = _dot(h, wg_ref[:, lo:hi])
        up = _dot(h, wu_ref[:, lo:hi])
        act = (gate * _sigmoid(gate) * up).astype(BF16)
        part = _dot(act, wd_ref[lo:hi, :])
        f = part if f is None else f + part
    mod = mod_ref[0]
    o_ref[...] = _ln(ALPHA * x1_ref[...] + mod[5:6] * f) * g_ref[...] + b_ref[...]


def _ffn(h2, x1, mod, g, b, wg, wu, wd, tm):
    t = h2.shape[0]
    nb = mod.shape[0]
    tpb = t // nb // tm
    dff = wg.shape[1]
    row = lambda i: (i, 0)
    const = lambda i: (0, 0)
    resident = pl.Buffered(1)
    return pl.pallas_call(
        _ffn_kernel,
        grid=(t // tm,),
        in_specs=[
            pl.BlockSpec((tm, D_MODEL), row),
            pl.BlockSpec((tm, D_MODEL), row),
            pl.BlockSpec((1, SUBLANES, D_MODEL), lambda i: (i // tpb, 0, 0)),
            pl.BlockSpec((1, D_MODEL), const),
            pl.BlockSpec((1, D_MODEL), const),
            pl.BlockSpec((D_MODEL, dff), const, pipeline_mode=resident),
            pl.BlockSpec((D_MODEL, dff), const, pipeline_mode=resident),
            pl.BlockSpec((dff, D_MODEL), const, pipeline_mode=resident),
        ],
        out_specs=pl.BlockSpec((tm, D_MODEL), row),
        out_shape=jax.ShapeDtypeStruct((t, D_MODEL), F32),
        compiler_params=_cp(("parallel",)),
        name="ffn",
    )(h2, x1, mod, g, b, wg, wu, wd)


MOE_TM = 512
MOE_TT = 512
DMA_ISSUE_UNROLL = 8


def _moe_rows(n_tokens):
    return 2 * n_tokens + N_EXPERTS * MOE_TM


def _scan_kernel(r_ref, o_ref, cnt_ref, run_ref):
    @pl.when(pl.program_id(0) == 0)
    def _():
        run_ref[...] = jnp.zeros_like(run_ref)

    r = r_ref[...]
    ts = r.shape[0]
    lane = lax.broadcasted_iota(jnp.int32, r.shape, 1)
    sel = jnp.where(lane < N_EXPERTS, r, 0.0)
    rows = lax.broadcasted_iota(jnp.int32, (ts, ts), 0)
    cols = lax.broadcasted_iota(jnp.int32, (ts, ts), 1)
    ltri = jnp.where(rows > cols, 1.0, 0.0).astype(BF16)
    prefix = _dot(ltri, sel.astype(BF16)) + run_ref[0:1, :]
    lane_f = lane.astype(F32)
    pick = lambda k: jnp.sum(jnp.where(lane == k, r, 0.0), axis=1, keepdims=True)
    e1, e2 = pick(RI_E1), pick(RI_E2)
    p1 = jnp.sum(jnp.where(lane_f == e1, prefix, 0.0), axis=1, keepdims=True)
    p2 = jnp.sum(jnp.where(lane_f == e2, prefix, 0.0), axis=1, keepdims=True)
    out = jnp.where(lane == 0, e1, jnp.where(lane == 1, e2, jnp.where(lane == 2, p1, jnp.where(lane == 3, p2, 0.0))))
    o_ref[...] = out.astype(jnp.int32)
    run_ref[...] = run_ref[...] + jnp.sum(sel, axis=0, keepdims=True)
    cnt_ref[...] = run_ref[...]


def _route_scan(rinfo, ts=512):
    t = rinfo.shape[0]
    return pl.pallas_call(
        _scan_kernel,
        grid=(t // ts,),
        in_specs=[pl.BlockSpec((ts, LANES), lambda i: (i, 0))],
        out_specs=[pl.BlockSpec((ts, LANES), lambda i: (i, 0)), pl.BlockSpec((SUBLANES, LANES), lambda i: (0, 0))],
        out_shape=[jax.ShapeDtypeStruct((t, LANES), jnp.int32), jax.ShapeDtypeStruct((SUBLANES, LANES), F32)],
        scratch_shapes=[pltpu.VMEM((SUBLANES, LANES), F32)],
        compiler_params=_cp(("arbitrary",)),
        name="route_scan",
    )(rinfo)


def _row_copy(src_ref, src_row, dst_ref, dst_row, sem):
    return pltpu.make_async_copy(src_ref.at[pl.ds(src_row, 1)], dst_ref.at[pl.ds(dst_row, 1)], sem)


def _dispatch_kernel(dest_ref, h_ref, xs_ref, sem):
    def issue(t, c):
        for k in range(2):
            _row_copy(h_ref, t, xs_ref, dest_ref[0, k, t], sem).start(priority=k)
        return c

    lax.fori_loop(0, MOE_TT, issue, 0, unroll=DMA_ISSUE_UNROLL)
    for k in range(2):
        pltpu.make_async_copy(h_ref, xs_ref.at[pl.ds(0, MOE_TT)], sem).wait()


def _dispatch(dest, h2):
    t = h2.shape[0]
    return pl.pallas_call(
        _dispatch_kernel,
        grid=(t // MOE_TT,),
        in_specs=[
            pl.BlockSpec((1, 2, MOE_TT), lambda i: (i, 0, 0), memory_space=pltpu.SMEM),
            pl.BlockSpec((MOE_TT, D_MODEL), lambda i: (i, 0)),
        ],
        out_specs=pl.BlockSpec(memory_space=pl.ANY),
        out_shape=jax.ShapeDtypeStruct((_moe_rows(t), D_MODEL), F32),
        scratch_shapes=[pltpu.SemaphoreType.DMA(())],
        compiler_params=_cp(("arbitrary",)),
        name="moe_dispatch",
    )(dest, h2)


def _experts_kernel(te_ref, tv_ref, nt_ref, xs_ref, wg_ref, wu_ref, wd_ref, y_ref, *, fc):
    @pl.when(pl.program_id(0) < nt_ref[0])
    def _():
        rows = lax.broadcasted_iota(jnp.int32, xs_ref.shape, 0)
        h = jnp.where(rows < tv_ref[pl.program_id(0)], xs_ref[...], 0.0).astype(BF16)
        acc = None
        for c in range(wg_ref.shape[2] // fc):
            cols = slice(c * fc, (c + 1) * fc)
            gate = _dot(h, wg_ref[0, :, cols])
            up = _dot(h, wu_ref[0, :, cols])
            act = (gate * _sigmoid(gate) * up).astype(BF16)
            part = _dot(act, wd_ref[0, cols, :])
            acc = part if acc is None else acc + part
        y_ref[...] = acc

    @pl.when(pl.program_id(0) >= nt_ref[0])
    def _():
        y_ref[...] = jnp.zeros_like(y_ref)


def _experts(tile_expert, tile_valid, n_tiles_used, xs, wg, wu, wd, fc):
    ne, _, dff = wg.shape
    resident = pl.Buffered(1)
    grid_spec = pltpu.PrefetchScalarGridSpec(
        num_scalar_prefetch=3,
        grid=(xs.shape[0] // MOE_TM,),
        in_specs=[
            pl.BlockSpec((MOE_TM, D_MODEL), lambda i, te, tv, nt: (i, 0)),
            pl.BlockSpec((1, D_MODEL, dff), lambda i, te, tv, nt: (te[i], 0, 0), pipeline_mode=resident),
            pl.BlockSpec((1, D_MODEL, dff), lambda i, te, tv, nt: (te[i], 0, 0), pipeline_mode=resident),
            pl.BlockSpec((1, dff, D_MODEL), lambda i, te, tv, nt: (te[i], 0, 0), pipeline_mode=resident),
        ],
        out_specs=pl.BlockSpec((MOE_TM, D_MODEL), lambda i, te, tv, nt: (i, 0)),
    )
    return pl.pallas_call(
        functools.partial(_experts_kernel, fc=fc),
        grid_spec=grid_spec,
        out_shape=jax.ShapeDtypeStruct(xs.shape, F32),
        compiler_params=_cp(("arbitrary",)),
        name="moe_experts",
    )(tile_expert, tile_valid, n_tiles_used, xs, wg, wu, wd)


def _combine_kernel(dest_ref, dest_next_ref, y_ref, x1_ref, rinfo_ref, mod_ref, g_ref, b_ref, o_ref,
                    ybuf_ref, sems):
    i = pl.program_id(0)
    slot = i % 2

    def gather(d_ref, s):
        def issue(t, c):
            for k in range(2):
                _row_copy(y_ref, d_ref[0, k, t], ybuf_ref.at[s, k], t, sems.at[s]).start(priority=k)
            return c
        lax.fori_loop(0, MOE_TT, issue, 0, unroll=DMA_ISSUE_UNROLL)

    @pl.when(i == 0)
    def _():
        gather(dest_ref, slot)

    @pl.when(i + 1 < pl.num_programs(0))
    def _():
        gather(dest_next_ref, 1 - slot)

    for k in range(2):
        pltpu.make_async_copy(y_ref.at[pl.ds(0, MOE_TT)], ybuf_ref.at[slot, k], sems.at[slot]).wait()

    r = rinfo_ref[...]
    lane = lax.broadcasted_iota(jnp.int32, r.shape, 1)
    w1 = jnp.sum(jnp.where(lane == RI_W1, r, 0.0), axis=1, keepdims=True)
    w2 = jnp.sum(jnp.where(lane == RI_W2, r, 0.0), axis=1, keepdims=True)
    f = w1 * ybuf_ref[slot, 0] + w2 * ybuf_ref[slot, 1]
    mod = mod_ref[0]
    o_ref[...] = _ln(ALPHA * x1_ref[...] + mod[5:6] * f) * g_ref[...] + b_ref[...]


def _combine(dest, y, x1, rinfo, mod, g, b):
    t = x1.shape[0]
    nb = mod.shape[0]
    tpb = t // nb // MOE_TT
    row = lambda i: (i, 0)
    const = lambda i: (0, 0)
    n_steps = t // MOE_TT
    return pl.pallas_call(
        _combine_kernel,
        grid=(n_steps,),
        in_specs=[
            pl.BlockSpec((1, 2, MOE_TT), lambda i: (i, 0, 0), memory_space=pltpu.SMEM),
            pl.BlockSpec((1, 2, MOE_TT), lambda i: (jnp.minimum(i + 1, n_steps - 1), 0, 0),
                         memory_space=pltpu.SMEM),
            pl.BlockSpec(memory_space=pl.ANY),
            pl.BlockSpec((MOE_TT, D_MODEL), row),
            pl.BlockSpec((MOE_TT, LANES), row),
            pl.BlockSpec((1, SUBLANES, D_MODEL), lambda i: (i // tpb, 0, 0)),
            pl.BlockSpec((1, D_MODEL), const),
            pl.BlockSpec((1, D_MODEL), const),
        ],
        out_specs=pl.BlockSpec((MOE_TT, D_MODEL), row),
        out_shape=jax.ShapeDtypeStruct((t, D_MODEL), F32),
        scratch_shapes=[pltpu.VMEM((2, 2, MOE_TT, D_MODEL), F32), pltpu.SemaphoreType.DMA((2,))],
        compiler_params=_cp(("arbitrary",)),
        name="moe_combine",
    )(dest, dest, y, x1, rinfo, mod, g, b)


def _moe(h2, x1, rinfo, mod, g, b, wg, wu, wd):
    t = h2.shape[0]
    route, counts = _route_scan(rinfo)
    cnt = counts[0, :N_EXPERTS].astype(jnp.int32)
    padded = (cnt + MOE_TM - 1) // MOE_TM * MOE_TM
    ends = jnp.cumsum(padded)
    starts = ends - padded
    dest = jnp.stack([starts[route[:, 0]] + route[:, 2], starts[route[:, 1]] + route[:, 3]])
    dest = dest.reshape(2, t // MOE_TT, MOE_TT).transpose(1, 0, 2)
    tile_row0 = jnp.arange(_moe_rows(t) // MOE_TM, dtype=jnp.int32) * MOE_TM
    tile_expert = jnp.minimum(jnp.sum(tile_row0[:, None] >= ends[None, :], axis=1), N_EXPERTS - 1).astype(jnp.int32)
    tile_valid = jnp.clip(starts[tile_expert] + cnt[tile_expert] - tile_row0, 0, MOE_TM).astype(jnp.int32)
    n_tiles_used = (ends[-1:] // MOE_TM).astype(jnp.int32)
    xs = _dispatch(dest, h2)
    y = _experts(tile_expert, tile_valid, n_tiles_used, xs, wg, wu, wd, fc=D_FF_EXPERT // 2)
    return _combine(dest, y, x1, rinfo, mod, g, b)


def _rope_np(n_tokens):
    rows = np.repeat(np.arange(n_tokens // GRID_W), GRID_W)
    cols = np.tile(np.arange(GRID_W), n_tokens // GRID_W)
    pos = np.stack([rows, cols], axis=-1).astype(np.float64)
    inv_freq = 1.0 / (ROPE_THETA ** (np.arange(ROPE_FREQS, dtype=np.float64) / ROPE_FREQS))
    ang = pos[..., None] * inv_freq
    cos, sin = np.cos(ang), np.sin(ang)
    cos32 = np.broadcast_to(cos[:, :, None, :], (n_tokens, 2, 2, ROPE_FREQS)).reshape(n_tokens, QK_ROPE)
    sgn = np.array([-1.0, 1.0])[None, None, :, None]
    sin32 = (np.broadcast_to(sin[:, :, None, :], (n_tokens, 2, 2, ROPE_FREQS)) * sgn).reshape(n_tokens, QK_ROPE)
    return cos32, sin32


def _half_swap_perm():
    p = np.arange(QK_ROPE).reshape(2, 2, ROPE_FREQS)
    return p[:, ::-1, :].reshape(QK_ROPE)


def _tables():
    c0 = SM_SCALE * math.log2(math.e)
    cos32, sin32 = _rope_np(SEQ)
    one64 = np.ones((SEQ, QK_NOPE))
    tq_lat = c0 * np.concatenate([one64, cos32, sin32], axis=1)
    tk_lat = np.concatenate([cos32, sin32, np.zeros((SEQ, LANES - 2 * QK_ROPE))], axis=1)
    tq_ctx = c0 * np.concatenate([np.ones((CTX_LEN, QK_NOPE + QK_ROPE)), np.zeros((CTX_LEN, QK_ROPE))], axis=1)
    tk_ctx = np.concatenate([np.ones((CTX_LEN, QK_ROPE)), np.zeros((CTX_LEN, LANES - QK_ROPE))], axis=1)
    f32 = lambda a: jnp.asarray(a, F32)
    bf = lambda a: jnp.asarray(a, BF16)

    def dft(n):
        k = np.arange(n)
        ang = 2.0 * np.pi * ((k[:, None] * k[None, :]) % n) / n
        return np.cos(ang), np.sin(ang)

    c128, s128 = dft(FFT_R)
    c256, s256 = dft(CTX_LEN)
    bd_lat = np.concatenate([c128, -s128], axis=1) / math.sqrt(SEQ * F_GROUP_DIM)
    bd_ctx = np.concatenate([c128, -s128], axis=1) / math.sqrt(CTX_LEN * F_GROUP_DIM)
    k1 = np.arange(FFT_R)
    tw = 2.0 * np.pi * ((k1[:, None] * k1[None, :]) % SEQ) / SEQ
    tc = jnp.broadcast_to(f32(np.cos(tw))[:, :, None], (FFT_R, FFT_R, LANES)).reshape(FFT_R, FFT_R * LANES)
    ts = jnp.broadcast_to(f32(np.sin(tw))[:, :, None], (FFT_R, FFT_R, LANES)).reshape(FFT_R, FFT_R * LANES)
    return dict(tq_lat=f32(tq_lat), tk_lat=f32(tk_lat), tq_ctx=f32(tq_ctx), tk_ctx=f32(tk_ctx),
                c128=bf(c128), s128=bf(s128), c256=bf(c256), s256=bf(s256),
                bd_lat=bf(bd_lat), bd_ctx=bf(bd_ctx), tc=tc, ts=ts)


def _arrange_weights(l, w_in, w_uq, w_uk, w_uv):
    sw = _half_swap_perm()
    wi = w_in[l]
    kr = wi[:, OFF_KR:OFF_F]
    w_ip = jnp.concatenate([
        wi[:, :OFF_KR], kr, kr[:, sw], jnp.zeros((D_MODEL, LANES - 2 * QK_ROPE), F32),
        wi[:, OFF_F:]], axis=1).astype(BF16)
    uq = w_uq[l].reshape(Q_LORA, N_HEADS, QK_NOPE + QK_ROPE)
    qr = uq[:, :, QK_NOPE:]
    wq = jnp.concatenate([uq, qr[:, :, sw]], axis=-1).reshape(Q_LORA, N_HEADS * HEAD_PAD).astype(BF16)
    uk = w_uk[l].reshape(KV_LORA, N_HEADS, QK_NOPE)
    top = jnp.concatenate([uk, jnp.zeros((KV_LORA, N_HEADS, HEAD_PAD - QK_NOPE), F32)], axis=-1)
    rmat = np.zeros((LANES, HEAD_PAD), np.float32)
    for i in range(QK_ROPE):
        for r in (i, QK_ROPE + i):
            rmat[r, QK_NOPE + i] = 1.0
            rmat[r, QK_NOPE + QK_ROPE + i] = 1.0
    bot = jnp.broadcast_to(jnp.asarray(rmat)[:, None, :], (LANES, N_HEADS, HEAD_PAD))
    wkt = jnp.transpose(jnp.concatenate([top, bot], axis=0), (1, 2, 0)).astype(BF16)
    uv = w_uv[l].reshape(KV_LORA, N_HEADS // 2, 2, V_HEAD)
    z = jnp.zeros_like(uv[:, :, 0])
    even = jnp.concatenate([uv[:, :, 0], z], axis=-1)
    odd = jnp.concatenate([z, uv[:, :, 1]], axis=-1)
    wv = jnp.stack([even, odd], axis=2).reshape(KV_LORA, N_HEADS * HEAD_PAD).astype(BF16)
    return w_ip, wq, wkt, wv


def kernel(x, c, ctx, c_ctx, w_mod, b_mod, w_in, g_q, w_uq, g_kv, w_uk, w_uv, w_ao, w_fo, w_out,
           ln1_g, ln1_b, ln2_g, ln2_b, w_ff_gate, w_ff_up, w_ff_down,
           w_router, b_router, w_e_gate, w_e_up, w_e_down):
    tb = _tables()
    t_lat = BATCH * SEQ
    t_ctx = BATCH * CTX_LEN

    cs = jnp.concatenate([c, c_ctx[None, :], jnp.zeros((SUBLANES - BATCH - 1, D_MODEL), F32)], axis=0)
    mods = _modulation(cs, w_mod.astype(BF16), b_mod).reshape(DEPTH, SUBLANES, N_MOD, D_MODEL)
    mods = jnp.concatenate([mods, jnp.zeros((DEPTH, SUBLANES, SUBLANES - N_MOD, D_MODEL), F32)], axis=2)

    xl = x.reshape(t_lat, D_MODEL)
    xc = ctx.reshape(t_ctx, D_MODEL)

    for l in range(DEPTH):
        last = l == DEPTH - 1
        mod_x = mods[l, :BATCH]
        mod_c = jnp.broadcast_to(mods[l, BATCH:BATCH + 1], (BATCH, SUBLANES, D_MODEL))
        w_ip, wq, wkt, wv = _arrange_weights(l, w_in, w_uq, w_uk, w_uv)
        gq = g_q[l][None, :]
        gkv = g_kv[l][None, :]
        wao = w_ao[l].astype(BF16)
        wfo = w_fo[l].astype(BF16)
        wout = w_out[l].astype(BF16)
        g1, b1 = ln1_g[l][None, :], ln1_b[l][None, :]
        g2, b2 = ln2_g[l][None, :], ln2_b[l][None, :]

        qlat_c, ckr_c, wf_c, gates_c = _inproj(xc, mod_c, w_ip, gq, gkv, tb["tk_ctx"], tb["bd_ctx"], tm=CTX_LEN)
        q_c, kt_c, v_c = _qkv(qlat_c, ckr_c, tb["tq_ctx"], wq, wkt, wv, nb=BATCH, tm=CTX_LEN)

        qlat, ckr, wf, gates = _inproj(xl, mod_x, w_ip, gq, gkv, tb["tk_lat"], tb["bd_lat"], tm=ROW_TM)
        q, kt, v = _qkv(qlat, ckr, tb["tq_lat"], wq, wkt, wv, nb=BATCH, tm=QKV_TM)
        attn = _attention(q, kt_c, v_c, kt, v, tq=ATTN_TQ, tk=ATTN_TK, peel=ATTN_PEEL, unroll=ATTN_UNROLL)
        four = _fourier_latent(wf.reshape(BATCH, SEQ, 2 * FOURIER_DIM), tb["c128"], tb["s128"], tb["tc"], tb["ts"])

        if l % 2 == 0:
            i = l // 2
            x1, h2 = _mixer(attn.reshape(t_lat, -1), four.reshape(t_lat, -1), gates, xl, mod_x, g1, b1,
                            wao, wfo, wout, tm=ROW_TM)
            wg = w_ff_gate[i].astype(BF16)
            wu = w_ff_up[i].astype(BF16)
            wd = w_ff_down[i].astype(BF16)
            xl_new = _ffn(h2, x1, mod_x, g2, b2, wg, wu, wd, tm=ROW_TM)
        else:
            i = l // 2
            wr = jnp.concatenate([w_router[i], jnp.zeros((D_MODEL, LANES - N_EXPERTS), F32)], axis=1)
            br = jnp.concatenate([b_router[i], jnp.zeros((LANES - N_EXPERTS,), F32)])[None, :]
            x1, h2, rinfo = _mixer(attn.reshape(t_lat, -1), four.reshape(t_lat, -1), gates, xl, mod_x, g1, b1,
                                   wao, wfo, wout, tm=ROUTE_TM, router=(wr, br))
            wg = w_e_gate[i].astype(BF16)
            wu = w_e_up[i].astype(BF16)
            wd = w_e_down[i].astype(BF16)
            xl_new = _moe(h2, x1, rinfo, mod_x, g2, b2, wg, wu, wd)

        if not last:
            assert l % 2 == 0
            attn_c = _attention(q_c, kt_c, v_c, None, None, tq=CTX_LEN, tk=CTX_LEN)
            four_c = _fourier_ctx(wf_c.reshape(BATCH, CTX_LEN, 2 * FOURIER_DIM), tb["c256"], tb["s256"])
            x1c, h2c = _mixer(attn_c.reshape(t_ctx, -1), four_c.reshape(t_ctx, -1), gates_c, xc, mod_c,
                              g1, b1, wao, wfo, wout, tm=CTX_LEN)
            xc = _ffn(h2c, x1c, mod_c, g2, b2, wg, wu, wd, tm=CTX_LEN)
        xl = xl_new

    return xl.reshape(BATCH, SEQ, D_MODEL)
```
